```python
import math
import jax
import jax.numpy as jnp
from jax import lax
import numpy as np

D_MODEL = 1024
BATCH = 32
SEQ = 2048
DEPTH = 2

N_AB_LAYERS = (DEPTH + 1) // 2
N_C_LAYERS = DEPTH // 2

S5_WIDTH = D_MODEL // 4
S5_GROUP = 16
S5_GROUPS = S5_WIDTH // S5_GROUP
S5_STATE = 64
GDN_HEAD_DIM = 128
GDN_HEADS = (D_MODEL - S5_WIDTH) // GDN_HEAD_DIM
GDN_WIDTH = GDN_HEADS * GDN_HEAD_DIM
GDN_CONV = 4
GDN_CHUNK = 64
AB_MIX_WIDTH = S5_WIDTH + GDN_WIDTH
AB_IN = S5_WIDTH + 4 * GDN_WIDTH + 2 * GDN_HEADS

NSA_HEADS = 16
NSA_HEAD_DIM = D_MODEL // NSA_HEADS
NSA_KV_GROUPS = 4
NSA_GROUP_SIZE = NSA_HEADS // NSA_KV_GROUPS
NSA_WIDTH = NSA_HEADS * NSA_HEAD_DIM
NSA_KV_WIDTH = NSA_KV_GROUPS * NSA_HEAD_DIM
NSA_IN = NSA_WIDTH + 6 * NSA_KV_WIDTH + 3 * NSA_HEADS
CMP_LEN = 32
CMP_STRIDE = 16
CMP_HIDDEN = 256
SLC_BLOCK = 64
N_SELECT = 4
WINDOW = 256
NSA_Q_BLOCK = 64

FFN_HIDDEN = 2816
FFN_CONV = 3

RMS_EPS = 1e-6
NEG_INF = -1e30
FORCE_SCORE = 1e9

kernel_name = 'hybrid_s5_gdn_nsa_convffn'


def rms_norm(x, g):
    x32 = x.astype(jnp.float32)
    y = x32 * lax.rsqrt(jnp.mean(x32 * x32, axis=-1, keepdims=True) + RMS_EPS)
    return (y * g.astype(jnp.float32)).astype(x.dtype)


def l2_normalize(x):
    return x * lax.rsqrt(jnp.sum(x * x, axis=-1, keepdims=True) + 1e-6)


def causal_dwconv(x, w):
    k_width = w.shape[0]
    seq = x.shape[1]
    xp = jnp.pad(x, ((0, 0), (k_width - 1, 0), (0, 0)))
    return sum(xp[:, k:k + seq] * w[k] for k in range(k_width))


def alibi_slopes(n):
    return np.asarray([2.0 ** (-8.0 * (h + 1) / n) for h in range(n)], dtype=np.float32)


def masked_softmax(s, mask):
    p = jax.nn.softmax(jnp.where(mask, s, NEG_INF), axis=-1)
    return jnp.where(mask, p, 0.0)


def s5_mixer(u, lam_re, lam_im, log_step, b_re, b_im, c_re, c_im, d, w_glu, b_glu):
    bsz, seq, _ = u.shape
    f32 = jnp.float32
    u32 = u.astype(f32).reshape(bsz, seq, S5_GROUPS, S5_GROUP)
    step = jnp.exp(log_step.astype(f32))[:, None]
    lr, li = lam_re.astype(f32), lam_im.astype(f32)
    mag = jnp.exp(lr * step)
    a_re = mag * jnp.cos(li * step)
    a_im = mag * jnp.sin(li * step)
    den = lr * lr + li * li
    n_re, n_im = a_re - 1.0, a_im
    z_re = (n_re * lr + n_im * li) / den
    z_im = (n_im * lr - n_re * li) / den
    b_re, b_im = b_re.astype(f32), b_im.astype(f32)
    bb_re = z_re[..., None] * b_re - z_im[..., None] * b_im
    bb_im = z_re[..., None] * b_im + z_im[..., None] * b_re
    bu_re = jnp.einsum('gph,bsgh->bsgp', bb_re, u32)
    bu_im = jnp.einsum('gph,bsgh->bsgp', bb_im, u32)
    a_re_s = jnp.broadcast_to(a_re[None, None], (1, seq, S5_GROUPS, S5_STATE))
    a_im_s = jnp.broadcast_to(a_im[None, None], (1, seq, S5_GROUPS, S5_STATE))

    def combine(e1, e2):
        ar1, ai1, br1, bi1 = e1
        ar2, ai2, br2, bi2 = e2
        return (ar2 * ar1 - ai2 * ai1,
                ar2 * ai1 + ai2 * ar1,
                ar2 * br1 - ai2 * bi1 + br2,
                ar2 * bi1 + ai2 * br1 + bi2)

    _, _, h_re, h_im = lax.associative_scan(combine, (a_re_s, a_im_s, bu_re, bu_im), axis=1)
    y = (jnp.einsum('ghp,bsgp->bsgh', c_re.astype(f32), h_re)
         - jnp.einsum('ghp,bsgp->bsgh', c_im.astype(f32), h_im)
         + d.astype(f32) * u32).reshape(bsz, seq, S5_WIDTH)
    z = jax.nn.gelu(y)
    out = z * jax.nn.sigmoid(z @ w_glu.astype(f32) + b_glu.astype(f32))
    return out.astype(u.dtype)


def chunk_gated_delta_rule(q, k, v, g, beta):
    bsz, seq, n_heads, dk = q.shape
    dv = v.shape[-1]
    c = GDN_CHUNK
    n_chunks = seq // c

    def to_chunks(t):
        return jnp.moveaxis(t.reshape((bsz, n_chunks, c, n_heads) + t.shape[3:]), 3, 1)

    q, k, v, g, beta = map(to_chunks, (q, k, v, g, beta))
    g = jnp.cumsum(g, axis=-1)
    incl = jnp.tril(jnp.ones((c, c), dtype=bool))
    strict = jnp.tril(jnp.ones((c, c), dtype=bool), -1)
    decay = jnp.exp(jnp.where(incl, g[..., :, None] - g[..., None, :], NEG_INF))
    kb = k * beta[..., None]
    vb = v * beta[..., None]
    lower = jnp.where(strict, jnp.einsum('bhncd,bhnsd->bhncs', kb, k) * decay, 0.0)
    rhs = jnp.concatenate([vb, kb * jnp.exp(g)[..., None]], axis=-1)
    sol = lax.linalg.triangular_solve(lower, rhs, left_side=True, lower=True, unit_diagonal=True)
    u, w = sol[..., :dv], sol[..., dv:]
    intra = jnp.einsum('bhncd,bhnsd->bhncs', q, k) * decay
    q_dec = q * jnp.exp(g)[..., None]
    k_dec = k * jnp.exp(g[..., -1:] - g)[..., None]
    g_last = jnp.exp(g[..., -1])
    xs = tuple(jnp.moveaxis(t, 2, 0) for t in (q_dec, k_dec, u, w, intra, g_last))

    def step(state, inp):
        qd, kd, ui, wi, ai, gl = inp
        v_new = ui - jnp.einsum('bhcd,bhde->bhce', wi, state)
        o = jnp.einsum('bhcd,bhde->bhce', qd, state) + jnp.einsum('bhcs,bhse->bhce', ai, v_new)
        state = state * gl[..., None, None] + jnp.einsum('bhcd,bhce->bhde', kd, v_new)
        return state, o

    state0 = jnp.zeros((bsz, n_heads, dk, dv), jnp.float32)
    _, o = lax.scan(step, state0, xs)
    return jnp.transpose(o, (1, 0, 3, 2, 4)).reshape(bsz, seq, n_heads, dv)


def gated_deltanet(q, k, v, z, beta_logit, a_logit, conv_w, a_log, dt_bias, norm_w):
    bsz, seq, _ = q.shape
    f32 = jnp.float32
    shp = (bsz, seq, GDN_HEADS, GDN_HEAD_DIM)
    qkv = jax.nn.silu(causal_dwconv(jnp.concatenate([q, k, v], axis=-1), conv_w)).astype(f32)
    q, k, v = jnp.split(qkv, 3, axis=-1)
    q = l2_normalize(q.reshape(shp)) * (GDN_HEAD_DIM ** -0.5)
    k = l2_normalize(k.reshape(shp))
    v = v.reshape(shp)
    beta = jax.nn.sigmoid(beta_logit.astype(f32))
    g = -jnp.exp(a_log.astype(f32)) * jax.nn.softplus(a_logit.astype(f32) + dt_bias.astype(f32))
    o = chunk_gated_delta_rule(q, k, v, g, beta)
    o = rms_norm(o, norm_w) * jax.nn.silu(z.astype(f32).reshape(shp))
    return o.reshape(bsz, seq, GDN_WIDTH).astype(z.dtype)


def ab_layer(h, w_in, w_out, lam_re, lam_im, log_step, b_re, b_im, c_re, c_im, d, w_glu, b_glu,
             conv_w, a_log, dt_bias, norm_w):
    proj = h @ w_in
    cuts = np.cumsum([S5_WIDTH, GDN_WIDTH, GDN_WIDTH, GDN_WIDTH, GDN_WIDTH, GDN_HEADS])
    u, q, k, v, z, bl, al = jnp.split(proj, cuts, axis=-1)
    y_a = s5_mixer(u, lam_re, lam_im, log_step, b_re, b_im, c_re, c_im, d, w_glu, b_glu)
    y_b = gated_deltanet(q, k, v, z, bl, al, conv_w, a_log, dt_bias, norm_w)
    return jnp.concatenate([y_a, y_b], axis=-1) @ w_out


def compress_tokens(x, tok, pe, w1, b1, w2):
    bsz = x.shape[0]
    blocks = x[:, tok] + pe[:, None, :]
    flat = jnp.swapaxes(blocks, 2, 3).reshape(bsz, tok.shape[0], NSA_KV_GROUPS, CMP_LEN * NSA_HEAD_DIM)
    return jax.nn.gelu(flat @ w1 + b1) @ w2


def nsa_layer(h, w_in, w_out, pe_k, pe_v, k_w1, k_b1, k_w2, v_w1, v_b1, v_w2):
    bsz, seq, _ = h.shape
    G, R, dh, QB = NSA_KV_GROUPS, NSA_GROUP_SIZE, NSA_HEAD_DIM, NSA_Q_BLOCK
    f32 = jnp.float32
    proj = (h @ w_in).astype(f32)
    cuts = np.cumsum([NSA_WIDTH] + [NSA_KV_WIDTH] * 6)
    q, kc, vc, ks, vs, kw, vw, gl = jnp.split(proj, cuts, axis=-1)
    q = q.reshape(bsz, seq, G, R, dh) * (dh ** -0.5)
    kc, vc, ks, vs, kw, vw = (t.reshape(bsz, seq, G, dh) for t in (kc, vc, ks, vs, kw, vw))
    gates = jax.nn.sigmoid(gl).reshape(bsz, seq, G, R, 3)

    n_cmp = (seq - CMP_LEN) // CMP_STRIDE + 1
    tok = (np.arange(n_cmp, dtype=np.int32)[:, None] * CMP_STRIDE
           + np.arange(CMP_LEN, dtype=np.int32)[None, :])
    k_cmp = compress_tokens(kc, tok, pe_k, k_w1, k_b1, k_w2).astype(f32)
    v_cmp = compress_tokens(vc, tok, pe_v, v_w1, v_b1, v_w2).astype(f32)
    cmp_end = jnp.asarray(tok[:, -1])
    cmp_center = jnp.asarray(tok.mean(axis=1).astype(np.float32))

    n_slc = seq // SLC_BLOCK
    n_sel = min(N_SELECT, n_slc)
    blk_ids = np.arange(n_slc, dtype=np.int32)
    overlap = jnp.asarray(((tok // SLC_BLOCK)[:, :, None] == blk_ids[None, None, :])
                          .mean(axis=1).astype(np.float32))
    k_blocks = ks.reshape(bsz, n_slc, SLC_BLOCK, G, dh).transpose(0, 3, 1, 2, 4)
    v_blocks = vs.reshape(bsz, n_slc, SLC_BLOCK, G, dh).transpose(0, 3, 1, 2, 4)
    kw_pad = jnp.pad(kw, ((0, 0), (WINDOW, 0), (0, 0), (0, 0)))
    vw_pad = jnp.pad(vw, ((0, 0), (WINDOW, 0), (0, 0), (0, 0)))
    slopes = jnp.asarray(alibi_slopes(NSA_HEADS).reshape(G, R))
    b_idx = jnp.arange(bsz)[:, None, None, None]
    g_idx = jnp.arange(G)[None, :, None, None]
    in_blk = jnp.arange(SLC_BLOCK)
    win_off = jnp.arange(WINDOW + QB)
    m_sel = n_sel * SLC_BLOCK

    def query_block(i):
        start = i * QB
        t = start + jnp.arange(QB)
        tf = t.astype(f32)
        qb = lax.dynamic_slice_in_dim(q, start, QB, axis=1)
        gb = lax.dynamic_slice_in_dim(gates, start, QB, axis=1)
        s_c = (jnp.einsum('bqgrd,bjgd->bgrqj', qb, k_cmp)
               - slopes[:, :, None, None] * (tf[:, None] - cmp_center[None, :]))
        p_c = masked_softmax(s_c, cmp_end[None, :] <= t[:, None])
        o_c = jnp.einsum('bgrqj,bjgd->bqgrd', p_c, v_cmp)
        imp = jnp.einsum('bgrqj,jn->bgqn', p_c, overlap)
        blk_valid = (blk_ids * SLC_BLOCK)[None, :] <= t[:, None]
        forced = (blk_ids[None, :] == (t // SLC_BLOCK)[:, None]) | (blk_ids[None, :] == 0)
        score = jnp.where(blk_valid, jnp.where(forced, FORCE_SCORE, imp), NEG_INF)
        _, sel = lax.top_k(score, n_sel)
        k_sel = k_blocks[b_idx, g_idx, sel].reshape(bsz, G, QB, m_sel, dh)
        v_sel = v_blocks[b_idx, g_idx, sel].reshape(bsz, G, QB, m_sel, dh)
        pos = (sel[..., None] * SLC_BLOCK + in_blk).reshape(bsz, G, QB, m_sel)
        s_s = (jnp.einsum('bqgrd,bgqmd->bgrqm', qb, k_sel)
               - slopes[None, :, :, None, None] * (tf[None, None, :, None] - pos.astype(f32))[:, :, None])
        p_s = masked_softmax(s_s, (pos <= t[None, None, :, None])[:, :, None])
        o_s = jnp.einsum('bgrqm,bgqmd->bqgrd', p_s, v_sel)
        kwb = lax.dynamic_slice_in_dim(kw_pad, start, WINDOW + QB, axis=1)
        vwb = lax.dynamic_slice_in_dim(vw_pad, start, WINDOW + QB, axis=1)
        kpos = start - WINDOW + win_off
        mask_w = ((kpos[None, :] <= t[:, None]) & (kpos[None, :] > t[:, None] - WINDOW)
                  & (kpos[None, :] >= 0))
        s_w = (jnp.einsum('bqgrd,bkgd->bgrqk', qb, kwb)
               - slopes[:, :, None, None] * (tf[:, None] - kpos.astype(f32)[None, :]))
        p_w = masked_softmax(s_w, mask_w)
        o_w = jnp.einsum('bgrqk,bkgd->bqgrd', p_w, vwb)
        return gb[..., 0:1] * o_c + gb[..., 1:2] * o_s + gb[..., 2:3] * o_w

    o = lax.map(query_block, jnp.arange(seq // QB))
    o = jnp.moveaxis(o, 0, 1).reshape(bsz, seq, NSA_WIDTH)
    return o.astype(h.dtype) @ w_out


def conv_ffn(h, w_in, conv_w, conv_b, w_out):
    up = causal_dwconv(h @ w_in, conv_w) + conv_b
    a, b = jnp.split(up, 2, axis=-1)
    return (jax.nn.silu(a) * b) @ w_out


def setup_inputs(seed: int = 0) -> dict:
    key = jax.random.key(seed)
    keys = iter(jax.random.split(key, 48))
    f32 = jnp.float32

    def nrm(shape, scale):
        return jax.random.normal(next(keys), shape, f32) * scale

    def unif(shape, lo, hi):
        return jax.random.uniform(next(keys), shape, f32, lo, hi)

    na, nc = N_AB_LAYERS, N_C_LAYERS
    dt = jnp.exp(unif((na, GDN_HEADS), math.log(1e-3), math.log(1e-1)))
    return {
        'x': nrm((BATCH, SEQ, D_MODEL), 1.0),
        'ab_w_in': nrm((na, D_MODEL, AB_IN), D_MODEL ** -0.5),
        'ab_w_out': nrm((na, AB_MIX_WIDTH, D_MODEL), AB_MIX_WIDTH ** -0.5),
        's5_lambda_re': -0.5 + nrm((na, S5_GROUPS, S5_STATE), 0.01),
        's5_lambda_im': math.pi * jnp.arange(S5_STATE, dtype=f32) + nrm((na, S5_GROUPS, S5_STATE), 0.01),
        's5_log_step': unif((na, S5_GROUPS), math.log(1e-3), math.log(1e-1)),
        's5_b_re': nrm((na, S5_GROUPS, S5_STATE, S5_GROUP), (2 * S5_GROUP) ** -0.5),
        's5_b_im': nrm((na, S5_GROUPS, S5_STATE, S5_GROUP), (2 * S5_GROUP) ** -0.5),
        's5_c_re': nrm((na, S5_GROUPS, S5_GROUP, S5_STATE), (2 * S5_STATE) ** -0.5),
        's5_c_im': nrm((na, S5_GROUPS, S5_GROUP, S5_STATE), (2 * S5_STATE) ** -0.5),
        's5_d': nrm((na, S5_GROUPS, S5_GROUP), 1.0),
        's5_w_glu': nrm((na, S5_WIDTH, S5_WIDTH), S5_WIDTH ** -0.5),
        's5_b_glu': nrm((na, S5_WIDTH), 0.01),
        'gdn_conv_w': nrm((na, GDN_CONV, 3 * GDN_WIDTH), GDN_CONV ** -0.5),
        'gdn_a_log': jnp.log(unif((na, GDN_HEADS), 1.0, 16.0)),
        'gdn_dt_bias': dt + jnp.log(-jnp.expm1(-dt)),
        'gdn_norm_w': 1.0 + nrm((na, GDN_HEAD_DIM), 0.01),
        'nsa_w_in': nrm((nc, D_MODEL, NSA_IN), D_MODEL ** -0.5),
        'nsa_w_out': nrm((nc, NSA_WIDTH, D_MODEL), NSA_WIDTH ** -0.5),
        'nsa_pe_k': nrm((nc, CMP_LEN, NSA_HEAD_DIM), 0.02),
        'nsa_pe_v': nrm((nc, CMP_LEN, NSA_HEAD_DIM), 0.02),
        'nsa_k_w1': nrm((nc, CMP_LEN * NSA_HEAD_DIM, CMP_HIDDEN), (CMP_LEN * NSA_HEAD_DIM) ** -0.5),
        'nsa_k_b1': nrm((nc, CMP_HIDDEN), 0.01),
        'nsa_k_w2': nrm((nc, CMP_HIDDEN, NSA_HEAD_DIM), CMP_HIDDEN ** -0.5),
        'nsa_v_w1': nrm((nc, CMP_LEN * NSA_HEAD_DIM, CMP_HIDDEN), (CMP_LEN * NSA_HEAD_DIM) ** -0.5),
        'nsa_v_b1': nrm((nc, CMP_HIDDEN), 0.01),
        'nsa_v_w2': nrm((nc, CMP_HIDDEN, NSA_HEAD_DIM), CMP_HIDDEN ** -0.5),
        'ffn_w_in': nrm((DEPTH, D_MODEL, 2 * FFN_HIDDEN), D_MODEL ** -0.5),
        'ffn_conv_w': nrm((DEPTH, FFN_CONV, 2 * FFN_HIDDEN), FFN_CONV ** -0.5),
        'ffn_conv_b': nrm((DEPTH, 2 * FFN_HIDDEN), 0.01),
        'ffn_w_out': nrm((DEPTH, FFN_HIDDEN, D_MODEL), FFN_HIDDEN ** -0.5),
        'norm_mix': 1.0 + nrm((DEPTH, D_MODEL), 0.01),
        'norm_ffn': 1.0 + nrm((DEPTH, D_MODEL), 0.01),
        'norm_final': 1.0 + nrm((D_MODEL,), 0.01),
    }


def reference(x, ab_w_in, ab_w_out, s5_lambda_re, s5_lambda_im, s5_log_step, s5_b_re, s5_b_im,
              s5_c_re, s5_c_im, s5_d, s5_w_glu, s5_b_glu, gdn_conv_w, gdn_a_log, gdn_dt_bias,
              gdn_norm_w, nsa_w_in, nsa_w_out, nsa_pe_k, nsa_pe_v, nsa_k_w1, nsa_k_b1, nsa_k_w2,
              nsa_v_w1, nsa_v_b1, nsa_v_w2, ffn_w_in, ffn_conv_w, ffn_conv_b, ffn_w_out,
              norm_mix, norm_ffn, norm_final):
    h = x
    for layer in range(DEPTH):
        hn = rms_norm(h, norm_mix[layer])
        i = layer // 2
        if layer % 2 == 0:
            mix = ab_layer(hn, ab_w_in[i], ab_w_out[i], s5_lambda_re[i], s5_lambda_im[i],
                           s5_log_step[i], s5_b_re[i], s5_b_im[i], s5_c_re[i], s5_c_im[i],
                           s5_d[i], s5_w_glu[i], s5_b_glu[i], gdn_conv_w[i], gdn_a_log[i],
                           gdn_dt_bias[i], gdn_norm_w[i])
        else:
            mix = nsa_layer(hn, nsa_w_in[i], nsa_w_out[i], nsa_pe_k[i], nsa_pe_v[i],
                            nsa_k_w1[i], nsa_k_b1[i], nsa_k_w2[i],
                            nsa_v_w1[i], nsa_v_b1[i], nsa_v_w2[i])
        h = h + mix
        h = h + conv_ffn(rms_norm(h, norm_ffn[layer]), ffn_w_in[layer], ffn_conv_w[layer],
                         ffn_conv_b[layer], ffn_w_out[layer])
    return rms_norm(h, norm_final)
```

```python
import functools
import math

import numpy as np
import jax
import jax.numpy as jnp
from jax import lax
from jax.experimental import pallas as pl
from jax.experimental.pallas import tpu as pltpu

F32 = jnp.float32
BF16 = jnp.bfloat16
HI = lax.Precision.HIGHEST

D_MODEL = 1024
S5_WIDTH = 256
S5_GROUP = 16
S5_GROUPS = 16
S5_STATE = 64
S5_CHUNK = 16
GDN_HEAD_DIM = 128
GDN_HEADS = 6
GDN_WIDTH = 768
GDN_CONV = 4
GDN_CHUNK = 64
GDN_HEADS_PER_STEP = 2
NSA_HEADS = 16
NSA_HEAD_DIM = 64
NSA_KV_GROUPS = 4
NSA_GROUP_SIZE = 4
NSA_KV_WIDTH = 256
CMP_LEN = 32
CMP_STRIDE = 16
CMP_HIDDEN = 256
SLC_BLOCK = 64
N_SELECT = 4
WINDOW = 256
NSA_Q_BLOCK = 64
NSA_KEY_CHUNK = 256
FFN_HIDDEN = 2816
FFN_CONV = 3
FFN_COL_CHUNK = 256
RMS_EPS = 1e-6
NEG_INF = -1e30
LANES = 128
BF16_SUBLANES = 16
VMEM_LIMIT = 56 * 1024 * 1024
ROW_TILE = 512


def _cparams(n_axes):
    return pltpu.CompilerParams(dimension_semantics=("arbitrary",) * n_axes,
                                vmem_limit_bytes=VMEM_LIMIT)


def _rms(x, g):
    return x * lax.rsqrt(jnp.mean(x * x, axis=-1, keepdims=True) + RMS_EPS) * g


def _gelu(x):
    return 0.5 * x * (1.0 + jnp.tanh(math.sqrt(2.0 / math.pi) * (x + 0.044715 * (x * x * x))))


def _sigmoid(x):
    return 1.0 / (1.0 + jnp.exp(-x))


def _silu(x):
    return x * _sigmoid(x)


def _dot(a, b):
    return jnp.dot(a, b, preferred_element_type=F32)


def _dot_nt(a, b):
    return lax.dot_general(a, b, (((1,), (1,)), ((), ())), preferred_element_type=F32)


def _dot_tn(a, b):
    return lax.dot_general(a, b, (((0,), (0,)), ((), ())), preferred_element_type=F32)


def _const_spec(shape):
    nd = len(shape)
    return pl.BlockSpec(shape, lambda *_: (0,) * nd)


def _norm_proj_kernel(x_ref, g_ref, *refs, n_out):
    xn = _rms(x_ref[...], g_ref[...]).astype(BF16)
    for w_ref, o_ref in zip(refs[:n_out], refs[n_out:]):
        o_ref[...] = _dot(xn, w_ref[...]).astype(o_ref.dtype)


def _norm_proj(h, gain, weights):
    t, d = h.shape
    n_out = len(weights)
    return pl.pallas_call(
        functools.partial(_norm_proj_kernel, n_out=n_out),
        out_shape=[jax.ShapeDtypeStruct((t, w.shape[1]), F32) for w in weights],
        grid=(t // ROW_TILE,),
        in_specs=[pl.BlockSpec((ROW_TILE, d), lambda i: (i, 0)), _const_spec((1, d))]
        + [_const_spec(w.shape) for w in weights],
        out_specs=[pl.BlockSpec((ROW_TILE, w.shape[1]), lambda i: (i, 0)) for w in weights],
        compiler_params=_cparams(1),
        name="norm_proj",
    )(h, gain.reshape(1, d), *weights)


def _out_proj_kernel(h_ref, *refs, n_in):
    acc = h_ref[...]
    for y_ref, w_ref in zip(refs[:n_in], refs[n_in:2 * n_in]):
        acc = acc + _dot(y_ref[...].astype(BF16), w_ref[...])
    refs[2 * n_in][...] = acc


def _out_proj(h, ys, ws):
    t, d = h.shape
    n_in = len(ys)
    return pl.pallas_call(
        functools.partial(_out_proj_kernel, n_in=n_in),
        out_shape=jax.ShapeDtypeStruct((t, d), F32),
        grid=(t // ROW_TILE,),
        in_specs=[pl.BlockSpec((ROW_TILE, d), lambda i: (i, 0))]
        + [pl.BlockSpec((ROW_TILE, y.shape[1]), lambda i: (i, 0)) for y in ys]
        + [_const_spec(w.shape) for w in ws],
        out_specs=pl.BlockSpec((ROW_TILE, d), lambda i: (i, 0)),
        compiler_params=_cparams(1),
        name="out_proj",
    )(h, *ys, *ws)


def _ffn_kernel(h_ref, hp_ref, g_ref, win_ref, cw_ref, cb_ref, wout_ref, gf_ref, o_ref,
                xn_ref, acc_ref, *, tiles_per_seq, n_chunks, final):
    halo = BF16_SUBLANES
    x = h_ref[...]
    g = g_ref[...]
    seq_start = pl.program_id(0) % tiles_per_seq == 0
    xn_ref[0:halo, :] = jnp.where(seq_start, 0.0, _rms(hp_ref[...], g)).astype(BF16)
    xn_ref[halo:, :] = _rms(x, g).astype(BF16)
    acc_ref[...] = jnp.zeros_like(acc_ref)

    def conv(u, cw, cb):
        y = pltpu.roll(u, 2, 0) * cw[0:1] + pltpu.roll(u, 1, 0) * cw[1:2] + u * cw[2:3] + cb
        return y[halo:]

    def body(j, carry):
        xe = xn_ref[...]
        a = conv(_dot(xe, win_ref[j]), cw_ref[j], cb_ref[j])
        b = conv(_dot(xe, win_ref[n_chunks + j]), cw_ref[n_chunks + j], cb_ref[n_chunks + j])
        act = (_silu(a) * b).astype(BF16)
        acc_ref[...] += _dot(act, wout_ref[j])
        return carry

    lax.fori_loop(0, n_chunks, body, 0)
    out = x + acc_ref[...]
    if final:
        out = _rms(out, gf_ref[...])
    o_ref[...] = out


def _ffn(h, seq, gain, w_in, conv_w, conv_b, w_out, gain_final, final):
    t, d = h.shape
    fc = FFN_COL_CHUNK
    n_chunks = FFN_HIDDEN // fc
    halo = BF16_SUBLANES
    win = w_in.astype(BF16).reshape(d, 2 * n_chunks, fc).transpose(1, 0, 2)
    cw = conv_w.astype(F32).reshape(FFN_CONV, 2 * n_chunks, fc).transpose(1, 0, 2)
    cb = conv_b.astype(F32).reshape(2 * n_chunks, 1, fc)
    wout = w_out.astype(BF16).reshape(n_chunks, fc, d)
    blocks_per_tile = ROW_TILE // halo
    return pl.pallas_call(
        functools.partial(_ffn_kernel, tiles_per_seq=seq // ROW_TILE, n_chunks=n_chunks, final=final),
        out_shape=jax.ShapeDtypeStruct((t, d), F32),
        grid=(t // ROW_TILE,),
        in_specs=[pl.BlockSpec((ROW_TILE, d), lambda i: (i, 0)),
                  pl.BlockSpec((halo, d), lambda i: (jnp.maximum(i * blocks_per_tile - 1, 0), 0)),
                  _const_spec((1, d)),
                  pl.BlockSpec(win.shape, lambda i: (0, 0, 0), pipeline_mode=pl.Buffered(1)),
                  _const_spec(cw.shape), _const_spec(cb.shape),
                  pl.BlockSpec(wout.shape, lambda i: (0, 0, 0), pipeline_mode=pl.Buffered(1)),
                  _const_spec((1, d))],
        out_specs=pl.BlockSpec((ROW_TILE, d), lambda i: (i, 0)),
        scratch_shapes=[pltpu.VMEM((ROW_TILE + halo, d), BF16), pltpu.VMEM((ROW_TILE, d), F32)],
        compiler_params=_cparams(1),
        name="conv_ffn",
    )(h, h, gain.reshape(1, d), win, cw, cb, wout, gain_final.reshape(1, d))


def _s5_prep(lam_re, lam_im, log_step, b_re, b_im, c_re, c_im, d_skip):
    g_n, p_n, h_n, L = S5_GROUPS, S5_STATE, S5_GROUP, S5_CHUNK
    step = jnp.exp(log_step.astype(F32))[:, None]
    lr, li = lam_re.astype(F32), lam_im.astype(F32)
    mag = jnp.exp(lr * step)
    a_re = mag * jnp.cos(li * step)
    a_im = mag * jnp.sin(li * step)
    den = lr * lr + li * li
    n_re, n_im = a_re - 1.0, a_im
    z_re = (n_re * lr + n_im * li) / den
    z_im = (n_im * lr - n_re * li) / den
    b_re, b_im = b_re.astype(F32), b_im.astype(F32)
    bb_re = z_re[..., None] * b_re - z_im[..., None] * b_im
    bb_im = z_re[..., None] * b_im + z_im[..., None] * b_re
    c_re, c_im = c_re.astype(F32), c_im.astype(F32)
    pw_re, pw_im = [jnp.ones_like(a_re)], [jnp.zeros_like(a_im)]
    for _ in range(L):
        pr, pi = pw_re[-1], pw_im[-1]
        pw_re.append(pr * a_re - pi * a_im)
        pw_im.append(pr * a_im + pi * a_re)
    eye_g = jnp.eye(g_n, dtype=F32)
    ks = []
    for j in range(L):
        ab_re = pw_re[j][..., None] * bb_re - pw_im[j][..., None] * bb_im
        ab_im = pw_re[j][..., None] * bb_im + pw_im[j][..., None] * bb_re
        kj = (jnp.einsum('gop,gpi->gio', c_re, ab_re, precision=HI)
              - jnp.einsum('gop,gpi->gio', c_im, ab_im, precision=HI))
        ks.append(jnp.einsum('gio,gk->giko', kj, eye_g).reshape(g_n * h_n, g_n * h_n))
    kstack = jnp.concatenate(ks[::-1], axis=0).astype(BF16)
    bb = jnp.stack([jnp.einsum('gph,gk->ghkp', bb_re, eye_g), jnp.einsum('gph,gk->ghkp', bb_im, eye_g)], axis=2)
    bb = bb.reshape(g_n * h_n, 2 * g_n * p_n).astype(BF16)
    cc = jnp.stack([jnp.einsum('ghp,gk->gpkh', c_re, eye_g), -jnp.einsum('ghp,gk->gpkh', c_im, eye_g)], axis=0)
    cc = cc.reshape(2 * g_n * p_n, g_n * h_n).astype(BF16)
    apow = jnp.stack([jnp.concatenate([r.reshape(-1), i.reshape(-1)]) for r, i in zip(pw_re, pw_im)], axis=0)
    dvec = jnp.tile(d_skip.astype(F32).reshape(1, g_n * h_n), (1, L))
    return kstack, bb, cc, apow, dvec


def _s5_kernel(u_ref, ks_ref, bb_ref, cc_ref, ap_ref, dv_ref, wg_ref, bg_ref, o_ref, x_scr, hp_scr,
               *, n_batch, n_chunk):
    L, w = S5_CHUNK, S5_WIDTH
    half = S5_GROUPS * S5_STATE
    rows = n_batch * n_chunk
    u2 = u_ref[...].reshape(rows, L * w)
    u2b = u2.astype(BF16)

    def cmul(j, x):
        ar, ai = ap_ref[j:j + 1, :half], ap_ref[j:j + 1, half:]
        xr, xi = x[:, :half], x[:, half:]
        return jnp.concatenate([ar * xr - ai * xi, ar * xi + ai * xr], axis=1)

    xin = _dot(u2b[:, (L - 1) * w:], bb_ref[...])
    for s in range(L - 1):
        xin = xin + cmul(L - 1 - s, _dot(u2b[:, s * w:(s + 1) * w], bb_ref[...]))
    x_scr[...] = xin

    a_l = ap_ref[L:L + 1, :]
    alr, ali = a_l[:, :half], a_l[:, half:]

    def scan_body(k, hs):
        new = []
        for b in range(n_batch):
            r = b * n_chunk + k
            h = hs[b]
            hp_scr[pl.ds(r, 1), :] = h
            hr, hi = h[:, :half], h[:, half:]
            new.append(jnp.concatenate([alr * hr - ali * hi, alr * hi + ali * hr], axis=1)
                       + x_scr[pl.ds(r, 1), :])
        return tuple(new)

    lax.fori_loop(0, n_chunk, scan_body, tuple(jnp.zeros((1, 2 * half), F32) for _ in range(n_batch)))
    hp = hp_scr[...]

    for t in range(L):
        y = _dot(u2b[:, :(t + 1) * w], ks_ref[(L - 1 - t) * w:, :])
        y = y + _dot(cmul(t + 1, hp).astype(BF16), cc_ref[...])
        y = y + dv_ref[:, t * w:(t + 1) * w] * u2[:, t * w:(t + 1) * w]
        z = _gelu(y)
        gate = _sigmoid(_dot(z.astype(BF16), wg_ref[...]) + bg_ref[...])
        o_ref[:, :, t * w:(t + 1) * w] = (z * gate).reshape(n_batch, n_chunk, w)


def _s5(u, seq, prep, w_glu, b_glu):
    t = u.shape[0]
    bsz = t // seq
    L, w = S5_CHUNK, S5_WIDTH
    n_chunk = seq // L
    n_batch = 2
    kstack, bb, cc, apow, dvec = prep
    u3 = u.reshape(bsz, n_chunk, L * w)
    out = pl.pallas_call(
        functools.partial(_s5_kernel, n_batch=n_batch, n_chunk=n_chunk),
        out_shape=jax.ShapeDtypeStruct((bsz, n_chunk, L * w), F32),
        grid=(bsz // n_batch,),
        in_specs=[pl.BlockSpec((n_batch, n_chunk, L * w), lambda i: (i, 0, 0)),
                  _const_spec(kstack.shape), _const_spec(bb.shape), _const_spec(cc.shape),
                  _const_spec(apow.shape), _const_spec(dvec.shape),
                  _const_spec((w, w)), _const_spec((1, w))],
        out_specs=pl.BlockSpec((n_batch, n_chunk, L * w), lambda i: (i, 0, 0)),
        scratch_shapes=[pltpu.VMEM((n_batch * n_chunk, 2 * S5_GROUPS * S5_STATE), F32),
                        pltpu.VMEM((n_batch * n_chunk, 2 * S5_GROUPS * S5_STATE), F32)],
        compiler_params=_cparams(1),
        name="s5_mixer",
    )(u3, kstack, bb, cc, apow, dvec, w_glu.astype(BF16), b_glu.astype(F32).reshape(1, w))
    return out.reshape(t, w)


def _tri_inverse(low):
    c = low.shape[0]
    n = -low
    eye = (lax.broadcasted_iota(jnp.int32, (c, c), 0) == lax.broadcasted_iota(jnp.int32, (c, c), 1)).astype(F32)
    t = eye + n
    p = n
    span = 2
    while span < c:
        p = _dot(p.astype(BF16), p.astype(BF16))
        t = t + _dot(t.astype(BF16), p.astype(BF16))
        span *= 2
    return t


def _gdn_kernel(q_ref, k_ref, v_ref, z_ref, bc_ref, ac_ref, ar_ref, cw_ref, hp_ref, hr_ref, nw_ref,
                o_ref, qs, ks, vs, st, *, seq, n_head):
    c, dh = GDN_CHUNK, GDN_HEAD_DIM
    row = lax.broadcasted_iota(jnp.int32, (seq, 1), 0)

    def conv_silu(x, cw):
        y = x * cw[3:4]
        for sh in (1, 2, 3):
            y = y + jnp.where(row >= sh, pltpu.roll(x, sh, 0), 0.0) * cw[3 - sh:4 - sh]
        return _silu(y)

    def l2n(x):
        return x * lax.rsqrt(jnp.sum(x * x, axis=-1, keepdims=True) + 1e-6)

    qc = conv_silu(q_ref[...], cw_ref[0, 0])
    kc = conv_silu(k_ref[...], cw_ref[0, 1])
    vs[...] = conv_silu(v_ref[...], cw_ref[0, 2])
    for j in range(n_head):
        sl = slice(j * dh, (j + 1) * dh)
        qs[:, sl] = l2n(qc[:, sl]) * (dh ** -0.5)
        ks[:, sl] = l2n(kc[:, sl])

    hp = hp_ref[0]
    hr = hr_ref[0]

    def softplus(x):
        return jnp.maximum(x, 0.0) + jnp.log(1.0 + jnp.exp(-jnp.abs(x)))

    ii = lax.broadcasted_iota(jnp.int32, (c, c), 0)
    jj = lax.broadcasted_iota(jnp.int32, (c, c), 1)
    tril = (ii >= jj).astype(F32)
    triu = (ii <= jj).astype(F32)
    st[...] = jnp.zeros_like(st)

    def chunk_body(ci, carry):
        r0 = pl.multiple_of(ci * c, c)
        rs = pl.ds(r0, c)
        beta = _sigmoid(bc_ref[0, 0, rs, :])
        g_col = -jnp.exp(hp[0:1, :]) * softplus(ac_ref[0, 0, rs, :] + hp[1:2, :])
        g_row = -jnp.exp(hr[:, 0:1]) * softplus(ar_ref[0, 0, ci] + hr[:, 1:2])
        gc_col = jnp.dot(tril, g_col, precision=HI, preferred_element_type=F32)
        gc_row = jnp.dot(g_row, triu, precision=HI, preferred_element_type=F32)
        for j in range(n_head):
            sl = slice(j * dh, (j + 1) * dh)
            q, k, v = qs[rs, sl], ks[rs, sl], vs[rs, sl]
            gcc = gc_col[:, j:j + 1]
            gcr = gc_row[j:j + 1, :]
            g_last = gcc[c - 1:c, :]
            decay = jnp.exp(jnp.where(ii >= jj, gcc - gcr, NEG_INF))
            bj = beta[:, j:j + 1]
            kb = k * bj
            vb = v * bj
            kbb, kbf = kb.astype(BF16), k.astype(BF16)
            low = jnp.where(ii > jj, _dot_nt(kbb, kbf) * decay, 0.0)
            tinv = _tri_inverse(low).astype(BF16)
            e_g = jnp.exp(gcc)
            sol = _dot(tinv, jnp.concatenate([vb, kb * e_g], axis=1).astype(BF16))
            u, w = sol[:, :dh], sol[:, dh:]
            intra = _dot_nt(q.astype(BF16), kbf) * decay
            q_dec = q * e_g
            k_dec = k * jnp.exp(g_last - gcc)
            s_prev = st[j]
            s_b = s_prev.astype(BF16)
            v_new = u - _dot(w.astype(BF16), s_b)
            o = _dot(q_dec.astype(BF16), s_b) + _dot(intra.astype(BF16), v_new.astype(BF16))
            st[j] = s_prev * jnp.exp(g_last) + _dot_tn(k_dec.astype(BF16), v_new.astype(BF16))
            on = o * lax.rsqrt(jnp.mean(o * o, axis=-1, keepdims=True) + RMS_EPS) * nw_ref[...]
            o_ref[rs, sl] = on * _silu(z_ref[rs, sl])
        return carry

    lax.fori_loop(0, seq // c, chunk_body, 0)


def _gdn(q, k, v, z, ba, seq, conv_w, a_log, dt_bias, norm_w):
    t = q.shape[0]
    bsz = t // seq
    nh = GDN_HEADS_PER_STEP
    n_grp = GDN_HEADS // nh
    wd = nh * GDN_HEAD_DIM
    bl = ba[:, :GDN_HEADS].reshape(bsz, seq, n_grp, nh)
    al = ba[:, GDN_HEADS:2 * GDN_HEADS].reshape(bsz, seq, n_grp, nh)
    b_col, a_col = bl.transpose(0, 2, 1, 3), al.transpose(0, 2, 1, 3)
    n_chunk = seq // GDN_CHUNK
    a_row = al.reshape(bsz, n_chunk, GDN_CHUNK, n_grp, nh).transpose(0, 3, 1, 4, 2)
    cw = conv_w.astype(F32).reshape(GDN_CONV, 3, n_grp, wd).transpose(2, 1, 0, 3)
    hp = jnp.stack([a_log.astype(F32).reshape(n_grp, nh), dt_bias.astype(F32).reshape(n_grp, nh)], axis=1)
    hr = hp.transpose(0, 2, 1)
    act_spec = pl.BlockSpec((seq, wd), lambda b, g: (b, g))
    col_spec = pl.BlockSpec((1, 1, seq, nh), lambda b, g: (b, g, 0, 0))
    row_spec = pl.BlockSpec((1, 1, n_chunk, nh, GDN_CHUNK), lambda b, g: (b, g, 0, 0, 0))
    return pl.pallas_call(
        functools.partial(_gdn_kernel, seq=seq, n_head=nh),
        out_shape=jax.ShapeDtypeStruct((t, GDN_WIDTH), F32),
        grid=(bsz, n_grp),
        in_specs=[act_spec, act_spec, act_spec, act_spec, col_spec, col_spec, row_spec,
                  pl.BlockSpec((1, 3, GDN_CONV, wd), lambda b, g: (g, 0, 0, 0)),
                  pl.BlockSpec((1, 2, nh), lambda b, g: (g, 0, 0)),
                  pl.BlockSpec((1, nh, 2), lambda b, g: (g, 0, 0)),
                  _const_spec((1, GDN_HEAD_DIM))],
        out_specs=act_spec,
        scratch_shapes=[pltpu.VMEM((seq, wd), F32), pltpu.VMEM((seq, wd), F32), pltpu.VMEM((seq, wd), F32),
                        pltpu.VMEM((nh, GDN_HEAD_DIM, GDN_HEAD_DIM), F32)],
        compiler_params=_cparams(2),
        name="gated_deltanet",
    )(q, k, v, z, b_col, a_col, a_row, cw, hp, hr, norm_w.astype(F32).reshape(1, GDN_HEAD_DIM))


def _masked_softmax_rows(s, mask):
    m = jnp.max(jnp.where(mask, s, NEG_INF), axis=-1, keepdims=True)
    e = jnp.where(mask, jnp.exp(s - m), 0.0)
    den = jnp.sum(e, axis=-1, keepdims=True)
    return e / jnp.where(den > 0.0, den, 1.0)


def _nsa_kernel(q_ref, kca_ref, vca_ref, ks_ref, vs_ref, kw_ref, vw_ref, gl_ref, sl_ref, pek_ref, pev_ref,
                kw1_ref, kb1_ref, kw2_ref, vw1_ref, vb1_ref, vw2_ref, ov_ref, ex_ref, o_ref,
                kcmp, vcmp, msk, *, seq):
    qb, dh, r_n = NSA_Q_BLOCK, NSA_HEAD_DIM, NSA_GROUP_SIZE
    rows = r_n * qb
    n_cmp_pad = seq // CMP_STRIDE
    n_cmp = (seq - CMP_LEN) // CMP_STRIDE + 1
    half = CMP_STRIDE * dh
    kc_n = NSA_KEY_CHUNK

    def compress(a_ref, pe_ref, w1_ref, b1_ref, w2_ref):
        a = a_ref[0, 0].astype(BF16)
        top = _dot(a, w1_ref[:half, :])
        bot = pltpu.roll(_dot(a, w1_ref[half:, :]), n_cmp_pad - 1, 0)
        const = _dot(pe_ref[...].astype(BF16), w1_ref[...]) + b1_ref[...]
        hid = _gelu(top + bot + const)
        return _dot(hid.astype(BF16), w2_ref[...])

    kcmp[...] = compress(kca_ref, pek_ref, kw1_ref, kb1_ref, kw2_ref).astype(BF16)
    vcmp[...] = compress(vca_ref, pev_ref, vw1_ref, vb1_ref, vw2_ref).astype(BF16)

    slope = sl_ref[0]
    tl = lax.broadcasted_iota(jnp.int32, (rows, 1), 0) % qb
    jc = lax.broadcasted_iota(jnp.int32, (1, n_cmp_pad), 1)
    cmp_end = jc * CMP_STRIDE + (CMP_LEN - 1)
    cmp_center = jc.astype(F32) * CMP_STRIDE + (CMP_LEN - 1) * 0.5
    n_slc_pad = ov_ref.shape[1]
    blk = lax.broadcasted_iota(jnp.int32, (qb, n_slc_pad), 1)
    lane_c = lax.broadcasted_iota(jnp.int32, (1, kc_n), 1)
    n_win = WINDOW + qb
    lane_w = lax.broadcasted_iota(jnp.int32, (1, n_win), 1)

    def qblock(i, carry):
        t0 = pl.multiple_of(i * qb, qb)
        q = jnp.concatenate([q_ref[0, pl.ds(t0, qb), r * dh:(r + 1) * dh] for r in range(r_n)], axis=0)
        q = (q * (dh ** -0.5)).astype(BF16)
        t = t0 + tl
        tf = t.astype(F32)

        s_c = _dot_nt(q, kcmp[...]) - slope * (tf - cmp_center)
        p_c = _masked_softmax_rows(s_c, (cmp_end <= t) & (jc < n_cmp))
        o_c = _dot(p_c.astype(BF16), vcmp[...])

        p_sum = p_c[0:qb]
        for r in range(1, r_n):
            p_sum = p_sum + p_c[r * qb:(r + 1) * qb]
        imp = jnp.dot(p_sum, ov_ref[...], precision=HI, preferred_element_type=F32)
        forced = (blk == 0) | (blk == i)
        cand = (blk < i) & (blk > 0)
        sc = jnp.where(cand, imp, NEG_INF)
        sel = forced
        for _ in range(N_SELECT - 2):
            best = jnp.max(sc, axis=-1, keepdims=True)
            first = jnp.min(jnp.where(sc == best, blk, n_slc_pad), axis=-1, keepdims=True)
            pick = (blk == first) & cand
            sel = sel | pick
            sc = jnp.where(pick, NEG_INF, sc)
        key_mask = _dot(jnp.where(sel, 1.0, 0.0).astype(BF16), ex_ref[...])
        for cc in range(seq // kc_n):
            msk[cc] = key_mask[:, cc * kc_n:(cc + 1) * kc_n]

        def key_chunk(cc, st):
            m_p, l_p, a_p = st
            k0 = pl.multiple_of(cc * kc_n, kc_n)
            kk = ks_ref[0, 0, pl.ds(k0, kc_n), :].astype(BF16)
            vv = vs_ref[0, 0, pl.ds(k0, kc_n), :].astype(BF16)
            pos = k0 + lane_c
            s = _dot_nt(q, kk) - slope * (tf - pos.astype(F32))
            mk1 = msk[cc] > 0.5
            mk = jnp.concatenate([mk1] * r_n, axis=0) & (pos <= t)
            m_n = jnp.maximum(m_p, jnp.max(jnp.where(mk, s, NEG_INF), axis=-1, keepdims=True))
            e = jnp.where(mk, jnp.exp(s - m_n), 0.0)
            corr = jnp.exp(m_p - m_n)
            l_n = corr * l_p + jnp.sum(e, axis=-1, keepdims=True)
            a_n = corr * a_p + _dot(e.astype(BF16), vv)
            return m_n, l_n, a_n

        init = (jnp.full((rows, 1), NEG_INF, F32), jnp.zeros((rows, 1), F32), jnp.zeros((rows, dh), F32))
        _, l_s, a_s = lax.fori_loop(0, i // (kc_n // SLC_BLOCK) + 1, key_chunk, init)
        o_s = a_s / l_s

        w0 = pl.multiple_of(jnp.maximum(t0 - WINDOW, 0), qb)
        kpos = w0 + lane_w
        s_w = _dot_nt(q, kw_ref[0, 0, pl.ds(w0, n_win), :].astype(BF16)) - slope * (tf - kpos.astype(F32))
        p_w = _masked_softmax_rows(s_w, (kpos <= t) & (kpos > t - WINDOW))
        o_w = _dot(p_w.astype(BF16), vw_ref[0, 0, pl.ds(w0, n_win), :].astype(BF16))

        gate = _sigmoid(gl_ref[0, 0, pl.ds(t0, qb), :])
        for r in range(r_n):
            rs = slice(r * qb, (r + 1) * qb)
            out = (gate[:, 3 * r:3 * r + 1] * o_c[rs] + gate[:, 3 * r + 1:3 * r + 2] * o_s[rs]
                   + gate[:, 3 * r + 2:3 * r + 3] * o_w[rs])
            o_ref[0, pl.ds(t0, qb), r * dh:(r + 1) * dh] = out
        return carry

    lax.fori_loop(0, seq // qb, qblock, 0)


def _nsa(q, kc, vc, ks, vs, kw, vw, gl, bsz, seq, pe_k, pe_v, k_w1, k_b1, k_w2, v_w1, v_b1, v_w2):
    g_n, r_n, dh, qb = NSA_KV_GROUPS, NSA_GROUP_SIZE, NSA_HEAD_DIM, NSA_Q_BLOCK
    n_cmp_pad = seq // CMP_STRIDE
    n_slc = seq // SLC_BLOCK
    n_slc_pad = LANES

    def per_group(x):
        return x.reshape(bsz, seq, g_n, dh).transpose(0, 2, 1, 3)

    def cmp_rows(x):
        return (x.reshape(bsz, n_cmp_pad, CMP_STRIDE, g_n, dh).transpose(0, 3, 1, 2, 4)
                .reshape(bsz, g_n, n_cmp_pad, CMP_STRIDE * dh))

    q3 = q.reshape(bsz, seq, NSA_HEADS * dh)
    gates = gl[:, :3 * NSA_HEADS].reshape(bsz, seq, g_n, 3 * r_n).transpose(0, 2, 1, 3)
    slopes = np.asarray([2.0 ** (-8.0 * (h + 1) / NSA_HEADS) for h in range(NSA_HEADS)], dtype=np.float32)
    slope_rows = jnp.asarray(np.repeat(slopes.reshape(g_n, r_n), qb, axis=1).reshape(g_n, r_n * qb, 1))
    tok = np.arange(n_cmp_pad)[:, None] * CMP_STRIDE + np.arange(CMP_LEN)[None, :]
    overlap = ((tok // SLC_BLOCK)[:, :, None] == np.arange(n_slc_pad)[None, None, :]).mean(axis=1).astype(np.float32)
    expand = (np.arange(n_slc_pad)[:, None] == (np.arange(seq) // SLC_BLOCK)[None, :]).astype(np.float32)
    assert n_slc <= n_slc_pad
    kv_spec = pl.BlockSpec((1, 1, seq, dh), lambda b, g: (b, g, 0, 0))
    cmp_spec = pl.BlockSpec((1, 1, n_cmp_pad, CMP_STRIDE * dh), lambda b, g: (b, g, 0, 0))
    q_spec = pl.BlockSpec((1, seq, r_n * dh), lambda b, g: (b, 0, g))
    w1s, b1s, w2s = (CMP_LEN * dh, CMP_HIDDEN), (1, CMP_HIDDEN), (CMP_HIDDEN, dh)
    out = pl.pallas_call(
        functools.partial(_nsa_kernel, seq=seq),
        out_shape=jax.ShapeDtypeStruct((bsz, seq, NSA_HEADS * dh), F32),
        grid=(bsz, g_n),
        in_specs=[q_spec, cmp_spec, cmp_spec, kv_spec, kv_spec, kv_spec, kv_spec,
                  pl.BlockSpec((1, 1, seq, 3 * r_n), lambda b, g: (b, g, 0, 0)),
                  pl.BlockSpec((1, r_n * qb, 1), lambda b, g: (g, 0, 0)),
                  _const_spec((1, CMP_LEN * dh)), _const_spec((1, CMP_LEN * dh)),
                  _const_spec(w1s), _const_spec(b1s), _const_spec(w2s),
                  _const_spec(w1s), _const_spec(b1s), _const_spec(w2s),
                  _const_spec((n_cmp_pad, n_slc_pad)), _const_spec((n_slc_pad, seq))],
        out_specs=q_spec,
        scratch_shapes=[pltpu.VMEM((n_cmp_pad, dh), BF16), pltpu.VMEM((n_cmp_pad, dh), BF16),
                        pltpu.VMEM((seq // NSA_KEY_CHUNK, qb, NSA_KEY_CHUNK), F32)],
        compiler_params=_cparams(2),
        name="nsa_attention",
    )(q3, cmp_rows(kc), cmp_rows(vc), per_group(ks), per_group(vs), per_group(kw), per_group(vw), gates,
      slope_rows, pe_k.astype(F32).reshape(1, -1), pe_v.astype(F32).reshape(1, -1),
      k_w1.astype(BF16), k_b1.astype(F32).reshape(1, -1), k_w2.astype(BF16),
      v_w1.astype(BF16), v_b1.astype(F32).reshape(1, -1), v_w2.astype(BF16),
      jnp.asarray(overlap), jnp.asarray(expand, dtype=BF16))
    return out.reshape(bsz * seq, NSA_HEADS * dh)


def _pad_cols(w, n):
    return jnp.pad(w, ((0, 0), (0, n - w.shape[1])))


def kernel(x, ab_w_in, ab_w_out, s5_lambda_re, s5_lambda_im, s5_log_step, s5_b_re, s5_b_im, s5_c_re, s5_c_im, s5_d, s5_w_glu, s5_b_glu, gdn_conv_w, gdn_a_log, gdn_dt_bias, gdn_norm_w, nsa_w_in, nsa_w_out, nsa_pe_k, nsa_pe_v, nsa_k_w1, nsa_k_b1, nsa_k_w2, nsa_v_w1, nsa_v_b1, nsa_v_w2, ffn_w_in, ffn_conv_w, ffn_conv_b, ffn_w_out, norm_mix, norm_ffn, norm_final):
    bsz, seq, d = x.shape
    depth = ffn_w_in.shape[0]
    h = x.astype(F32).reshape(bsz * seq, d)
    for layer in range(depth):
        i = layer // 2
        if layer % 2 == 0:
            w = ab_w_in[i].astype(BF16)
            cuts = np.cumsum([0, S5_WIDTH, GDN_WIDTH, GDN_WIDTH, GDN_WIDTH, GDN_WIDTH])
            ws = [w[:, cuts[j]:cuts[j + 1]] for j in range(5)] + [_pad_cols(w[:, cuts[5]:], LANES)]
            u, q, k, v, z, ba = _norm_proj(h, norm_mix[layer], ws)
            prep = _s5_prep(s5_lambda_re[i], s5_lambda_im[i], s5_log_step[i], s5_b_re[i], s5_b_im[i],
                            s5_c_re[i], s5_c_im[i], s5_d[i])
            y_a = _s5(u, seq, prep, s5_w_glu[i], s5_b_glu[i])
            y_b = _gdn(q, k, v, z, ba, seq, gdn_conv_w[i], gdn_a_log[i], gdn_dt_bias[i], gdn_norm_w[i])
            wo = ab_w_out[i].astype(BF16)
            h = _out_proj(h, [y_a, y_b], [wo[:S5_WIDTH], wo[S5_WIDTH:]])
        else:
            w = nsa_w_in[i].astype(BF16)
            nq = NSA_HEADS * NSA_HEAD_DIM
            cuts = [0, nq] + [nq + NSA_KV_WIDTH * (j + 1) for j in range(6)]
            ws = [w[:, cuts[j]:cuts[j + 1]] for j in range(7)] + [_pad_cols(w[:, cuts[7]:], LANES)]
            q, kc, vc, ks, vs, kw, vw, gl = _norm_proj(h, norm_mix[layer], ws)
            o = _nsa(q, kc, vc, ks, vs, kw, vw, gl, bsz, seq, nsa_pe_k[i], nsa_pe_v[i], nsa_k_w1[i],
                     nsa_k_b1[i], nsa_k_w2[i], nsa_v_w1[i], nsa_v_b1[i], nsa_v_w2[i])
            h = _out_proj(h, [o], [nsa_w_out[i].astype(BF16)])
        h = _ffn(h, seq, norm_ffn[layer], ffn_w_in[layer], ffn_conv_w[layer], ffn_conv_b[layer],
                 ffn_w_out[layer], norm_final, final=(layer == depth - 1))
    return h.reshape(bsz, seq, d).astype(x.dtype)
```

```python
import functools
import math

import numpy as np
import jax
import jax.numpy as jnp
from jax import lax
from jax.experimental import pallas as pl
from jax.experimental.pallas import tpu as pltpu

F32 = jnp.float32
BF16 = jnp.bfloat16
HI = lax.Precision.HIGHEST

D_MODEL = 1024
S5_WIDTH = 256
S5_GROUP = 16
S5_GROUPS = 16
S5_STATE = 64
S5_CHUNK = 16
GDN_HEAD_DIM = 128
GDN_HEADS = 6
GDN_WIDTH = 768
GDN_CONV = 4
GDN_CHUNK = 64
GDN_HEADS_PER_STEP = 2
GDN_CHUNKS_PER_ITER = 4
NSA_HEADS = 16
NSA_HEAD_DIM = 64
NSA_KV_GROUPS = 4
NSA_GROUP_SIZE = 4
NSA_KV_WIDTH = 256
CMP_LEN = 32
CMP_STRIDE = 16
CMP_HIDDEN = 256
SLC_BLOCK = 64
N_SELECT = 4
WINDOW = 256
NSA_Q_BLOCK = 64
NSA_SLAB_KEYS = 512
NSA_QB_PER_ITER = 2
LOG2E = math.log2(math.e)
FFN_HIDDEN = 2816
FFN_CONV = 3
FFN_COL_CHUNK = 256
RMS_EPS = 1e-6
NEG_INF = -1e30
LANES = 128
BF16_SUBLANES = 16
VMEM_LIMIT = 56 * 1024 * 1024
ROW_TILE = 512


def _cparams(n_axes):
    return pltpu.CompilerParams(dimension_semantics=("arbitrary",) * n_axes,
                                vmem_limit_bytes=VMEM_LIMIT)


def _rms(x, g):
    return x * lax.rsqrt(jnp.mean(x * x, axis=-1, keepdims=True) + RMS_EPS) * g


def _gelu(x):
    return 0.5 * x * (1.0 + jnp.tanh(math.sqrt(2.0 / math.pi) * (x + 0.044715 * (x * x * x))))


def _sigmoid(x):
    return 1.0 / (1.0 + jnp.exp(-x))


def _silu(x):
    return x * _sigmoid(x)


def _dot(a, b):
    return jnp.dot(a, b, preferred_element_type=F32)


def _dot_nt(a, b):
    return lax.dot_general(a, b, (((1,), (1,)), ((), ())), preferred_element_type=F32)


def _dot_tn(a, b):
    return lax.dot_general(a, b, (((0,), (0,)), ((), ())), preferred_element_type=F32)


def _const_spec(shape):
    nd = len(shape)
    return pl.BlockSpec(shape, lambda *_: (0,) * nd)


def _norm_proj_kernel(x_ref, g_ref, *refs, n_out):
    xn = _rms(x_ref[...], g_ref[...]).astype(BF16)
    for w_ref, o_ref in zip(refs[:n_out], refs[n_out:]):
        o_ref[...] = _dot(xn, w_ref[...]).astype(o_ref.dtype)


def _norm_proj(h, gain, weights):
    t, d = h.shape
    n_out = len(weights)
    return pl.pallas_call(
        functools.partial(_norm_proj_kernel, n_out=n_out),
        out_shape=[jax.ShapeDtypeStruct((t, w.shape[1]), F32) for w in weights],
        grid=(t // ROW_TILE,),
        in_specs=[pl.BlockSpec((ROW_TILE, d), lambda i: (i, 0)), _const_spec((1, d))]
        + [_const_spec(w.shape) for w in weights],
        out_specs=[pl.BlockSpec((ROW_TILE, w.shape[1]), lambda i: (i, 0)) for w in weights],
        compiler_params=_cparams(1),
        name="norm_proj",
    )(h, gain.reshape(1, d), *weights)


def _out_proj_kernel(h_ref, *refs, n_in):
    acc = h_ref[...]
    for y_ref, w_ref in zip(refs[:n_in], refs[n_in:2 * n_in]):
        acc = acc + _dot(y_ref[...].astype(BF16), w_ref[...])
    refs[2 * n_in][...] = acc


def _out_proj(h, ys, ws):
    t, d = h.shape
    n_in = len(ys)
    return pl.pallas_call(
        functools.partial(_out_proj_kernel, n_in=n_in),
        out_shape=jax.ShapeDtypeStruct((t, d), F32),
        grid=(t // ROW_TILE,),
        in_specs=[pl.BlockSpec((ROW_TILE, d), lambda i: (i, 0))]
        + [pl.BlockSpec((ROW_TILE, y.shape[1]), lambda i: (i, 0)) for y in ys]
        + [_const_spec(w.shape) for w in ws],
        out_specs=pl.BlockSpec((ROW_TILE, d), lambda i: (i, 0)),
        compiler_params=_cparams(1),
        name="out_proj",
    )(h, *ys, *ws)


def _ffn_kernel(h_ref, hp_ref, g_ref, win_ref, cw_ref, cb_ref, wout_ref, gf_ref, o_ref,
                xn_ref, acc_ref, *, tiles_per_seq, n_chunks, final):
    halo = BF16_SUBLANES
    x = h_ref[...]
    g = g_ref[...]
    seq_start = pl.program_id(0) % tiles_per_seq == 0
    xn_ref[0:halo, :] = jnp.where(seq_start, 0.0, _rms(hp_ref[...], g)).astype(BF16)
    xn_ref[halo:, :] = _rms(x, g).astype(BF16)
    acc_ref[...] = jnp.zeros_like(acc_ref)

    def conv(u, cw, cb):
        y = pltpu.roll(u, 2, 0) * cw[0:1] + pltpu.roll(u, 1, 0) * cw[1:2] + u * cw[2:3] + cb
        return y[halo:]

    def body(j, carry):
        xe = xn_ref[...]
        a = conv(_dot(xe, win_ref[j]), cw_ref[j], cb_ref[j])
        b = conv(_dot(xe, win_ref[n_chunks + j]), cw_ref[n_chunks + j], cb_ref[n_chunks + j])
        act = (_silu(a) * b).astype(BF16)
        acc_ref[...] += _dot(act, wout_ref[j])
        return carry

    lax.fori_loop(0, n_chunks, body, 0)
    out = x + acc_ref[...]
    if final:
        out = _rms(out, gf_ref[...])
    o_ref[...] = out


def _ffn(h, seq, gain, w_in, conv_w, conv_b, w_out, gain_final, final):
    t, d = h.shape
    fc = FFN_COL_CHUNK
    n_chunks = FFN_HIDDEN // fc
    halo = BF16_SUBLANES
    win = w_in.astype(BF16).reshape(d, 2 * n_chunks, fc).transpose(1, 0, 2)
    cw = conv_w.astype(F32).reshape(FFN_CONV, 2 * n_chunks, fc).transpose(1, 0, 2)
    cb = conv_b.astype(F32).reshape(2 * n_chunks, 1, fc)
    wout = w_out.astype(BF16).reshape(n_chunks, fc, d)
    blocks_per_tile = ROW_TILE // halo
    return pl.pallas_call(
        functools.partial(_ffn_kernel, tiles_per_seq=seq // ROW_TILE, n_chunks=n_chunks, final=final),
        out_shape=jax.ShapeDtypeStruct((t, d), F32),
        grid=(t // ROW_TILE,),
        in_specs=[pl.BlockSpec((ROW_TILE, d), lambda i: (i, 0)),
                  pl.BlockSpec((halo, d), lambda i: (jnp.maximum(i * blocks_per_tile - 1, 0), 0)),
                  _const_spec((1, d)),
                  pl.BlockSpec(win.shape, lambda i: (0, 0, 0), pipeline_mode=pl.Buffered(1)),
                  _const_spec(cw.shape), _const_spec(cb.shape),
                  pl.BlockSpec(wout.shape, lambda i: (0, 0, 0), pipeline_mode=pl.Buffered(1)),
                  _const_spec((1, d))],
        out_specs=pl.BlockSpec((ROW_TILE, d), lambda i: (i, 0)),
        scratch_shapes=[pltpu.VMEM((ROW_TILE + halo, d), BF16), pltpu.VMEM((ROW_TILE, d), F32)],
        compiler_params=_cparams(1),
        name="conv_ffn",
    )(h, h, gain.reshape(1, d), win, cw, cb, wout, gain_final.reshape(1, d))


def _s5_prep(lam_re, lam_im, log_step, b_re, b_im, c_re, c_im, d_skip):
    g_n, p_n, h_n, L = S5_GROUPS, S5_STATE, S5_GROUP, S5_CHUNK
    step = jnp.exp(log_step.astype(F32))[:, None]
    lr, li = lam_re.astype(F32), lam_im.astype(F32)
    mag = jnp.exp(lr * step)
    a_re = mag * jnp.cos(li * step)
    a_im = mag * jnp.sin(li * step)
    den = lr * lr + li * li
    n_re, n_im = a_re - 1.0, a_im
    z_re = (n_re * lr + n_im * li) / den
    z_im = (n_im * lr - n_re * li) / den
    b_re, b_im = b_re.astype(F32), b_im.astype(F32)
    bb_re = z_re[..., None] * b_re - z_im[..., None] * b_im
    bb_im = z_re[..., None] * b_im + z_im[..., None] * b_re
    c_re, c_im = c_re.astype(F32), c_im.astype(F32)
    pw_re, pw_im = [jnp.ones_like(a_re)], [jnp.zeros_like(a_im)]
    for _ in range(L):
        pr, pi = pw_re[-1], pw_im[-1]
        pw_re.append(pr * a_re - pi * a_im)
        pw_im.append(pr * a_im + pi * a_re)
    eye_g = jnp.eye(g_n, dtype=F32)
    ks = []
    for j in range(L):
        ab_re = pw_re[j][..., None] * bb_re - pw_im[j][..., None] * bb_im
        ab_im = pw_re[j][..., None] * bb_im + pw_im[j][..., None] * bb_re
        kj = (jnp.einsum('gop,gpi->gio', c_re, ab_re, precision=HI)
              - jnp.einsum('gop,gpi->gio', c_im, ab_im, precision=HI))
        ks.append(jnp.einsum('gio,gk->giko', kj, eye_g).reshape(g_n * h_n, g_n * h_n))
    kstack = jnp.concatenate(ks[::-1], axis=0).astype(BF16)
    bb = jnp.stack([jnp.einsum('gph,gk->ghkp', bb_re, eye_g), jnp.einsum('gph,gk->ghkp', bb_im, eye_g)], axis=2)
    bb = bb.reshape(g_n * h_n, 2 * g_n * p_n).astype(BF16)
    cc = jnp.stack([jnp.einsum('ghp,gk->gpkh', c_re, eye_g), -jnp.einsum('ghp,gk->gpkh', c_im, eye_g)], axis=0)
    cc = cc.reshape(2 * g_n * p_n, g_n * h_n).astype(BF16)
    apow = jnp.stack([jnp.concatenate([r.reshape(-1), i.reshape(-1)]) for r, i in zip(pw_re, pw_im)], axis=0)
    dvec = jnp.tile(d_skip.astype(F32).reshape(1, g_n * h_n), (1, L))
    return kstack, bb, cc, apow, dvec


def _s5_kernel(u_ref, ks_ref, bb_ref, cc_ref, ap_ref, dv_ref, wg_ref, bg_ref, o_ref, x_scr, hp_scr,
               *, n_batch, n_chunk):
    L, w = S5_CHUNK, S5_WIDTH
    half = S5_GROUPS * S5_STATE
    rows = n_batch * n_chunk
    u2 = u_ref[...].reshape(rows, L * w)
    u2b = u2.astype(BF16)

    def cmul(j, x):
        ar, ai = ap_ref[j:j + 1, :half], ap_ref[j:j + 1, half:]
        xr, xi = x[:, :half], x[:, half:]
        return jnp.concatenate([ar * xr - ai * xi, ar * xi + ai * xr], axis=1)

    xin = _dot(u2b[:, (L - 1) * w:], bb_ref[...])
    for s in range(L - 1):
        xin = xin + cmul(L - 1 - s, _dot(u2b[:, s * w:(s + 1) * w], bb_ref[...]))
    x_scr[...] = xin

    a_l = ap_ref[L:L + 1, :]
    alr, ali = a_l[:, :half], a_l[:, half:]

    def scan_body(k, hs):
        new = []
        for b in range(n_batch):
            r = b * n_chunk + k
            h = hs[b]
            hp_scr[pl.ds(r, 1), :] = h
            hr, hi = h[:, :half], h[:, half:]
            new.append(jnp.concatenate([alr * hr - ali * hi, alr * hi + ali * hr], axis=1)
                       + x_scr[pl.ds(r, 1), :])
        return tuple(new)

    lax.fori_loop(0, n_chunk, scan_body, tuple(jnp.zeros((1, 2 * half), F32) for _ in range(n_batch)))
    hp = hp_scr[...]

    for t in range(L):
        y = _dot(u2b[:, :(t + 1) * w], ks_ref[(L - 1 - t) * w:, :])
        y = y + _dot(cmul(t + 1, hp).astype(BF16), cc_ref[...])
        y = y + dv_ref[:, t * w:(t + 1) * w] * u2[:, t * w:(t + 1) * w]
        z = _gelu(y)
        gate = _sigmoid(_dot(z.astype(BF16), wg_ref[...]) + bg_ref[...])
        o_ref[:, :, t * w:(t + 1) * w] = (z * gate).reshape(n_batch, n_chunk, w)


def _s5(u, seq, prep, w_glu, b_glu):
    t = u.shape[0]
    bsz = t // seq
    L, w = S5_CHUNK, S5_WIDTH
    n_chunk = seq // L
    n_batch = 2
    kstack, bb, cc, apow, dvec = prep
    u3 = u.reshape(bsz, n_chunk, L * w)
    out = pl.pallas_call(
        functools.partial(_s5_kernel, n_batch=n_batch, n_chunk=n_chunk),
        out_shape=jax.ShapeDtypeStruct((bsz, n_chunk, L * w), F32),
        grid=(bsz // n_batch,),
        in_specs=[pl.BlockSpec((n_batch, n_chunk, L * w), lambda i: (i, 0, 0)),
                  _const_spec(kstack.shape), _const_spec(bb.shape), _const_spec(cc.shape),
                  _const_spec(apow.shape), _const_spec(dvec.shape),
                  _const_spec((w, w)), _const_spec((1, w))],
        out_specs=pl.BlockSpec((n_batch, n_chunk, L * w), lambda i: (i, 0, 0)),
        scratch_shapes=[pltpu.VMEM((n_batch * n_chunk, 2 * S5_GROUPS * S5_STATE), F32),
                        pltpu.VMEM((n_batch * n_chunk, 2 * S5_GROUPS * S5_STATE), F32)],
        compiler_params=_cparams(1),
        name="s5_mixer",
    )(u3, kstack, bb, cc, apow, dvec, w_glu.astype(BF16), b_glu.astype(F32).reshape(1, w))
    return out.reshape(t, w)


def _gdn_kernel(q_ref, k_ref, v_ref, z_ref, bc_ref, ac_ref, ar_ref, cw_ref, hp_ref, hr_ref, nw_ref,
                o_ref, qs, ks, vs, xs, ns, qe, o0, gls, gcc_s, gcr_s, *, seq, n_head):
    c, dh = GDN_CHUNK, GDN_HEAD_DIM
    n_chunk = seq // c
    row8 = lax.broadcasted_iota(jnp.int32, (8, 1), 0)

    def conv_silu(x, cw, first_rows):
        y = x * cw[3:4]
        for sh in (1, 2, 3):
            xr = pltpu.roll(x, sh, 0)
            if first_rows:
                xr = jnp.where(row8 >= sh, xr, 0.0)
            y = y + xr * cw[3 - sh:4 - sh]
        return _silu(y)

    def finish(y, kind):
        if kind == "v":
            return y
        parts = []
        for j in range(n_head):
            p = y[:, j * dh:(j + 1) * dh]
            p = p * lax.rsqrt(jnp.sum(p * p, axis=-1, keepdims=True) + 1e-6)
            parts.append(p * (dh ** -0.5) if kind == "q" else p)
        return jnp.concatenate(parts, axis=1)

    for idx, (src, dst, kind) in enumerate(((q_ref, qs, "q"), (k_ref, ks, "k"), (v_ref, vs, "v"))):
        dst[...] = finish(conv_silu(src[...], cw_ref[0, idx], False), kind)
        dst[0:8, :] = finish(conv_silu(src[0:8, :], cw_ref[0, idx], True), kind)

    hp = hp_ref[0]
    hr = hr_ref[0]

    def softplus(x):
        return jnp.maximum(x, 0.0) + jnp.log(1.0 + jnp.exp(-jnp.abs(x)))

    ii = lax.broadcasted_iota(jnp.int32, (c, c), 0)
    jj = lax.broadcasted_iota(jnp.int32, (c, c), 1)
    eye = (ii == jj).astype(F32)
    g_col = -jnp.exp(hp[0:1, :]) * softplus(ac_ref[0, 0] + hp[1:2, :])
    g_row = -jnp.exp(hr[:, 0:1]) * softplus(ar_ref[0, 0] + hr[:, 1:2])
    gc_col = jnp.dot((ii >= jj).astype(F32), g_col, precision=HI, preferred_element_type=F32)
    gcr_s[...] = jnp.dot(g_row, (ii <= jj).astype(F32), precision=HI, preferred_element_type=F32)
    for ci in range(n_chunk):
        gcc_s[ci * c:(ci + 1) * c, :] = gc_col[:, ci * n_head:(ci + 1) * n_head]

    def local_body(it, carry):
        pr = []
        for cc in range(GDN_CHUNKS_PER_ITER):
            ci = it * GDN_CHUNKS_PER_ITER + cc
            rs = pl.ds(pl.multiple_of(ci * c, c), c)
            beta = _sigmoid(bc_ref[0, 0, rs, :])
            gccs = gcc_s[rs, :]
            for j in range(n_head):
                sl = slice(j * dh, (j + 1) * dh)
                gcc = gccs[:, j:j + 1]
                gcr = gcr_s[pl.ds(ci * n_head + j, 1), :]
                q, k, v = qs[rs, sl], ks[rs, sl], vs[rs, sl]
                bj = beta[:, j:j + 1]
                kb = k * bj
                e_g = jnp.exp(gcc)
                g_last = gcc[c - 1:c, :]
                pr.append(dict(ci=ci, j=j, q=q, kf=k.astype(BF16), kbb=kb.astype(BF16),
                               decay=jnp.exp(jnp.where(ii >= jj, gcc - gcr, NEG_INF)),
                               rhs=jnp.concatenate([v * bj, kb * e_g], axis=1).astype(BF16),
                               qd=q * e_g, kd=(k * jnp.exp(g_last - gcc)).astype(BF16),
                               gl=jnp.broadcast_to(jnp.exp(g_last), (8, dh))))
        kk = [_dot_nt(p["kbb"], p["kf"]) for p in pr]
        qk = [_dot_nt(p["q"].astype(BF16), p["kf"]) for p in pr]
        ps = [-jnp.where(ii > jj, a * p["decay"], 0.0) for a, p in zip(kk, pr)]
        ts = [eye + n for n in ps]
        span = 2
        while span < c:
            ps = [_dot(n.astype(BF16), n.astype(BF16)) for n in ps]
            ts = [t + _dot(t.astype(BF16), n.astype(BF16)) for t, n in zip(ts, ps)]
            span *= 2
        uws = [_dot(t.astype(BF16), p["rhs"]).astype(BF16) for t, p in zip(ts, pr)]
        intras = [(s * p["decay"]).astype(BF16) for s, p in zip(qk, pr)]
        i_uws = [_dot(a, uw) for a, uw in zip(intras, uws)]
        kd_uws = [_dot_tn(p["kd"], uw) for p, uw in zip(pr, uws)]
        for p, i_uw, kd_uw in zip(pr, i_uws, kd_uws):
            j, ci = p["j"], p["ci"]
            ns[j, ci] = kd_uw[:, :dh]
            xs[j, ci] = kd_uw[:, dh:].astype(BF16)
            o0[j, ci] = i_uw[:, :dh]
            qe[j, ci] = (p["qd"] - i_uw[:, dh:]).astype(BF16)
            gls[j, ci] = p["gl"]
        return carry

    lax.fori_loop(0, n_chunk // GDN_CHUNKS_PER_ITER, local_body, 0)

    def rec_body(ci, states):
        r0 = pl.multiple_of(ci * c, c)
        rs = pl.ds(r0, c)
        new = []
        for j in range(n_head):
            sl = slice(j * dh, (j + 1) * dh)
            s_prev = states[j]
            s_b = s_prev.astype(BF16)
            o = _dot(qe[j, ci], s_b) + o0[j, ci]
            on = o * lax.rsqrt(jnp.mean(o * o, axis=-1, keepdims=True) + RMS_EPS) * nw_ref[...]
            o_ref[rs, sl] = on * _silu(z_ref[rs, sl])
            new.append(gls[j, ci][0:1, :] * s_prev - _dot(xs[j, ci], s_b) + ns[j, ci])
        return tuple(new)

    lax.fori_loop(0, n_chunk, rec_body, tuple(jnp.zeros((dh, dh), F32) for _ in range(n_head)))


def _gdn(q, k, v, z, ba, seq, conv_w, a_log, dt_bias, norm_w):
    t = q.shape[0]
    bsz = t // seq
    nh = GDN_HEADS_PER_STEP
    n_grp = GDN_HEADS // nh
    wd = nh * GDN_HEAD_DIM
    bl = ba[:, :GDN_HEADS].reshape(bsz, seq, n_grp, nh)
    al = ba[:, GDN_HEADS:2 * GDN_HEADS].reshape(bsz, seq, n_grp, nh)
    b_col = bl.transpose(0, 2, 1, 3)
    n_chunk = seq // GDN_CHUNK
    n_prob = n_chunk * nh
    a5 = al.reshape(bsz, n_chunk, GDN_CHUNK, n_grp, nh)
    a_col = a5.transpose(0, 3, 2, 1, 4).reshape(bsz, n_grp, GDN_CHUNK, n_prob)
    a_row = a5.transpose(0, 3, 1, 4, 2).reshape(bsz, n_grp, n_prob, GDN_CHUNK)
    cw = conv_w.astype(F32).reshape(GDN_CONV, 3, n_grp, wd).transpose(2, 1, 0, 3)
    hp = jnp.stack([jnp.tile(a_log.astype(F32).reshape(n_grp, nh), (1, n_chunk)),
                    jnp.tile(dt_bias.astype(F32).reshape(n_grp, nh), (1, n_chunk))], axis=1)
    hr = hp.transpose(0, 2, 1)
    act_spec = pl.BlockSpec((seq, wd), lambda b, g: (b, g))
    col_spec = pl.BlockSpec((1, 1, seq, nh), lambda b, g: (b, g, 0, 0))
    return pl.pallas_call(
        functools.partial(_gdn_kernel, seq=seq, n_head=nh),
        out_shape=jax.ShapeDtypeStruct((t, GDN_WIDTH), F32),
        grid=(bsz, n_grp),
        in_specs=[act_spec, act_spec, act_spec, act_spec, col_spec,
                  pl.BlockSpec((1, 1, GDN_CHUNK, n_prob), lambda b, g: (b, g, 0, 0)),
                  pl.BlockSpec((1, 1, n_prob, GDN_CHUNK), lambda b, g: (b, g, 0, 0)),
                  pl.BlockSpec((1, 3, GDN_CONV, wd), lambda b, g: (g, 0, 0, 0)),
                  pl.BlockSpec((1, 2, n_prob), lambda b, g: (g, 0, 0)),
                  pl.BlockSpec((1, n_prob, 2), lambda b, g: (g, 0, 0)),
                  _const_spec((1, GDN_HEAD_DIM))],
        out_specs=act_spec,
        scratch_shapes=[pltpu.VMEM((seq, wd), F32), pltpu.VMEM((seq, wd), F32), pltpu.VMEM((seq, wd), F32),
                        pltpu.VMEM((nh, n_chunk, GDN_HEAD_DIM, GDN_HEAD_DIM), BF16),
                        pltpu.VMEM((nh, n_chunk, GDN_HEAD_DIM, GDN_HEAD_DIM), F32),
                        pltpu.VMEM((nh, n_chunk, GDN_CHUNK, GDN_HEAD_DIM), BF16),
                        pltpu.VMEM((nh, n_chunk, GDN_CHUNK, GDN_HEAD_DIM), F32),
                        pltpu.VMEM((nh, n_chunk, 8, GDN_HEAD_DIM), F32),
                        pltpu.VMEM((seq, nh), F32), pltpu.VMEM((n_prob, GDN_CHUNK), F32)],
        compiler_params=_cparams(2),
        name="gated_deltanet",
    )(q, k, v, z, b_col, a_col, a_row, cw, hp, hr, norm_w.astype(F32).reshape(1, GDN_HEAD_DIM))


NSA_FEAT = 64
F_BLK, F_POS, F_ONE = 0, 32, 38


def _split3(x):
    x1 = x.astype(BF16).astype(F32)
    x2 = (x - x1).astype(BF16).astype(F32)
    return x1, x2, x - x1 - x2


def _nsa_proj_kernel(x_ref, g_ref, wq_ref, wks_ref, wkw_ref, wvs_ref, wvw_ref, wkc_ref, wvc_ref, wg_ref, kf_ref,
                     q_ref, ksa_ref, kwa_ref, vst_ref, vwt_ref, kc_ref, vc_ref, gl_ref):
    xn = _rms(x_ref[...], g_ref[...]).astype(BF16)
    q_ref[...] = (_dot(xn, wq_ref[...]) * (NSA_HEAD_DIM ** -0.5 * LOG2E)).astype(BF16)
    kf = jnp.concatenate([kf_ref[...]] * NSA_KV_GROUPS, axis=1)
    ksa_ref[...] = (_dot(xn, wks_ref[...]) + kf).astype(BF16)
    kwa_ref[...] = (_dot(xn, wkw_ref[...]) + kf).astype(BF16)
    vst_ref[0] = _dot_nt(wvs_ref[...], xn).astype(BF16)
    vwt_ref[0] = _dot_nt(wvw_ref[...], xn).astype(BF16)
    kc_ref[...] = _dot(xn, wkc_ref[...])
    vc_ref[...] = _dot(xn, wvc_ref[...])
    gl_ref[...] = _dot(xn, wg_ref[...])


def _nsa_proj(h, gain, w_in, bsz, seq):
    t, d = h.shape
    g_n, dh = NSA_KV_GROUPS, NSA_HEAD_DIM
    nq, kvw = NSA_HEADS * dh, NSA_KV_WIDTH
    w = w_in.astype(BF16)
    wq = w[:, :nq]
    wkc, wvc, wks, wvs, wkw, wvw = (w[:, nq + j * kvw: nq + (j + 1) * kvw] for j in range(6))
    wgl = w[:, nq + 6 * kvw:]

    def lane_padded(x, width):
        return jnp.pad(x.reshape(d, g_n, width), ((0, 0), (0, 0), (0, LANES - width))).reshape(d, g_n * LANES)

    pos = np.arange(seq)
    kf = np.zeros((seq, LANES), np.float32)
    kf[pos, dh + F_BLK + pos // SLC_BLOCK] = 1.0
    kf[:, dh + F_POS:dh + F_POS + 3] = ((pos // SLC_BLOCK) * SLC_BLOCK)[:, None]
    kf[:, dh + F_POS + 3:dh + F_POS + 6] = (pos % SLC_BLOCK)[:, None]
    kf[:, dh + F_ONE:dh + F_ONE + 3] = 1.0
    tps = seq // ROW_TILE
    row_spec = lambda n: pl.BlockSpec((ROW_TILE, n), lambda i: (i, 0))
    tr_spec = pl.BlockSpec((1, kvw, ROW_TILE), lambda i: (i // tps, 0, i % tps))
    ws = [wq, lane_padded(wks, dh), lane_padded(wkw, dh), wvs.T, wvw.T, wkc, wvc, lane_padded(wgl, 3 * NSA_GROUP_SIZE)]
    return pl.pallas_call(
        _nsa_proj_kernel,
        out_shape=[jax.ShapeDtypeStruct((t, nq), BF16),
                   jax.ShapeDtypeStruct((t, g_n * LANES), BF16), jax.ShapeDtypeStruct((t, g_n * LANES), BF16),
                   jax.ShapeDtypeStruct((bsz, kvw, seq), BF16), jax.ShapeDtypeStruct((bsz, kvw, seq), BF16),
                   jax.ShapeDtypeStruct((t, kvw), F32), jax.ShapeDtypeStruct((t, kvw), F32),
                   jax.ShapeDtypeStruct((t, g_n * LANES), F32)],
        grid=(t // ROW_TILE,),
        in_specs=[row_spec(d), _const_spec((1, d))] + [_const_spec(x.shape) for x in ws]
        + [pl.BlockSpec((ROW_TILE, LANES), lambda i: (i % tps, 0))],
        out_specs=[row_spec(nq), row_spec(g_n * LANES), row_spec(g_n * LANES), tr_spec, tr_spec,
                   row_spec(kvw), row_spec(kvw), row_spec(g_n * LANES)],
        compiler_params=_cparams(1),
        name="nsa_proj",
    )(h, gain.reshape(1, d), *ws, jnp.asarray(kf))


def _nsa_kernel(q_ref, ksa_ref, kwa_ref, vst_ref, vwt_ref, kca_ref, vca_ref, gl_ref, sl_ref, sf_ref, cf_ref,
                pek_ref, pev_ref, kw1_ref, kb1_ref, kw2_ref, vw1_ref, vb1_ref, vw2t_ref, ovt_ref, o_ref,
                kcmp, vcmpt, vs3, vw3, *, seq):
    qb, dh, r_n, nq = NSA_Q_BLOCK, NSA_HEAD_DIM, NSA_GROUP_SIZE, NSA_QB_PER_ITER
    span = nq * qb
    cols = nq * r_n * qb
    n_cmp_pad = seq // CMP_STRIDE
    n_cmp = (seq - CMP_LEN) // CMP_STRIDE + 1
    half = CMP_STRIDE * dh
    n_blk = seq // SLC_BLOCK
    n_win = WINDOW + span
    it_per_slab = NSA_SLAB_KEYS // span
    nf = NSA_FEAT

    def hidden(a_ref, pe_ref, w1_ref, b1_ref):
        a = a_ref[0, 0].astype(BF16)
        top = _dot(a, w1_ref[:half, :])
        bot = pltpu.roll(_dot(a, w1_ref[half:, :]), n_cmp_pad - 1, 0)
        const = _dot(pe_ref[...].astype(BF16), w1_ref[...]) + b1_ref[...]
        return _gelu(top + bot + const).astype(BF16)

    kcmp[...] = jnp.concatenate([_dot(hidden(kca_ref, pek_ref, kw1_ref, kb1_ref), kw2_ref[...]), cf_ref[...]],
                                axis=1).astype(BF16)
    vcmpt[...] = _dot_nt(vw2t_ref[...], hidden(vca_ref, pev_ref, vw1_ref, vb1_ref)).astype(BF16)
    for n in range(seq // span):
        vs3[n] = vst_ref[0, :, n * span:(n + 1) * span]
        vw3[n] = vwt_ref[0, :, n * span:(n + 1) * span]

    def iota(shape, axis):
        return lax.broadcasted_iota(jnp.int32, shape, axis)

    slope = sl_ref[0]
    colc, colr = iota((cols, 1), 0), iota((1, cols), 1)
    tc_col = (colc // (r_n * qb)) * qb + colc % qb
    tc_row = (colr // (r_n * qb)) * qb + colr % qb
    qi_row = colr // (r_n * qb)
    lane_f = iota((1, nf), 1)
    jr = iota((n_cmp_pad, 1), 0)
    nr = iota((nf, 1), 0)
    lane_s = iota((1, span), 1)
    qi_s = lane_s // qb
    key_d = iota((span, 1), 0)
    key_w = iota((n_win, 1), 0)
    eye_s = (iota((span, span), 0) == iota((span, span), 1)).astype(BF16)

    def softmax_cols(s):
        m = jnp.max(s, axis=0, keepdims=True)
        e = jnp.exp2(s - m)
        return e, jnp.sum(e, axis=0, keepdims=True)

    def make_body(n_keys):
        def body(p, carry):
            t0 = pl.multiple_of(p * span, span)
            qf = jnp.concatenate([q_ref[pl.ds(t0 + qi * qb, qb), r * dh:(r + 1) * dh]
                                  for qi in range(nq) for r in range(r_n)], axis=0).astype(F32)
            c1, c2, c3 = _split3(-(slope * (t0 + tc_col).astype(F32)))
            f_plain = jnp.where(lane_f == F_ONE, c1, jnp.where(lane_f == F_ONE + 1, c2,
                                jnp.where(lane_f == F_ONE + 2, c3, sf_ref[0])))
            qa_plain = jnp.concatenate([qf, f_plain], axis=1).astype(BF16)

            ok_c = (jr * CMP_STRIDE + (CMP_LEN - 1) <= t0 + tc_row) & (jr < n_cmp)
            s_c = jnp.where(ok_c, _dot_nt(kcmp[...], qa_plain), NEG_INF)
            e_c = jnp.where(ok_c, jnp.exp2(s_c - jnp.max(s_c, axis=0, keepdims=True)), 0.0)
            l_c = jnp.sum(e_c, axis=0, keepdims=True)
            p_c = e_c / jnp.where(l_c > 0.0, l_c, 1.0)
            oc_t = _dot(vcmpt[...], p_c.astype(BF16))

            imp4 = jnp.dot(ovt_ref[...], p_c, precision=HI, preferred_element_type=F32)
            halves = []
            for qi in range(nq):
                a = imp4[:, (2 * qi) * LANES:(2 * qi + 1) * LANES] + imp4[:, (2 * qi + 1) * LANES:(2 * qi + 2) * LANES]
                halves.append(a + pltpu.roll(a, qb, 1))
            imp = jnp.where(qi_s == 0, halves[0], halves[1])

            iq = nq * p + qi_s
            cand = (nr > 0) & (nr < iq)
            sc = jnp.where(cand, imp, NEG_INF)
            sel = (nr == 0) & (iq > 0)
            for _ in range(N_SELECT - 2):
                best = jnp.max(sc, axis=0, keepdims=True)
                first = jnp.min(jnp.where(sc == best, nr, nf), axis=0, keepdims=True)
                pick = (nr == first) & cand
                sel = sel | pick
                sc = jnp.where(pick, NEG_INF, sc)
            neg_slab = jnp.where(sel & (nr < nq * p), 0.0, NEG_INF)
            neg_cur = jnp.where(nr == iq, 0.0, jnp.where((nr >= nq * p) & (nr < iq) & sel, 0.0, NEG_INF))
            neg_t = jnp.concatenate([neg_slab, neg_cur], axis=0).astype(BF16)
            neg = _dot_nt(eye_s, neg_t)
            rows_of = lambda x: jnp.concatenate([x[qi * qb:(qi + 1) * qb] for qi in range(nq) for _ in range(r_n)], axis=0)
            f_slab = jnp.where(lane_f < n_blk, rows_of(neg[:, :nf]), f_plain)
            f_cur = jnp.where(lane_f < n_blk, rows_of(neg[:, nf:]), f_plain)
            qa_slab = jnp.concatenate([qf, f_slab], axis=1).astype(BF16)
            qa_cur = jnp.concatenate([qf, f_cur], axis=1).astype(BF16)

            s_sl = _dot_nt(ksa_ref[0:n_keys, :], qa_slab)
            own = (key_d // qb) == qi_row
            s_cu = jnp.where(own & (key_d % qb > tc_row % qb), NEG_INF, _dot_nt(ksa_ref[pl.ds(t0, span), :], qa_cur))
            m_s = jnp.maximum(jnp.max(s_sl, axis=0, keepdims=True), jnp.max(s_cu, axis=0, keepdims=True))
            e_sl = jnp.exp2(s_sl - m_s)
            e_cu = jnp.exp2(s_cu - m_s)
            l_s = jnp.sum(e_sl, axis=0, keepdims=True) + jnp.sum(e_cu, axis=0, keepdims=True)
            os_t = (_dot(vst_ref[0, :, 0:n_keys], e_sl.astype(BF16)) + _dot(vs3[p], e_cu.astype(BF16))) / l_s

            w0 = pl.multiple_of(jnp.maximum(t0 - WINDOW, 0), span)
            rel = (t0 - w0) + tc_row - key_w
            s_w = jnp.where((rel >= 0) & (rel < WINDOW), _dot_nt(kwa_ref[pl.ds(w0, n_win), :], qa_plain), NEG_INF)
            e_w, l_w = softmax_cols(s_w)
            e_wb = e_w.astype(BF16)
            b0 = w0 // span
            ow_t = _dot(vw3[b0], e_wb[0:span])
            for j in range(1, n_win // span):
                ow_t = ow_t + _dot(vw3[b0 + j], e_wb[j * span:(j + 1) * span])
            ow_t = ow_t / l_w

            o_all = jnp.concatenate([os_t, ow_t, oc_t, jnp.zeros_like(oc_t)], axis=0).T
            gate = _sigmoid(gl_ref[pl.ds(t0, span), 0:3 * r_n])
            for qi in range(nq):
                g_q = gate[qi * qb:(qi + 1) * qb]
                for r in range(r_n):
                    rs = slice((qi * r_n + r) * qb, (qi * r_n + r + 1) * qb)
                    out = (g_q[:, 3 * r:3 * r + 1] * o_all[rs, 2 * dh:3 * dh] + g_q[:, 3 * r + 1:3 * r + 2] * o_all[rs, :dh]
                           + g_q[:, 3 * r + 2:3 * r + 3] * o_all[rs, dh:2 * dh])
                    o_ref[pl.ds(t0 + qi * qb, qb), r * dh:(r + 1) * dh] = out
            return carry
        return body

    for sb in range(seq // NSA_SLAB_KEYS):
        lax.fori_loop(sb * it_per_slab, (sb + 1) * it_per_slab, make_body((sb + 1) * NSA_SLAB_KEYS), 0)


def _nsa(q, ksa, kwa, vst, vwt, kc, vc, gl, bsz, seq, pe_k, pe_v, k_w1, k_b1, k_w2, v_w1, v_b1, v_w2):
    g_n, r_n, dh, qb = NSA_KV_GROUPS, NSA_GROUP_SIZE, NSA_HEAD_DIM, NSA_Q_BLOCK
    t = bsz * seq
    n_cmp_pad = seq // CMP_STRIDE
    n_blk = seq // SLC_BLOCK
    span = NSA_QB_PER_ITER * qb
    assert n_blk <= F_POS and seq % NSA_SLAB_KEYS == 0 and seq >= WINDOW + span

    def cmp_rows(x):
        return (x.reshape(bsz, n_cmp_pad, CMP_STRIDE, g_n, dh).transpose(0, 3, 1, 2, 4)
                .reshape(bsz, g_n, n_cmp_pad, CMP_STRIDE * dh))

    slopes = np.asarray([2.0 ** (-8.0 * (h + 1) / NSA_HEADS) for h in range(NSA_HEADS)], dtype=np.float32)
    cols = NSA_QB_PER_ITER * r_n * qb
    slope_rows = jnp.asarray(np.tile(np.repeat(slopes.reshape(g_n, r_n), qb, axis=1), (1, NSA_QB_PER_ITER))
                             .reshape(g_n, cols, 1)) * LOG2E
    s3 = jnp.concatenate(_split3(slope_rows) * 2, axis=2)
    sfeat = jnp.pad(s3, ((0, 0), (0, 0), (F_POS, NSA_FEAT - F_POS - 6)))
    cf = np.zeros((n_cmp_pad, NSA_FEAT), np.float32)
    cf[:, F_POS:F_POS + 3] = (np.arange(n_cmp_pad) * CMP_STRIDE)[:, None]
    cf[:, F_POS + 3:F_POS + 6] = (CMP_LEN - 1) * 0.5
    cf[:, F_ONE:F_ONE + 3] = 1.0
    tok = np.arange(n_cmp_pad)[:, None] * CMP_STRIDE + np.arange(CMP_LEN)[None, :]
    overlap_t = ((tok // SLC_BLOCK)[:, :, None] == np.arange(NSA_FEAT)[None, None, :]).mean(axis=1).astype(np.float32).T
    cmp_spec = pl.BlockSpec((1, 1, n_cmp_pad, CMP_STRIDE * dh), lambda b, g: (b, g, 0, 0))
    q_spec = pl.BlockSpec((seq, r_n * dh), lambda b, g: (b, g))
    k_spec = pl.BlockSpec((seq, LANES), lambda b, g: (b, g))
    v_spec = pl.BlockSpec((1, dh, seq), lambda b, g: (b, g, 0))
    w1s, b1s, w2s = (CMP_LEN * dh, CMP_HIDDEN), (1, CMP_HIDDEN), (CMP_HIDDEN, dh)
    return pl.pallas_call(
        functools.partial(_nsa_kernel, seq=seq),
        out_shape=jax.ShapeDtypeStruct((t, NSA_HEADS * dh), F32),
        grid=(bsz, g_n),
        in_specs=[q_spec, k_spec, k_spec, v_spec, v_spec, cmp_spec, cmp_spec, k_spec,
                  pl.BlockSpec((1, cols, 1), lambda b, g: (g, 0, 0)),
                  pl.BlockSpec((1, cols, NSA_FEAT), lambda b, g: (g, 0, 0)),
                  _const_spec((n_cmp_pad, NSA_FEAT)),
                  _const_spec((1, CMP_LEN * dh)), _const_spec((1, CMP_LEN * dh)),
                  _const_spec(w1s), _const_spec(b1s), _const_spec(w2s),
                  _const_spec(w1s), _const_spec(b1s), _const_spec((dh, CMP_HIDDEN)),
                  _const_spec((NSA_FEAT, n_cmp_pad))],
        out_specs=q_spec,
        scratch_shapes=[pltpu.VMEM((n_cmp_pad, dh + NSA_FEAT), BF16), pltpu.VMEM((dh, n_cmp_pad), BF16),
                        pltpu.VMEM((seq // span, dh, span), BF16), pltpu.VMEM((seq // span, dh, span), BF16)],
        compiler_params=_cparams(2),
        name="nsa_attention",
    )(q, ksa, kwa, vst, vwt, cmp_rows(kc), cmp_rows(vc), gl, slope_rows, sfeat, jnp.asarray(cf),
      pe_k.astype(F32).reshape(1, -1), pe_v.astype(F32).reshape(1, -1),
      k_w1.astype(BF16), k_b1.astype(F32).reshape(1, -1), k_w2.astype(BF16),
      v_w1.astype(BF16), v_b1.astype(F32).reshape(1, -1), v_w2.astype(BF16).T, jnp.asarray(overlap_t))


def _pad_cols(w, n):
    return jnp.pad(w, ((0, 0), (0, n - w.shape[1])))


def kernel(x, ab_w_in, ab_w_out, s5_lambda_re, s5_lambda_im, s5_log_step, s5_b_re, s5_b_im, s5_c_re, s5_c_im, s5_d, s5_w_glu, s5_b_glu, gdn_conv_w, gdn_a_log, gdn_dt_bias, gdn_norm_w, nsa_w_in, nsa_w_out, nsa_pe_k, nsa_pe_v, nsa_k_w1, nsa_k_b1, nsa_k_w2, nsa_v_w1, nsa_v_b1, nsa_v_w2, ffn_w_in, ffn_conv_w, ffn_conv_b, ffn_w_out, norm_mix, norm_ffn, norm_final):
    bsz, seq, d = x.shape
    depth = ffn_w_in.shape[0]
    h = x.astype(F32).reshape(bsz * seq, d)
    for layer in range(depth):
        i = layer // 2
        if layer % 2 == 0:
            w = ab_w_in[i].astype(BF16)
            cuts = np.cumsum([0, S5_WIDTH, GDN_WIDTH, GDN_WIDTH, GDN_WIDTH, GDN_WIDTH])
            ws = [w[:, cuts[j]:cuts[j + 1]] for j in range(5)] + [_pad_cols(w[:, cuts[5]:], LANES)]
            u, q, k, v, z, ba = _norm_proj(h, norm_mix[layer], ws)
            prep = _s5_prep(s5_lambda_re[i], s5_lambda_im[i], s5_log_step[i], s5_b_re[i], s5_b_im[i],
                            s5_c_re[i], s5_c_im[i], s5_d[i])
            y_a = _s5(u, seq, prep, s5_w_glu[i], s5_b_glu[i])
            y_b = _gdn(q, k, v, z, ba, seq, gdn_conv_w[i], gdn_a_log[i], gdn_dt_bias[i], gdn_norm_w[i])
            wo = ab_w_out[i].astype(BF16)
            h = _out_proj(h, [y_a, y_b], [wo[:S5_WIDTH], wo[S5_WIDTH:]])
        else:
            q, ksa, kwa, vst, vwt, kc, vc, gl = _nsa_proj(h, norm_mix[layer], nsa_w_in[i], bsz, seq)
            o = _nsa(q, ksa, kwa, vst, vwt, kc, vc, gl, bsz, seq, nsa_pe_k[i], nsa_pe_v[i], nsa_k_w1[i],
                     nsa_k_b1[i], nsa_k_w2[i], nsa_v_w1[i], nsa_v_b1[i], nsa_v_w2[i])
            h = _out_proj(h, [o], [nsa_w_out[i].astype(BF16)])
        h = _ffn(h, seq, norm_ffn[layer], ffn_w_in[layer], ffn_conv_w[layer], ffn_conv_b[layer],
                 ffn_w_out[layer], norm_final, final=(layer == depth - 1))
    return h.reshape(bsz, seq, d).astype(x.dtype)
```

```python
import functools
import math

import numpy as np
import jax
import jax.numpy as jnp
from jax import lax
from jax.experimental import pallas as pl
from jax.experimental.pallas import tpu as pltpu

F32 = jnp.float32
BF16 = jnp.bfloat16
HI = lax.Precision.HIGHEST

D_MODEL = 1024
S5_WIDTH = 256
S5_GROUP = 16
S5_GROUPS = 16
S5_STATE = 64
S5_CHUNK = 16
GDN_HEAD_DIM = 128
GDN_HEADS = 6
GDN_WIDTH = 768
GDN_CONV = 4
GDN_CHUNK = 64
GDN_HEADS_PER_STEP = 2
GDN_CHUNKS_PER_ITER = 4
NSA_HEADS = 16
NSA_HEAD_DIM = 64
NSA_KV_GROUPS = 4
NSA_GROUP_SIZE = 4
NSA_KV_WIDTH = 256
CMP_LEN = 32
CMP_STRIDE = 16
CMP_HIDDEN = 256
SLC_BLOCK = 64
N_SELECT = 4
WINDOW = 256
NSA_Q_BLOCK = 64
NSA_SLAB_KEYS = 512
NSA_QB_PER_ITER = 2
LOG2E = math.log2(math.e)
FFN_HIDDEN = 2816
FFN_CONV = 3
FFN_COL_CHUNK = 256
RMS_EPS = 1e-6
NEG_INF = -1e30
LANES = 128
BF16_SUBLANES = 16
VMEM_LIMIT = 56 * 1024 * 1024
ROW_TILE = 512


def _cparams(n_axes):
    return pltpu.CompilerParams(dimension_semantics=("arbitrary",) * n_axes,
                                vmem_limit_bytes=VMEM_LIMIT)


def _rms(x, g):
    return x * lax.rsqrt(jnp.mean(x * x, axis=-1, keepdims=True) + RMS_EPS) * g


def _gelu(x):
    return 0.5 * x * (1.0 + jnp.tanh(math.sqrt(2.0 / math.pi) * (x + 0.044715 * (x * x * x))))


def _sigmoid(x):
    return 1.0 / (1.0 + jnp.exp(-x))


def _silu(x):
    return x * _sigmoid(x)


def _dot(a, b):
    return jnp.dot(a, b, preferred_element_type=F32)


def _dot_nt(a, b):
    return lax.dot_general(a, b, (((1,), (1,)), ((), ())), preferred_element_type=F32)


def _dot_tn(a, b):
    return lax.dot_general(a, b, (((0,), (0,)), ((), ())), preferred_element_type=F32)


def _const_spec(shape):
    nd = len(shape)
    return pl.BlockSpec(shape, lambda *_: (0,) * nd)


def _norm_proj_kernel(x_ref, g_ref, *refs, n_out):
    xn = _rms(x_ref[...], g_ref[...]).astype(BF16)
    for w_ref, o_ref in zip(refs[:n_out], refs[n_out:]):
        o_ref[...] = _dot(xn, w_ref[...]).astype(o_ref.dtype)


def _norm_proj(h, gain, weights):
    t, d = h.shape
    n_out = len(weights)
    return pl.pallas_call(
        functools.partial(_norm_proj_kernel, n_out=n_out),
        out_shape=[jax.ShapeDtypeStruct((t, w.shape[1]), F32) for w in weights],
        grid=(t // ROW_TILE,),
        in_specs=[pl.BlockSpec((ROW_TILE, d), lambda i: (i, 0)), _const_spec((1, d))]
        + [_const_spec(w.shape) for w in weights],
        out_specs=[pl.BlockSpec((ROW_TILE, w.shape[1]), lambda i: (i, 0)) for w in weights],
        compiler_params=_cparams(1),
        name="norm_proj",
    )(h, gain.reshape(1, d), *weights)


def _ffn_kernel(h_ref, hp_ref, *refs, n_mix, tiles_per_seq, n_chunks, final):
    y_refs, yp_refs, wm_refs = refs[:n_mix], refs[n_mix:2 * n_mix], refs[2 * n_mix:3 * n_mix]
    (g_ref, win_ref, cw_ref, cb_ref, wout_ref, gf_ref, o_ref,
     xn_ref, up0_ref, up1_ref, act_ref, x1_ref) = refs[3 * n_mix:]
    halo = BF16_SUBLANES
    rows = h_ref.shape[0]
    sub = 128
    g = g_ref[...]
    x1, x1p = h_ref[...], hp_ref[...]
    for y_ref, yp_ref, wm_ref in zip(y_refs, yp_refs, wm_refs):
        x1 = x1 + _dot(y_ref[...], wm_ref[...])
        x1p = x1p + _dot(yp_ref[...], wm_ref[...])
    x1_ref[...] = x1
    seq_start = pl.program_id(0) % tiles_per_seq == 0
    xn_ref[0:halo, :] = jnp.where(seq_start, 0.0, _rms(x1p, g)).astype(BF16)
    xn_ref[halo:, :] = _rms(x1, g).astype(BF16)

    def up(j, buf):
        xe = xn_ref[...]
        buf[0] = _dot(xe, win_ref[j])
        buf[1] = _dot(xe, win_ref[n_chunks + j])

    def gate(j, buf):
        def conv(half, r0, cw, cb):
            y = cb
            for k in range(FFN_CONV):
                start = halo + r0 - (FFN_CONV - 1) + k
                y = y + buf[half, start:start + sub, :] * cw[k:k + 1]
            return y
        for r0 in range(0, rows, sub):
            a = conv(0, r0, cw_ref[j], cb_ref[j])
            b = conv(1, r0, cw_ref[n_chunks + j], cb_ref[n_chunks + j])
            act_ref[j, r0:r0 + sub, :] = (_silu(a) * b).astype(BF16)

    def body(i, carry):
        up(2 * i + 1, up1_ref)
        gate(2 * i, up0_ref)
        up(2 * i + 2, up0_ref)
        gate(2 * i + 1, up1_ref)
        return carry

    up(0, up0_ref)
    lax.fori_loop(0, (n_chunks - 1) // 2, body, 0)
    gate(n_chunks - 1, up0_ref)

    out = x1_ref[...]
    for j in range(n_chunks):
        out = out + _dot(act_ref[j], wout_ref[j])
    if final:
        out = _rms(out, gf_ref[...])
    o_ref[...] = out


def _mix_ffn(h, ys, wms, seq, gain, w_in, conv_w, conv_b, w_out, gain_final, final):
    t, d = h.shape
    n_mix = len(ys)
    fc = FFN_COL_CHUNK
    n_chunks = FFN_HIDDEN // fc
    assert n_chunks % 2 == 1
    halo = BF16_SUBLANES
    win = w_in.astype(BF16).reshape(d, 2 * n_chunks, fc).transpose(1, 0, 2)
    cw = conv_w.astype(F32).reshape(FFN_CONV, 2 * n_chunks, fc).transpose(1, 0, 2)
    cb = conv_b.astype(F32).reshape(2 * n_chunks, 1, fc)
    wout = w_out.astype(BF16).reshape(n_chunks, fc, d)
    blocks_per_tile = ROW_TILE // halo
    tile_spec = lambda n: pl.BlockSpec((ROW_TILE, n), lambda i: (i, 0))
    prev_spec = lambda n: pl.BlockSpec((halo, n), lambda i: (jnp.maximum(i * blocks_per_tile - 1, 0), 0))
    return pl.pallas_call(
        functools.partial(_ffn_kernel, n_mix=n_mix, tiles_per_seq=seq // ROW_TILE, n_chunks=n_chunks, final=final),
        out_shape=jax.ShapeDtypeStruct((t, d), F32),
        grid=(t // ROW_TILE,),
        in_specs=[tile_spec(d), prev_spec(d)]
        + [tile_spec(y.shape[1]) for y in ys] + [prev_spec(y.shape[1]) for y in ys]
        + [_const_spec(w.shape) for w in wms]
        + [_const_spec((1, d)),
                  pl.BlockSpec(win.shape, lambda i: (0, 0, 0), pipeline_mode=pl.Buffered(1)),
                  _const_spec(cw.shape), _const_spec(cb.shape),
                  pl.BlockSpec(wout.shape, lambda i: (0, 0, 0), pipeline_mode=pl.Buffered(1)),
                  _const_spec((1, d))],
        out_specs=pl.BlockSpec((ROW_TILE, d), lambda i: (i, 0)),
        scratch_shapes=[pltpu.VMEM((ROW_TILE + halo, d), BF16), pltpu.VMEM((2, ROW_TILE + halo, fc), F32),
                        pltpu.VMEM((2, ROW_TILE + halo, fc), F32), pltpu.VMEM((n_chunks, ROW_TILE, fc), BF16),
                        pltpu.VMEM((ROW_TILE, d), F32)],
        compiler_params=_cparams(1),
        name="mix_ffn",
    )(h, h, *ys, *ys, *wms, gain.reshape(1, d), win, cw, cb, wout, gain_final.reshape(1, d))


def _s5_prep(lam_re, lam_im, log_step, b_re, b_im, c_re, c_im, d_skip):
    g_n, p_n, h_n, L = S5_GROUPS, S5_STATE, S5_GROUP, S5_CHUNK
    step = jnp.exp(log_step.astype(F32))[:, None]
    lr, li = lam_re.astype(F32), lam_im.astype(F32)
    mag = jnp.exp(lr * step)
    a_re = mag * jnp.cos(li * step)
    a_im = mag * jnp.sin(li * step)
    den = lr * lr + li * li
    n_re, n_im = a_re - 1.0, a_im
    z_re = (n_re * lr + n_im * li) / den
    z_im = (n_im * lr - n_re * li) / den
    b_re, b_im = b_re.astype(F32), b_im.astype(F32)
    bb_re = z_re[..., None] * b_re - z_im[..., None] * b_im
    bb_im = z_re[..., None] * b_im + z_im[..., None] * b_re
    c_re, c_im = c_re.astype(F32), c_im.astype(F32)
    pw_re, pw_im = [jnp.ones_like(a_re)], [jnp.zeros_like(a_im)]
    for _ in range(L):
        pr, pi = pw_re[-1], pw_im[-1]
        pw_re.append(pr * a_re - pi * a_im)
        pw_im.append(pr * a_im + pi * a_re)
    eye_g = jnp.eye(g_n, dtype=F32)
    ks = []
    for j in range(L):
        ab_re = pw_re[j][..., None] * bb_re - pw_im[j][..., None] * bb_im
        ab_im = pw_re[j][..., None] * bb_im + pw_im[j][..., None] * bb_re
        kj = (jnp.einsum('gop,gpi->gio', c_re, ab_re, precision=HI)
              - jnp.einsum('gop,gpi->gio', c_im, ab_im, precision=HI))
        ks.append(jnp.einsum('gio,gk->giko', kj, eye_g).reshape(g_n * h_n, g_n * h_n))
    kstack = jnp.concatenate(ks[::-1], axis=0).astype(BF16)
    bb = jnp.stack([jnp.einsum('gph,gk->ghkp', bb_re, eye_g), jnp.einsum('gph,gk->ghkp', bb_im, eye_g)], axis=2)
    bb = bb.reshape(g_n * h_n, 2 * g_n * p_n).astype(BF16)
    cc = jnp.stack([jnp.einsum('ghp,gk->gpkh', c_re, eye_g), -jnp.einsum('ghp,gk->gpkh', c_im, eye_g)], axis=0)
    cc = cc.reshape(2 * g_n * p_n, g_n * h_n).astype(BF16)
    apow = jnp.stack([jnp.concatenate([r.reshape(-1), i.reshape(-1)]) for r, i in zip(pw_re, pw_im)], axis=0)
    dvec = jnp.tile(d_skip.astype(F32).reshape(1, g_n * h_n), (1, L))
    return kstack, bb, cc, apow, dvec


def _s5_kernel(u_ref, ks_ref, bb_ref, cc_ref, ap_ref, dv_ref, wg_ref, bg_ref, o_ref, x_scr, hp_scr,
               *, n_batch, n_chunk):
    L, w = S5_CHUNK, S5_WIDTH
    half = S5_GROUPS * S5_STATE
    rows = n_batch * n_chunk
    u2 = u_ref[...].reshape(rows, L * w)
    u2b = u2.astype(BF16)

    def cmul(j, x):
        ar, ai = ap_ref[j:j + 1, :half], ap_ref[j:j + 1, half:]
        xr, xi = x[:, :half], x[:, half:]
        return jnp.concatenate([ar * xr - ai * xi, ar * xi + ai * xr], axis=1)

    xin = _dot(u2b[:, (L - 1) * w:], bb_ref[...])
    for s in range(L - 1):
        xin = xin + cmul(L - 1 - s, _dot(u2b[:, s * w:(s + 1) * w], bb_ref[...]))
    x_scr[...] = xin

    a_l = ap_ref[L:L + 1, :]
    alr, ali = a_l[:, :half], a_l[:, half:]

    def scan_body(k, hs):
        new = []
        for b in range(n_batch):
            r = b * n_chunk + k
            h = hs[b]
            hp_scr[pl.ds(r, 1), :] = h
            hr, hi = h[:, :half], h[:, half:]
            new.append(jnp.concatenate([alr * hr - ali * hi, alr * hi + ali * hr], axis=1)
                       + x_scr[pl.ds(r, 1), :])
        return tuple(new)

    lax.fori_loop(0, n_chunk, scan_body, tuple(jnp.zeros((1, 2 * half), F32) for _ in range(n_batch)))
    hp = hp_scr[...]

    for t in range(L):
        y = _dot(u2b[:, :(t + 1) * w], ks_ref[(L - 1 - t) * w:, :])
        y = y + _dot(cmul(t + 1, hp).astype(BF16), cc_ref[...])
        y = y + dv_ref[:, t * w:(t + 1) * w] * u2[:, t * w:(t + 1) * w]
        z = _gelu(y)
        gate = _sigmoid(_dot(z.astype(BF16), wg_ref[...]) + bg_ref[...])
        o_ref[:, :, t * w:(t + 1) * w] = (z * gate).astype(o_ref.dtype).reshape(n_batch, n_chunk, w)


def _s5(u, seq, prep, w_glu, b_glu):
    t = u.shape[0]
    bsz = t // seq
    L, w = S5_CHUNK, S5_WIDTH
    n_chunk = seq // L
    n_batch = 2
    kstack, bb, cc, apow, dvec = prep
    u3 = u.reshape(bsz, n_chunk, L * w)
    out = pl.pallas_call(
        functools.partial(_s5_kernel, n_batch=n_batch, n_chunk=n_chunk),
        out_shape=jax.ShapeDtypeStruct((bsz, n_chunk, L * w), BF16),
        grid=(bsz // n_batch,),
        in_specs=[pl.BlockSpec((n_batch, n_chunk, L * w), lambda i: (i, 0, 0)),
                  _const_spec(kstack.shape), _const_spec(bb.shape), _const_spec(cc.shape),
                  _const_spec(apow.shape), _const_spec(dvec.shape),
                  _const_spec((w, w)), _const_spec((1, w))],
        out_specs=pl.BlockSpec((n_batch, n_chunk, L * w), lambda i: (i, 0, 0)),
        scratch_shapes=[pltpu.VMEM((n_batch * n_chunk, 2 * S5_GROUPS * S5_STATE), F32),
                        pltpu.VMEM((n_batch * n_chunk, 2 * S5_GROUPS * S5_STATE), F32)],
        compiler_params=_cparams(1),
        name="s5_mixer",
    )(u3, kstack, bb, cc, apow, dvec, w_glu.astype(BF16), b_glu.astype(F32).reshape(1, w))
    return out.reshape(t, w)


def _gdn_kernel(q_ref, k_ref, v_ref, z_ref, bc_ref, ac_ref, ar_ref, cw_ref, hp_ref, hr_ref, nw_ref,
                o_ref, qs, ks, vs, xs, ns, qe, o0, gls, gcc_s, gcr_s, xp, *, seq, n_head):
    c, dh = GDN_CHUNK, GDN_HEAD_DIM
    n_chunk = seq // c
    pad = 8

    def finish(y, kind):
        if kind == "v":
            return y
        parts = []
        for j in range(n_head):
            p = y[:, j * dh:(j + 1) * dh]
            p = p * lax.rsqrt(jnp.sum(p * p, axis=-1, keepdims=True) + 1e-6)
            parts.append(p * (dh ** -0.5) if kind == "q" else p)
        return jnp.concatenate(parts, axis=1)

    xp[0:pad, :] = jnp.zeros((pad, xp.shape[1]), F32)
    for idx, (src, dst, kind) in enumerate(((q_ref, qs, "q"), (k_ref, ks, "k"), (v_ref, vs, "v"))):
        xp[pad:, :] = src[...]
        cw = cw_ref[0, idx]
        y = src[...] * cw[GDN_CONV - 1:GDN_CONV]
        for sh in range(1, GDN_CONV):
            y = y + xp[pad - sh:pad - sh + seq, :] * cw[GDN_CONV - 1 - sh:GDN_CONV - sh]
        dst[...] = finish(_silu(y), kind)

    hp = hp_ref[0]
    hr = hr_ref[0]

    def softplus(x):
        return jnp.maximum(x, 0.0) + jnp.log(1.0 + jnp.exp(-jnp.abs(x)))

    ii = lax.broadcasted_iota(jnp.int32, (c, c), 0)
    jj = lax.broadcasted_iota(jnp.int32, (c, c), 1)
    eye = (ii == jj).astype(F32)
    g_col = -jnp.exp(hp[0:1, :]) * softplus(ac_ref[0, 0] + hp[1:2, :])
    g_row = -jnp.exp(hr[:, 0:1]) * softplus(ar_ref[0, 0] + hr[:, 1:2])
    gc_col = jnp.dot((ii >= jj).astype(F32), g_col, precision=HI, preferred_element_type=F32)
    gcr_s[...] = jnp.dot(g_row, (ii <= jj).astype(F32), precision=HI, preferred_element_type=F32)
    for ci in range(n_chunk):
        gcc_s[ci * c:(ci + 1) * c, :] = gc_col[:, ci * n_head:(ci + 1) * n_head]

    def local_body(it, carry):
        pr = []
        for cc in range(GDN_CHUNKS_PER_ITER):
            ci = it * GDN_CHUNKS_PER_ITER + cc
            rs = pl.ds(pl.multiple_of(ci * c, c), c)
            beta = _sigmoid(bc_ref[0, 0, rs, :])
            gccs = gcc_s[rs, :]
            for j in range(n_head):
                sl = slice(j * dh, (j + 1) * dh)
                gcc = gccs[:, j:j + 1]
                gcr = gcr_s[pl.ds(ci * n_head + j, 1), :]
                q, k, v = qs[rs, sl], ks[rs, sl], vs[rs, sl]
                bj = beta[:, j:j + 1]
                kb = k * bj
                e_g = jnp.exp(gcc)
                g_last = gcc[c - 1:c, :]
                pr.append(dict(ci=ci, j=j, q=q, kf=k.astype(BF16), kbb=kb.astype(BF16),
                               decay=jnp.exp(jnp.where(ii >= jj, gcc - gcr, NEG_INF)),
                               rhs=jnp.concatenate([v * bj, kb * e_g], axis=1).astype(BF16),
                               qd=q * e_g, kd=(k * jnp.exp(g_last - gcc)).astype(BF16),
                               gl=jnp.broadcast_to(jnp.exp(g_last), (8, dh))))
        kk = [_dot_nt(p["kbb"], p["kf"]) for p in pr]
        qk = [_dot_nt(p["q"].astype(BF16), p["kf"]) for p in pr]
        ps = [-jnp.where(ii > jj, a * p["decay"], 0.0) for a, p in zip(kk, pr)]
        ts = [eye + n for n in ps]
        span = 2
        while span < c:
            ps = [_dot(n.astype(BF16), n.astype(BF16)) for n in ps]
            ts = [t + _dot(t.astype(BF16), n.astype(BF16)) for t, n in zip(ts, ps)]
            span *= 2
        uws = [_dot(t.astype(BF16), p["rhs"]).astype(BF16) for t, p in zip(ts, pr)]
        intras = [(s * p["decay"]).astype(BF16) for s, p in zip(qk, pr)]
        i_uws = [_dot(a, uw) for a, uw in zip(intras, uws)]
        kd_uws = [_dot_tn(p["kd"], uw) for p, uw in zip(pr, uws)]
        for p, i_uw, kd_uw in zip(pr, i_uws, kd_uws):
            j, ci = p["j"], p["ci"]
            ns[j, ci] = kd_uw[:, :dh]
            xs[j, ci] = kd_uw[:, dh:].astype(BF16)
            o0[j, ci] = i_uw[:, :dh]
            qe[j, ci] = (p["qd"] - i_uw[:, dh:]).astype(BF16)
            gls[j, ci] = p["gl"]
        return carry

    lax.fori_loop(0, n_chunk // GDN_CHUNKS_PER_ITER, local_body, 0)

    def rec_body(ci, states):
        r0 = pl.multiple_of(ci * c, c)
        rs = pl.ds(r0, c)
        s_bs = [s.astype(BF16) for s in states]
        xd = [_dot(xs[j, ci], s_bs[j]) for j in range(n_head)]
        od = [_dot(qe[j, ci], s_bs[j]) for j in range(n_head)]
        new = tuple(gls[j, ci][0:1, :] * states[j] - xd[j] + ns[j, ci] for j in range(n_head))
        for j in range(n_head):
            sl = slice(j * dh, (j + 1) * dh)
            o = od[j] + o0[j, ci]
            on = o * lax.rsqrt(jnp.mean(o * o, axis=-1, keepdims=True) + RMS_EPS) * nw_ref[...]
            o_ref[rs, sl] = (on * _silu(z_ref[rs, sl])).astype(o_ref.dtype)
        return new

    lax.fori_loop(0, n_chunk, rec_body, tuple(jnp.zeros((dh, dh), F32) for _ in range(n_head)))


def _gdn(q, k, v, z, ba, seq, conv_w, a_log, dt_bias, norm_w):
    t = q.shape[0]
    bsz = t // seq
    nh = GDN_HEADS_PER_STEP
    n_grp = GDN_HEADS // nh
    wd = nh * GDN_HEAD_DIM
    bl = ba[:, :GDN_HEADS].reshape(bsz, seq, n_grp, nh)
    al = ba[:, GDN_HEADS:2 * GDN_HEADS].reshape(bsz, seq, n_grp, nh)
    b_col = bl.transpose(0, 2, 1, 3)
    n_chunk = seq // GDN_CHUNK
    n_prob = n_chunk * nh
    a5 = al.reshape(bsz, n_chunk, GDN_CHUNK, n_grp, nh)
    a_col = a5.transpose(0, 3, 2, 1, 4).reshape(bsz, n_grp, GDN_CHUNK, n_prob)
    a_row = a5.transpose(0, 3, 1, 4, 2).reshape(bsz, n_grp, n_prob, GDN_CHUNK)
    cw = conv_w.astype(F32).reshape(GDN_CONV, 3, n_grp, wd).transpose(2, 1, 0, 3)
    hp = jnp.stack([jnp.tile(a_log.astype(F32).reshape(n_grp, nh), (1, n_chunk)),
                    jnp.tile(dt_bias.astype(F32).reshape(n_grp, nh), (1, n_chunk))], axis=1)
    hr = hp.transpose(0, 2, 1)
    act_spec = pl.BlockSpec((seq, wd), lambda b, g: (b, g))
    col_spec = pl.BlockSpec((1, 1, seq, nh), lambda b, g: (b, g, 0, 0))
    return pl.pallas_call(
        functools.partial(_gdn_kernel, seq=seq, n_head=nh),
        out_shape=jax.ShapeDtypeStruct((t, GDN_WIDTH), BF16),
        grid=(bsz, n_grp),
        in_specs=[act_spec, act_spec, act_spec, act_spec, col_spec,
                  pl.BlockSpec((1, 1, GDN_CHUNK, n_prob), lambda b, g: (b, g, 0, 0)),
                  pl.BlockSpec((1, 1, n_prob, GDN_CHUNK), lambda b, g: (b, g, 0, 0)),
                  pl.BlockSpec((1, 3, GDN_CONV, wd), lambda b, g: (g, 0, 0, 0)),
                  pl.BlockSpec((1, 2, n_prob), lambda b, g: (g, 0, 0)),
                  pl.BlockSpec((1, n_prob, 2), lambda b, g: (g, 0, 0)),
                  _const_spec((1, GDN_HEAD_DIM))],
        out_specs=act_spec,
        scratch_shapes=[pltpu.VMEM((seq, wd), F32), pltpu.VMEM((seq, wd), F32), pltpu.VMEM((seq, wd), F32),
                        pltpu.VMEM((nh, n_chunk, GDN_HEAD_DIM, GDN_HEAD_DIM), BF16),
                        pltpu.VMEM((nh, n_chunk, GDN_HEAD_DIM, GDN_HEAD_DIM), F32),
                        pltpu.VMEM((nh, n_chunk, GDN_CHUNK, GDN_HEAD_DIM), BF16),
                        pltpu.VMEM((nh, n_chunk, GDN_CHUNK, GDN_HEAD_DIM), F32),
                        pltpu.VMEM((nh, n_chunk, 8, GDN_HEAD_DIM), F32),
                        pltpu.VMEM((seq, nh), F32), pltpu.VMEM((n_prob, GDN_CHUNK), F32),
                        pltpu.VMEM((seq + 8, wd), F32)],
        compiler_params=_cparams(2),
        name="gated_deltanet",
    )(q, k, v, z, b_col, a_col, a_row, cw, hp, hr, norm_w.astype(F32).reshape(1, GDN_HEAD_DIM))


NSA_FEAT = 64
F_BLK, F_POS, F_ONE = 0, 32, 38


def _split3(x):
    x1 = x.astype(BF16).astype(F32)
    x2 = (x - x1).astype(BF16).astype(F32)
    return x1, x2, x - x1 - x2


def _nsa_proj_kernel(x_ref, g_ref, wq_ref, wks_ref, wkw_ref, wvs_ref, wvw_ref, wkc_ref, wvc_ref, wg_ref, kf_ref,
                     q_ref, ksa_ref, kwa_ref, vst_ref, vwt_ref, kc_ref, vc_ref, gl_ref):
    xn = _rms(x_ref[...], g_ref[...]).astype(BF16)
    q_ref[...] = (_dot(xn, wq_ref[...]) * (NSA_HEAD_DIM ** -0.5 * LOG2E)).astype(BF16)
    kf = jnp.concatenate([kf_ref[...]] * NSA_KV_GROUPS, axis=1)
    ksa_ref[...] = (_dot(xn, wks_ref[...]) + kf).astype(BF16)
    kwa_ref[...] = (_dot(xn, wkw_ref[...]) + kf).astype(BF16)
    vst_ref[0] = _dot_nt(wvs_ref[...], xn).astype(BF16)
    vwt_ref[0] = _dot_nt(wvw_ref[...], xn).astype(BF16)
    kc_ref[...] = _dot(xn, wkc_ref[...])
    vc_ref[...] = _dot(xn, wvc_ref[...])
    gl_ref[...] = _dot(xn, wg_ref[...])


def _nsa_proj(h, gain, w_in, bsz, seq):
    t, d = h.shape
    g_n, dh = NSA_KV_GROUPS, NSA_HEAD_DIM
    nq, kvw = NSA_HEADS * dh, NSA_KV_WIDTH
    w = w_in.astype(BF16)
    wq = w[:, :nq]
    wkc, wvc, wks, wvs, wkw, wvw = (w[:, nq + j * kvw: nq + (j + 1) * kvw] for j in range(6))
    wgl = w[:, nq + 6 * kvw:]

    def lane_padded(x, width):
        return jnp.pad(x.reshape(d, g_n, width), ((0, 0), (0, 0), (0, LANES - width))).reshape(d, g_n * LANES)

    pos = np.arange(seq)
    kf = np.zeros((seq, LANES), np.float32)
    kf[pos, dh + F_BLK + pos // SLC_BLOCK] = 1.0
    kf[:, dh + F_POS:dh + F_POS + 3] = ((pos // SLC_BLOCK) * SLC_BLOCK)[:, None]
    kf[:, dh + F_POS + 3:dh + F_POS + 6] = (pos % SLC_BLOCK)[:, None]
    kf[:, dh + F_ONE:dh + F_ONE + 3] = 1.0
    tps = seq // ROW_TILE
    row_spec = lambda n: pl.BlockSpec((ROW_TILE, n), lambda i: (i, 0))
    tr_spec = pl.BlockSpec((1, kvw, ROW_TILE), lambda i: (i // tps, 0, i % tps))
    ws = [wq, lane_padded(wks, dh), lane_padded(wkw, dh), wvs.T, wvw.T, wkc, wvc, lane_padded(wgl, 3 * NSA_GROUP_SIZE)]
    return pl.pallas_call(
        _nsa_proj_kernel,
        out_shape=[jax.ShapeDtypeStruct((t, nq), BF16),
                   jax.ShapeDtypeStruct((t, g_n * LANES), BF16), jax.ShapeDtypeStruct((t, g_n * LANES), BF16),
                   jax.ShapeDtypeStruct((bsz, kvw, seq), BF16), jax.ShapeDtypeStruct((bsz, kvw, seq), BF16),
                   jax.ShapeDtypeStruct((t, kvw), F32), jax.ShapeDtypeStruct((t, kvw), F32),
                   jax.ShapeDtypeStruct((t, g_n * LANES), F32)],
        grid=(t // ROW_TILE,),
        in_specs=[row_spec(d), _const_spec((1, d))] + [_const_spec(x.shape) for x in ws]
        + [pl.BlockSpec((ROW_TILE, LANES), lambda i: (i % tps, 0))],
        out_specs=[row_spec(nq), row_spec(g_n * LANES), row_spec(g_n * LANES), tr_spec, tr_spec,
                   row_spec(kvw), row_spec(kvw), row_spec(g_n * LANES)],
        compiler_params=_cparams(1),
        name="nsa_proj",
    )(h, gain.reshape(1, d), *ws, jnp.asarray(kf))


def _nsa_kernel(q_ref, ksa_ref, kwa_ref, vst_ref, vwt_ref, kca_ref, vca_ref, gl_ref, sl_ref, sf_ref, cf_ref,
                pek_ref, pev_ref, kw1_ref, kb1_ref, kw2_ref, vw1_ref, vb1_ref, vw2t_ref, ovt_ref, o_ref,
                kcmp, vcmpt, vs3, vw3, qa_s, oc_s, *, seq):
    qb, dh, r_n, nq = NSA_Q_BLOCK, NSA_HEAD_DIM, NSA_GROUP_SIZE, NSA_QB_PER_ITER
    span = nq * qb
    cols = nq * r_n * qb
    n_cmp_pad = seq // CMP_STRIDE
    n_cmp = (seq - CMP_LEN) // CMP_STRIDE + 1
    half = CMP_STRIDE * dh
    n_blk = seq // SLC_BLOCK
    n_win = WINDOW + span
    it_per_slab = NSA_SLAB_KEYS // span
    nf = NSA_FEAT

    def hidden(a_ref, pe_ref, w1_ref, b1_ref):
        a = a_ref[0, 0].astype(BF16)
        top = _dot(a, w1_ref[:half, :])
        bot = pltpu.roll(_dot(a, w1_ref[half:, :]), n_cmp_pad - 1, 0)
        const = _dot(pe_ref[...].astype(BF16), w1_ref[...]) + b1_ref[...]
        return _gelu(top + bot + const).astype(BF16)

    kcmp[...] = jnp.concatenate([_dot(hidden(kca_ref, pek_ref, kw1_ref, kb1_ref), kw2_ref[...]), cf_ref[...]],
                                axis=1).astype(BF16)
    vcmpt[...] = _dot_nt(vw2t_ref[...], hidden(vca_ref, pev_ref, vw1_ref, vb1_ref)).astype(BF16)
    for n in range(seq // span):
        vs3[n] = vst_ref[0, :, n * span:(n + 1) * span]
        vw3[n] = vwt_ref[0, :, n * span:(n + 1) * span]

    def iota(shape, axis):
        return lax.broadcasted_iota(jnp.int32, shape, axis)

    slope = sl_ref[0]
    colc, colr = iota((cols, 1), 0), iota((1, cols), 1)
    tc_col = (colc // (r_n * qb)) * qb + colc % qb
    tc_row = (colr // (r_n * qb)) * qb + colr % qb
    qi_row = colr // (r_n * qb)
    lane_f = iota((1, nf), 1)
    jr = iota((n_cmp_pad, 1), 0)
    nr = iota((nf, 1), 0)
    lane_s = iota((1, span), 1)
    qi_s = lane_s // qb
    key_d = iota((span, 1), 0)
    key_w = iota((n_win, 1), 0)
    eye_s = (iota((span, span), 0) == iota((span, span), 1)).astype(BF16)

    def prepare(p):
        t0 = pl.multiple_of(p * span, span)
        qf = jnp.concatenate([q_ref[pl.ds(t0 + qi * qb, qb), r * dh:(r + 1) * dh]
                              for qi in range(nq) for r in range(r_n)], axis=0).astype(F32)
        c1, c2, c3 = _split3(-(slope * (t0 + tc_col).astype(F32)))
        f_plain = jnp.where(lane_f == F_ONE, c1, jnp.where(lane_f == F_ONE + 1, c2,
                            jnp.where(lane_f == F_ONE + 2, c3, sf_ref[0])))
        qa_plain = jnp.concatenate([qf, f_plain], axis=1).astype(BF16)

        ok_c = (jr * CMP_STRIDE + (CMP_LEN - 1) <= t0 + tc_row) & (jr < n_cmp)
        s_c = jnp.where(ok_c, _dot_nt(kcmp[...], qa_plain), NEG_INF)
        e_c = jnp.where(ok_c, jnp.exp2(s_c - jnp.max(s_c, axis=0, keepdims=True)), 0.0)
        l_c = jnp.sum(e_c, axis=0, keepdims=True)
        p_c = e_c / jnp.where(l_c > 0.0, l_c, 1.0)
        oc_s[...] = _dot(vcmpt[...], p_c.astype(BF16))

        imp4 = sum(_dot(ovt_ref[...], part.astype(BF16)) for part in _split3(p_c))
        halves = []
        for qi in range(nq):
            a = imp4[:, (2 * qi) * LANES:(2 * qi + 1) * LANES] + imp4[:, (2 * qi + 1) * LANES:(2 * qi + 2) * LANES]
            halves.append(a + pltpu.roll(a, qb, 1))
        imp = jnp.where(qi_s == 0, halves[0], halves[1])

        iq = nq * p + qi_s
        cand = (nr > 0) & (nr < iq)
        sc = jnp.where(cand, imp, NEG_INF)
        sel = (nr == 0) & (iq > 0)
        for _ in range(N_SELECT - 2):
            best = jnp.max(sc, axis=0, keepdims=True)
            first = jnp.min(jnp.where(sc == best, nr, nf), axis=0, keepdims=True)
            pick = (nr == first) & cand
            sel = sel | pick
            sc = jnp.where(pick, NEG_INF, sc)
        neg_slab = jnp.where(sel & (nr < nq * p), 0.0, NEG_INF)
        neg_cur = jnp.where(nr == iq, 0.0, jnp.where((nr >= nq * p) & (nr < iq) & sel, 0.0, NEG_INF))
        neg_t = jnp.concatenate([neg_slab, neg_cur], axis=0).astype(BF16)
        neg = _dot_nt(eye_s, neg_t)
        rows_of = lambda x: jnp.concatenate([x[qi * qb:(qi + 1) * qb] for qi in range(nq) for _ in range(r_n)], axis=0)
        f_slab = jnp.where(lane_f < n_blk, rows_of(neg[:, :nf]), f_plain)
        f_cur = jnp.where(lane_f < n_blk, rows_of(neg[:, nf:]), f_plain)
        qa_s[0] = qa_plain
        qa_s[1] = jnp.concatenate([qf, f_slab], axis=1).astype(BF16)
        qa_s[2] = jnp.concatenate([qf, f_cur], axis=1).astype(BF16)

    n_iter = seq // span

    def make_body(n_keys):
        def body(p, carry):
            t0 = pl.multiple_of(p * span, span)
            w0 = pl.multiple_of(jnp.maximum(t0 - WINDOW, 0), span)
            oc_t = oc_s[...]
            s_cu = _dot_nt(ksa_ref[pl.ds(t0, span), :], qa_s[2])
            s_w = _dot_nt(kwa_ref[pl.ds(w0, n_win), :], qa_s[0])
            s_sl = _dot_nt(ksa_ref[0:n_keys, :], qa_s[1])
            prepare(jnp.minimum(p + 1, n_iter - 1))

            own = (key_d // qb) == qi_row
            s_cu = jnp.where(own & (key_d % qb > tc_row % qb), NEG_INF, s_cu)
            m_s = jnp.maximum(jnp.max(s_sl, axis=0, keepdims=True), jnp.max(s_cu, axis=0, keepdims=True))
            e_sl = jnp.exp2(s_sl - m_s)
            e_cu = jnp.exp2(s_cu - m_s)
            l_s = jnp.sum(e_sl, axis=0, keepdims=True) + jnp.sum(e_cu, axis=0, keepdims=True)
            os_t = (_dot(vst_ref[0, :, 0:n_keys], e_sl.astype(BF16)) + _dot(vs3[p], e_cu.astype(BF16))) / l_s

            rel = (t0 - w0) + tc_row - key_w
            s_w = jnp.where((rel >= 0) & (rel < WINDOW), s_w, NEG_INF)
            e_w = jnp.exp2(s_w - jnp.max(s_w, axis=0, keepdims=True))
            e_wb = e_w.astype(BF16)
            b0 = w0 // span
            ow_t = _dot(vw3[b0], e_wb[0:span])
            for j in range(1, n_win // span):
                ow_t = ow_t + _dot(vw3[b0 + j], e_wb[j * span:(j + 1) * span])
            ow_t = ow_t / jnp.sum(e_w, axis=0, keepdims=True)

            o_all = jnp.concatenate([os_t, ow_t, oc_t, jnp.zeros_like(oc_t)], axis=0).T
            gate = _sigmoid(gl_ref[pl.ds(t0, span), 0:3 * r_n])
            for qi in range(nq):
                g_q = gate[qi * qb:(qi + 1) * qb]
                heads = []
                for r in range(r_n):
                    rs = slice((qi * r_n + r) * qb, (qi * r_n + r + 1) * qb)
                    heads.append(g_q[:, 3 * r:3 * r + 1] * o_all[rs, 2 * dh:3 * dh]
                                 + g_q[:, 3 * r + 1:3 * r + 2] * o_all[rs, :dh]
                                 + g_q[:, 3 * r + 2:3 * r + 3] * o_all[rs, dh:2 * dh])
                o_ref[pl.ds(t0 + qi * qb, qb), :] = jnp.concatenate(heads, axis=1).astype(o_ref.dtype)
            return carry
        return body

    prepare(jnp.int32(0))
    for sb in range(seq // NSA_SLAB_KEYS):
        lax.fori_loop(sb * it_per_slab, (sb + 1) * it_per_slab, make_body((sb + 1) * NSA_SLAB_KEYS), 0)


def _nsa(q, ksa, kwa, vst, vwt, kc, vc, gl, bsz, seq, pe_k, pe_v, k_w1, k_b1, k_w2, v_w1, v_b1, v_w2):
    g_n, r_n, dh, qb = NSA_KV_GROUPS, NSA_GROUP_SIZE, NSA_HEAD_DIM, NSA_Q_BLOCK
    t = bsz * seq
    n_cmp_pad = seq // CMP_STRIDE
    n_blk = seq // SLC_BLOCK
    span = NSA_QB_PER_ITER * qb
    assert n_blk <= F_POS and seq % NSA_SLAB_KEYS == 0 and seq >= WINDOW + span

    def cmp_rows(x):
        return (x.reshape(bsz, n_cmp_pad, CMP_STRIDE, g_n, dh).transpose(0, 3, 1, 2, 4)
                .reshape(bsz, g_n, n_cmp_pad, CMP_STRIDE * dh))

    slopes = np.asarray([2.0 ** (-8.0 * (h + 1) / NSA_HEADS) for h in range(NSA_HEADS)], dtype=np.float32)
    cols = NSA_QB_PER_ITER * r_n * qb
    slope_rows = jnp.asarray(np.tile(np.repeat(slopes.reshape(g_n, r_n), qb, axis=1), (1, NSA_QB_PER_ITER))
                             .reshape(g_n, cols, 1)) * LOG2E
    s3 = jnp.concatenate(_split3(slope_rows) * 2, axis=2)
    sfeat = jnp.pad(s3, ((0, 0), (0, 0), (F_POS, NSA_FEAT - F_POS - 6)))
    cf = np.zeros((n_cmp_pad, NSA_FEAT), np.float32)
    cf[:, F_POS:F_POS + 3] = (np.arange(n_cmp_pad) * CMP_STRIDE)[:, None]
    cf[:, F_POS + 3:F_POS + 6] = (CMP_LEN - 1) * 0.5
    cf[:, F_ONE:F_ONE + 3] = 1.0
    tok = np.arange(n_cmp_pad)[:, None] * CMP_STRIDE + np.arange(CMP_LEN)[None, :]
    overlap_t = ((tok // SLC_BLOCK)[:, :, None] == np.arange(NSA_FEAT)[None, None, :]).mean(axis=1).astype(np.float32).T
    cmp_spec = pl.BlockSpec((1, 1, n_cmp_pad, CMP_STRIDE * dh), lambda b, g: (b, g, 0, 0))
    q_spec = pl.BlockSpec((seq, r_n * dh), lambda b, g: (b, g))
    k_spec = pl.BlockSpec((seq, LANES), lambda b, g: (b, g))
    v_spec = pl.BlockSpec((1, dh, seq), lambda b, g: (b, g, 0))
    w1s, b1s, w2s = (CMP_LEN * dh, CMP_HIDDEN), (1, CMP_HIDDEN), (CMP_HIDDEN, dh)
    return pl.pallas_call(
        functools.partial(_nsa_kernel, seq=seq),
        out_shape=jax.ShapeDtypeStruct((t, NSA_HEADS * dh), BF16),
        grid=(bsz, g_n),
        in_specs=[q_spec, k_spec, k_spec, v_spec, v_spec, cmp_spec, cmp_spec, k_spec,
                  pl.BlockSpec((1, cols, 1), lambda b, g: (g, 0, 0)),
                  pl.BlockSpec((1, cols, NSA_FEAT), lambda b, g: (g, 0, 0)),
                  _const_spec((n_cmp_pad, NSA_FEAT)),
                  _const_spec((1, CMP_LEN * dh)), _const_spec((1, CMP_LEN * dh)),
                  _const_spec(w1s), _const_spec(b1s), _const_spec(w2s),
                  _const_spec(w1s), _const_spec(b1s), _const_spec((dh, CMP_HIDDEN)),
                  _const_spec((NSA_FEAT, n_cmp_pad))],
        out_specs=q_spec,
        scratch_shapes=[pltpu.VMEM((n_cmp_pad, dh + NSA_FEAT), BF16), pltpu.VMEM((dh, n_cmp_pad), BF16),
                        pltpu.VMEM((seq // span, dh, span), BF16), pltpu.VMEM((seq // span, dh, span), BF16),
                        pltpu.VMEM((3, cols, dh + NSA_FEAT), BF16), pltpu.VMEM((dh, cols), F32)],
        compiler_params=_cparams(2),
        name="nsa_attention",
    )(q, ksa, kwa, vst, vwt, cmp_rows(kc), cmp_rows(vc), gl, slope_rows, sfeat, jnp.asarray(cf),
      pe_k.astype(F32).reshape(1, -1), pe_v.astype(F32).reshape(1, -1),
      k_w1.astype(BF16), k_b1.astype(F32).reshape(1, -1), k_w2.astype(BF16),
      v_w1.astype(BF16), v_b1.astype(F32).reshape(1, -1), v_w2.astype(BF16).T, jnp.asarray(overlap_t, dtype=BF16))


def _pad_cols(w, n):
    return jnp.pad(w, ((0, 0), (0, n - w.shape[1])))


def kernel(x, ab_w_in, ab_w_out, s5_lambda_re, s5_lambda_im, s5_log_step, s5_b_re, s5_b_im, s5_c_re, s5_c_im, s5_d, s5_w_glu, s5_b_glu, gdn_conv_w, gdn_a_log, gdn_dt_bias, gdn_norm_w, nsa_w_in, nsa_w_out, nsa_pe_k, nsa_pe_v, nsa_k_w1, nsa_k_b1, nsa_k_w2, nsa_v_w1, nsa_v_b1, nsa_v_w2, ffn_w_in, ffn_conv_w, ffn_conv_b, ffn_w_out, norm_mix, norm_ffn, norm_final):
    bsz, seq, d = x.shape
    depth = ffn_w_in.shape[0]
    h = x.astype(F32).reshape(bsz * seq, d)
    for layer in range(depth):
        i = layer // 2
        if layer % 2 == 0:
            w = ab_w_in[i].astype(BF16)
            cuts = np.cumsum([0, S5_WIDTH, GDN_WIDTH, GDN_WIDTH, GDN_WIDTH, GDN_WIDTH])
            ws = [w[:, cuts[j]:cuts[j + 1]] for j in range(5)] + [_pad_cols(w[:, cuts[5]:], LANES)]
            u, q, k, v, z, ba = _norm_proj(h, norm_mix[layer], ws)
            prep = _s5_prep(s5_lambda_re[i], s5_lambda_im[i], s5_log_step[i], s5_b_re[i], s5_b_im[i],
                            s5_c_re[i], s5_c_im[i], s5_d[i])
            y_a = _s5(u, seq, prep, s5_w_glu[i], s5_b_glu[i])
            y_b = _gdn(q, k, v, z, ba, seq, gdn_conv_w[i], gdn_a_log[i], gdn_dt_bias[i], gdn_norm_w[i])
            wo = ab_w_out[i].astype(BF16)
            mix, wms = [y_a, y_b], [wo[:S5_WIDTH], wo[S5_WIDTH:]]
        else:
            q, ksa, kwa, vst, vwt, kc, vc, gl = _nsa_proj(h, norm_mix[layer], nsa_w_in[i], bsz, seq)
            o = _nsa(q, ksa, kwa, vst, vwt, kc, vc, gl, bsz, seq, nsa_pe_k[i], nsa_pe_v[i], nsa_k_w1[i],
                     nsa_k_b1[i], nsa_k_w2[i], nsa_v_w1[i], nsa_v_b1[i], nsa_v_w2[i])
            mix, wms = [o], [nsa_w_out[i].astype(BF16)]
        h = _mix_ffn(h, mix, wms, seq, norm_ffn[layer], ffn_w_in[layer], ffn_conv_w[layer], ffn_conv_b[layer],
                     ffn_w_out[layer], norm_final, final=(layer == depth - 1))
    return h.reshape(bsz, seq, d).astype(x.dtype)
```

```python
import functools
import math

import numpy as np
import jax
import jax.numpy as jnp
from jax import lax
from jax.experimental import pallas as pl
from jax.experimental.pallas import tpu as pltpu

F32 = jnp.float32
BF16 = jnp.bfloat16
HI = lax.Precision.HIGHEST

D_MODEL = 1024
S5_WIDTH = 256
S5_GROUP = 16
S5_GROUPS = 16
S5_STATE = 64
S5_CHUNK = 16
GDN_HEAD_DIM = 128
GDN_HEADS = 6
GDN_WIDTH = 768
GDN_CONV = 4
GDN_CHUNK = 64
GDN_HEADS_PER_STEP = 2
GDN_CHUNKS_PER_ITER = 8
NSA_HEADS = 16
NSA_HEAD_DIM = 64
NSA_KV_GROUPS = 4
NSA_GROUP_SIZE = 4
NSA_KV_WIDTH = 256
CMP_LEN = 32
CMP_STRIDE = 16
CMP_HIDDEN = 256
SLC_BLOCK = 64
N_SELECT = 4
WINDOW = 256
NSA_Q_BLOCK = 64
NSA_SLAB_KEYS = 512
NSA_QB_PER_ITER = 2
LOG2E = math.log2(math.e)
FFN_HIDDEN = 2816
FFN_CONV = 3
FFN_COL_CHUNK = 256
RMS_EPS = 1e-6
NEG_INF = -1e30
LANES = 128
BF16_SUBLANES = 16
VMEM_LIMIT = 56 * 1024 * 1024
ROW_TILE = 512


def _cparams(n_axes):
    return pltpu.CompilerParams(dimension_semantics=("arbitrary",) * n_axes,
                                vmem_limit_bytes=VMEM_LIMIT)


def _rms(x, g):
    return x * lax.rsqrt(jnp.mean(x * x, axis=-1, keepdims=True) + RMS_EPS) * g


def _gelu(x):
    return 0.5 * x * (1.0 + jnp.tanh(math.sqrt(2.0 / math.pi) * (x + 0.044715 * (x * x * x))))


def _sigmoid(x):
    return 1.0 / (1.0 + jnp.exp(-x))


def _silu(x):
    return x * _sigmoid(x)


def _dot(a, b):
    return jnp.dot(a, b, preferred_element_type=F32)


def _dot_nt(a, b):
    return lax.dot_general(a, b, (((1,), (1,)), ((), ())), preferred_element_type=F32)


def _dot_tn(a, b):
    return lax.dot_general(a, b, (((0,), (0,)), ((), ())), preferred_element_type=F32)


def _const_spec(shape):
    nd = len(shape)
    return pl.BlockSpec(shape, lambda *_: (0,) * nd)


def _norm_proj_kernel(x_ref, g_ref, *refs, n_out):
    xn = _rms(x_ref[...], g_ref[...]).astype(BF16)
    for w_ref, o_ref in zip(refs[:n_out], refs[n_out:]):
        o_ref[...] = _dot(xn, w_ref[...]).astype(o_ref.dtype)


def _norm_proj(h, gain, weights):
    t, d = h.shape
    n_out = len(weights)
    return pl.pallas_call(
        functools.partial(_norm_proj_kernel, n_out=n_out),
        out_shape=[jax.ShapeDtypeStruct((t, w.shape[1]), F32) for w in weights],
        grid=(t // ROW_TILE,),
        in_specs=[pl.BlockSpec((ROW_TILE, d), lambda i: (i, 0)), _const_spec((1, d))]
        + [_const_spec(w.shape) for w in weights],
        out_specs=[pl.BlockSpec((ROW_TILE, w.shape[1]), lambda i: (i, 0)) for w in weights],
        compiler_params=_cparams(1),
        name="norm_proj",
    )(h, gain.reshape(1, d), *weights)


def _ffn_kernel(h_ref, hp_ref, *refs, n_mix, tiles_per_seq, n_chunks, final):
    y_refs, yp_refs, wm_refs = refs[:n_mix], refs[n_mix:2 * n_mix], refs[2 * n_mix:3 * n_mix]
    (g_ref, win_ref, cw_ref, cb_ref, wout_ref, gf_ref, o_ref,
     xn_ref, up0_ref, up1_ref, act_ref, x1_ref) = refs[3 * n_mix:]
    halo = BF16_SUBLANES
    rows = h_ref.shape[0]
    sub = 128
    g = g_ref[...]
    n_piece = rows // sub
    bounds = [0] + [halo + r0 + sub for r0 in range(0, rows, sub)]

    def residual_piece(r):
        rs = slice(r * sub, (r + 1) * sub)
        x1 = h_ref[rs, :]
        for y_ref, wm_ref in zip(y_refs, wm_refs):
            x1 = x1 + _dot(y_ref[rs, :], wm_ref[...])
        x1_ref[rs, :] = x1
        xn_ref[halo + r * sub:halo + (r + 1) * sub, :] = _rms(x1, g).astype(BF16)
        if r == 0:
            x1p = hp_ref[...]
            for yp_ref, wm_ref in zip(yp_refs, wm_refs):
                x1p = x1p + _dot(yp_ref[...], wm_ref[...])
            seq_start = pl.program_id(0) % tiles_per_seq == 0
            xn_ref[0:halo, :] = jnp.where(seq_start, 0.0, _rms(x1p, g)).astype(BF16)

    def up_piece(j, buf, r):
        xe = xn_ref[bounds[r]:bounds[r + 1], :]
        buf[0, bounds[r]:bounds[r + 1], :] = _dot(xe, win_ref[j])
        buf[1, bounds[r]:bounds[r + 1], :] = _dot(xe, win_ref[n_chunks + j])

    def gate_piece(j, buf, r):
        r0 = r * sub

        def conv(half, cw, cb):
            y = cb
            for k in range(FFN_CONV):
                start = halo + r0 - (FFN_CONV - 1) + k
                y = y + buf[half, start:start + sub, :] * cw[k:k + 1]
            return y
        a = conv(0, cw_ref[j], cb_ref[j])
        b = conv(1, cw_ref[n_chunks + j], cb_ref[n_chunks + j])
        act_ref[j, r0:r0 + sub, :] = (_silu(a) * b).astype(BF16)

    def body(i, carry):
        for r in range(n_piece):
            up_piece(2 * i + 1, up1_ref, r)
            gate_piece(2 * i, up0_ref, r)
        for r in range(n_piece):
            up_piece(2 * i + 2, up0_ref, r)
            gate_piece(2 * i + 1, up1_ref, r)
        return carry

    for r in range(n_piece):
        residual_piece(r)
        up_piece(0, up0_ref, r)
    lax.fori_loop(0, (n_chunks - 1) // 2, body, 0)

    gate_piece(n_chunks - 1, up0_ref, 0)
    for r in range(n_piece):
        rs = slice(r * sub, (r + 1) * sub)
        out = x1_ref[rs, :]
        for j in range(n_chunks):
            out = out + _dot(act_ref[j, rs, :], wout_ref[j])
        if r + 1 < n_piece:
            gate_piece(n_chunks - 1, up0_ref, r + 1)
        if final:
            out = _rms(out, gf_ref[...])
        o_ref[rs, :] = out


def _mix_ffn(h, ys, wms, seq, gain, w_in, conv_w, conv_b, w_out, gain_final, final):
    t, d = h.shape
    n_mix = len(ys)
    fc = FFN_COL_CHUNK
    n_chunks = FFN_HIDDEN // fc
    assert n_chunks % 2 == 1
    halo = BF16_SUBLANES
    win = w_in.astype(BF16).reshape(d, 2 * n_chunks, fc).transpose(1, 0, 2)
    cw = conv_w.astype(F32).reshape(FFN_CONV, 2 * n_chunks, fc).transpose(1, 0, 2)
    cb = conv_b.astype(F32).reshape(2 * n_chunks, 1, fc)
    wout = w_out.astype(BF16).reshape(n_chunks, fc, d)
    blocks_per_tile = ROW_TILE // halo
    tile_spec = lambda n: pl.BlockSpec((ROW_TILE, n), lambda i: (i, 0))
    prev_spec = lambda n: pl.BlockSpec((halo, n), lambda i: (jnp.maximum(i * blocks_per_tile - 1, 0), 0))
    return pl.pallas_call(
        functools.partial(_ffn_kernel, n_mix=n_mix, tiles_per_seq=seq // ROW_TILE, n_chunks=n_chunks, final=final),
        out_shape=jax.ShapeDtypeStruct((t, d), F32),
        grid=(t // ROW_TILE,),
        in_specs=[tile_spec(d), prev_spec(d)]
        + [tile_spec(y.shape[1]) for y in ys] + [prev_spec(y.shape[1]) for y in ys]
        + [_const_spec(w.shape) for w in wms]
        + [_const_spec((1, d)),
                  pl.BlockSpec(win.shape, lambda i: (0, 0, 0), pipeline_mode=pl.Buffered(1)),
                  _const_spec(cw.shape), _const_spec(cb.shape),
                  pl.BlockSpec(wout.shape, lambda i: (0, 0, 0), pipeline_mode=pl.Buffered(1)),
                  _const_spec((1, d))],
        out_specs=pl.BlockSpec((ROW_TILE, d), lambda i: (i, 0)),
        scratch_shapes=[pltpu.VMEM((ROW_TILE + halo, d), BF16), pltpu.VMEM((2, ROW_TILE + halo, fc), F32),
                        pltpu.VMEM((2, ROW_TILE + halo, fc), F32), pltpu.VMEM((n_chunks, ROW_TILE, fc), BF16),
                        pltpu.VMEM((ROW_TILE, d), F32)],
        compiler_params=_cparams(1),
        name="mix_ffn",
    )(h, h, *ys, *ys, *wms, gain.reshape(1, d), win, cw, cb, wout, gain_final.reshape(1, d))


def _s5_prep(lam_re, lam_im, log_step, b_re, b_im, c_re, c_im, d_skip):
    g_n, p_n, h_n, L = S5_GROUPS, S5_STATE, S5_GROUP, S5_CHUNK
    step = jnp.exp(log_step.astype(F32))[:, None]
    lr, li = lam_re.astype(F32), lam_im.astype(F32)
    mag = jnp.exp(lr * step)
    a_re = mag * jnp.cos(li * step)
    a_im = mag * jnp.sin(li * step)
    den = lr * lr + li * li
    n_re, n_im = a_re - 1.0, a_im
    z_re = (n_re * lr + n_im * li) / den
    z_im = (n_im * lr - n_re * li) / den
    b_re, b_im = b_re.astype(F32), b_im.astype(F32)
    bb_re = z_re[..., None] * b_re - z_im[..., None] * b_im
    bb_im = z_re[..., None] * b_im + z_im[..., None] * b_re
    c_re, c_im = c_re.astype(F32), c_im.astype(F32)
    pw_re, pw_im = [jnp.ones_like(a_re)], [jnp.zeros_like(a_im)]
    for _ in range(L):
        pr, pi = pw_re[-1], pw_im[-1]
        pw_re.append(pr * a_re - pi * a_im)
        pw_im.append(pr * a_im + pi * a_re)
    eye_g = jnp.eye(g_n, dtype=F32)
    ks = []
    for j in range(L):
        ab_re = pw_re[j][..., None] * bb_re - pw_im[j][..., None] * bb_im
        ab_im = pw_re[j][..., None] * bb_im + pw_im[j][..., None] * bb_re
        kj = (jnp.einsum('gop,gpi->gio', c_re, ab_re, precision=HI)
              - jnp.einsum('gop,gpi->gio', c_im, ab_im, precision=HI))
        ks.append(jnp.einsum('gio,gk->giko', kj, eye_g).reshape(g_n * h_n, g_n * h_n))
    kstack = jnp.concatenate(ks[::-1], axis=0).astype(BF16)
    bb = jnp.stack([jnp.einsum('gph,gk->ghkp', bb_re, eye_g), jnp.einsum('gph,gk->ghkp', bb_im, eye_g)], axis=2)
    bb = bb.reshape(g_n * h_n, 2 * g_n * p_n).astype(BF16)
    cc = jnp.stack([jnp.einsum('ghp,gk->gpkh', c_re, eye_g), -jnp.einsum('ghp,gk->gpkh', c_im, eye_g)], axis=0)
    cc = cc.reshape(2 * g_n * p_n, g_n * h_n).astype(BF16)
    apow = jnp.stack([jnp.concatenate([r.reshape(-1), i.reshape(-1)]) for r, i in zip(pw_re, pw_im)], axis=0)
    dvec = jnp.tile(d_skip.astype(F32).reshape(1, g_n * h_n), (1, L))
    return kstack, bb, cc, apow, dvec


def _s5_kernel(u_ref, ks_ref, bb_ref, cc_ref, ap_ref, dv_ref, wg_ref, bg_ref, o_ref, x_scr, hp_scr,
               *, n_batch, n_chunk):
    L, w = S5_CHUNK, S5_WIDTH
    half = S5_GROUPS * S5_STATE
    rows = n_batch * n_chunk
    u2 = u_ref[...].reshape(rows, L * w)
    u2b = u2.astype(BF16)

    def cmul(j, x):
        ar, ai = ap_ref[j:j + 1, :half], ap_ref[j:j + 1, half:]
        xr, xi = x[:, :half], x[:, half:]
        return jnp.concatenate([ar * xr - ai * xi, ar * xi + ai * xr], axis=1)

    xin = _dot(u2b[:, (L - 1) * w:], bb_ref[...])
    for s in range(L - 1):
        xin = xin + cmul(L - 1 - s, _dot(u2b[:, s * w:(s + 1) * w], bb_ref[...]))
    x_scr[...] = xin

    a_l = ap_ref[L:L + 1, :]
    alr, ali = a_l[:, :half], a_l[:, half:]

    def scan_body(k, hs):
        new = []
        for b in range(n_batch):
            r = b * n_chunk + k
            h = hs[b]
            hp_scr[pl.ds(r, 1), :] = h
            hr, hi = h[:, :half], h[:, half:]
            new.append(jnp.concatenate([alr * hr - ali * hi, alr * hi + ali * hr], axis=1)
                       + x_scr[pl.ds(r, 1), :])
        return tuple(new)

    lax.fori_loop(0, n_chunk, scan_body, tuple(jnp.zeros((1, 2 * half), F32) for _ in range(n_batch)))
    hp = hp_scr[...]

    for t in range(L):
        y = _dot(u2b[:, :(t + 1) * w], ks_ref[(L - 1 - t) * w:, :])
        y = y + _dot(cmul(t + 1, hp).astype(BF16), cc_ref[...])
        y = y + dv_ref[:, t * w:(t + 1) * w] * u2[:, t * w:(t + 1) * w]
        z = _gelu(y)
        gate = _sigmoid(_dot(z.astype(BF16), wg_ref[...]) + bg_ref[...])
        o_ref[:, :, t * w:(t + 1) * w] = (z * gate).astype(o_ref.dtype).reshape(n_batch, n_chunk, w)


def _s5(u, seq, prep, w_glu, b_glu):
    t = u.shape[0]
    bsz = t // seq
    L, w = S5_CHUNK, S5_WIDTH
    n_chunk = seq // L
    n_batch = 2
    kstack, bb, cc, apow, dvec = prep
    u3 = u.reshape(bsz, n_chunk, L * w)
    out = pl.pallas_call(
        functools.partial(_s5_kernel, n_batch=n_batch, n_chunk=n_chunk),
        out_shape=jax.ShapeDtypeStruct((bsz, n_chunk, L * w), BF16),
        grid=(bsz // n_batch,),
        in_specs=[pl.BlockSpec((n_batch, n_chunk, L * w), lambda i: (i, 0, 0)),
                  _const_spec(kstack.shape), _const_spec(bb.shape), _const_spec(cc.shape),
                  _const_spec(apow.shape), _const_spec(dvec.shape),
                  _const_spec((w, w)), _const_spec((1, w))],
        out_specs=pl.BlockSpec((n_batch, n_chunk, L * w), lambda i: (i, 0, 0)),
        scratch_shapes=[pltpu.VMEM((n_batch * n_chunk, 2 * S5_GROUPS * S5_STATE), F32),
                        pltpu.VMEM((n_batch * n_chunk, 2 * S5_GROUPS * S5_STATE), F32)],
        compiler_params=_cparams(1),
        name="s5_mixer",
    )(u3, kstack, bb, cc, apow, dvec, w_glu.astype(BF16), b_glu.astype(F32).reshape(1, w))
    return out.reshape(t, w)


def _gdn_kernel(q_ref, k_ref, v_ref, z_ref, bc_ref, ac_ref, ar_ref, cw_ref, hp_ref, hr_ref, nw_ref,
                o_ref, qs, ks, vs, xs, ns, qe, o0, gls, gcc_s, gcr_s, xp, xs2, ns2, gls2, *, seq, n_head):
    c, dh = GDN_CHUNK, GDN_HEAD_DIM
    n_chunk = seq // c
    pad = 8

    def finish(y, kind):
        if kind == "v":
            return y
        parts = []
        for j in range(n_head):
            p = y[:, j * dh:(j + 1) * dh]
            p = p * lax.rsqrt(jnp.sum(p * p, axis=-1, keepdims=True) + 1e-6)
            parts.append(p * (dh ** -0.5) if kind == "q" else p)
        return jnp.concatenate(parts, axis=1)

    xp[0:pad, :] = jnp.zeros((pad, xp.shape[1]), F32)
    for idx, (src, dst, kind) in enumerate(((q_ref, qs, "q"), (k_ref, ks, "k"), (v_ref, vs, "v"))):
        xp[pad:, :] = src[...]
        cw = cw_ref[0, idx]
        y = src[...] * cw[GDN_CONV - 1:GDN_CONV]
        for sh in range(1, GDN_CONV):
            y = y + xp[pad - sh:pad - sh + seq, :] * cw[GDN_CONV - 1 - sh:GDN_CONV - sh]
        dst[...] = finish(_silu(y), kind)

    hp = hp_ref[0]
    hr = hr_ref[0]

    def softplus(x):
        return jnp.maximum(x, 0.0) + jnp.log(1.0 + jnp.exp(-jnp.abs(x)))

    ii = lax.broadcasted_iota(jnp.int32, (c, c), 0)
    jj = lax.broadcasted_iota(jnp.int32, (c, c), 1)
    eye = (ii == jj).astype(F32)
    g_col = -jnp.exp(hp[0:1, :]) * softplus(ac_ref[0, 0] + hp[1:2, :])
    g_row = -jnp.exp(hr[:, 0:1]) * softplus(ar_ref[0, 0] + hr[:, 1:2])
    gc_col = jnp.dot((ii >= jj).astype(F32), g_col, precision=HI, preferred_element_type=F32)
    gcr_s[...] = jnp.dot(g_row, (ii <= jj).astype(F32), precision=HI, preferred_element_type=F32)
    for ci in range(n_chunk):
        gcc_s[ci * c:(ci + 1) * c, :] = gc_col[:, ci * n_head:(ci + 1) * n_head]

    def local_body(it, carry):
        pr = []
        for cc in range(GDN_CHUNKS_PER_ITER):
            ci = it * GDN_CHUNKS_PER_ITER + cc
            rs = pl.ds(pl.multiple_of(ci * c, c), c)
            beta = _sigmoid(bc_ref[0, 0, rs, :])
            gccs = gcc_s[rs, :]
            for j in range(n_head):
                sl = slice(j * dh, (j + 1) * dh)
                gcc = gccs[:, j:j + 1]
                gcr = gcr_s[pl.ds(ci * n_head + j, 1), :]
                q, k, v = qs[rs, sl], ks[rs, sl], vs[rs, sl]
                bj = beta[:, j:j + 1]
                kb = k * bj
                e_g = jnp.exp(gcc)
                g_last = gcc[c - 1:c, :]
                pr.append(dict(ci=ci, j=j, q=q, kf=k.astype(BF16), kbb=kb.astype(BF16),
                               decay=jnp.exp(jnp.where(ii >= jj, gcc - gcr, NEG_INF)),
                               rhs=jnp.concatenate([v * bj, kb * e_g], axis=1).astype(BF16),
                               qd=q * e_g, kd=(k * jnp.exp(g_last - gcc)).astype(BF16),
                               gl=jnp.broadcast_to(jnp.exp(g_last), (8, dh))))
        kk = [_dot_nt(p["kbb"], p["kf"]) for p in pr]
        qk = [_dot_nt(p["q"].astype(BF16), p["kf"]) for p in pr]
        ps = [-jnp.where(ii > jj, a * p["decay"], 0.0) for a, p in zip(kk, pr)]
        ts = [eye + n for n in ps]
        span = 2
        while span < c:
            ps = [_dot(n.astype(BF16), n.astype(BF16)) for n in ps]
            ts = [t + _dot(t.astype(BF16), n.astype(BF16)) for t, n in zip(ts, ps)]
            span *= 2
        uws = [_dot(t.astype(BF16), p["rhs"]).astype(BF16) for t, p in zip(ts, pr)]
        intras = [(s * p["decay"]).astype(BF16) for s, p in zip(qk, pr)]
        i_uws = [_dot(a, uw) for a, uw in zip(intras, uws)]
        kd_uws = [_dot_tn(p["kd"], uw) for p, uw in zip(pr, uws)]
        for p, i_uw, kd_uw in zip(pr, i_uws, kd_uws):
            j, ci = p["j"], p["ci"]
            ns[j, ci] = kd_uw[:, :dh]
            xs[j, ci] = kd_uw[:, dh:].astype(BF16)
            o0[j, ci] = i_uw[:, :dh]
            qe[j, ci] = (p["qd"] - i_uw[:, dh:]).astype(BF16)
            gls[j, ci] = p["gl"]
        first = [idx for idx, p in enumerate(pr) if (idx // n_head) % 2 == 0]
        x1b = [kd_uws[idx + n_head][:, dh:].astype(BF16) for idx in first]
        x1x0 = [_dot(x1, kd_uws[idx][:, dh:].astype(BF16)) for x1, idx in zip(x1b, first)]
        x1n0 = [_dot(x1, kd_uws[idx][:, :dh].astype(BF16)) for x1, idx in zip(x1b, first)]
        for idx, xx, xn in zip(first, x1x0, x1n0):
            p0, p1 = pr[idx], pr[idx + n_head]
            g0, g1 = p0["gl"][0:1, :], p1["gl"][0:1, :]
            pi = it * (GDN_CHUNKS_PER_ITER // 2) + (idx // n_head) // 2
            xs2[p0["j"], pi] = (g1 * kd_uws[idx][:, dh:] + g0 * kd_uws[idx + n_head][:, dh:] - xx).astype(BF16)
            ns2[p0["j"], pi] = g1 * kd_uws[idx][:, :dh] - xn + kd_uws[idx + n_head][:, :dh]
            gls2[p0["j"], pi] = p0["gl"] * p1["gl"]
        return carry

    lax.fori_loop(0, n_chunk // GDN_CHUNKS_PER_ITER, local_body, 0)

    def rec_body(pi, states):
        c0, c1 = 2 * pi, 2 * pi + 1
        s_bs = [s.astype(BF16) for s in states]
        xd2 = [_dot(xs2[j, pi], s_bs[j]) for j in range(n_head)]
        xd0 = [_dot(xs[j, c0], s_bs[j]) for j in range(n_head)]
        od0 = [_dot(qe[j, c0], s_bs[j]) for j in range(n_head)]
        new = tuple(gls2[j, pi][0:1, :] * states[j] - xd2[j] + ns2[j, pi] for j in range(n_head))
        mid = [(gls[j, c0][0:1, :] * states[j] - xd0[j] + ns[j, c0]).astype(BF16) for j in range(n_head)]
        od1 = [_dot(qe[j, c1], mid[j]) for j in range(n_head)]
        for ci, od in ((c0, od0), (c1, od1)):
            rs = pl.ds(pl.multiple_of(ci * c, c), c)
            for j in range(n_head):
                sl = slice(j * dh, (j + 1) * dh)
                o = od[j] + o0[j, ci]
                on = o * lax.rsqrt(jnp.mean(o * o, axis=-1, keepdims=True) + RMS_EPS) * nw_ref[...]
                o_ref[rs, sl] = (on * _silu(z_ref[rs, sl])).astype(o_ref.dtype)
        return new

    lax.fori_loop(0, n_chunk // 2, rec_body, tuple(jnp.zeros((dh, dh), F32) for _ in range(n_head)))


def _gdn(q, k, v, z, ba, seq, conv_w, a_log, dt_bias, norm_w):
    t = q.shape[0]
    bsz = t // seq
    nh = GDN_HEADS_PER_STEP
    n_grp = GDN_HEADS // nh
    wd = nh * GDN_HEAD_DIM
    bl = ba[:, :GDN_HEADS].reshape(bsz, seq, n_grp, nh)
    al = ba[:, GDN_HEADS:2 * GDN_HEADS].reshape(bsz, seq, n_grp, nh)
    b_col = bl.transpose(0, 2, 1, 3)
    n_chunk = seq // GDN_CHUNK
    n_prob = n_chunk * nh
    a5 = al.reshape(bsz, n_chunk, GDN_CHUNK, n_grp, nh)
    a_col = a5.transpose(0, 3, 2, 1, 4).reshape(bsz, n_grp, GDN_CHUNK, n_prob)
    a_row = a5.transpose(0, 3, 1, 4, 2).reshape(bsz, n_grp, n_prob, GDN_CHUNK)
    cw = conv_w.astype(F32).reshape(GDN_CONV, 3, n_grp, wd).transpose(2, 1, 0, 3)
    hp = jnp.stack([jnp.tile(a_log.astype(F32).reshape(n_grp, nh), (1, n_chunk)),
                    jnp.tile(dt_bias.astype(F32).reshape(n_grp, nh), (1, n_chunk))], axis=1)
    hr = hp.transpose(0, 2, 1)
    act_spec = pl.BlockSpec((seq, wd), lambda b, g: (b, g))
    col_spec = pl.BlockSpec((1, 1, seq, nh), lambda b, g: (b, g, 0, 0))
    return pl.pallas_call(
        functools.partial(_gdn_kernel, seq=seq, n_head=nh),
        out_shape=jax.ShapeDtypeStruct((t, GDN_WIDTH), BF16),
        grid=(bsz, n_grp),
        in_specs=[act_spec, act_spec, act_spec, act_spec, col_spec,
                  pl.BlockSpec((1, 1, GDN_CHUNK, n_prob), lambda b, g: (b, g, 0, 0)),
                  pl.BlockSpec((1, 1, n_prob, GDN_CHUNK), lambda b, g: (b, g, 0, 0)),
                  pl.BlockSpec((1, 3, GDN_CONV, wd), lambda b, g: (g, 0, 0, 0)),
                  pl.BlockSpec((1, 2, n_prob), lambda b, g: (g, 0, 0)),
                  pl.BlockSpec((1, n_prob, 2), lambda b, g: (g, 0, 0)),
                  _const_spec((1, GDN_HEAD_DIM))],
        out_specs=act_spec,
        scratch_shapes=[pltpu.VMEM((seq, wd), F32), pltpu.VMEM((seq, wd), F32), pltpu.VMEM((seq, wd), F32),
                        pltpu.VMEM((nh, n_chunk, GDN_HEAD_DIM, GDN_HEAD_DIM), BF16),
                        pltpu.VMEM((nh, n_chunk, GDN_HEAD_DIM, GDN_HEAD_DIM), F32),
                        pltpu.VMEM((nh, n_chunk, GDN_CHUNK, GDN_HEAD_DIM), BF16),
                        pltpu.VMEM((nh, n_chunk, GDN_CHUNK, GDN_HEAD_DIM), F32),
                        pltpu.VMEM((nh, n_chunk, 8, GDN_HEAD_DIM), F32),
                        pltpu.VMEM((seq, nh), F32), pltpu.VMEM((n_prob, GDN_CHUNK), F32),
                        pltpu.VMEM((seq + 8, wd), F32),
                        pltpu.VMEM((nh, n_chunk // 2, GDN_HEAD_DIM, GDN_HEAD_DIM), BF16),
                        pltpu.VMEM((nh, n_chunk // 2, GDN_HEAD_DIM, GDN_HEAD_DIM), F32),
                        pltpu.VMEM((nh, n_chunk // 2, 8, GDN_HEAD_DIM), F32)],
        compiler_params=_cparams(2),
        name="gated_deltanet",
    )(q, k, v, z, b_col, a_col, a_row, cw, hp, hr, norm_w.astype(F32).reshape(1, GDN_HEAD_DIM))


NSA_FEAT = 64
F_BLK, F_POS, F_ONE = 0, 32, 38


def _split3(x):
    x1 = x.astype(BF16).astype(F32)
    x2 = (x - x1).astype(BF16).astype(F32)
    return x1, x2, x - x1 - x2


def _nsa_proj_kernel(x_ref, g_ref, wq_ref, wks_ref, wkw_ref, wvs_ref, wvw_ref, wkc_ref, wvc_ref, wg_ref, kf_ref,
                     q_ref, ksa_ref, kwa_ref, vst_ref, vwt_ref, kc_ref, vc_ref, gl_ref):
    xn = _rms(x_ref[...], g_ref[...]).astype(BF16)
    q_ref[...] = (_dot(xn, wq_ref[...]) * (NSA_HEAD_DIM ** -0.5 * LOG2E)).astype(BF16)
    kf = jnp.concatenate([kf_ref[...]] * NSA_KV_GROUPS, axis=1)
    ksa_ref[...] = (_dot(xn, wks_ref[...]) + kf).astype(BF16)
    kwa_ref[...] = (_dot(xn, wkw_ref[...]) + kf).astype(BF16)
    vst_ref[0] = _dot_nt(wvs_ref[...], xn).astype(BF16)
    vwt_ref[0] = _dot_nt(wvw_ref[...], xn).astype(BF16)
    kc_ref[...] = _dot(xn, wkc_ref[...])
    vc_ref[...] = _dot(xn, wvc_ref[...])
    gl_ref[...] = _dot(xn, wg_ref[...])


def _nsa_proj(h, gain, w_in, bsz, seq):
    t, d = h.shape
    g_n, dh = NSA_KV_GROUPS, NSA_HEAD_DIM
    nq, kvw = NSA_HEADS * dh, NSA_KV_WIDTH
    w = w_in.astype(BF16)
    wq = w[:, :nq]
    wkc, wvc, wks, wvs, wkw, wvw = (w[:, nq + j * kvw: nq + (j + 1) * kvw] for j in range(6))
    wgl = w[:, nq + 6 * kvw:]

    def lane_padded(x, width):
        return jnp.pad(x.reshape(d, g_n, width), ((0, 0), (0, 0), (0, LANES - width))).reshape(d, g_n * LANES)

    pos = np.arange(seq)
    kf = np.zeros((seq, LANES), np.float32)
    kf[pos, dh + F_BLK + pos // SLC_BLOCK] = 1.0
    kf[:, dh + F_POS:dh + F_POS + 3] = ((pos // SLC_BLOCK) * SLC_BLOCK)[:, None]
    kf[:, dh + F_POS + 3:dh + F_POS + 6] = (pos % SLC_BLOCK)[:, None]
    kf[:, dh + F_ONE:dh + F_ONE + 3] = 1.0
    tps = seq // ROW_TILE
    row_spec = lambda n: pl.BlockSpec((ROW_TILE, n), lambda i: (i, 0))
    tr_spec = pl.BlockSpec((1, kvw, ROW_TILE), lambda i: (i // tps, 0, i % tps))
    ws = [wq, lane_padded(wks, dh), lane_padded(wkw, dh), wvs.T, wvw.T, wkc, wvc, lane_padded(wgl, 3 * NSA_GROUP_SIZE)]
    return pl.pallas_call(
        _nsa_proj_kernel,
        out_shape=[jax.ShapeDtypeStruct((t, nq), BF16),
                   jax.ShapeDtypeStruct((t, g_n * LANES), BF16), jax.ShapeDtypeStruct((t, g_n * LANES), BF16),
                   jax.ShapeDtypeStruct((bsz, kvw, seq), BF16), jax.ShapeDtypeStruct((bsz, kvw, seq), BF16),
                   jax.ShapeDtypeStruct((t, kvw), F32), jax.ShapeDtypeStruct((t, kvw), F32),
                   jax.ShapeDtypeStruct((t, g_n * LANES), F32)],
        grid=(t // ROW_TILE,),
        in_specs=[row_spec(d), _const_spec((1, d))] + [_const_spec(x.shape) for x in ws]
        + [pl.BlockSpec((ROW_TILE, LANES), lambda i: (i % tps, 0))],
        out_specs=[row_spec(nq), row_spec(g_n * LANES), row_spec(g_n * LANES), tr_spec, tr_spec,
                   row_spec(kvw), row_spec(kvw), row_spec(g_n * LANES)],
        compiler_params=_cparams(1),
        name="nsa_proj",
    )(h, gain.reshape(1, d), *ws, jnp.asarray(kf))


def _nsa_kernel(q_ref, ksa_ref, kwa_ref, vst_ref, vwt_ref, kca_ref, vca_ref, gl_ref, sl_ref, sf_ref, cf_ref,
                pek_ref, pev_ref, kw1_ref, kb1_ref, kw2_ref, vw1_ref, vb1_ref, vw2t_ref, ovt_ref, o_ref,
                kcmp, vcmpt, vs3, vw3, qa_s, oc_s, *, seq):
    qb, dh, r_n, nq = NSA_Q_BLOCK, NSA_HEAD_DIM, NSA_GROUP_SIZE, NSA_QB_PER_ITER
    span = nq * qb
    cols = nq * r_n * qb
    n_cmp_pad = seq // CMP_STRIDE
    n_cmp = (seq - CMP_LEN) // CMP_STRIDE + 1
    half = CMP_STRIDE * dh
    n_blk = seq // SLC_BLOCK
    n_win = WINDOW + span
    it_per_slab = NSA_SLAB_KEYS // span
    nf = NSA_FEAT

    def hidden(a_ref, pe_ref, w1_ref, b1_ref):
        a = a_ref[0, 0].astype(BF16)
        top = _dot(a, w1_ref[:half, :])
        bot = pltpu.roll(_dot(a, w1_ref[half:, :]), n_cmp_pad - 1, 0)
        const = _dot(pe_ref[...].astype(BF16), w1_ref[...]) + b1_ref[...]
        return _gelu(top + bot + const).astype(BF16)

    kcmp[...] = jnp.concatenate([_dot(hidden(kca_ref, pek_ref, kw1_ref, kb1_ref), kw2_ref[...]), cf_ref[...]],
                                axis=1).astype(BF16)
    vcmpt[...] = _dot_nt(vw2t_ref[...], hidden(vca_ref, pev_ref, vw1_ref, vb1_ref)).astype(BF16)
    for n in range(seq // span):
        vs3[n] = vst_ref[0, :, n * span:(n + 1) * span]
        vw3[n] = vwt_ref[0, :, n * span:(n + 1) * span]

    def iota(shape, axis):
        return lax.broadcasted_iota(jnp.int32, shape, axis)

    slope = sl_ref[0]
    colc, colr = iota((cols, 1), 0), iota((1, cols), 1)
    tc_col = (colc // (r_n * qb)) * qb + colc % qb
    tc_row = (colr // (r_n * qb)) * qb + colr % qb
    qi_row = colr // (r_n * qb)
    lane_f = iota((1, nf), 1)
    jr = iota((n_cmp_pad, 1), 0)
    nr = iota((nf, 1), 0)
    lane_s = iota((1, span), 1)
    qi_s = lane_s // qb
    key_d = iota((span, 1), 0)
    key_w = iota((n_win, 1), 0)
    eye_s = (iota((span, span), 0) == iota((span, span), 1)).astype(BF16)

    def prepare(p):
        t0 = pl.multiple_of(p * span, span)
        qf = jnp.concatenate([q_ref[pl.ds(t0 + qi * qb, qb), r * dh:(r + 1) * dh]
                              for qi in range(nq) for r in range(r_n)], axis=0).astype(F32)
        c1, c2, c3 = _split3(-(slope * (t0 + tc_col).astype(F32)))
        f_plain = jnp.where(lane_f == F_ONE, c1, jnp.where(lane_f == F_ONE + 1, c2,
                            jnp.where(lane_f == F_ONE + 2, c3, sf_ref[0])))
        qa_plain = jnp.concatenate([qf, f_plain], axis=1).astype(BF16)

        ok_c = (jr * CMP_STRIDE + (CMP_LEN - 1) <= t0 + tc_row) & (jr < n_cmp)
        s_c = jnp.where(ok_c, _dot_nt(kcmp[...], qa_plain), NEG_INF)
        e_c = jnp.where(ok_c, jnp.exp2(s_c - jnp.max(s_c, axis=0, keepdims=True)), 0.0)
        l_c = jnp.sum(e_c, axis=0, keepdims=True)
        p_c = e_c / jnp.where(l_c > 0.0, l_c, 1.0)
        oc_s[...] = _dot(vcmpt[...], p_c.astype(BF16))

        imp4 = sum(_dot(ovt_ref[...], part.astype(BF16)) for part in _split3(p_c))
        halves = []
        for qi in range(nq):
            a = imp4[:, (2 * qi) * LANES:(2 * qi + 1) * LANES] + imp4[:, (2 * qi + 1) * LANES:(2 * qi + 2) * LANES]
            halves.append(a + pltpu.roll(a, qb, 1))
        imp = jnp.where(qi_s == 0, halves[0], halves[1])

        iq = nq * p + qi_s
        cand = (nr > 0) & (nr < iq)
        sc = jnp.where(cand, imp, NEG_INF)
        sel = (nr == 0) & (iq > 0)
        for _ in range(N_SELECT - 2):
            best = jnp.max(sc, axis=0, keepdims=True)
            first = jnp.min(jnp.where(sc == best, nr, nf), axis=0, keepdims=True)
            pick = (nr == first) & cand
            sel = sel | pick
            sc = jnp.where(pick, NEG_INF, sc)
        neg_slab = jnp.where(sel & (nr < nq * p), 0.0, NEG_INF)
        neg_cur = jnp.where(nr == iq, 0.0, jnp.where((nr >= nq * p) & (nr < iq) & sel, 0.0, NEG_INF))
        neg_t = jnp.concatenate([neg_slab, neg_cur], axis=0).astype(BF16)
        neg = _dot_nt(eye_s, neg_t)
        rows_of = lambda x: jnp.concatenate([x[qi * qb:(qi + 1) * qb] for qi in range(nq) for _ in range(r_n)], axis=0)
        f_slab = jnp.where(lane_f < n_blk, rows_of(neg[:, :nf]), f_plain)
        f_cur = jnp.where(lane_f < n_blk, rows_of(neg[:, nf:]), f_plain)
        qa_s[0] = qa_plain
        qa_s[1] = jnp.concatenate([qf, f_slab], axis=1).astype(BF16)
        qa_s[2] = jnp.concatenate([qf, f_cur], axis=1).astype(BF16)

    n_iter = seq // span

    def make_body(n_keys):
        def body(p, carry):
            t0 = pl.multiple_of(p * span, span)
            w0 = pl.multiple_of(jnp.maximum(t0 - WINDOW, 0), span)
            oc_t = oc_s[...]
            qa_slab = qa_s[1]
            n_slab = n_keys // NSA_SLAB_KEYS

            def slab_scores(k):
                return _dot_nt(ksa_ref[k * NSA_SLAB_KEYS:(k + 1) * NSA_SLAB_KEYS, :], qa_slab)

            s_cu = _dot_nt(ksa_ref[pl.ds(t0, span), :], qa_s[2])
            s_w = _dot_nt(kwa_ref[pl.ds(w0, n_win), :], qa_s[0])
            s_next = slab_scores(0)
            prepare(jnp.minimum(p + 1, n_iter - 1))

            own = (key_d // qb) == qi_row
            s_cu = jnp.where(own & (key_d % qb > tc_row % qb), NEG_INF, s_cu)
            m_run = jnp.max(s_cu, axis=0, keepdims=True)
            e_cu = jnp.exp2(s_cu - m_run)
            l_run = jnp.sum(e_cu, axis=0, keepdims=True)
            acc = _dot(vs3[p], e_cu.astype(BF16))
            for k in range(n_slab):
                s_k = s_next
                if k + 1 < n_slab:
                    s_next = slab_scores(k + 1)
                m_new = jnp.maximum(m_run, jnp.max(s_k, axis=0, keepdims=True))
                e_k = jnp.exp2(s_k - m_new)
                alpha = jnp.exp2(m_run - m_new)
                l_run = alpha * l_run + jnp.sum(e_k, axis=0, keepdims=True)
                acc = alpha * acc + _dot(vst_ref[0, :, k * NSA_SLAB_KEYS:(k + 1) * NSA_SLAB_KEYS], e_k.astype(BF16))
                m_run = m_new
            os_t = acc / l_run

            rel = (t0 - w0) + tc_row - key_w
            s_w = jnp.where((rel >= 0) & (rel < WINDOW), s_w, NEG_INF)
            e_w = jnp.exp2(s_w - jnp.max(s_w, axis=0, keepdims=True))
            e_wb = e_w.astype(BF16)
            b0 = w0 // span
            ow_t = _dot(vw3[b0], e_wb[0:span])
            for j in range(1, n_win // span):
                ow_t = ow_t + _dot(vw3[b0 + j], e_wb[j * span:(j + 1) * span])
            ow_t = ow_t / jnp.sum(e_w, axis=0, keepdims=True)

            o_all = jnp.concatenate([os_t, ow_t, oc_t, jnp.zeros_like(oc_t)], axis=0).T
            gate = _sigmoid(gl_ref[pl.ds(t0, span), 0:3 * r_n])
            for qi in range(nq):
                g_q = gate[qi * qb:(qi + 1) * qb]
                heads = []
                for r in range(r_n):
                    rs = slice((qi * r_n + r) * qb, (qi * r_n + r + 1) * qb)
                    heads.append(g_q[:, 3 * r:3 * r + 1] * o_all[rs, 2 * dh:3 * dh]
                                 + g_q[:, 3 * r + 1:3 * r + 2] * o_all[rs, :dh]
                                 + g_q[:, 3 * r + 2:3 * r + 3] * o_all[rs, dh:2 * dh])
                o_ref[pl.ds(t0 + qi * qb, qb), :] = jnp.concatenate(heads, axis=1).astype(o_ref.dtype)
            return carry
        return body

    prepare(jnp.int32(0))
    for sb in range(seq // NSA_SLAB_KEYS):
        lax.fori_loop(sb * it_per_slab, (sb + 1) * it_per_slab, make_body((sb + 1) * NSA_SLAB_KEYS), 0)


def _nsa(q, ksa, kwa, vst, vwt, kc, vc, gl, bsz, seq, pe_k, pe_v, k_w1, k_b1, k_w2, v_w1, v_b1, v_w2):
    g_n, r_n, dh, qb = NSA_KV_GROUPS, NSA_GROUP_SIZE, NSA_HEAD_DIM, NSA_Q_BLOCK
    t = bsz * seq
    n_cmp_pad = seq // CMP_STRIDE
    n_blk = seq // SLC_BLOCK
    span = NSA_QB_PER_ITER * qb
    assert n_blk <= F_POS and seq % NSA_SLAB_KEYS == 0 and seq >= WINDOW + span

    def cmp_rows(x):
        return (x.reshape(bsz, n_cmp_pad, CMP_STRIDE, g_n, dh).transpose(0, 3, 1, 2, 4)
                .reshape(bsz, g_n, n_cmp_pad, CMP_STRIDE * dh))

    slopes = np.asarray([2.0 ** (-8.0 * (h + 1) / NSA_HEADS) for h in range(NSA_HEADS)], dtype=np.float32)
    cols = NSA_QB_PER_ITER * r_n * qb
    slope_rows = jnp.asarray(np.tile(np.repeat(slopes.reshape(g_n, r_n), qb, axis=1), (1, NSA_QB_PER_ITER))
                             .reshape(g_n, cols, 1)) * LOG2E
    s3 = jnp.concatenate(_split3(slope_rows) * 2, axis=2)
    sfeat = jnp.pad(s3, ((0, 0), (0, 0), (F_POS, NSA_FEAT - F_POS - 6)))
    cf = np.zeros((n_cmp_pad, NSA_FEAT), np.float32)
    cf[:, F_POS:F_POS + 3] = (np.arange(n_cmp_pad) * CMP_STRIDE)[:, None]
    cf[:, F_POS + 3:F_POS + 6] = (CMP_LEN - 1) * 0.5
    cf[:, F_ONE:F_ONE + 3] = 1.0
    tok = np.arange(n_cmp_pad)[:, None] * CMP_STRIDE + np.arange(CMP_LEN)[None, :]
    overlap_t = ((tok // SLC_BLOCK)[:, :, None] == np.arange(NSA_FEAT)[None, None, :]).mean(axis=1).astype(np.float32).T
    cmp_spec = pl.BlockSpec((1, 1, n_cmp_pad, CMP_STRIDE * dh), lambda b, g: (b, g, 0, 0))
    q_spec = pl.BlockSpec((seq, r_n * dh), lambda b, g: (b, g))
    k_spec = pl.BlockSpec((seq, LANES), lambda b, g: (b, g))
    v_spec = pl.BlockSpec((1, dh, seq), lambda b, g: (b, g, 0))
    w1s, b1s, w2s = (CMP_LEN * dh, CMP_HIDDEN), (1, CMP_HIDDEN), (CMP_HIDDEN, dh)
    return pl.pallas_call(
        functools.partial(_nsa_kernel, seq=seq),
        out_shape=jax.ShapeDtypeStruct((t, NSA_HEADS * dh), BF16),
        grid=(bsz, g_n),
        in_specs=[q_spec, k_spec, k_spec, v_spec, v_spec, cmp_spec, cmp_spec, k_spec,
                  pl.BlockSpec((1, cols, 1), lambda b, g: (g, 0, 0)),
                  pl.BlockSpec((1, cols, NSA_FEAT), lambda b, g: (g, 0, 0)),
                  _const_spec((n_cmp_pad, NSA_FEAT)),
                  _const_spec((1, CMP_LEN * dh)), _const_spec((1, CMP_LEN * dh)),
                  _const_spec(w1s), _const_spec(b1s), _const_spec(w2s),
                  _const_spec(w1s), _const_spec(b1s), _const_spec((dh, CMP_HIDDEN)),
                  _const_spec((NSA_FEAT, n_cmp_pad))],
        out_specs=q_spec,
        scratch_shapes=[pltpu.VMEM((n_cmp_pad, dh + NSA_FEAT), BF16), pltpu.VMEM((dh, n_cmp_pad), BF16),
                        pltpu.VMEM((seq // span, dh, span), BF16), pltpu.VMEM((seq // span, dh, span), BF16),
                        pltpu.VMEM((3, cols, dh + NSA_FEAT), BF16), pltpu.VMEM((dh, cols), F32)],
        compiler_params=_cparams(2),
        name="nsa_attention",
    )(q, ksa, kwa, vst, vwt, cmp_rows(kc), cmp_rows(vc), gl, slope_rows, sfeat, jnp.asarray(cf),
      pe_k.astype(F32).reshape(1, -1), pe_v.astype(F32).reshape(1, -1),
      k_w1.astype(BF16), k_b1.astype(F32).reshape(1, -1), k_w2.astype(BF16),
      v_w1.astype(BF16), v_b1.astype(F32).reshape(1, -1), v_w2.astype(BF16).T, jnp.asarray(overlap_t, dtype=BF16))


def _pad_cols(w, n):
    return jnp.pad(w, ((0, 0), (0, n - w.shape[1])))


def kernel(x, ab_w_in, ab_w_out, s5_lambda_re, s5_lambda_im, s5_log_step, s5_b_re, s5_b_im, s5_c_re, s5_c_im, s5_d, s5_w_glu, s5_b_glu, gdn_conv_w, gdn_a_log, gdn_dt_bias, gdn_norm_w, nsa_w_in, nsa_w_out, nsa_pe_k, nsa_pe_v, nsa_k_w1, nsa_k_b1, nsa_k_w2, nsa_v_w1, nsa_v_b1, nsa_v_w2, ffn_w_in, ffn_conv_w, ffn_conv_b, ffn_w_out, norm_mix, norm_ffn, norm_final):
    bsz, seq, d = x.shape
    depth = ffn_w_in.shape[0]
    h = x.astype(F32).reshape(bsz * seq, d)
    for layer in range(depth):
        i = layer // 2
        if layer % 2 == 0:
            w = ab_w_in[i].astype(BF16)
            cuts = np.cumsum([0, S5_WIDTH, GDN_WIDTH, GDN_WIDTH, GDN_WIDTH, GDN_WIDTH])
            ws = [w[:, cuts[j]:cuts[j + 1]] for j in range(5)] + [_pad_cols(w[:, cuts[5]:], LANES)]
            u, q, k, v, z, ba = _norm_proj(h, norm_mix[layer], ws)
            prep = _s5_prep(s5_lambda_re[i], s5_lambda_im[i], s5_log_step[i], s5_b_re[i], s5_b_im[i],
                            s5_c_re[i], s5_c_im[i], s5_d[i])
            y_a = _s5(u, seq, prep, s5_w_glu[i], s5_b_glu[i])
            y_b = _gdn(q, k, v, z, ba, seq, gdn_conv_w[i], gdn_a_log[i], gdn_dt_bias[i], gdn_norm_w[i])
            wo = ab_w_out[i].astype(BF16)
            mix, wms = [y_a, y_b], [wo[:S5_WIDTH], wo[S5_WIDTH:]]
        else:
            q, ksa, kwa, vst, vwt, kc, vc, gl = _nsa_proj(h, norm_mix[layer], nsa_w_in[i], bsz, seq)
            o = _nsa(q, ksa, kwa, vst, vwt, kc, vc, gl, bsz, seq, nsa_pe_k[i], nsa_pe_v[i], nsa_k_w1[i],
                     nsa_k_b1[i], nsa_k_w2[i], nsa_v_w1[i], nsa_v_b1[i], nsa_v_w2[i])
            mix, wms = [o], [nsa_w_out[i].astype(BF16)]
        h = _mix_ffn(h, mix, wms, seq, norm_ffn[layer], ffn_w_in[layer], ffn_conv_w[layer], ffn_conv_b[layer],
                     ffn_w_out[layer], norm_final, final=(layer == depth - 1))
    return h.reshape(bsz, seq, d).astype(x.dtype)
```

```python
import functools
import math

import numpy as np
import jax
import jax.numpy as jnp
from jax import lax
from jax.experimental import pallas as pl
from jax.experimental.pallas import tpu as pltpu

F32 = jnp.float32
BF16 = jnp.bfloat16
HI = lax.Precision.HIGHEST

D_MODEL = 1024
S5_WIDTH = 256
S5_GROUP = 16
S5_GROUPS = 16
S5_STATE = 64
S5_CHUNK = 16
GDN_HEAD_DIM = 128
GDN_HEADS = 6
GDN_WIDTH = 768
GDN_CONV = 4
GDN_CHUNK = 64
GDN_HEADS_PER_STEP = 2
GDN_CHUNKS_PER_ITER = 8
NSA_HEADS = 16
NSA_HEAD_DIM = 64
NSA_KV_GROUPS = 4
NSA_GROUP_SIZE = 4
NSA_KV_WIDTH = 256
CMP_LEN = 32
CMP_STRIDE = 16
CMP_HIDDEN = 256
SLC_BLOCK = 64
N_SELECT = 4
WINDOW = 256
NSA_Q_BLOCK = 64
NSA_SLAB_KEYS = 512
NSA_QB_PER_ITER = 2
LOG2E = math.log2(math.e)
FFN_HIDDEN = 2816
FFN_CONV = 3
FFN_COL_CHUNK = 256
RMS_EPS = 1e-6
NEG_INF = -1e30
LANES = 128
BF16_SUBLANES = 16
VMEM_LIMIT = 56 * 1024 * 1024
ROW_TILE = 512


def _cparams(n_axes):
    return pltpu.CompilerParams(dimension_semantics=("arbitrary",) * n_axes,
                                vmem_limit_bytes=VMEM_LIMIT)


def _rms(x, g):
    return x * lax.rsqrt(jnp.mean(x * x, axis=-1, keepdims=True) + RMS_EPS) * g


def _gelu(x):
    return 0.5 * x * (1.0 + jnp.tanh(math.sqrt(2.0 / math.pi) * (x + 0.044715 * (x * x * x))))


def _sigmoid(x):
    return 1.0 / (1.0 + jnp.exp(-x))


def _silu(x):
    return x * _sigmoid(x)


def _dot(a, b):
    return jnp.dot(a, b, preferred_element_type=F32)


def _dot_nt(a, b):
    return lax.dot_general(a, b, (((1,), (1,)), ((), ())), preferred_element_type=F32)


def _dot_tn(a, b):
    return lax.dot_general(a, b, (((0,), (0,)), ((), ())), preferred_element_type=F32)


def _const_spec(shape):
    nd = len(shape)
    return pl.BlockSpec(shape, lambda *_: (0,) * nd)


def _ab_proj_kernel(x_ref, g_ref, wu_ref, wq_ref, wk_ref, wv_ref, wz_ref, wba_ref, cw_ref,
                    u_ref, q_ref, k_ref, v_ref, z_ref, ba_ref, ext_ref, *, tiles_per_seq):
    rows = x_ref.shape[0]
    pad = 8
    dh = GDN_HEAD_DIM
    xn = _rms(x_ref[...], g_ref[...]).astype(BF16)
    seq_start = pl.program_id(0) % tiles_per_seq == 0

    @pl.when(pl.program_id(0) == 0)
    def _():
        ext_ref[...] = jnp.zeros_like(ext_ref)

    def finish(idx, pre, o_ref):
        prev = jnp.where(seq_start, 0.0, ext_ref[idx, rows:rows + pad, :])
        ext_ref[idx, 0:pad, :] = prev
        ext_ref[idx, pad:, :] = pre
        cw = cw_ref[idx]
        y = pre * cw[GDN_CONV - 1:GDN_CONV]
        for sh in range(1, GDN_CONV):
            y = y + ext_ref[idx, pad - sh:pad - sh + rows, :] * cw[GDN_CONV - 1 - sh:GDN_CONV - sh]
        y = _silu(y)
        if idx < 2:
            parts = []
            for j in range(GDN_HEADS):
                p = y[:, j * dh:(j + 1) * dh]
                p = p * lax.rsqrt(jnp.sum(p * p, axis=-1, keepdims=True) + 1e-6)
                parts.append(p * (dh ** -0.5) if idx == 0 else p)
            y = jnp.concatenate(parts, axis=1)
        o_ref[...] = y

    pre_q = _dot(xn, wq_ref[...])
    pre_k = _dot(xn, wk_ref[...])
    finish(0, pre_q, q_ref)
    pre_v = _dot(xn, wv_ref[...])
    finish(1, pre_k, k_ref)
    z_ref[...] = _dot(xn, wz_ref[...])
    finish(2, pre_v, v_ref)
    u_ref[...] = _dot(xn, wu_ref[...])
    ba_ref[...] = _dot(xn, wba_ref[...])


def _ab_proj(h, gain, w_in, conv_w, seq):
    t, d = h.shape
    w = w_in.astype(BF16)
    cuts = np.cumsum([0, S5_WIDTH, GDN_WIDTH, GDN_WIDTH, GDN_WIDTH, GDN_WIDTH])
    ws = [w[:, cuts[j]:cuts[j + 1]] for j in range(5)] + [_pad_cols(w[:, cuts[5]:], LANES)]
    cw = conv_w.astype(F32).reshape(GDN_CONV, 3, GDN_WIDTH).transpose(1, 0, 2)
    row_spec = lambda n: pl.BlockSpec((ROW_TILE, n), lambda i: (i, 0))
    return pl.pallas_call(
        functools.partial(_ab_proj_kernel, tiles_per_seq=seq // ROW_TILE),
        out_shape=[jax.ShapeDtypeStruct((t, x.shape[1]), F32) for x in ws],
        grid=(t // ROW_TILE,),
        in_specs=[row_spec(d), _const_spec((1, d))] + [_const_spec(x.shape) for x in ws] + [_const_spec(cw.shape)],
        out_specs=[row_spec(x.shape[1]) for x in ws],
        scratch_shapes=[pltpu.VMEM((3, ROW_TILE + 8, GDN_WIDTH), F32)],
        compiler_params=_cparams(1),
        name="ab_proj",
    )(h, gain.reshape(1, d), *ws, cw)


def _ffn_kernel(h_ref, hp_ref, *refs, n_mix, tiles_per_seq, n_chunks, final):
    y_refs, yp_refs, wm_refs = refs[:n_mix], refs[n_mix:2 * n_mix], refs[2 * n_mix:3 * n_mix]
    (g_ref, win_ref, cw_ref, cb_ref, wout_ref, gf_ref, o_ref,
     xn_ref, up0_ref, up1_ref, act_ref, x1_ref) = refs[3 * n_mix:]
    halo = BF16_SUBLANES
    rows = h_ref.shape[0]
    sub = 128
    g = g_ref[...]
    n_piece = rows // sub
    kw = xn_ref.shape[1] // n_piece
    x1, x1p = h_ref[...], hp_ref[...]
    for y_ref, yp_ref, wm_ref in zip(y_refs, yp_refs, wm_refs):
        x1 = x1 + _dot(y_ref[...], wm_ref[...])
        x1p = x1p + _dot(yp_ref[...], wm_ref[...])
    x1_ref[...] = x1
    seq_start = pl.program_id(0) % tiles_per_seq == 0
    xn_ref[0:halo, :] = jnp.where(seq_start, 0.0, _rms(x1p, g)).astype(BF16)
    xn_ref[halo:, :] = _rms(x1, g).astype(BF16)

    def gate_piece(j, buf, r):
        r0 = r * sub

        def conv(half, cw, cb):
            y = cb
            for k in range(FFN_CONV):
                start = halo + r0 - (FFN_CONV - 1) + k
                y = y + buf[half, start:start + sub, :] * cw[k:k + 1]
            return y
        a = conv(0, cw_ref[j], cb_ref[j])
        b = conv(1, cw_ref[n_chunks + j], cb_ref[n_chunks + j])
        act_ref[j, r0:r0 + sub, :] = (_silu(a) * b).astype(BF16)

    def up_and_gate(j_up, buf_up, j_gate, buf_gate):
        xe = xn_ref[...]
        for half in range(2):
            buf_up[half] = _dot(xe, win_ref[half * n_chunks + j_up])
            if j_gate is not None:
                for r in range(half * n_piece // 2, (half + 1) * n_piece // 2):
                    gate_piece(j_gate, buf_gate, r)

    def body(i, carry):
        up_and_gate(2 * i + 1, up1_ref, 2 * i, up0_ref)
        up_and_gate(2 * i + 2, up0_ref, 2 * i + 1, up1_ref)
        return carry

    up_and_gate(0, up0_ref, None, None)
    lax.fori_loop(0, (n_chunks - 1) // 2, body, 0)
    for r in range(n_piece):
        gate_piece(n_chunks - 1, up0_ref, r)

    out = x1_ref[...]
    for j in range(n_chunks):
        out = out + _dot(act_ref[j], wout_ref[j])
    if final:
        out = _rms(out, gf_ref[...])
    o_ref[...] = out


def _mix_ffn(h, ys, wms, seq, gain, w_in, conv_w, conv_b, w_out, gain_final, final):
    t, d = h.shape
    n_mix = len(ys)
    fc = FFN_COL_CHUNK
    n_chunks = FFN_HIDDEN // fc
    assert n_chunks % 2 == 1
    halo = BF16_SUBLANES
    win = w_in.astype(BF16).reshape(d, 2 * n_chunks, fc).transpose(1, 0, 2)
    cw = conv_w.astype(F32).reshape(FFN_CONV, 2 * n_chunks, fc).transpose(1, 0, 2)
    cb = conv_b.astype(F32).reshape(2 * n_chunks, 1, fc)
    wout = w_out.astype(BF16).reshape(n_chunks, fc, d)
    blocks_per_tile = ROW_TILE // halo
    tile_spec = lambda n: pl.BlockSpec((ROW_TILE, n), lambda i: (i, 0))
    prev_spec = lambda n: pl.BlockSpec((halo, n), lambda i: (jnp.maximum(i * blocks_per_tile - 1, 0), 0))
    return pl.pallas_call(
        functools.partial(_ffn_kernel, n_mix=n_mix, tiles_per_seq=seq // ROW_TILE, n_chunks=n_chunks, final=final),
        out_shape=jax.ShapeDtypeStruct((t, d), F32),
        grid=(t // ROW_TILE,),
        in_specs=[tile_spec(d), prev_spec(d)]
        + [tile_spec(y.shape[1]) for y in ys] + [prev_spec(y.shape[1]) for y in ys]
        + [_const_spec(w.shape) for w in wms]
        + [_const_spec((1, d)),
                  pl.BlockSpec(win.shape, lambda i: (0, 0, 0), pipeline_mode=pl.Buffered(1)),
                  _const_spec(cw.shape), _const_spec(cb.shape),
                  pl.BlockSpec(wout.shape, lambda i: (0, 0, 0), pipeline_mode=pl.Buffered(1)),
                  _const_spec((1, d))],
        out_specs=pl.BlockSpec((ROW_TILE, d), lambda i: (i, 0)),
        scratch_shapes=[pltpu.VMEM((ROW_TILE + halo, d), BF16), pltpu.VMEM((2, ROW_TILE + halo, fc), F32),
                        pltpu.VMEM((2, ROW_TILE + halo, fc), F32), pltpu.VMEM((n_chunks, ROW_TILE, fc), BF16),
                        pltpu.VMEM((ROW_TILE, d), F32)],
        compiler_params=_cparams(1),
        name="mix_ffn",
    )(h, h, *ys, *ys, *wms, gain.reshape(1, d), win, cw, cb, wout, gain_final.reshape(1, d))


def _s5_prep(lam_re, lam_im, log_step, b_re, b_im, c_re, c_im, d_skip):
    g_n, p_n, h_n, L = S5_GROUPS, S5_STATE, S5_GROUP, S5_CHUNK
    step = jnp.exp(log_step.astype(F32))[:, None]
    lr, li = lam_re.astype(F32), lam_im.astype(F32)
    mag = jnp.exp(lr * step)
    a_re = mag * jnp.cos(li * step)
    a_im = mag * jnp.sin(li * step)
    den = lr * lr + li * li
    n_re, n_im = a_re - 1.0, a_im
    z_re = (n_re * lr + n_im * li) / den
    z_im = (n_im * lr - n_re * li) / den
    b_re, b_im = b_re.astype(F32), b_im.astype(F32)
    bb_re = z_re[..., None] * b_re - z_im[..., None] * b_im
    bb_im = z_re[..., None] * b_im + z_im[..., None] * b_re
    c_re, c_im = c_re.astype(F32), c_im.astype(F32)
    pw_re, pw_im = [jnp.ones_like(a_re)], [jnp.zeros_like(a_im)]
    for _ in range(L):
        pr, pi = pw_re[-1], pw_im[-1]
        pw_re.append(pr * a_re - pi * a_im)
        pw_im.append(pr * a_im + pi * a_re)
    eye_g = jnp.eye(g_n, dtype=F32)
    ks = []
    for j in range(L):
        ab_re = pw_re[j][..., None] * bb_re - pw_im[j][..., None] * bb_im
        ab_im = pw_re[j][..., None] * bb_im + pw_im[j][..., None] * bb_re
        kj = (jnp.einsum('gop,gpi->gio', c_re, ab_re, precision=HI)
              - jnp.einsum('gop,gpi->gio', c_im, ab_im, precision=HI))
        ks.append(jnp.einsum('gio,gk->giko', kj, eye_g).reshape(g_n * h_n, g_n * h_n))
    kstack = jnp.concatenate(ks[::-1], axis=0).astype(BF16)
    bb = jnp.stack([jnp.einsum('gph,gk->ghkp', bb_re, eye_g), jnp.einsum('gph,gk->ghkp', bb_im, eye_g)], axis=2)
    bb = bb.reshape(g_n * h_n, 2 * g_n * p_n).astype(BF16)
    cc = jnp.stack([jnp.einsum('ghp,gk->gpkh', c_re, eye_g), -jnp.einsum('ghp,gk->gpkh', c_im, eye_g)], axis=0)
    cc = cc.reshape(2 * g_n * p_n, g_n * h_n).astype(BF16)
    apow = jnp.stack([jnp.concatenate([r.reshape(-1), i.reshape(-1)]) for r, i in zip(pw_re, pw_im)], axis=0)
    dvec = jnp.tile(d_skip.astype(F32).reshape(1, g_n * h_n), (1, L))
    return kstack, bb, cc, apow, dvec


def _s5_kernel(u_ref, ks_ref, bb_ref, cc_ref, ap_ref, dv_ref, wg_ref, bg_ref, o_ref, x_scr, hp_scr,
               *, n_batch, n_chunk):
    L, w = S5_CHUNK, S5_WIDTH
    half = S5_GROUPS * S5_STATE
    rows = n_batch * n_chunk
    u2 = u_ref[...].reshape(rows, L * w)
    u2b = u2.astype(BF16)

    def cmul(j, x):
        ar, ai = ap_ref[j:j + 1, :half], ap_ref[j:j + 1, half:]
        xr, xi = x[:, :half], x[:, half:]
        return jnp.concatenate([ar * xr - ai * xi, ar * xi + ai * xr], axis=1)

    xin = _dot(u2b[:, (L - 1) * w:], bb_ref[...])
    for s in range(L - 1):
        xin = xin + cmul(L - 1 - s, _dot(u2b[:, s * w:(s + 1) * w], bb_ref[...]))
    x_scr[...] = xin

    a_l = ap_ref[L:L + 1, :]
    alr, ali = a_l[:, :half], a_l[:, half:]

    def scan_body(k, hs):
        new = []
        for b in range(n_batch):
            r = b * n_chunk + k
            h = hs[b]
            hp_scr[pl.ds(r, 1), :] = h
            hr, hi = h[:, :half], h[:, half:]
            new.append(jnp.concatenate([alr * hr - ali * hi, alr * hi + ali * hr], axis=1)
                       + x_scr[pl.ds(r, 1), :])
        return tuple(new)

    lax.fori_loop(0, n_chunk, scan_body, tuple(jnp.zeros((1, 2 * half), F32) for _ in range(n_batch)))
    hp = hp_scr[...]

    for t in range(L):
        y = _dot(u2b[:, :(t + 1) * w], ks_ref[(L - 1 - t) * w:, :])
        y = y + _dot(cmul(t + 1, hp).astype(BF16), cc_ref[...])
        y = y + dv_ref[:, t * w:(t + 1) * w] * u2[:, t * w:(t + 1) * w]
        z = _gelu(y)
        gate = _sigmoid(_dot(z.astype(BF16), wg_ref[...]) + bg_ref[...])
        o_ref[:, :, t * w:(t + 1) * w] = (z * gate).astype(o_ref.dtype).reshape(n_batch, n_chunk, w)


def _s5(u, seq, prep, w_glu, b_glu):
    t = u.shape[0]
    bsz = t // seq
    L, w = S5_CHUNK, S5_WIDTH
    n_chunk = seq // L
    n_batch = 2
    kstack, bb, cc, apow, dvec = prep
    u3 = u.reshape(bsz, n_chunk, L * w)
    out = pl.pallas_call(
        functools.partial(_s5_kernel, n_batch=n_batch, n_chunk=n_chunk),
        out_shape=jax.ShapeDtypeStruct((bsz, n_chunk, L * w), BF16),
        grid=(bsz // n_batch,),
        in_specs=[pl.BlockSpec((n_batch, n_chunk, L * w), lambda i: (i, 0, 0)),
                  _const_spec(kstack.shape), _const_spec(bb.shape), _const_spec(cc.shape),
                  _const_spec(apow.shape), _const_spec(dvec.shape),
                  _const_spec((w, w)), _const_spec((1, w))],
        out_specs=pl.BlockSpec((n_batch, n_chunk, L * w), lambda i: (i, 0, 0)),
        scratch_shapes=[pltpu.VMEM((n_batch * n_chunk, 2 * S5_GROUPS * S5_STATE), F32),
                        pltpu.VMEM((n_batch * n_chunk, 2 * S5_GROUPS * S5_STATE), F32)],
        compiler_params=_cparams(1),
        name="s5_mixer",
    )(u3, kstack, bb, cc, apow, dvec, w_glu.astype(BF16), b_glu.astype(F32).reshape(1, w))
    return out.reshape(t, w)


def _gdn_kernel(qs, ks, vs, z_ref, bc_ref, ac_ref, ar_ref, hp_ref, hr_ref, nw_ref,
                o_ref, xs, ns, qe, o0, gls, gcc_s, gcr_s, xs2, ns2, gls2, *, seq, n_head):
    c, dh = GDN_CHUNK, GDN_HEAD_DIM
    n_chunk = seq // c

    hp = hp_ref[0]
    hr = hr_ref[0]

    def softplus(x):
        return jnp.maximum(x, 0.0) + jnp.log(1.0 + jnp.exp(-jnp.abs(x)))

    ii = lax.broadcasted_iota(jnp.int32, (c, c), 0)
    jj = lax.broadcasted_iota(jnp.int32, (c, c), 1)
    eye = (ii == jj).astype(F32)
    g_col = -jnp.exp(hp[0:1, :]) * softplus(ac_ref[0, 0] + hp[1:2, :])
    g_row = -jnp.exp(hr[:, 0:1]) * softplus(ar_ref[0, 0] + hr[:, 1:2])
    gc_col = jnp.dot((ii >= jj).astype(F32), g_col, precision=HI, preferred_element_type=F32)
    gcr_s[...] = jnp.dot(g_row, (ii <= jj).astype(F32), precision=HI, preferred_element_type=F32)
    for ci in range(n_chunk):
        gcc_s[ci * c:(ci + 1) * c, :] = gc_col[:, ci * n_head:(ci + 1) * n_head]

    def local_body(it, carry):
        pr = []
        for cc in range(GDN_CHUNKS_PER_ITER):
            ci = it * GDN_CHUNKS_PER_ITER + cc
            rs = pl.ds(pl.multiple_of(ci * c, c), c)
            beta = _sigmoid(bc_ref[0, 0, rs, :])
            gccs = gcc_s[rs, :]
            for j in range(n_head):
                sl = slice(j * dh, (j + 1) * dh)
                gcc = gccs[:, j:j + 1]
                gcr = gcr_s[pl.ds(ci * n_head + j, 1), :]
                q, k, v = qs[rs, sl], ks[rs, sl], vs[rs, sl]
                bj = beta[:, j:j + 1]
                kb = k * bj
                e_g = jnp.exp(gcc)
                g_last = gcc[c - 1:c, :]
                pr.append(dict(ci=ci, j=j, q=q, kf=k.astype(BF16), kbb=kb.astype(BF16),
                               decay=jnp.exp(jnp.where(ii >= jj, gcc - gcr, NEG_INF)),
                               rhs=jnp.concatenate([v * bj, kb * e_g], axis=1).astype(BF16),
                               qd=q * e_g, kd=(k * jnp.exp(g_last - gcc)).astype(BF16),
                               gl=jnp.broadcast_to(jnp.exp(g_last), (8, dh))))
        kk = [_dot_nt(p["kbb"], p["kf"]) for p in pr]
        qk = [_dot_nt(p["q"].astype(BF16), p["kf"]) for p in pr]
        ps = [-jnp.where(ii > jj, a * p["decay"], 0.0) for a, p in zip(kk, pr)]
        ts = [eye + n for n in ps]
        span = 2
        while span < c:
            ps = [_dot(n.astype(BF16), n.astype(BF16)) for n in ps]
            ts = [t + _dot(t.astype(BF16), n.astype(BF16)) for t, n in zip(ts, ps)]
            span *= 2
        uws = [_dot(t.astype(BF16), p["rhs"]).astype(BF16) for t, p in zip(ts, pr)]
        intras = [(s * p["decay"]).astype(BF16) for s, p in zip(qk, pr)]
        i_uws = [_dot(a, uw) for a, uw in zip(intras, uws)]
        kd_uws = [_dot_tn(p["kd"], uw) for p, uw in zip(pr, uws)]
        for p, i_uw, kd_uw in zip(pr, i_uws, kd_uws):
            j, ci = p["j"], p["ci"]
            ns[j, ci] = kd_uw[:, :dh]
            xs[j, ci] = kd_uw[:, dh:].astype(BF16)
            o0[j, ci] = i_uw[:, :dh]
            qe[j, ci] = (p["qd"] - i_uw[:, dh:]).astype(BF16)
            gls[j, ci] = p["gl"]
        first = [idx for idx, p in enumerate(pr) if (idx // n_head) % 2 == 0]
        x1b = [kd_uws[idx + n_head][:, dh:].astype(BF16) for idx in first]
        x1x0 = [_dot(x1, kd_uws[idx][:, dh:].astype(BF16)) for x1, idx in zip(x1b, first)]
        x1n0 = [_dot(x1, kd_uws[idx][:, :dh].astype(BF16)) for x1, idx in zip(x1b, first)]
        for idx, xx, xn in zip(first, x1x0, x1n0):
            p0, p1 = pr[idx], pr[idx + n_head]
            g0, g1 = p0["gl"][0:1, :], p1["gl"][0:1, :]
            pi = it * (GDN_CHUNKS_PER_ITER // 2) + (idx // n_head) // 2
            xs2[p0["j"], pi] = (g1 * kd_uws[idx][:, dh:] + g0 * kd_uws[idx + n_head][:, dh:] - xx).astype(BF16)
            ns2[p0["j"], pi] = g1 * kd_uws[idx][:, :dh] - xn + kd_uws[idx + n_head][:, :dh]
            gls2[p0["j"], pi] = p0["gl"] * p1["gl"]
        return carry

    lax.fori_loop(0, n_chunk // GDN_CHUNKS_PER_ITER, local_body, 0)

    def rec_body(pi, states):
        c0, c1 = 2 * pi, 2 * pi + 1
        s_bs = [s.astype(BF16) for s in states]
        xd2 = [_dot(xs2[j, pi], s_bs[j]) for j in range(n_head)]
        xd0 = [_dot(xs[j, c0], s_bs[j]) for j in range(n_head)]
        od0 = [_dot(qe[j, c0], s_bs[j]) for j in range(n_head)]
        new = tuple(gls2[j, pi][0:1, :] * states[j] - xd2[j] + ns2[j, pi] for j in range(n_head))
        mid = [(gls[j, c0][0:1, :] * states[j] - xd0[j] + ns[j, c0]).astype(BF16) for j in range(n_head)]
        od1 = [_dot(qe[j, c1], mid[j]) for j in range(n_head)]
        for ci, od in ((c0, od0), (c1, od1)):
            rs = pl.ds(pl.multiple_of(ci * c, c), c)
            for j in range(n_head):
                sl = slice(j * dh, (j + 1) * dh)
                o = od[j] + o0[j, ci]
                on = o * lax.rsqrt(jnp.mean(o * o, axis=-1, keepdims=True) + RMS_EPS) * nw_ref[...]
                o_ref[rs, sl] = (on * _silu(z_ref[rs, sl])).astype(o_ref.dtype)
        return new

    lax.fori_loop(0, n_chunk // 2, rec_body, tuple(jnp.zeros((dh, dh), F32) for _ in range(n_head)))


def _gdn(q, k, v, z, ba, seq, a_log, dt_bias, norm_w):
    t = q.shape[0]
    bsz = t // seq
    nh = GDN_HEADS_PER_STEP
    n_grp = GDN_HEADS // nh
    wd = nh * GDN_HEAD_DIM
    bl = ba[:, :GDN_HEADS].reshape(bsz, seq, n_grp, nh)
    al = ba[:, GDN_HEADS:2 * GDN_HEADS].reshape(bsz, seq, n_grp, nh)
    b_col = bl.transpose(0, 2, 1, 3)
    n_chunk = seq // GDN_CHUNK
    n_prob = n_chunk * nh
    a5 = al.reshape(bsz, n_chunk, GDN_CHUNK, n_grp, nh)
    a_col = a5.transpose(0, 3, 2, 1, 4).reshape(bsz, n_grp, GDN_CHUNK, n_prob)
    a_row = a5.transpose(0, 3, 1, 4, 2).reshape(bsz, n_grp, n_prob, GDN_CHUNK)
    hp = jnp.stack([jnp.tile(a_log.astype(F32).reshape(n_grp, nh), (1, n_chunk)),
                    jnp.tile(dt_bias.astype(F32).reshape(n_grp, nh), (1, n_chunk))], axis=1)
    hr = hp.transpose(0, 2, 1)
    act_spec = pl.BlockSpec((seq, wd), lambda b, g: (b, g))
    col_spec = pl.BlockSpec((1, 1, seq, nh), lambda b, g: (b, g, 0, 0))
    return pl.pallas_call(
        functools.partial(_gdn_kernel, seq=seq, n_head=nh),
        out_shape=jax.ShapeDtypeStruct((t, GDN_WIDTH), BF16),
        grid=(bsz, n_grp),
        in_specs=[act_spec, act_spec, act_spec, act_spec, col_spec,
                  pl.BlockSpec((1, 1, GDN_CHUNK, n_prob), lambda b, g: (b, g, 0, 0)),
                  pl.BlockSpec((1, 1, n_prob, GDN_CHUNK), lambda b, g: (b, g, 0, 0)),
                  pl.BlockSpec((1, 2, n_prob), lambda b, g: (g, 0, 0)),
                  pl.BlockSpec((1, n_prob, 2), lambda b, g: (g, 0, 0)),
                  _const_spec((1, GDN_HEAD_DIM))],
        out_specs=act_spec,
        scratch_shapes=[pltpu.VMEM((nh, n_chunk, GDN_HEAD_DIM, GDN_HEAD_DIM), BF16),
                        pltpu.VMEM((nh, n_chunk, GDN_HEAD_DIM, GDN_HEAD_DIM), F32),
                        pltpu.VMEM((nh, n_chunk, GDN_CHUNK, GDN_HEAD_DIM), BF16),
                        pltpu.VMEM((nh, n_chunk, GDN_CHUNK, GDN_HEAD_DIM), F32),
                        pltpu.VMEM((nh, n_chunk, 8, GDN_HEAD_DIM), F32),
                        pltpu.VMEM((seq, nh), F32), pltpu.VMEM((n_prob, GDN_CHUNK), F32),
                        pltpu.VMEM((nh, n_chunk // 2, GDN_HEAD_DIM, GDN_HEAD_DIM), BF16),
                        pltpu.VMEM((nh, n_chunk // 2, GDN_HEAD_DIM, GDN_HEAD_DIM), F32),
                        pltpu.VMEM((nh, n_chunk // 2, 8, GDN_HEAD_DIM), F32)],
        compiler_params=_cparams(2),
        name="gated_deltanet",
    )(q, k, v, z, b_col, a_col, a_row, hp, hr, norm_w.astype(F32).reshape(1, GDN_HEAD_DIM))


NSA_FEAT = 64
F_BLK, F_POS, F_ONE = 0, 32, 38


def _split3(x):
    x1 = x.astype(BF16).astype(F32)
    x2 = (x - x1).astype(BF16).astype(F32)
    return x1, x2, x - x1 - x2


def _nsa_proj_kernel(x_ref, g_ref, wq_ref, wks_ref, wkw_ref, wvs_ref, wvw_ref, wkc_ref, wvc_ref, wg_ref, kf_ref,
                     q_ref, ksa_ref, kwa_ref, vst_ref, vwt_ref, kc_ref, vc_ref, gl_ref):
    xn = _rms(x_ref[...], g_ref[...]).astype(BF16)
    q_ref[...] = (_dot(xn, wq_ref[...]) * (NSA_HEAD_DIM ** -0.5 * LOG2E)).astype(BF16)
    kf = jnp.concatenate([kf_ref[...]] * NSA_KV_GROUPS, axis=1)
    ksa_ref[...] = (_dot(xn, wks_ref[...]) + kf).astype(BF16)
    kwa_ref[...] = (_dot(xn, wkw_ref[...]) + kf).astype(BF16)
    vst_ref[0] = _dot_nt(wvs_ref[...], xn).astype(BF16)
    vwt_ref[0] = _dot_nt(wvw_ref[...], xn).astype(BF16)
    kc_ref[...] = _dot(xn, wkc_ref[...])
    vc_ref[...] = _dot(xn, wvc_ref[...])
    gl_ref[...] = _dot(xn, wg_ref[...])


def _nsa_proj(h, gain, w_in, bsz, seq):
    t, d = h.shape
    g_n, dh = NSA_KV_GROUPS, NSA_HEAD_DIM
    nq, kvw = NSA_HEADS * dh, NSA_KV_WIDTH
    w = w_in.astype(BF16)
    wq = w[:, :nq]
    wkc, wvc, wks, wvs, wkw, wvw = (w[:, nq + j * kvw: nq + (j + 1) * kvw] for j in range(6))
    wgl = w[:, nq + 6 * kvw:]

    def lane_padded(x, width):
        return jnp.pad(x.reshape(d, g_n, width), ((0, 0), (0, 0), (0, LANES - width))).reshape(d, g_n * LANES)

    pos = np.arange(seq)
    kf = np.zeros((seq, LANES), np.float32)
    kf[pos, dh + F_BLK + pos // SLC_BLOCK] = 1.0
    kf[:, dh + F_POS:dh + F_POS + 3] = ((pos // SLC_BLOCK) * SLC_BLOCK)[:, None]
    kf[:, dh + F_POS + 3:dh + F_POS + 6] = (pos % SLC_BLOCK)[:, None]
    kf[:, dh + F_ONE:dh + F_ONE + 3] = 1.0
    tps = seq // ROW_TILE
    row_spec = lambda n: pl.BlockSpec((ROW_TILE, n), lambda i: (i, 0))
    tr_spec = pl.BlockSpec((1, kvw, ROW_TILE), lambda i: (i // tps, 0, i % tps))
    ws = [wq, lane_padded(wks, dh), lane_padded(wkw, dh), wvs.T, wvw.T, wkc, wvc, lane_padded(wgl, 3 * NSA_GROUP_SIZE)]
    return pl.pallas_call(
        _nsa_proj_kernel,
        out_shape=[jax.ShapeDtypeStruct((t, nq), BF16),
                   jax.ShapeDtypeStruct((t, g_n * LANES), BF16), jax.ShapeDtypeStruct((t, g_n * LANES), BF16),
                   jax.ShapeDtypeStruct((bsz, kvw, seq), BF16), jax.ShapeDtypeStruct((bsz, kvw, seq), BF16),
                   jax.ShapeDtypeStruct((t, kvw), F32), jax.ShapeDtypeStruct((t, kvw), F32),
                   jax.ShapeDtypeStruct((t, g_n * LANES), F32)],
        grid=(t // ROW_TILE,),
        in_specs=[row_spec(d), _const_spec((1, d))] + [_const_spec(x.shape) for x in ws]
        + [pl.BlockSpec((ROW_TILE, LANES), lambda i: (i % tps, 0))],
        out_specs=[row_spec(nq), row_spec(g_n * LANES), row_spec(g_n * LANES), tr_spec, tr_spec,
                   row_spec(kvw), row_spec(kvw), row_spec(g_n * LANES)],
        compiler_params=_cparams(1),
        name="nsa_proj",
    )(h, gain.reshape(1, d), *ws, jnp.asarray(kf))


def _nsa_kernel(q_ref, ksa_ref, kwa_ref, vst_ref, vwt_ref, kca_ref, vca_ref, gl_ref, sl_ref, sf_ref, cf_ref,
                pek_ref, pev_ref, kw1_ref, kb1_ref, kw2_ref, vw1_ref, vb1_ref, vw2t_ref, ovt_ref, o_ref,
                kcmp, vcmpt, vs3, vw3, qa_s, oc_s, *, seq):
    qb, dh, r_n, nq = NSA_Q_BLOCK, NSA_HEAD_DIM, NSA_GROUP_SIZE, NSA_QB_PER_ITER
    span = nq * qb
    cols = nq * r_n * qb
    n_cmp_pad = seq // CMP_STRIDE
    n_cmp = (seq - CMP_LEN) // CMP_STRIDE + 1
    half = CMP_STRIDE * dh
    n_blk = seq // SLC_BLOCK
    n_win = WINDOW + span
    it_per_slab = NSA_SLAB_KEYS // span
    nf = NSA_FEAT

    def hidden(a_ref, pe_ref, w1_ref, b1_ref):
        a = a_ref[0, 0].astype(BF16)
        top = _dot(a, w1_ref[:half, :])
        bot = pltpu.roll(_dot(a, w1_ref[half:, :]), n_cmp_pad - 1, 0)
        const = _dot(pe_ref[...].astype(BF16), w1_ref[...]) + b1_ref[...]
        return _gelu(top + bot + const).astype(BF16)

    kcmp[...] = jnp.concatenate([_dot(hidden(kca_ref, pek_ref, kw1_ref, kb1_ref), kw2_ref[...]), cf_ref[...]],
                                axis=1).astype(BF16)
    vcmpt[...] = _dot_nt(vw2t_ref[...], hidden(vca_ref, pev_ref, vw1_ref, vb1_ref)).astype(BF16)
    for n in range(seq // span):
        vs3[n] = vst_ref[0, :, n * span:(n + 1) * span]
        vw3[n] = vwt_ref[0, :, n * span:(n + 1) * span]

    def iota(shape, axis):
        return lax.broadcasted_iota(jnp.int32, shape, axis)

    slope = sl_ref[0]
    colc, colr = iota((cols, 1), 0), iota((1, cols), 1)
    tc_col = (colc // (r_n * qb)) * qb + colc % qb
    tc_row = (colr // (r_n * qb)) * qb + colr % qb
    qi_row = colr // (r_n * qb)
    lane_f = iota((1, nf), 1)
    jr = iota((n_cmp_pad, 1), 0)
    nr = iota((nf, 1), 0)
    lane_s = iota((1, span), 1)
    qi_s = lane_s // qb
    key_d = iota((span, 1), 0)
    key_w = iota((n_win, 1), 0)
    eye_s = (iota((span, span), 0) == iota((span, span), 1)).astype(BF16)

    def prepare(p):
        t0 = pl.multiple_of(p * span, span)
        qf = jnp.concatenate([q_ref[pl.ds(t0 + qi * qb, qb), r * dh:(r + 1) * dh]
                              for qi in range(nq) for r in range(r_n)], axis=0).astype(F32)
        c1, c2, c3 = _split3(-(slope * (t0 + tc_col).astype(F32)))
        f_plain = jnp.where(lane_f == F_ONE, c1, jnp.where(lane_f == F_ONE + 1, c2,
                            jnp.where(lane_f == F_ONE + 2, c3, sf_ref[0])))
        qa_plain = jnp.concatenate([qf, f_plain], axis=1).astype(BF16)

        ok_c = (jr * CMP_STRIDE + (CMP_LEN - 1) <= t0 + tc_row) & (jr < n_cmp)
        s_c = jnp.where(ok_c, _dot_nt(kcmp[...], qa_plain), NEG_INF)
        e_c = jnp.where(ok_c, jnp.exp2(s_c - jnp.max(s_c, axis=0, keepdims=True)), 0.0)
        l_c = jnp.sum(e_c, axis=0, keepdims=True)
        p_c = e_c / jnp.where(l_c > 0.0, l_c, 1.0)
        oc_s[...] = _dot(vcmpt[...], p_c.astype(BF16))

        imp4 = sum(_dot(ovt_ref[...], part.astype(BF16)) for part in _split3(p_c))
        halves = []
        for qi in range(nq):
            a = imp4[:, (2 * qi) * LANES:(2 * qi + 1) * LANES] + imp4[:, (2 * qi + 1) * LANES:(2 * qi + 2) * LANES]
            halves.append(a + pltpu.roll(a, qb, 1))
        imp = jnp.where(qi_s == 0, halves[0], halves[1])

        iq = nq * p + qi_s
        cand = (nr > 0) & (nr < iq)
        sc = jnp.where(cand, imp, NEG_INF)
        sel = (nr == 0) & (iq > 0)
        for _ in range(N_SELECT - 2):
            best = jnp.max(sc, axis=0, keepdims=True)
            first = jnp.min(jnp.where(sc == best, nr, nf), axis=0, keepdims=True)
            pick = (nr == first) & cand
            sel = sel | pick
            sc = jnp.where(pick, NEG_INF, sc)
        neg_slab = jnp.where(sel & (nr < nq * p), 0.0, NEG_INF)
        neg_cur = jnp.where(nr == iq, 0.0, jnp.where((nr >= nq * p) & (nr < iq) & sel, 0.0, NEG_INF))
        neg_t = jnp.concatenate([neg_slab, neg_cur], axis=0).astype(BF16)
        neg = _dot_nt(eye_s, neg_t)
        rows_of = lambda x: jnp.concatenate([x[qi * qb:(qi + 1) * qb] for qi in range(nq) for _ in range(r_n)], axis=0)
        f_slab = jnp.where(lane_f < n_blk, rows_of(neg[:, :nf]), f_plain)
        f_cur = jnp.where(lane_f < n_blk, rows_of(neg[:, nf:]), f_plain)
        qa_s[0] = qa_plain
        qa_s[1] = jnp.concatenate([qf, f_slab], axis=1).astype(BF16)
        qa_s[2] = jnp.concatenate([qf, f_cur], axis=1).astype(BF16)

    n_iter = seq // span

    def make_body(n_keys):
        def body(p, carry):
            t0 = pl.multiple_of(p * span, span)
            w0 = pl.multiple_of(jnp.maximum(t0 - WINDOW, 0), span)
            oc_t = oc_s[...]
            s_cu = _dot_nt(ksa_ref[pl.ds(t0, span), :], qa_s[2])
            s_w = _dot_nt(kwa_ref[pl.ds(w0, n_win), :], qa_s[0])
            s_sl = _dot_nt(ksa_ref[0:n_keys, :], qa_s[1])
            prepare(jnp.minimum(p + 1, n_iter - 1))

            own = (key_d // qb) == qi_row
            s_cu = jnp.where(own & (key_d % qb > tc_row % qb), NEG_INF, s_cu)
            m_s = jnp.maximum(jnp.max(s_sl, axis=0, keepdims=True), jnp.max(s_cu, axis=0, keepdims=True))
            e_sl = jnp.exp2(s_sl - m_s)
            e_cu = jnp.exp2(s_cu - m_s)
            l_s = jnp.sum(e_sl, axis=0, keepdims=True) + jnp.sum(e_cu, axis=0, keepdims=True)
            os_t = (_dot(vst_ref[0, :, 0:n_keys], e_sl.astype(BF16)) + _dot(vs3[p], e_cu.astype(BF16))) / l_s

            rel = (t0 - w0) + tc_row - key_w
            s_w = jnp.where((rel >= 0) & (rel < WINDOW), s_w, NEG_INF)
            e_w = jnp.exp2(s_w - jnp.max(s_w, axis=0, keepdims=True))
            e_wb = e_w.astype(BF16)
            b0 = w0 // span
            ow_t = _dot(vw3[b0], e_wb[0:span])
            for j in range(1, n_win // span):
                ow_t = ow_t + _dot(vw3[b0 + j], e_wb[j * span:(j + 1) * span])
            ow_t = ow_t / jnp.sum(e_w, axis=0, keepdims=True)

            o_all = jnp.concatenate([os_t, ow_t, oc_t, jnp.zeros_like(oc_t)], axis=0).T
            gate = _sigmoid(gl_ref[pl.ds(t0, span), 0:3 * r_n])
            for qi in range(nq):
                g_q = gate[qi * qb:(qi + 1) * qb]
                heads = []
                for r in range(r_n):
                    rs = slice((qi * r_n + r) * qb, (qi * r_n + r + 1) * qb)
                    heads.append(g_q[:, 3 * r:3 * r + 1] * o_all[rs, 2 * dh:3 * dh]
                                 + g_q[:, 3 * r + 1:3 * r + 2] * o_all[rs, :dh]
                                 + g_q[:, 3 * r + 2:3 * r + 3] * o_all[rs, dh:2 * dh])
                o_ref[pl.ds(t0 + qi * qb, qb), :] = jnp.concatenate(heads, axis=1).astype(o_ref.dtype)
            return carry
        return body

    prepare(jnp.int32(0))
    for sb in range(seq // NSA_SLAB_KEYS):
        lax.fori_loop(sb * it_per_slab, (sb + 1) * it_per_slab, make_body((sb + 1) * NSA_SLAB_KEYS), 0)


def _nsa(q, ksa, kwa, vst, vwt, kc, vc, gl, bsz, seq, pe_k, pe_v, k_w1, k_b1, k_w2, v_w1, v_b1, v_w2):
    g_n, r_n, dh, qb = NSA_KV_GROUPS, NSA_GROUP_SIZE, NSA_HEAD_DIM, NSA_Q_BLOCK
    t = bsz * seq
    n_cmp_pad = seq // CMP_STRIDE
    n_blk = seq // SLC_BLOCK
    span = NSA_QB_PER_ITER * qb
    assert n_blk <= F_POS and seq % NSA_SLAB_KEYS == 0 and seq >= WINDOW + span

    def cmp_rows(x):
        return (x.reshape(bsz, n_cmp_pad, CMP_STRIDE, g_n, dh).transpose(0, 3, 1, 2, 4)
                .reshape(bsz, g_n, n_cmp_pad, CMP_STRIDE * dh))

    slopes = np.asarray([2.0 ** (-8.0 * (h + 1) / NSA_HEADS) for h in range(NSA_HEADS)], dtype=np.float32)
    cols = NSA_QB_PER_ITER * r_n * qb
    slope_rows = jnp.asarray(np.tile(np.repeat(slopes.reshape(g_n, r_n), qb, axis=1), (1, NSA_QB_PER_ITER))
                             .reshape(g_n, cols, 1)) * LOG2E
    s3 = jnp.concatenate(_split3(slope_rows) * 2, axis=2)
    sfeat = jnp.pad(s3, ((0, 0), (0, 0), (F_POS, NSA_FEAT - F_POS - 6)))
    cf = np.zeros((n_cmp_pad, NSA_FEAT), np.float32)
    cf[:, F_POS:F_POS + 3] = (np.arange(n_cmp_pad) * CMP_STRIDE)[:, None]
    cf[:, F_POS + 3:F_POS + 6] = (CMP_LEN - 1) * 0.5
    cf[:, F_ONE:F_ONE + 3] = 1.0
    tok = np.arange(n_cmp_pad)[:, None] * CMP_STRIDE + np.arange(CMP_LEN)[None, :]
    overlap_t = ((tok // SLC_BLOCK)[:, :, None] == np.arange(NSA_FEAT)[None, None, :]).mean(axis=1).astype(np.float32).T
    cmp_spec = pl.BlockSpec((1, 1, n_cmp_pad, CMP_STRIDE * dh), lambda b, g: (b, g, 0, 0))
    q_spec = pl.BlockSpec((seq, r_n * dh), lambda b, g: (b, g))
    k_spec = pl.BlockSpec((seq, LANES), lambda b, g: (b, g))
    v_spec = pl.BlockSpec((1, dh, seq), lambda b, g: (b, g, 0))
    w1s, b1s, w2s = (CMP_LEN * dh, CMP_HIDDEN), (1, CMP_HIDDEN), (CMP_HIDDEN, dh)
    return pl.pallas_call(
        functools.partial(_nsa_kernel, seq=seq),
        out_shape=jax.ShapeDtypeStruct((t, NSA_HEADS * dh), BF16),
        grid=(bsz, g_n),
        in_specs=[q_spec, k_spec, k_spec, v_spec, v_spec, cmp_spec, cmp_spec, k_spec,
                  pl.BlockSpec((1, cols, 1), lambda b, g: (g, 0, 0)),
                  pl.BlockSpec((1, cols, NSA_FEAT), lambda b, g: (g, 0, 0)),
                  _const_spec((n_cmp_pad, NSA_FEAT)),
                  _const_spec((1, CMP_LEN * dh)), _const_spec((1, CMP_LEN * dh)),
                  _const_spec(w1s), _const_spec(b1s), _const_spec(w2s),
                  _const_spec(w1s), _const_spec(b1s), _const_spec((dh, CMP_HIDDEN)),
                  _const_spec((NSA_FEAT, n_cmp_pad))],
        out_specs=q_spec,
        scratch_shapes=[pltpu.VMEM((n_cmp_pad, dh + NSA_FEAT), BF16), pltpu.VMEM((dh, n_cmp_pad), BF16),
                        pltpu.VMEM((seq // span, dh, span), BF16), pltpu.VMEM((seq // span, dh, span), BF16),
                        pltpu.VMEM((3, cols, dh + NSA_FEAT), BF16), pltpu.VMEM((dh, cols), F32)],
        compiler_params=_cparams(2),
        name="nsa_attention",
    )(q, ksa, kwa, vst, vwt, cmp_rows(kc), cmp_rows(vc), gl, slope_rows, sfeat, jnp.asarray(cf),
      pe_k.astype(F32).reshape(1, -1), pe_v.astype(F32).reshape(1, -1),
      k_w1.astype(BF16), k_b1.astype(F32).reshape(1, -1), k_w2.astype(BF16),
      v_w1.astype(BF16), v_b1.astype(F32).reshape(1, -1), v_w2.astype(BF16).T, jnp.asarray(overlap_t, dtype=BF16))


def _pad_cols(w, n):
    return jnp.pad(w, ((0, 0), (0, n - w.shape[1])))


def kernel(x, ab_w_in, ab_w_out, s5_lambda_re, s5_lambda_im, s5_log_step, s5_b_re, s5_b_im, s5_c_re, s5_c_im, s5_d, s5_w_glu, s5_b_glu, gdn_conv_w, gdn_a_log, gdn_dt_bias, gdn_norm_w, nsa_w_in, nsa_w_out, nsa_pe_k, nsa_pe_v, nsa_k_w1, nsa_k_b1, nsa_k_w2, nsa_v_w1, nsa_v_b1, nsa_v_w2, ffn_w_in, ffn_conv_w, ffn_conv_b, ffn_w_out, norm_mix, norm_ffn, norm_final):
    bsz, seq, d = x.shape
    depth = ffn_w_in.shape[0]
    h = x.astype(F32).reshape(bsz * seq, d)
    for layer in range(depth):
        i = layer // 2
        if layer % 2 == 0:
            u, q, k, v, z, ba = _ab_proj(h, norm_mix[layer], ab_w_in[i], gdn_conv_w[i], seq)
            prep = _s5_prep(s5_lambda_re[i], s5_lambda_im[i], s5_log_step[i], s5_b_re[i], s5_b_im[i],
                            s5_c_re[i], s5_c_im[i], s5_d[i])
            y_a = _s5(u, seq, prep, s5_w_glu[i], s5_b_glu[i])
            y_b = _gdn(q, k, v, z, ba, seq, gdn_a_log[i], gdn_dt_bias[i], gdn_norm_w[i])
            wo = ab_w_out[i].astype(BF16)
            mix, wms = [y_a, y_b], [wo[:S5_WIDTH], wo[S5_WIDTH:]]
        else:
            q, ksa, kwa, vst, vwt, kc, vc, gl = _nsa_proj(h, norm_mix[layer], nsa_w_in[i], bsz, seq)
            o = _nsa(q, ksa, kwa, vst, vwt, kc, vc, gl, bsz, seq, nsa_pe_k[i], nsa_pe_v[i], nsa_k_w1[i],
                     nsa_k_b1[i], nsa_k_w2[i], nsa_v_w1[i], nsa_v_b1[i], nsa_v_w2[i])
            mix, wms = [o], [nsa_w_out[i].astype(BF16)]
        h = _mix_ffn(h, mix, wms, seq, norm_ffn[layer], ffn_w_in[layer], ffn_conv_w[layer], ffn_conv_b[layer],
                     ffn_w_out[layer], norm_final, final=(layer == depth - 1))
    return h.reshape(bsz, seq, d).astype(x.dtype)
```

```python
import functools
import math

import numpy as np
import jax
import jax.numpy as jnp
from jax import lax
from jax.experimental import pallas as pl
from jax.experimental.pallas import tpu as pltpu

F32 = jnp.float32
BF16 = jnp.bfloat16
HI = lax.Precision.HIGHEST

D_MODEL = 1024
S5_WIDTH = 256
S5_GROUP = 16
S5_GROUPS = 16
S5_STATE = 64
S5_CHUNK = 16
GDN_HEAD_DIM = 128
GDN_HEADS = 6
GDN_WIDTH = 768
GDN_CONV = 4
GDN_CHUNK = 64
GDN_HEADS_PER_STEP = 2
GDN_CHUNKS_PER_ITER = 8
NSA_HEADS = 16
NSA_HEAD_DIM = 64
NSA_KV_GROUPS = 4
NSA_GROUP_SIZE = 4
NSA_KV_WIDTH = 256
CMP_LEN = 32
CMP_STRIDE = 16
CMP_HIDDEN = 256
SLC_BLOCK = 64
N_SELECT = 4
WINDOW = 256
NSA_Q_BLOCK = 64
NSA_SLAB_KEYS = 512
NSA_QB_PER_ITER = 2
LOG2E = math.log2(math.e)
FFN_HIDDEN = 2816
FFN_CONV = 3
FFN_COL_CHUNK = 256
RMS_EPS = 1e-6
NEG_INF = -1e30
LANES = 128
BF16_SUBLANES = 16
VMEM_LIMIT = 56 * 1024 * 1024
ROW_TILE = 512


def _cparams(n_axes):
    return pltpu.CompilerParams(dimension_semantics=("arbitrary",) * n_axes,
                                vmem_limit_bytes=VMEM_LIMIT)


def _rms(x, g):
    return x * lax.rsqrt(jnp.mean(x * x, axis=-1, keepdims=True) + RMS_EPS) * g


def _gelu(x):
    return 0.5 * x * (1.0 + jnp.tanh(math.sqrt(2.0 / math.pi) * (x + 0.044715 * (x * x * x))))


def _sigmoid(x):
    return 1.0 / (1.0 + jnp.exp(-x))


def _silu(x):
    return x * _sigmoid(x)


def _dot(a, b):
    return jnp.dot(a, b, preferred_element_type=F32)


def _dot_nt(a, b):
    return lax.dot_general(a, b, (((1,), (1,)), ((), ())), preferred_element_type=F32)


def _dot_tn(a, b):
    return lax.dot_general(a, b, (((0,), (0,)), ((), ())), preferred_element_type=F32)


def _const_spec(shape):
    nd = len(shape)
    return pl.BlockSpec(shape, lambda *_: (0,) * nd)


def _ab_proj_kernel(x_ref, g_ref, wu_ref, wq_ref, wk_ref, wv_ref, wz_ref, wba_ref, cw_ref,
                    u_ref, q_ref, k_ref, v_ref, z_ref, ba_ref, ext_ref, *, tiles_per_seq):
    rows = x_ref.shape[0]
    pad = 8
    dh = GDN_HEAD_DIM
    xn = _rms(x_ref[...], g_ref[...]).astype(BF16)
    seq_start = pl.program_id(0) % tiles_per_seq == 0

    @pl.when(pl.program_id(0) == 0)
    def _():
        ext_ref[...] = jnp.zeros_like(ext_ref)

    def finish(idx, pre, o_ref):
        prev = jnp.where(seq_start, 0.0, ext_ref[idx, rows:rows + pad, :])
        ext_ref[idx, 0:pad, :] = prev
        ext_ref[idx, pad:, :] = pre
        cw = cw_ref[idx]
        y = pre * cw[GDN_CONV - 1:GDN_CONV]
        for sh in range(1, GDN_CONV):
            y = y + ext_ref[idx, pad - sh:pad - sh + rows, :] * cw[GDN_CONV - 1 - sh:GDN_CONV - sh]
        y = _silu(y)
        if idx < 2:
            parts = []
            for j in range(GDN_HEADS):
                p = y[:, j * dh:(j + 1) * dh]
                p = p * lax.rsqrt(jnp.sum(p * p, axis=-1, keepdims=True) + 1e-6)
                parts.append(p * (dh ** -0.5) if idx == 0 else p)
            y = jnp.concatenate(parts, axis=1)
        o_ref[...] = y

    pre_q = _dot(xn, wq_ref[...])
    pre_k = _dot(xn, wk_ref[...])
    finish(0, pre_q, q_ref)
    pre_v = _dot(xn, wv_ref[...])
    finish(1, pre_k, k_ref)
    z_ref[...] = _dot(xn, wz_ref[...])
    finish(2, pre_v, v_ref)
    u_ref[...] = _dot(xn, wu_ref[...])
    ba_ref[...] = _dot(xn, wba_ref[...])


def _ab_proj(h, gain, w_in, conv_w, seq):
    t, d = h.shape
    w = w_in.astype(BF16)
    cuts = np.cumsum([0, S5_WIDTH, GDN_WIDTH, GDN_WIDTH, GDN_WIDTH, GDN_WIDTH])
    ws = [w[:, cuts[j]:cuts[j + 1]] for j in range(5)] + [_pad_cols(w[:, cuts[5]:], LANES)]
    cw = conv_w.astype(F32).reshape(GDN_CONV, 3, GDN_WIDTH).transpose(1, 0, 2)
    row_spec = lambda n: pl.BlockSpec((ROW_TILE, n), lambda i: (i, 0))
    return pl.pallas_call(
        functools.partial(_ab_proj_kernel, tiles_per_seq=seq // ROW_TILE),
        out_shape=[jax.ShapeDtypeStruct((t, x.shape[1]), F32) for x in ws],
        grid=(t // ROW_TILE,),
        in_specs=[row_spec(d), _const_spec((1, d))] + [_const_spec(x.shape) for x in ws] + [_const_spec(cw.shape)],
        out_specs=[row_spec(x.shape[1]) for x in ws],
        scratch_shapes=[pltpu.VMEM((3, ROW_TILE + 8, GDN_WIDTH), F32)],
        compiler_params=_cparams(1),
        name="ab_proj",
    )(h, gain.reshape(1, d), *ws, cw)


def _ffn_kernel(h_ref, hp_ref, *refs, n_mix, tiles_per_seq, n_chunks, final):
    y_refs, yp_refs, wm_refs = refs[:n_mix], refs[n_mix:2 * n_mix], refs[2 * n_mix:3 * n_mix]
    (g_ref, win_ref, cw_ref, cb_ref, wout_ref, gf_ref, o_ref,
     xn_ref, up0_ref, up1_ref, act_ref, x1_ref) = refs[3 * n_mix:]
    halo = BF16_SUBLANES
    rows = h_ref.shape[0]
    sub = 128
    fcw = act_ref.shape[2]
    g = g_ref[...]
    n_piece = rows // sub
    x1, x1p = h_ref[...], hp_ref[...]
    for y_ref, yp_ref, wm_ref in zip(y_refs, yp_refs, wm_refs):
        x1 = x1 + _dot(y_ref[...], wm_ref[...])
        x1p = x1p + _dot(yp_ref[...], wm_ref[...])
    x1_ref[...] = x1
    seq_start = pl.program_id(0) % tiles_per_seq == 0
    xn_ref[0:halo, :] = jnp.where(seq_start, 0.0, _rms(x1p, g)).astype(BF16)
    xn_ref[halo:, :] = _rms(x1, g).astype(BF16)

    def gate_piece(j, buf, r):
        r0 = r * sub

        def conv(half, cw, cb):
            y = cb
            for k in range(FFN_CONV):
                start = halo + r0 - (FFN_CONV - 1) + k
                y = y + buf[half, start:start + sub, :] * cw[k:k + 1]
            return y
        a = conv(0, cw_ref[j], cb_ref[j])
        b = conv(1, cw_ref[n_chunks + j], cb_ref[n_chunks + j])
        act_ref[j, r0:r0 + sub, :] = (_silu(a) * b).astype(BF16)

    def up_and_gate(j_up, buf_up, j_gate, buf_gate):
        xe = xn_ref[...]
        for half in range(2):
            col = pl.multiple_of((half * n_chunks + j_up) * fcw, fcw)
            buf_up[half] = _dot(xe, win_ref[:, pl.ds(col, fcw)])
            if j_gate is not None:
                for r in range(half * n_piece // 2, (half + 1) * n_piece // 2):
                    gate_piece(j_gate, buf_gate, r)

    def body(i, carry):
        up_and_gate(2 * i + 1, up1_ref, 2 * i, up0_ref)
        up_and_gate(2 * i + 2, up0_ref, 2 * i + 1, up1_ref)
        return carry

    up_and_gate(0, up0_ref, None, None)
    lax.fori_loop(0, (n_chunks - 1) // 2, body, 0)
    for r in range(n_piece):
        gate_piece(n_chunks - 1, up0_ref, r)

    out = x1_ref[...]
    for j in range(n_chunks):
        out = out + _dot(act_ref[j], wout_ref[j])
    if final:
        out = _rms(out, gf_ref[...])
    o_ref[...] = out


def _mix_ffn(h, ys, wms, seq, gain, w_in, conv_w, conv_b, w_out, gain_final, final):
    t, d = h.shape
    n_mix = len(ys)
    fc = FFN_COL_CHUNK
    n_chunks = FFN_HIDDEN // fc
    assert n_chunks % 2 == 1
    halo = BF16_SUBLANES
    win = w_in.astype(BF16)
    cw = conv_w.astype(F32).reshape(FFN_CONV, 2 * n_chunks, fc).transpose(1, 0, 2)
    cb = conv_b.astype(F32).reshape(2 * n_chunks, 1, fc)
    wout = w_out.astype(BF16).reshape(n_chunks, fc, d)
    blocks_per_tile = ROW_TILE // halo
    tile_spec = lambda n: pl.BlockSpec((ROW_TILE, n), lambda i: (i, 0))
    prev_spec = lambda n: pl.BlockSpec((halo, n), lambda i: (jnp.maximum(i * blocks_per_tile - 1, 0), 0))
    return pl.pallas_call(
        functools.partial(_ffn_kernel, n_mix=n_mix, tiles_per_seq=seq // ROW_TILE, n_chunks=n_chunks, final=final),
        out_shape=jax.ShapeDtypeStruct((t, d), F32),
        grid=(t // ROW_TILE,),
        in_specs=[tile_spec(d), prev_spec(d)]
        + [tile_spec(y.shape[1]) for y in ys] + [prev_spec(y.shape[1]) for y in ys]
        + [_const_spec(w.shape) for w in wms]
        + [_const_spec((1, d)),
                  pl.BlockSpec(win.shape, lambda i: (0, 0), pipeline_mode=pl.Buffered(1)),
                  _const_spec(cw.shape), _const_spec(cb.shape),
                  pl.BlockSpec(wout.shape, lambda i: (0, 0, 0), pipeline_mode=pl.Buffered(1)),
                  _const_spec((1, d))],
        out_specs=pl.BlockSpec((ROW_TILE, d), lambda i: (i, 0)),
        scratch_shapes=[pltpu.VMEM((ROW_TILE + halo, d), BF16), pltpu.VMEM((2, ROW_TILE + halo, fc), F32),
                        pltpu.VMEM((2, ROW_TILE + halo, fc), F32), pltpu.VMEM((n_chunks, ROW_TILE, fc), BF16),
                        pltpu.VMEM((ROW_TILE, d), F32)],
        compiler_params=_cparams(1),
        name="mix_ffn",
    )(h, h, *ys, *ys, *wms, gain.reshape(1, d), win, cw, cb, wout, gain_final.reshape(1, d))


def _s5_prep(lam_re, lam_im, log_step, b_re, b_im, c_re, c_im, d_skip):
    g_n, p_n, h_n, L = S5_GROUPS, S5_STATE, S5_GROUP, S5_CHUNK
    step = jnp.exp(log_step.astype(F32))[:, None]
    lr, li = lam_re.astype(F32), lam_im.astype(F32)
    mag = jnp.exp(lr * step)
    a_re = mag * jnp.cos(li * step)
    a_im = mag * jnp.sin(li * step)
    den = lr * lr + li * li
    n_re, n_im = a_re - 1.0, a_im
    z_re = (n_re * lr + n_im * li) / den
    z_im = (n_im * lr - n_re * li) / den
    b_re, b_im = b_re.astype(F32), b_im.astype(F32)
    bb_re = z_re[..., None] * b_re - z_im[..., None] * b_im
    bb_im = z_re[..., None] * b_im + z_im[..., None] * b_re
    c_re, c_im = c_re.astype(F32), c_im.astype(F32)
    pw_re, pw_im = [jnp.ones_like(a_re)], [jnp.zeros_like(a_im)]
    for _ in range(L):
        pr, pi = pw_re[-1], pw_im[-1]
        pw_re.append(pr * a_re - pi * a_im)
        pw_im.append(pr * a_im + pi * a_re)
    eye_g = jnp.eye(g_n, dtype=F32)
    ks = []
    for j in range(L):
        ab_re = pw_re[j][..., None] * bb_re - pw_im[j][..., None] * bb_im
        ab_im = pw_re[j][..., None] * bb_im + pw_im[j][..., None] * bb_re
        kj = (jnp.einsum('gop,gpi->gio', c_re, ab_re, precision=HI)
              - jnp.einsum('gop,gpi->gio', c_im, ab_im, precision=HI))
        ks.append(jnp.einsum('gio,gk->giko', kj, eye_g).reshape(g_n * h_n, g_n * h_n))
    kstack = jnp.concatenate(ks[::-1], axis=0).astype(BF16)
    bb = jnp.stack([jnp.einsum('gph,gk->ghkp', bb_re, eye_g), jnp.einsum('gph,gk->ghkp', bb_im, eye_g)], axis=2)
    bb = bb.reshape(g_n * h_n, 2 * g_n * p_n).astype(BF16)
    cc = jnp.stack([jnp.einsum('ghp,gk->gpkh', c_re, eye_g), -jnp.einsum('ghp,gk->gpkh', c_im, eye_g)], axis=0)
    cc = cc.reshape(2 * g_n * p_n, g_n * h_n).astype(BF16)
    apow = jnp.stack([jnp.concatenate([r.reshape(-1), i.reshape(-1)]) for r, i in zip(pw_re, pw_im)], axis=0)
    dvec = jnp.tile(d_skip.astype(F32).reshape(1, g_n * h_n), (1, L))
    return kstack, bb, cc, apow, dvec


def _s5_kernel(u_ref, ks_ref, bb_ref, cc_ref, ap_ref, dv_ref, wg_ref, bg_ref, o_ref, x_scr, hp_scr,
               *, n_batch, n_chunk):
    L, w = S5_CHUNK, S5_WIDTH
    half = S5_GROUPS * S5_STATE
    rows = n_batch * n_chunk
    u2 = u_ref[...].reshape(rows, L * w)
    u2b = u2.astype(BF16)

    def cmul(j, x):
        ar, ai = ap_ref[j:j + 1, :half], ap_ref[j:j + 1, half:]
        xr, xi = x[:, :half], x[:, half:]
        return jnp.concatenate([ar * xr - ai * xi, ar * xi + ai * xr], axis=1)

    xin = _dot(u2b[:, (L - 1) * w:], bb_ref[...])
    for s in range(L - 1):
        xin = xin + cmul(L - 1 - s, _dot(u2b[:, s * w:(s + 1) * w], bb_ref[...]))
    x_scr[...] = xin

    a_l = ap_ref[L:L + 1, :]
    alr, ali = a_l[:, :half], a_l[:, half:]

    def scan_body(k, hs):
        new = []
        for b in range(n_batch):
            r = b * n_chunk + k
            h = hs[b]
            hp_scr[pl.ds(r, 1), :] = h
            hr, hi = h[:, :half], h[:, half:]
            new.append(jnp.concatenate([alr * hr - ali * hi, alr * hi + ali * hr], axis=1)
                       + x_scr[pl.ds(r, 1), :])
        return tuple(new)

    lax.fori_loop(0, n_chunk, scan_body, tuple(jnp.zeros((1, 2 * half), F32) for _ in range(n_batch)))
    hp = hp_scr[...]

    for t in range(L):
        y = _dot(u2b[:, :(t + 1) * w], ks_ref[(L - 1 - t) * w:, :])
        y = y + _dot(cmul(t + 1, hp).astype(BF16), cc_ref[...])
        y = y + dv_ref[:, t * w:(t + 1) * w] * u2[:, t * w:(t + 1) * w]
        z = _gelu(y)
        gate = _sigmoid(_dot(z.astype(BF16), wg_ref[...]) + bg_ref[...])
        o_ref[:, :, t * w:(t + 1) * w] = (z * gate).astype(o_ref.dtype).reshape(n_batch, n_chunk, w)


def _s5(u, seq, prep, w_glu, b_glu):
    t = u.shape[0]
    bsz = t // seq
    L, w = S5_CHUNK, S5_WIDTH
    n_chunk = seq // L
    n_batch = 2
    kstack, bb, cc, apow, dvec = prep
    u3 = u.reshape(bsz, n_chunk, L * w)
    out = pl.pallas_call(
        functools.partial(_s5_kernel, n_batch=n_batch, n_chunk=n_chunk),
        out_shape=jax.ShapeDtypeStruct((bsz, n_chunk, L * w), BF16),
        grid=(bsz // n_batch,),
        in_specs=[pl.BlockSpec((n_batch, n_chunk, L * w), lambda i: (i, 0, 0)),
                  _const_spec(kstack.shape), _const_spec(bb.shape), _const_spec(cc.shape),
                  _const_spec(apow.shape), _const_spec(dvec.shape),
                  _const_spec((w, w)), _const_spec((1, w))],
        out_specs=pl.BlockSpec((n_batch, n_chunk, L * w), lambda i: (i, 0, 0)),
        scratch_shapes=[pltpu.VMEM((n_batch * n_chunk, 2 * S5_GROUPS * S5_STATE), F32),
                        pltpu.VMEM((n_batch * n_chunk, 2 * S5_GROUPS * S5_STATE), F32)],
        compiler_params=_cparams(1),
        name="s5_mixer",
    )(u3, kstack, bb, cc, apow, dvec, w_glu.astype(BF16), b_glu.astype(F32).reshape(1, w))
    return out.reshape(t, w)


def _gdn_kernel(qs, ks, vs, z_ref, bc_ref, ac_ref, ar_ref, hp_ref, hr_ref, nw_ref,
                o_ref, xs, ns, qe, o0, gls, gcc_s, gcr_s, xs2, ns2, gls2, *, seq, n_head):
    c, dh = GDN_CHUNK, GDN_HEAD_DIM
    n_chunk = seq // c

    hp = hp_ref[0]
    hr = hr_ref[0]

    def softplus(x):
        return jnp.maximum(x, 0.0) + jnp.log(1.0 + jnp.exp(-jnp.abs(x)))

    ii = lax.broadcasted_iota(jnp.int32, (c, c), 0)
    jj = lax.broadcasted_iota(jnp.int32, (c, c), 1)
    eye = (ii == jj).astype(F32)
    g_col = -jnp.exp(hp[0:1, :]) * softplus(ac_ref[0, 0] + hp[1:2, :])
    g_row = -jnp.exp(hr[:, 0:1]) * softplus(ar_ref[0, 0] + hr[:, 1:2])
    gc_col = jnp.dot((ii >= jj).astype(F32), g_col, precision=HI, preferred_element_type=F32)
    gcr_s[...] = jnp.dot(g_row, (ii <= jj).astype(F32), precision=HI, preferred_element_type=F32)
    for ci in range(n_chunk):
        gcc_s[ci * c:(ci + 1) * c, :] = gc_col[:, ci * n_head:(ci + 1) * n_head]

    def local_body(it, carry):
        pr = []
        for cc in range(GDN_CHUNKS_PER_ITER):
            ci = it * GDN_CHUNKS_PER_ITER + cc
            rs = pl.ds(pl.multiple_of(ci * c, c), c)
            beta = _sigmoid(bc_ref[0, 0, rs, :])
            gccs = gcc_s[rs, :]
            for j in range(n_head):
                sl = slice(j * dh, (j + 1) * dh)
                gcc = gccs[:, j:j + 1]
                gcr = gcr_s[pl.ds(ci * n_head + j, 1), :]
                q, k, v = qs[rs, sl], ks[rs, sl], vs[rs, sl]
                bj = beta[:, j:j + 1]
                kb = k * bj
                e_g = jnp.exp(gcc)
                g_last = gcc[c - 1:c, :]
                pr.append(dict(ci=ci, j=j, q=q, kf=k.astype(BF16), kbb=kb.astype(BF16),
                               decay=jnp.exp(jnp.where(ii >= jj, gcc - gcr, NEG_INF)),
                               rhs=jnp.concatenate([v * bj, kb * e_g], axis=1).astype(BF16),
                               qd=q * e_g, kd=(k * jnp.exp(g_last - gcc)).astype(BF16),
                               gl=jnp.broadcast_to(jnp.exp(g_last), (8, dh))))
        kk = [_dot_nt(p["kbb"], p["kf"]) for p in pr]
        qk = [_dot_nt(p["q"].astype(BF16), p["kf"]) for p in pr]
        ps = [-jnp.where(ii > jj, a * p["decay"], 0.0) for a, p in zip(kk, pr)]
        ts = [eye + n for n in ps]
        span = 2
        while span < c:
            ps = [_dot(n.astype(BF16), n.astype(BF16)) for n in ps]
            ts = [t + _dot(t.astype(BF16), n.astype(BF16)) for t, n in zip(ts, ps)]
            span *= 2
        uws = [_dot(t.astype(BF16), p["rhs"]).astype(BF16) for t, p in zip(ts, pr)]
        intras = [(s * p["decay"]).astype(BF16) for s, p in zip(qk, pr)]
        i_uws = [_dot(a, uw) for a, uw in zip(intras, uws)]
        kd_uws = [_dot_tn(p["kd"], uw) for p, uw in zip(pr, uws)]
        for p, i_uw, kd_uw in zip(pr, i_uws, kd_uws):
            j, ci = p["j"], p["ci"]
            ns[j, ci] = kd_uw[:, :dh]
            xs[j, ci] = kd_uw[:, dh:].astype(BF16)
            o0[j, ci] = i_uw[:, :dh]
            qe[j, ci] = (p["qd"] - i_uw[:, dh:]).astype(BF16)
            gls[j, ci] = p["gl"]
        first = [idx for idx, p in enumerate(pr) if (idx // n_head) % 2 == 0]
        x1b = [kd_uws[idx + n_head][:, dh:].astype(BF16) for idx in first]
        x1x0 = [_dot(x1, kd_uws[idx][:, dh:].astype(BF16)) for x1, idx in zip(x1b, first)]
        x1n0 = [_dot(x1, kd_uws[idx][:, :dh].astype(BF16)) for x1, idx in zip(x1b, first)]
        for idx, xx, xn in zip(first, x1x0, x1n0):
            p0, p1 = pr[idx], pr[idx + n_head]
            g0, g1 = p0["gl"][0:1, :], p1["gl"][0:1, :]
            pi = it * (GDN_CHUNKS_PER_ITER // 2) + (idx // n_head) // 2
            xs2[p0["j"], pi] = (g1 * kd_uws[idx][:, dh:] + g0 * kd_uws[idx + n_head][:, dh:] - xx).astype(BF16)
            ns2[p0["j"], pi] = g1 * kd_uws[idx][:, :dh] - xn + kd_uws[idx + n_head][:, :dh]
            gls2[p0["j"], pi] = p0["gl"] * p1["gl"]
        return carry

    lax.fori_loop(0, n_chunk // GDN_CHUNKS_PER_ITER, local_body, 0)

    def rec_body(pi, states):
        c0, c1 = 2 * pi, 2 * pi + 1
        s_bs = [s.astype(BF16) for s in states]
        xd2 = [_dot(xs2[j, pi], s_bs[j]) for j in range(n_head)]
        xd0 = [_dot(xs[j, c0], s_bs[j]) for j in range(n_head)]
        od0 = [_dot(qe[j, c0], s_bs[j]) for j in range(n_head)]
        new = tuple(gls2[j, pi][0:1, :] * states[j] - xd2[j] + ns2[j, pi] for j in range(n_head))
        mid = [(gls[j, c0][0:1, :] * states[j] - xd0[j] + ns[j, c0]).astype(BF16) for j in range(n_head)]
        od1 = [_dot(qe[j, c1], mid[j]) for j in range(n_head)]
        for ci, od in ((c0, od0), (c1, od1)):
            rs = pl.ds(pl.multiple_of(ci * c, c), c)
            for j in range(n_head):
                sl = slice(j * dh, (j + 1) * dh)
                o = od[j] + o0[j, ci]
                on = o * lax.rsqrt(jnp.mean(o * o, axis=-1, keepdims=True) + RMS_EPS) * nw_ref[...]
                o_ref[rs, sl] = (on * _silu(z_ref[rs, sl])).astype(o_ref.dtype)
        return new

    lax.fori_loop(0, n_chunk // 2, rec_body, tuple(jnp.zeros((dh, dh), F32) for _ in range(n_head)), unroll=2)


def _gdn(q, k, v, z, ba, seq, a_log, dt_bias, norm_w):
    t = q.shape[0]
    bsz = t // seq
    nh = GDN_HEADS_PER_STEP
    n_grp = GDN_HEADS // nh
    wd = nh * GDN_HEAD_DIM
    bl = ba[:, :GDN_HEADS].reshape(bsz, seq, n_grp, nh)
    al = ba[:, GDN_HEADS:2 * GDN_HEADS].reshape(bsz, seq, n_grp, nh)
    b_col = bl.transpose(0, 2, 1, 3)
    n_chunk = seq // GDN_CHUNK
    n_prob = n_chunk * nh
    a5 = al.reshape(bsz, n_chunk, GDN_CHUNK, n_grp, nh)
    a_col = a5.transpose(0, 3, 2, 1, 4).reshape(bsz, n_grp, GDN_CHUNK, n_prob)
    a_row = a5.transpose(0, 3, 1, 4, 2).reshape(bsz, n_grp, n_prob, GDN_CHUNK)
    hp = jnp.stack([jnp.tile(a_log.astype(F32).reshape(n_grp, nh), (1, n_chunk)),
                    jnp.tile(dt_bias.astype(F32).reshape(n_grp, nh), (1, n_chunk))], axis=1)
    hr = hp.transpose(0, 2, 1)
    act_spec = pl.BlockSpec((seq, wd), lambda b, g: (b, g))
    col_spec = pl.BlockSpec((1, 1, seq, nh), lambda b, g: (b, g, 0, 0))
    return pl.pallas_call(
        functools.partial(_gdn_kernel, seq=seq, n_head=nh),
        out_shape=jax.ShapeDtypeStruct((t, GDN_WIDTH), BF16),
        grid=(bsz, n_grp),
        in_specs=[act_spec, act_spec, act_spec, act_spec, col_spec,
                  pl.BlockSpec((1, 1, GDN_CHUNK, n_prob), lambda b, g: (b, g, 0, 0)),
                  pl.BlockSpec((1, 1, n_prob, GDN_CHUNK), lambda b, g: (b, g, 0, 0)),
                  pl.BlockSpec((1, 2, n_prob), lambda b, g: (g, 0, 0)),
                  pl.BlockSpec((1, n_prob, 2), lambda b, g: (g, 0, 0)),
                  _const_spec((1, GDN_HEAD_DIM))],
        out_specs=act_spec,
        scratch_shapes=[pltpu.VMEM((nh, n_chunk, GDN_HEAD_DIM, GDN_HEAD_DIM), BF16),
                        pltpu.VMEM((nh, n_chunk, GDN_HEAD_DIM, GDN_HEAD_DIM), F32),
                        pltpu.VMEM((nh, n_chunk, GDN_CHUNK, GDN_HEAD_DIM), BF16),
                        pltpu.VMEM((nh, n_chunk, GDN_CHUNK, GDN_HEAD_DIM), F32),
                        pltpu.VMEM((nh, n_chunk, 8, GDN_HEAD_DIM), F32),
                        pltpu.VMEM((seq, nh), F32), pltpu.VMEM((n_prob, GDN_CHUNK), F32),
                        pltpu.VMEM((nh, n_chunk // 2, GDN_HEAD_DIM, GDN_HEAD_DIM), BF16),
                        pltpu.VMEM((nh, n_chunk // 2, GDN_HEAD_DIM, GDN_HEAD_DIM), F32),
                        pltpu.VMEM((nh, n_chunk // 2, 8, GDN_HEAD_DIM), F32)],
        compiler_params=_cparams(2),
        name="gated_deltanet",
    )(q, k, v, z, b_col, a_col, a_row, hp, hr, norm_w.astype(F32).reshape(1, GDN_HEAD_DIM))


NSA_FEAT = 64
F_BLK, F_POS, F_ONE = 0, 32, 38


def _split3(x):
    x1 = x.astype(BF16).astype(F32)
    x2 = (x - x1).astype(BF16).astype(F32)
    return x1, x2, x - x1 - x2


def _nsa_proj_kernel(x_ref, g_ref, wq_ref, wks_ref, wkw_ref, wvs_ref, wvw_ref, wkc_ref, wvc_ref, wg_ref, kf_ref,
                     q_ref, ksa_ref, kwa_ref, vst_ref, vwt_ref, kc_ref, vc_ref, gl_ref):
    xn = _rms(x_ref[...], g_ref[...]).astype(BF16)
    q_ref[...] = (_dot(xn, wq_ref[...]) * (NSA_HEAD_DIM ** -0.5 * LOG2E)).astype(BF16)
    kf = jnp.concatenate([kf_ref[...]] * NSA_KV_GROUPS, axis=1)
    ksa_ref[...] = (_dot(xn, wks_ref[...]) + kf).astype(BF16)
    kwa_ref[...] = (_dot(xn, wkw_ref[...]) + kf).astype(BF16)
    vst_ref[0] = _dot_nt(wvs_ref[...], xn).astype(BF16)
    vwt_ref[0] = _dot_nt(wvw_ref[...], xn).astype(BF16)
    kc_ref[...] = _dot(xn, wkc_ref[...]).astype(BF16)
    vc_ref[...] = _dot(xn, wvc_ref[...]).astype(BF16)
    gl_ref[...] = _dot(xn, wg_ref[...])


def _nsa_proj(h, gain, w_in, bsz, seq):
    t, d = h.shape
    g_n, dh = NSA_KV_GROUPS, NSA_HEAD_DIM
    nq, kvw = NSA_HEADS * dh, NSA_KV_WIDTH
    w = w_in.astype(BF16)
    wq = w[:, :nq]
    wkc, wvc, wks, wvs, wkw, wvw = (w[:, nq + j * kvw: nq + (j + 1) * kvw] for j in range(6))
    wgl = w[:, nq + 6 * kvw:]

    def lane_padded(x, width):
        return jnp.pad(x.reshape(d, g_n, width), ((0, 0), (0, 0), (0, LANES - width))).reshape(d, g_n * LANES)

    pos = np.arange(seq)
    kf = np.zeros((seq, LANES), np.float32)
    kf[pos, dh + F_BLK + pos // SLC_BLOCK] = 1.0
    kf[:, dh + F_POS:dh + F_POS + 3] = ((pos // SLC_BLOCK) * SLC_BLOCK)[:, None]
    kf[:, dh + F_POS + 3:dh + F_POS + 6] = (pos % SLC_BLOCK)[:, None]
    kf[:, dh + F_ONE:dh + F_ONE + 3] = 1.0
    tps = seq // ROW_TILE
    row_spec = lambda n: pl.BlockSpec((ROW_TILE, n), lambda i: (i, 0))
    tr_spec = pl.BlockSpec((1, kvw, ROW_TILE), lambda i: (i // tps, 0, i % tps))
    ws = [wq, lane_padded(wks, dh), lane_padded(wkw, dh), wvs.T, wvw.T, wkc, wvc, lane_padded(wgl, 3 * NSA_GROUP_SIZE)]
    return pl.pallas_call(
        _nsa_proj_kernel,
        out_shape=[jax.ShapeDtypeStruct((t, nq), BF16),
                   jax.ShapeDtypeStruct((t, g_n * LANES), BF16), jax.ShapeDtypeStruct((t, g_n * LANES), BF16),
                   jax.ShapeDtypeStruct((bsz, kvw, seq), BF16), jax.ShapeDtypeStruct((bsz, kvw, seq), BF16),
                   jax.ShapeDtypeStruct((t, kvw), BF16), jax.ShapeDtypeStruct((t, kvw), BF16),
                   jax.ShapeDtypeStruct((t, g_n * LANES), F32)],
        grid=(t // ROW_TILE,),
        in_specs=[row_spec(d), _const_spec((1, d))] + [_const_spec(x.shape) for x in ws]
        + [pl.BlockSpec((ROW_TILE, LANES), lambda i: (i % tps, 0))],
        out_specs=[row_spec(nq), row_spec(g_n * LANES), row_spec(g_n * LANES), tr_spec, tr_spec,
                   row_spec(kvw), row_spec(kvw), row_spec(g_n * LANES)],
        compiler_params=_cparams(1),
        name="nsa_proj",
    )(h, gain.reshape(1, d), *ws, jnp.asarray(kf))


def _nsa_kernel(q_ref, ksa_ref, kwa_ref, vst_ref, vwt_ref, kca_ref, vca_ref, gl_ref, sl_ref, sf_ref, cf_ref,
                pek_ref, pev_ref, kw1_ref, kb1_ref, kw2_ref, vw1_ref, vb1_ref, vw2t_ref, ovt_ref, o_ref,
                kcmp, vcmpt, vs3, vw3, qa_s, oc_s, *, seq):
    qb, dh, r_n, nq = NSA_Q_BLOCK, NSA_HEAD_DIM, NSA_GROUP_SIZE, NSA_QB_PER_ITER
    span = nq * qb
    cols = nq * r_n * qb
    n_cmp_pad = seq // CMP_STRIDE
    n_cmp = (seq - CMP_LEN) // CMP_STRIDE + 1
    half = CMP_STRIDE * dh
    n_blk = seq // SLC_BLOCK
    n_win = WINDOW + span
    it_per_slab = NSA_SLAB_KEYS // span
    nf = NSA_FEAT

    def hidden(a_ref, pe_ref, w1_ref, b1_ref):
        a = a_ref[0, 0]
        top = _dot(a, w1_ref[:half, :])
        bot = pltpu.roll(_dot(a, w1_ref[half:, :]), n_cmp_pad - 1, 0)
        const = _dot(pe_ref[...].astype(BF16), w1_ref[...]) + b1_ref[...]
        return _gelu(top + bot + const).astype(BF16)

    kcmp[...] = jnp.concatenate([_dot(hidden(kca_ref, pek_ref, kw1_ref, kb1_ref), kw2_ref[...]), cf_ref[...]],
                                axis=1).astype(BF16)
    vcmpt[...] = _dot_nt(vw2t_ref[...], hidden(vca_ref, pev_ref, vw1_ref, vb1_ref)).astype(BF16)
    for n in range(seq // span):
        vs3[n] = vst_ref[0, :, n * span:(n + 1) * span]
        vw3[n] = vwt_ref[0, :, n * span:(n + 1) * span]

    def iota(shape, axis):
        return lax.broadcasted_iota(jnp.int32, shape, axis)

    slope = sl_ref[0]
    colc, colr = iota((cols, 1), 0), iota((1, cols), 1)
    tc_col = (colc // (r_n * qb)) * qb + colc % qb
    tc_row = (colr // (r_n * qb)) * qb + colr % qb
    qi_row = colr // (r_n * qb)
    lane_f = iota((1, nf), 1)
    jr = iota((n_cmp_pad, 1), 0)
    nr = iota((nf, 1), 0)
    lane_s = iota((1, span), 1)
    qi_s = lane_s // qb
    key_d = iota((span, 1), 0)
    key_w = iota((n_win, 1), 0)
    eye_s = (iota((span, span), 0) == iota((span, span), 1)).astype(BF16)

    def prepare(p):
        t0 = pl.multiple_of(p * span, span)
        qf = jnp.concatenate([q_ref[pl.ds(t0 + qi * qb, qb), r * dh:(r + 1) * dh]
                              for qi in range(nq) for r in range(r_n)], axis=0).astype(F32)
        c1, c2, c3 = _split3(-(slope * (t0 + tc_col).astype(F32)))
        f_plain = jnp.where(lane_f == F_ONE, c1, jnp.where(lane_f == F_ONE + 1, c2,
                            jnp.where(lane_f == F_ONE + 2, c3, sf_ref[0])))
        qa_plain = jnp.concatenate([qf, f_plain], axis=1).astype(BF16)

        ok_c = (jr * CMP_STRIDE + (CMP_LEN - 1) <= t0 + tc_row) & (jr < n_cmp)
        s_c = jnp.where(ok_c, _dot_nt(kcmp[...], qa_plain), NEG_INF)
        e_c = jnp.where(ok_c, jnp.exp2(s_c - jnp.max(s_c, axis=0, keepdims=True)), 0.0)
        l_c = jnp.sum(e_c, axis=0, keepdims=True)
        p_c = e_c / jnp.where(l_c > 0.0, l_c, 1.0)
        oc_s[...] = _dot(vcmpt[...], p_c.astype(BF16))

        imp4 = sum(_dot(ovt_ref[...], part.astype(BF16)) for part in _split3(p_c))
        halves = []
        for qi in range(nq):
            a = imp4[:, (2 * qi) * LANES:(2 * qi + 1) * LANES] + imp4[:, (2 * qi + 1) * LANES:(2 * qi + 2) * LANES]
            halves.append(a + pltpu.roll(a, qb, 1))
        low_half = lax.broadcasted_iota(jnp.int32, (1, LANES), 1) < qb
        imp = jnp.concatenate([jnp.where(low_half, halves[2 * v], halves[2 * v + 1]) for v in range(nq // 2)],
                              axis=1)

        iq = nq * p + qi_s
        cand = (nr > 0) & (nr < iq)
        sc = jnp.where(cand, imp, NEG_INF)
        sel = (nr == 0) & (iq > 0)
        for _ in range(N_SELECT - 2):
            best = jnp.max(sc, axis=0, keepdims=True)
            first = jnp.min(jnp.where(sc == best, nr, nf), axis=0, keepdims=True)
            pick = (nr == first) & cand
            sel = sel | pick
            sc = jnp.where(pick, NEG_INF, sc)
        neg_slab = jnp.where(sel & (nr < nq * p), 0.0, NEG_INF)
        neg_cur = jnp.where(nr == iq, 0.0, jnp.where((nr >= nq * p) & (nr < iq) & sel, 0.0, NEG_INF))
        neg_t = jnp.concatenate([neg_slab, neg_cur], axis=0).astype(BF16)
        neg = _dot_nt(eye_s, neg_t)
        rows_of = lambda x: jnp.concatenate([x[qi * qb:(qi + 1) * qb] for qi in range(nq) for _ in range(r_n)], axis=0)
        f_slab = jnp.where(lane_f < n_blk, rows_of(neg[:, :nf]), f_plain)
        f_cur = jnp.where(lane_f < n_blk, rows_of(neg[:, nf:]), f_plain)
        qa_s[0] = qa_plain
        qa_s[1] = jnp.concatenate([qf, f_slab], axis=1).astype(BF16)
        qa_s[2] = jnp.concatenate([qf, f_cur], axis=1).astype(BF16)

    n_iter = seq // span

    def make_body(n_keys):
        def body(p, carry):
            t0 = pl.multiple_of(p * span, span)
            w0 = pl.multiple_of(jnp.maximum(t0 - WINDOW, 0), span)
            oc_t = oc_s[...]
            s_cu = _dot_nt(ksa_ref[pl.ds(t0, span), :], qa_s[2])
            s_w = _dot_nt(kwa_ref[pl.ds(w0, n_win), :], qa_s[0])
            s_sl = _dot_nt(ksa_ref[0:n_keys, :], qa_s[1])
            prepare(jnp.minimum(p + 1, n_iter - 1))

            own = (key_d // qb) == qi_row
            s_cu = jnp.where(own & (key_d % qb > tc_row % qb), NEG_INF, s_cu)
            m_s = jnp.maximum(jnp.max(s_sl, axis=0, keepdims=True), jnp.max(s_cu, axis=0, keepdims=True))
            e_sl = jnp.exp2(s_sl - m_s)
            e_cu = jnp.exp2(s_cu - m_s)
            l_s = jnp.sum(e_sl, axis=0, keepdims=True) + jnp.sum(e_cu, axis=0, keepdims=True)
            os_t = (_dot(vst_ref[0, :, 0:n_keys], e_sl.astype(BF16)) + _dot(vs3[p], e_cu.astype(BF16))) / l_s

            rel = (t0 - w0) + tc_row - key_w
            s_w = jnp.where((rel >= 0) & (rel < WINDOW), s_w, NEG_INF)
            e_w = jnp.exp2(s_w - jnp.max(s_w, axis=0, keepdims=True))
            e_wb = e_w.astype(BF16)
            b0 = w0 // span
            ow_t = _dot(vw3[b0], e_wb[0:span])
            for j in range(1, n_win // span):
                ow_t = ow_t + _dot(vw3[b0 + j], e_wb[j * span:(j + 1) * span])
            ow_t = ow_t / jnp.sum(e_w, axis=0, keepdims=True)

            o_all = jnp.concatenate([os_t, ow_t, oc_t, jnp.zeros_like(oc_t)], axis=0).T
            gate = _sigmoid(gl_ref[pl.ds(t0, span), 0:3 * r_n])
            for qi in range(nq):
                g_q = gate[qi * qb:(qi + 1) * qb]
                heads = []
                for r in range(r_n):
                    rs = slice((qi * r_n + r) * qb, (qi * r_n + r + 1) * qb)
                    heads.append(g_q[:, 3 * r:3 * r + 1] * o_all[rs, 2 * dh:3 * dh]
                                 + g_q[:, 3 * r + 1:3 * r + 2] * o_all[rs, :dh]
                                 + g_q[:, 3 * r + 2:3 * r + 3] * o_all[rs, dh:2 * dh])
                o_ref[pl.ds(t0 + qi * qb, qb), :] = jnp.concatenate(heads, axis=1).astype(o_ref.dtype)
            return carry
        return body

    prepare(jnp.int32(0))
    for sb in range(seq // NSA_SLAB_KEYS):
        lax.fori_loop(sb * it_per_slab, (sb + 1) * it_per_slab, make_body((sb + 1) * NSA_SLAB_KEYS), 0)


def _nsa(q, ksa, kwa, vst, vwt, kc, vc, gl, bsz, seq, pe_k, pe_v, k_w1, k_b1, k_w2, v_w1, v_b1, v_w2):
    g_n, r_n, dh, qb = NSA_KV_GROUPS, NSA_GROUP_SIZE, NSA_HEAD_DIM, NSA_Q_BLOCK
    t = bsz * seq
    n_cmp_pad = seq // CMP_STRIDE
    n_blk = seq // SLC_BLOCK
    span = NSA_QB_PER_ITER * qb
    assert n_blk <= F_POS and seq % NSA_SLAB_KEYS == 0 and seq >= WINDOW + span

    def cmp_rows(x):
        return (x.reshape(bsz, n_cmp_pad, CMP_STRIDE, g_n, dh).transpose(0, 3, 1, 2, 4)
                .reshape(bsz, g_n, n_cmp_pad, CMP_STRIDE * dh))

    slopes = np.asarray([2.0 ** (-8.0 * (h + 1) / NSA_HEADS) for h in range(NSA_HEADS)], dtype=np.float32)
    cols = NSA_QB_PER_ITER * r_n * qb
    slope_rows = jnp.asarray(np.tile(np.repeat(slopes.reshape(g_n, r_n), qb, axis=1), (1, NSA_QB_PER_ITER))
                             .reshape(g_n, cols, 1)) * LOG2E
    s3 = jnp.concatenate(_split3(slope_rows) * 2, axis=2)
    sfeat = jnp.pad(s3, ((0, 0), (0, 0), (F_POS, NSA_FEAT - F_POS - 6)))
    cf = np.zeros((n_cmp_pad, NSA_FEAT), np.float32)
    cf[:, F_POS:F_POS + 3] = (np.arange(n_cmp_pad) * CMP_STRIDE)[:, None]
    cf[:, F_POS + 3:F_POS + 6] = (CMP_LEN - 1) * 0.5
    cf[:, F_ONE:F_ONE + 3] = 1.0
    tok = np.arange(n_cmp_pad)[:, None] * CMP_STRIDE + np.arange(CMP_LEN)[None, :]
    overlap_t = ((tok // SLC_BLOCK)[:, :, None] == np.arange(NSA_FEAT)[None, None, :]).mean(axis=1).astype(np.float32).T
    cmp_spec = pl.BlockSpec((1, 1, n_cmp_pad, CMP_STRIDE * dh), lambda b, g: (b, g, 0, 0))
    q_spec = pl.BlockSpec((seq, r_n * dh), lambda b, g: (b, g))
    k_spec = pl.BlockSpec((seq, LANES), lambda b, g: (b, g))
    v_spec = pl.BlockSpec((1, dh, seq), lambda b, g: (b, g, 0))
    w1s, b1s, w2s = (CMP_LEN * dh, CMP_HIDDEN), (1, CMP_HIDDEN), (CMP_HIDDEN, dh)
    return pl.pallas_call(
        functools.partial(_nsa_kernel, seq=seq),
        out_shape=jax.ShapeDtypeStruct((t, NSA_HEADS * dh), BF16),
        grid=(bsz, g_n),
        in_specs=[q_spec, k_spec, k_spec, v_spec, v_spec, cmp_spec, cmp_spec, k_spec,
                  pl.BlockSpec((1, cols, 1), lambda b, g: (g, 0, 0)),
                  pl.BlockSpec((1, cols, NSA_FEAT), lambda b, g: (g, 0, 0)),
                  _const_spec((n_cmp_pad, NSA_FEAT)),
                  _const_spec((1, CMP_LEN * dh)), _const_spec((1, CMP_LEN * dh)),
                  _const_spec(w1s), _const_spec(b1s), _const_spec(w2s),
                  _const_spec(w1s), _const_spec(b1s), _const_spec((dh, CMP_HIDDEN)),
                  _const_spec((NSA_FEAT, n_cmp_pad))],
        out_specs=q_spec,
        scratch_shapes=[pltpu.VMEM((n_cmp_pad, dh + NSA_FEAT), BF16), pltpu.VMEM((dh, n_cmp_pad), BF16),
                        pltpu.VMEM((seq // span, dh, span), BF16), pltpu.VMEM((seq // span, dh, span), BF16),
                        pltpu.VMEM((3, cols, dh + NSA_FEAT), BF16), pltpu.VMEM((dh, cols), F32)],
        compiler_params=_cparams(2),
        name="nsa_attention",
    )(q, ksa, kwa, vst, vwt, cmp_rows(kc), cmp_rows(vc), gl, slope_rows, sfeat, jnp.asarray(cf),
      pe_k.astype(F32).reshape(1, -1), pe_v.astype(F32).reshape(1, -1),
      k_w1.astype(BF16), k_b1.astype(F32).reshape(1, -1), k_w2.astype(BF16),
      v_w1.astype(BF16), v_b1.astype(F32).reshape(1, -1), v_w2.astype(BF16).T, jnp.asarray(overlap_t, dtype=BF16))


def _pad_cols(w, n):
    return jnp.pad(w, ((0, 0), (0, n - w.shape[1])))


def kernel(x, ab_w_in, ab_w_out, s5_lambda_re, s5_lambda_im, s5_log_step, s5_b_re, s5_b_im, s5_c_re, s5_c_im, s5_d, s5_w_glu, s5_b_glu, gdn_conv_w, gdn_a_log, gdn_dt_bias, gdn_norm_w, nsa_w_in, nsa_w_out, nsa_pe_k, nsa_pe_v, nsa_k_w1, nsa_k_b1, nsa_k_w2, nsa_v_w1, nsa_v_b1, nsa_v_w2, ffn_w_in, ffn_conv_w, ffn_conv_b, ffn_w_out, norm_mix, norm_ffn, norm_final):
    bsz, seq, d = x.shape
    depth = ffn_w_in.shape[0]
    h = x.astype(F32).reshape(bsz * seq, d)
    for layer in range(depth):
        i = layer // 2
        if layer % 2 == 0:
            u, q, k, v, z, ba = _ab_proj(h, norm_mix[layer], ab_w_in[i], gdn_conv_w[i], seq)
            prep = _s5_prep(s5_lambda_re[i], s5_lambda_im[i], s5_log_step[i], s5_b_re[i], s5_b_im[i],
                            s5_c_re[i], s5_c_im[i], s5_d[i])
            y_a = _s5(u, seq, prep, s5_w_glu[i], s5_b_glu[i])
            y_b = _gdn(q, k, v, z, ba, seq, gdn_a_log[i], gdn_dt_bias[i], gdn_norm_w[i])
            wo = ab_w_out[i].astype(BF16)
            mix, wms = [y_a, y_b], [wo[:S5_WIDTH], wo[S5_WIDTH:]]
        else:
            q, ksa, kwa, vst, vwt, kc, vc, gl = _nsa_proj(h, norm_mix[layer], nsa_w_in[i], bsz, seq)
            o = _nsa(q, ksa, kwa, vst, vwt, kc, vc, gl, bsz, seq, nsa_pe_k[i], nsa_pe_v[i], nsa_k_w1[i],
                     nsa_k_b1[i], nsa_k_w2[i], nsa_v_w1[i], nsa_v_b1[i], nsa_v_w2[i])
            mix, wms = [o], [nsa_w_out[i].astype(BF16)]
        h = _mix_ffn(h, mix, wms, seq, norm_ffn[layer], ffn_w_in[layer], ffn_conv_w[layer], ffn_conv_b[layer],
                     ffn_w_out[layer], norm_final, final=(layer == depth - 1))
    return h.reshape(bsz, seq, d).astype(x.dtype)
```

```python
import functools
import math

import numpy as np
import jax
import jax.numpy as jnp
from jax import lax
from jax.experimental import pallas as pl
from jax.experimental.pallas import tpu as pltpu

F32 = jnp.float32
BF16 = jnp.bfloat16
HI = lax.Precision.HIGHEST

D_MODEL = 1024
S5_WIDTH = 256
S5_GROUP = 16
S5_GROUPS = 16
S5_STATE = 64
S5_CHUNK = 16
GDN_HEAD_DIM = 128
GDN_HEADS = 6
GDN_WIDTH = 768
GDN_CONV = 4
GDN_CHUNK = 64
GDN_HEADS_PER_STEP = 3
GDN_CHUNKS_PER_ITER = 8
NSA_HEADS = 16
NSA_HEAD_DIM = 64
NSA_KV_GROUPS = 4
NSA_GROUP_SIZE = 4
NSA_KV_WIDTH = 256
CMP_LEN = 32
CMP_STRIDE = 16
CMP_HIDDEN = 256
SLC_BLOCK = 64
N_SELECT = 4
WINDOW = 256
NSA_Q_BLOCK = 64
NSA_SLAB_KEYS = 512
NSA_QB_PER_ITER = 2
LOG2E = math.log2(math.e)
FFN_HIDDEN = 2816
FFN_CONV = 3
FFN_COL_CHUNK = 256
RMS_EPS = 1e-6
NEG_INF = -1e30
LANES = 128
BF16_SUBLANES = 16
VMEM_LIMIT = 56 * 1024 * 1024
ROW_TILE = 512


def _cparams(n_axes):
    return pltpu.CompilerParams(dimension_semantics=("arbitrary",) * n_axes,
                                vmem_limit_bytes=VMEM_LIMIT)


def _rms(x, g):
    return x * lax.rsqrt(jnp.mean(x * x, axis=-1, keepdims=True) + RMS_EPS) * g


def _gelu(x):
    return 0.5 * x * (1.0 + jnp.tanh(math.sqrt(2.0 / math.pi) * (x + 0.044715 * (x * x * x))))


def _sigmoid(x):
    return 1.0 / (1.0 + jnp.exp(-x))


def _silu(x):
    return x * _sigmoid(x)


def _dot(a, b):
    return jnp.dot(a, b, preferred_element_type=F32)


def _dot_nt(a, b):
    return lax.dot_general(a, b, (((1,), (1,)), ((), ())), preferred_element_type=F32)


def _dot_tn(a, b):
    return lax.dot_general(a, b, (((0,), (0,)), ((), ())), preferred_element_type=F32)


def _const_spec(shape):
    nd = len(shape)
    return pl.BlockSpec(shape, lambda *_: (0,) * nd)


def _ab_proj_kernel(x_ref, g_ref, wu_ref, wq_ref, wk_ref, wv_ref, wz_ref, wba_ref, cw_ref,
                    u_ref, q_ref, k_ref, v_ref, z_ref, ba_ref, ext_ref, *, tiles_per_seq):
    rows = x_ref.shape[0]
    pad = 8
    dh = GDN_HEAD_DIM
    xn = _rms(x_ref[...], g_ref[...]).astype(BF16)
    seq_start = pl.program_id(0) % tiles_per_seq == 0

    @pl.when(pl.program_id(0) == 0)
    def _():
        ext_ref[...] = jnp.zeros_like(ext_ref)

    def finish(idx, pre, o_ref):
        prev = jnp.where(seq_start, 0.0, ext_ref[idx, rows:rows + pad, :])
        ext_ref[idx, 0:pad, :] = prev
        ext_ref[idx, pad:, :] = pre
        cw = cw_ref[idx]
        y = pre * cw[GDN_CONV - 1:GDN_CONV]
        for sh in range(1, GDN_CONV):
            y = y + ext_ref[idx, pad - sh:pad - sh + rows, :] * cw[GDN_CONV - 1 - sh:GDN_CONV - sh]
        y = _silu(y)
        if idx < 2:
            parts = []
            for j in range(GDN_HEADS):
                p = y[:, j * dh:(j + 1) * dh]
                p = p * lax.rsqrt(jnp.sum(p * p, axis=-1, keepdims=True) + 1e-6)
                parts.append(p * (dh ** -0.5) if idx == 0 else p)
            y = jnp.concatenate(parts, axis=1)
        o_ref[...] = y

    pre_q = _dot(xn, wq_ref[...])
    pre_k = _dot(xn, wk_ref[...])
    finish(0, pre_q, q_ref)
    pre_v = _dot(xn, wv_ref[...])
    finish(1, pre_k, k_ref)
    z_ref[...] = _dot(xn, wz_ref[...])
    finish(2, pre_v, v_ref)
    u_ref[...] = _dot(xn, wu_ref[...])
    ba_ref[...] = _dot(xn, wba_ref[...])


def _ab_proj(h, gain, w_in, conv_w, seq):
    t, d = h.shape
    w = w_in.astype(BF16)
    cuts = np.cumsum([0, S5_WIDTH, GDN_WIDTH, GDN_WIDTH, GDN_WIDTH, GDN_WIDTH])
    ws = [w[:, cuts[j]:cuts[j + 1]] for j in range(5)] + [_pad_cols(w[:, cuts[5]:], LANES)]
    cw = conv_w.astype(F32).reshape(GDN_CONV, 3, GDN_WIDTH).transpose(1, 0, 2)
    row_spec = lambda n: pl.BlockSpec((ROW_TILE, n), lambda i: (i, 0))
    return pl.pallas_call(
        functools.partial(_ab_proj_kernel, tiles_per_seq=seq // ROW_TILE),
        out_shape=[jax.ShapeDtypeStruct((t, x.shape[1]), F32) for x in ws],
        grid=(t // ROW_TILE,),
        in_specs=[row_spec(d), _const_spec((1, d))] + [_const_spec(x.shape) for x in ws] + [_const_spec(cw.shape)],
        out_specs=[row_spec(x.shape[1]) for x in ws],
        scratch_shapes=[pltpu.VMEM((3, ROW_TILE + 8, GDN_WIDTH), F32)],
        compiler_params=_cparams(1),
        name="ab_proj",
    )(h, gain.reshape(1, d), *ws, cw)


def _ffn_kernel(h_ref, hp_ref, *refs, n_mix, tiles_per_seq, n_chunks, final):
    y_refs, yp_refs, wm_refs = refs[:n_mix], refs[n_mix:2 * n_mix], refs[2 * n_mix:3 * n_mix]
    (g_ref, win_ref, cw_ref, cb_ref, wout_ref, gf_ref, o_ref,
     xn_ref, up0_ref, up1_ref, act_ref, x1_ref) = refs[3 * n_mix:]
    halo = BF16_SUBLANES
    rows = h_ref.shape[0]
    sub = 128
    fcw = act_ref.shape[2]
    g = g_ref[...]
    n_piece = rows // sub
    x1, x1p = h_ref[...], hp_ref[...]
    for y_ref, yp_ref, wm_ref in zip(y_refs, yp_refs, wm_refs):
        x1 = x1 + _dot(y_ref[...], wm_ref[...])
        x1p = x1p + _dot(yp_ref[...], wm_ref[...])
    x1_ref[...] = x1
    seq_start = pl.program_id(0) % tiles_per_seq == 0
    xn_ref[0:halo, :] = jnp.where(seq_start, 0.0, _rms(x1p, g)).astype(BF16)
    xn_ref[halo:, :] = _rms(x1, g).astype(BF16)

    def gate_piece(j, buf, r):
        r0 = r * sub

        def conv(half, cw, cb):
            y = cb
            for k in range(FFN_CONV):
                start = halo + r0 - (FFN_CONV - 1) + k
                y = y + buf[half, start:start + sub, :] * cw[k:k + 1]
            return y
        a = conv(0, cw_ref[j], cb_ref[j])
        b = conv(1, cw_ref[n_chunks + j], cb_ref[n_chunks + j])
        act_ref[j, r0:r0 + sub, :] = (_silu(a) * b).astype(BF16)

    def up_and_gate(j_up, buf_up, j_gate, buf_gate):
        xe = xn_ref[...]
        for half in range(2):
            col = pl.multiple_of((half * n_chunks + j_up) * fcw, fcw)
            buf_up[half] = _dot(xe, win_ref[:, pl.ds(col, fcw)])
            if j_gate is not None:
                for r in range(half * n_piece // 2, (half + 1) * n_piece // 2):
                    gate_piece(j_gate, buf_gate, r)

    def body(i, carry):
        up_and_gate(2 * i + 1, up1_ref, 2 * i, up0_ref)
        up_and_gate(2 * i + 2, up0_ref, 2 * i + 1, up1_ref)
        return carry

    up_and_gate(0, up0_ref, None, None)
    lax.fori_loop(0, (n_chunks - 1) // 2, body, 0)
    for r in range(n_piece):
        gate_piece(n_chunks - 1, up0_ref, r)

    out = x1_ref[...]
    for j in range(n_chunks):
        out = out + _dot(act_ref[j], wout_ref[j])
    if final:
        out = _rms(out, gf_ref[...])
    o_ref[...] = out


def _mix_ffn(h, ys, wms, seq, gain, w_in, conv_w, conv_b, w_out, gain_final, final):
    t, d = h.shape
    n_mix = len(ys)
    fc = FFN_COL_CHUNK
    n_chunks = FFN_HIDDEN // fc
    assert n_chunks % 2 == 1
    halo = BF16_SUBLANES
    win = w_in.astype(BF16)
    cw = conv_w.astype(F32).reshape(FFN_CONV, 2 * n_chunks, fc).transpose(1, 0, 2)
    cb = conv_b.astype(F32).reshape(2 * n_chunks, 1, fc)
    wout = w_out.astype(BF16).reshape(n_chunks, fc, d)
    blocks_per_tile = ROW_TILE // halo
    tile_spec = lambda n: pl.BlockSpec((ROW_TILE, n), lambda i: (i, 0))
    prev_spec = lambda n: pl.BlockSpec((halo, n), lambda i: (jnp.maximum(i * blocks_per_tile - 1, 0), 0))
    return pl.pallas_call(
        functools.partial(_ffn_kernel, n_mix=n_mix, tiles_per_seq=seq // ROW_TILE, n_chunks=n_chunks, final=final),
        out_shape=jax.ShapeDtypeStruct((t, d), F32),
        grid=(t // ROW_TILE,),
        in_specs=[tile_spec(d), prev_spec(d)]
        + [tile_spec(y.shape[1]) for y in ys] + [prev_spec(y.shape[1]) for y in ys]
        + [_const_spec(w.shape) for w in wms]
        + [_const_spec((1, d)),
                  pl.BlockSpec(win.shape, lambda i: (0, 0), pipeline_mode=pl.Buffered(1)),
                  _const_spec(cw.shape), _const_spec(cb.shape),
                  pl.BlockSpec(wout.shape, lambda i: (0, 0, 0), pipeline_mode=pl.Buffered(1)),
                  _const_spec((1, d))],
        out_specs=pl.BlockSpec((ROW_TILE, d), lambda i: (i, 0)),
        scratch_shapes=[pltpu.VMEM((ROW_TILE + halo, d), BF16), pltpu.VMEM((2, ROW_TILE + halo, fc), F32),
                        pltpu.VMEM((2, ROW_TILE + halo, fc), F32), pltpu.VMEM((n_chunks, ROW_TILE, fc), BF16),
                        pltpu.VMEM((ROW_TILE, d), F32)],
        compiler_params=_cparams(1),
        name="mix_ffn",
    )(h, h, *ys, *ys, *wms, gain.reshape(1, d), win, cw, cb, wout, gain_final.reshape(1, d))


def _s5_prep(lam_re, lam_im, log_step, b_re, b_im, c_re, c_im, d_skip):
    g_n, p_n, h_n, L = S5_GROUPS, S5_STATE, S5_GROUP, S5_CHUNK
    step = jnp.exp(log_step.astype(F32))[:, None]
    lr, li = lam_re.astype(F32), lam_im.astype(F32)
    mag = jnp.exp(lr * step)
    a_re = mag * jnp.cos(li * step)
    a_im = mag * jnp.sin(li * step)
    den = lr * lr + li * li
    n_re, n_im = a_re - 1.0, a_im
    z_re = (n_re * lr + n_im * li) / den
    z_im = (n_im * lr - n_re * li) / den
    b_re, b_im = b_re.astype(F32), b_im.astype(F32)
    bb_re = z_re[..., None] * b_re - z_im[..., None] * b_im
    bb_im = z_re[..., None] * b_im + z_im[..., None] * b_re
    c_re, c_im = c_re.astype(F32), c_im.astype(F32)
    pw_re, pw_im = [jnp.ones_like(a_re)], [jnp.zeros_like(a_im)]
    for _ in range(L):
        pr, pi = pw_re[-1], pw_im[-1]
        pw_re.append(pr * a_re - pi * a_im)
        pw_im.append(pr * a_im + pi * a_re)
    eye_g = jnp.eye(g_n, dtype=F32)
    ks = []
    for j in range(L):
        ab_re = pw_re[j][..., None] * bb_re - pw_im[j][..., None] * bb_im
        ab_im = pw_re[j][..., None] * bb_im + pw_im[j][..., None] * bb_re
        kj = (jnp.einsum('gop,gpi->gio', c_re, ab_re, precision=HI)
              - jnp.einsum('gop,gpi->gio', c_im, ab_im, precision=HI))
        ks.append(jnp.einsum('gio,gk->giko', kj, eye_g).reshape(g_n * h_n, g_n * h_n))
    kstack = jnp.concatenate(ks[::-1], axis=0).astype(BF16)
    bb = jnp.stack([jnp.einsum('gph,gk->ghkp', bb_re, eye_g), jnp.einsum('gph,gk->ghkp', bb_im, eye_g)], axis=2)
    bb = bb.reshape(g_n * h_n, 2 * g_n * p_n).astype(BF16)
    cc = jnp.stack([jnp.einsum('ghp,gk->gpkh', c_re, eye_g), -jnp.einsum('ghp,gk->gpkh', c_im, eye_g)], axis=0)
    cc = cc.reshape(2 * g_n * p_n, g_n * h_n).astype(BF16)
    apow = jnp.stack([jnp.concatenate([r.reshape(-1), i.reshape(-1)]) for r, i in zip(pw_re, pw_im)], axis=0)
    dvec = jnp.tile(d_skip.astype(F32).reshape(1, g_n * h_n), (1, L))
    return kstack, bb, cc, apow, dvec


def _s5_kernel(u_ref, ks_ref, bb_ref, cc_ref, ap_ref, dv_ref, wg_ref, bg_ref, o_ref, x_scr, hp_scr,
               *, n_batch, n_chunk):
    L, w = S5_CHUNK, S5_WIDTH
    half = S5_GROUPS * S5_STATE
    rows = n_batch * n_chunk
    u2 = u_ref[...].reshape(rows, L * w)
    u2b = u2.astype(BF16)

    def cmul(j, x):
        ar, ai = ap_ref[j:j + 1, :half], ap_ref[j:j + 1, half:]
        xr, xi = x[:, :half], x[:, half:]
        return jnp.concatenate([ar * xr - ai * xi, ar * xi + ai * xr], axis=1)

    xin = _dot(u2b[:, (L - 1) * w:], bb_ref[...])
    for s in range(L - 1):
        xin = xin + cmul(L - 1 - s, _dot(u2b[:, s * w:(s + 1) * w], bb_ref[...]))
    x_scr[...] = xin

    a_l = ap_ref[L:L + 1, :]
    alr, ali = a_l[:, :half], a_l[:, half:]

    def scan_body(k, hs):
        new = []
        for b in range(n_batch):
            r = b * n_chunk + k
            h = hs[b]
            hp_scr[pl.ds(r, 1), :] = h
            hr, hi = h[:, :half], h[:, half:]
            new.append(jnp.concatenate([alr * hr - ali * hi, alr * hi + ali * hr], axis=1)
                       + x_scr[pl.ds(r, 1), :])
        return tuple(new)

    lax.fori_loop(0, n_chunk, scan_body, tuple(jnp.zeros((1, 2 * half), F32) for _ in range(n_batch)))
    hp = hp_scr[...]

    for t in range(L):
        y = _dot(u2b[:, :(t + 1) * w], ks_ref[(L - 1 - t) * w:, :])
        y = y + _dot(cmul(t + 1, hp).astype(BF16), cc_ref[...])
        y = y + dv_ref[:, t * w:(t + 1) * w] * u2[:, t * w:(t + 1) * w]
        z = _gelu(y)
        gate = _sigmoid(_dot(z.astype(BF16), wg_ref[...]) + bg_ref[...])
        o_ref[:, :, t * w:(t + 1) * w] = (z * gate).astype(o_ref.dtype).reshape(n_batch, n_chunk, w)


def _s5(u, seq, prep, w_glu, b_glu):
    t = u.shape[0]
    bsz = t // seq
    L, w = S5_CHUNK, S5_WIDTH
    n_chunk = seq // L
    n_batch = 2
    kstack, bb, cc, apow, dvec = prep
    u3 = u.reshape(bsz, n_chunk, L * w)
    out = pl.pallas_call(
        functools.partial(_s5_kernel, n_batch=n_batch, n_chunk=n_chunk),
        out_shape=jax.ShapeDtypeStruct((bsz, n_chunk, L * w), BF16),
        grid=(bsz // n_batch,),
        in_specs=[pl.BlockSpec((n_batch, n_chunk, L * w), lambda i: (i, 0, 0)),
                  _const_spec(kstack.shape), _const_spec(bb.shape), _const_spec(cc.shape),
                  _const_spec(apow.shape), _const_spec(dvec.shape),
                  _const_spec((w, w)), _const_spec((1, w))],
        out_specs=pl.BlockSpec((n_batch, n_chunk, L * w), lambda i: (i, 0, 0)),
        scratch_shapes=[pltpu.VMEM((n_batch * n_chunk, 2 * S5_GROUPS * S5_STATE), F32),
                        pltpu.VMEM((n_batch * n_chunk, 2 * S5_GROUPS * S5_STATE), F32)],
        compiler_params=_cparams(1),
        name="s5_mixer",
    )(u3, kstack, bb, cc, apow, dvec, w_glu.astype(BF16), b_glu.astype(F32).reshape(1, w))
    return out.reshape(t, w)


def _gdn_kernel(qs, ks, vs, z_ref, bc_ref, ac_ref, ar_ref, hp_ref, hr_ref, nw_ref,
                o_ref, xs, ns, qe, o0, gls, gcc_s, gcr_s, xs2, ns2, gls2, *, seq, n_head):
    c, dh = GDN_CHUNK, GDN_HEAD_DIM
    n_chunk = seq // c

    hp = hp_ref[0]
    hr = hr_ref[0]

    def softplus(x):
        return jnp.maximum(x, 0.0) + jnp.log(1.0 + jnp.exp(-jnp.abs(x)))

    ii = lax.broadcasted_iota(jnp.int32, (c, c), 0)
    jj = lax.broadcasted_iota(jnp.int32, (c, c), 1)
    eye = (ii == jj).astype(F32)
    g_col = -jnp.exp(hp[0:1, :]) * softplus(ac_ref[0, 0] + hp[1:2, :])
    g_row = -jnp.exp(hr[:, 0:1]) * softplus(ar_ref[0, 0] + hr[:, 1:2])
    gc_col = jnp.dot((ii >= jj).astype(F32), g_col, precision=HI, preferred_element_type=F32)
    gcr_s[...] = jnp.dot(g_row, (ii <= jj).astype(F32), precision=HI, preferred_element_type=F32)
    for ci in range(n_chunk):
        gcc_s[ci * c:(ci + 1) * c, :] = gc_col[:, ci * n_head:(ci + 1) * n_head]

    def local_body(it, carry):
        pr = []
        for cc in range(GDN_CHUNKS_PER_ITER):
            ci = it * GDN_CHUNKS_PER_ITER + cc
            rs = pl.ds(pl.multiple_of(ci * c, c), c)
            beta = _sigmoid(bc_ref[0, 0, rs, :])
            gccs = gcc_s[rs, :]
            for j in range(n_head):
                sl = slice(j * dh, (j + 1) * dh)
                gcc = gccs[:, j:j + 1]
                gcr = gcr_s[pl.ds(ci * n_head + j, 1), :]
                q, k, v = qs[rs, sl], ks[rs, sl], vs[rs, sl]
                bj = beta[:, j:j + 1]
                kb = k * bj
                e_g = jnp.exp(gcc)
                g_last = gcc[c - 1:c, :]
                pr.append(dict(ci=ci, j=j, q=q, kf=k.astype(BF16), kbb=kb.astype(BF16),
                               decay=jnp.exp(jnp.where(ii >= jj, gcc - gcr, NEG_INF)),
                               rhs=jnp.concatenate([v * bj, kb * e_g], axis=1).astype(BF16),
                               qd=q * e_g, kd=(k * jnp.exp(g_last - gcc)).astype(BF16),
                               gl=jnp.broadcast_to(jnp.exp(g_last), (8, dh))))
        kk = [_dot_nt(p["kbb"], p["kf"]) for p in pr]
        qk = [_dot_nt(p["q"].astype(BF16), p["kf"]) for p in pr]
        ps = [-jnp.where(ii > jj, a * p["decay"], 0.0) for a, p in zip(kk, pr)]
        ts = [eye + n for n in ps]
        span = 2
        while span < c:
            ps = [_dot(n.astype(BF16), n.astype(BF16)) for n in ps]
            ts = [t + _dot(t.astype(BF16), n.astype(BF16)) for t, n in zip(ts, ps)]
            span *= 2
        uws = [_dot(t.astype(BF16), p["rhs"]).astype(BF16) for t, p in zip(ts, pr)]
        intras = [(s * p["decay"]).astype(BF16) for s, p in zip(qk, pr)]
        i_uws = [_dot(a, uw) for a, uw in zip(intras, uws)]
        kd_uws = [_dot_tn(p["kd"], uw) for p, uw in zip(pr, uws)]
        for p, i_uw, kd_uw in zip(pr, i_uws, kd_uws):
            j, ci = p["j"], p["ci"]
            ns[j, ci] = kd_uw[:, :dh]
            xs[j, ci] = kd_uw[:, dh:].astype(BF16)
            o0[j, ci] = i_uw[:, :dh]
            qe[j, ci] = (p["qd"] - i_uw[:, dh:]).astype(BF16)
            gls[j, ci] = p["gl"]
        first = [idx for idx, p in enumerate(pr) if (idx // n_head) % 2 == 0]
        x1b = [kd_uws[idx + n_head][:, dh:].astype(BF16) for idx in first]
        x1x0 = [_dot(x1, kd_uws[idx][:, dh:].astype(BF16)) for x1, idx in zip(x1b, first)]
        x1n0 = [_dot(x1, kd_uws[idx][:, :dh].astype(BF16)) for x1, idx in zip(x1b, first)]
        for idx, xx, xn in zip(first, x1x0, x1n0):
            p0, p1 = pr[idx], pr[idx + n_head]
            g0, g1 = p0["gl"][0:1, :], p1["gl"][0:1, :]
            pi = it * (GDN_CHUNKS_PER_ITER // 2) + (idx // n_head) // 2
            xs2[p0["j"], pi] = (g1 * kd_uws[idx][:, dh:] + g0 * kd_uws[idx + n_head][:, dh:] - xx).astype(BF16)
            ns2[p0["j"], pi] = g1 * kd_uws[idx][:, :dh] - xn + kd_uws[idx + n_head][:, :dh]
            gls2[p0["j"], pi] = p0["gl"] * p1["gl"]
        return carry

    lax.fori_loop(0, n_chunk // GDN_CHUNKS_PER_ITER, local_body, 0)

    def rec_body(pi, states):
        c0, c1 = 2 * pi, 2 * pi + 1
        s_bs = [s.astype(BF16) for s in states]
        xd2 = [_dot(xs2[j, pi], s_bs[j]) for j in range(n_head)]
        xd0 = [_dot(xs[j, c0], s_bs[j]) for j in range(n_head)]
        od0 = [_dot(qe[j, c0], s_bs[j]) for j in range(n_head)]
        new = tuple(gls2[j, pi][0:1, :] * states[j] - xd2[j] + ns2[j, pi] for j in range(n_head))
        mid = [(gls[j, c0][0:1, :] * states[j] - xd0[j] + ns[j, c0]).astype(BF16) for j in range(n_head)]
        od1 = [_dot(qe[j, c1], mid[j]) for j in range(n_head)]
        for ci, od in ((c0, od0), (c1, od1)):
            rs = pl.ds(pl.multiple_of(ci * c, c), c)
            for j in range(n_head):
                sl = slice(j * dh, (j + 1) * dh)
                o = od[j] + o0[j, ci]
                on = o * lax.rsqrt(jnp.mean(o * o, axis=-1, keepdims=True) + RMS_EPS) * nw_ref[...]
                o_ref[rs, sl] = (on * _silu(z_ref[rs, sl])).astype(o_ref.dtype)
        return new

    lax.fori_loop(0, n_chunk // 2, rec_body, tuple(jnp.zeros((dh, dh), F32) for _ in range(n_head)), unroll=2)


def _gdn(q, k, v, z, ba, seq, a_log, dt_bias, norm_w):
    t = q.shape[0]
    bsz = t // seq
    nh = GDN_HEADS_PER_STEP
    n_grp = GDN_HEADS // nh
    wd = nh * GDN_HEAD_DIM
    bl = ba[:, :GDN_HEADS].reshape(bsz, seq, n_grp, nh)
    al = ba[:, GDN_HEADS:2 * GDN_HEADS].reshape(bsz, seq, n_grp, nh)
    b_col = bl.transpose(0, 2, 1, 3)
    n_chunk = seq // GDN_CHUNK
    n_prob = n_chunk * nh
    a5 = al.reshape(bsz, n_chunk, GDN_CHUNK, n_grp, nh)
    a_col = a5.transpose(0, 3, 2, 1, 4).reshape(bsz, n_grp, GDN_CHUNK, n_prob)
    a_row = a5.transpose(0, 3, 1, 4, 2).reshape(bsz, n_grp, n_prob, GDN_CHUNK)
    hp = jnp.stack([jnp.tile(a_log.astype(F32).reshape(n_grp, nh), (1, n_chunk)),
                    jnp.tile(dt_bias.astype(F32).reshape(n_grp, nh), (1, n_chunk))], axis=1)
    hr = hp.transpose(0, 2, 1)
    act_spec = pl.BlockSpec((seq, wd), lambda b, g: (b, g))
    col_spec = pl.BlockSpec((1, 1, seq, nh), lambda b, g: (b, g, 0, 0))
    return pl.pallas_call(
        functools.partial(_gdn_kernel, seq=seq, n_head=nh),
        out_shape=jax.ShapeDtypeStruct((t, GDN_WIDTH), BF16),
        grid=(bsz, n_grp),
        in_specs=[act_spec, act_spec, act_spec, act_spec, col_spec,
                  pl.BlockSpec((1, 1, GDN_CHUNK, n_prob), lambda b, g: (b, g, 0, 0)),
                  pl.BlockSpec((1, 1, n_prob, GDN_CHUNK), lambda b, g: (b, g, 0, 0)),
                  pl.BlockSpec((1, 2, n_prob), lambda b, g: (g, 0, 0)),
                  pl.BlockSpec((1, n_prob, 2), lambda b, g: (g, 0, 0)),
                  _const_spec((1, GDN_HEAD_DIM))],
        out_specs=act_spec,
        scratch_shapes=[pltpu.VMEM((nh, n_chunk, GDN_HEAD_DIM, GDN_HEAD_DIM), BF16),
                        pltpu.VMEM((nh, n_chunk, GDN_HEAD_DIM, GDN_HEAD_DIM), F32),
                        pltpu.VMEM((nh, n_chunk, GDN_CHUNK, GDN_HEAD_DIM), BF16),
                        pltpu.VMEM((nh, n_chunk, GDN_CHUNK, GDN_HEAD_DIM), F32),
                        pltpu.VMEM((nh, n_chunk, 8, GDN_HEAD_DIM), F32),
                        pltpu.VMEM((seq, nh), F32), pltpu.VMEM((n_prob, GDN_CHUNK), F32),
                        pltpu.VMEM((nh, n_chunk // 2, GDN_HEAD_DIM, GDN_HEAD_DIM), BF16),
                        pltpu.VMEM((nh, n_chunk // 2, GDN_HEAD_DIM, GDN_HEAD_DIM), F32),
                        pltpu.VMEM((nh, n_chunk // 2, 8, GDN_HEAD_DIM), F32)],
        compiler_params=_cparams(2),
        name="gated_deltanet",
    )(q, k, v, z, b_col, a_col, a_row, hp, hr, norm_w.astype(F32).reshape(1, GDN_HEAD_DIM))


NSA_FEAT = 64
F_BLK, F_POS, F_ONE = 0, 32, 38


def _split3(x):
    x1 = x.astype(BF16).astype(F32)
    x2 = (x - x1).astype(BF16).astype(F32)
    return x1, x2, x - x1 - x2


def _nsa_proj_kernel(x_ref, g_ref, wq_ref, wks_ref, wkw_ref, wvs_ref, wvw_ref, wkc_ref, wvc_ref, wg_ref, kf_ref,
                     q_ref, ksa_ref, kwa_ref, vst_ref, vwt_ref, kc_ref, vc_ref, gl_ref):
    xn = _rms(x_ref[...], g_ref[...]).astype(BF16)
    q_ref[...] = (_dot(xn, wq_ref[...]) * (NSA_HEAD_DIM ** -0.5 * LOG2E)).astype(BF16)
    kf = jnp.concatenate([kf_ref[...]] * NSA_KV_GROUPS, axis=1)
    ksa_ref[...] = (_dot(xn, wks_ref[...]) + kf).astype(BF16)
    kwa_ref[...] = (_dot(xn, wkw_ref[...]) + kf).astype(BF16)
    vst_ref[0] = _dot_nt(wvs_ref[...], xn).astype(BF16)
    vwt_ref[0] = _dot_nt(wvw_ref[...], xn).astype(BF16)
    kc_ref[...] = _dot(xn, wkc_ref[...]).astype(BF16)
    vc_ref[...] = _dot(xn, wvc_ref[...]).astype(BF16)
    gl_ref[...] = _dot(xn, wg_ref[...])


def _nsa_proj(h, gain, w_in, bsz, seq):
    t, d = h.shape
    g_n, dh = NSA_KV_GROUPS, NSA_HEAD_DIM
    nq, kvw = NSA_HEADS * dh, NSA_KV_WIDTH
    w = w_in.astype(BF16)
    wq = w[:, :nq]
    wkc, wvc, wks, wvs, wkw, wvw = (w[:, nq + j * kvw: nq + (j + 1) * kvw] for j in range(6))
    wgl = w[:, nq + 6 * kvw:]

    def lane_padded(x, width):
        return jnp.pad(x.reshape(d, g_n, width), ((0, 0), (0, 0), (0, LANES - width))).reshape(d, g_n * LANES)

    pos = np.arange(seq)
    kf = np.zeros((seq, LANES), np.float32)
    kf[pos, dh + F_BLK + pos // SLC_BLOCK] = 1.0
    kf[:, dh + F_POS:dh + F_POS + 3] = ((pos // SLC_BLOCK) * SLC_BLOCK)[:, None]
    kf[:, dh + F_POS + 3:dh + F_POS + 6] = (pos % SLC_BLOCK)[:, None]
    kf[:, dh + F_ONE:dh + F_ONE + 3] = 1.0
    tps = seq // ROW_TILE
    row_spec = lambda n: pl.BlockSpec((ROW_TILE, n), lambda i: (i, 0))
    tr_spec = pl.BlockSpec((1, kvw, ROW_TILE), lambda i: (i // tps, 0, i % tps))
    ws = [wq, lane_padded(wks, dh), lane_padded(wkw, dh), wvs.T, wvw.T, wkc, wvc, lane_padded(wgl, 3 * NSA_GROUP_SIZE)]
    return pl.pallas_call(
        _nsa_proj_kernel,
        out_shape=[jax.ShapeDtypeStruct((t, nq), BF16),
                   jax.ShapeDtypeStruct((t, g_n * LANES), BF16), jax.ShapeDtypeStruct((t, g_n * LANES), BF16),
                   jax.ShapeDtypeStruct((bsz, kvw, seq), BF16), jax.ShapeDtypeStruct((bsz, kvw, seq), BF16),
                   jax.ShapeDtypeStruct((t, kvw), BF16), jax.ShapeDtypeStruct((t, kvw), BF16),
                   jax.ShapeDtypeStruct((t, g_n * LANES), F32)],
        grid=(t // ROW_TILE,),
        in_specs=[row_spec(d), _const_spec((1, d))] + [_const_spec(x.shape) for x in ws]
        + [pl.BlockSpec((ROW_TILE, LANES), lambda i: (i % tps, 0))],
        out_specs=[row_spec(nq), row_spec(g_n * LANES), row_spec(g_n * LANES), tr_spec, tr_spec,
                   row_spec(kvw), row_spec(kvw), row_spec(g_n * LANES)],
        compiler_params=_cparams(1),
        name="nsa_proj",
    )(h, gain.reshape(1, d), *ws, jnp.asarray(kf))


def _nsa_kernel(q_ref, ksa_ref, kwa_ref, vst_ref, vwt_ref, kca_ref, vca_ref, gl_ref, sl_ref, sf_ref, cf_ref,
                kw1_ref, kc1_ref, kw2_ref, vw1_ref, vc1_ref, vw2t_ref, ovt_ref, o_ref,
                kcmp, vcmpt, vs3, vw3, qa_s, oc_s, *, seq):
    qb, dh, r_n, nq = NSA_Q_BLOCK, NSA_HEAD_DIM, NSA_GROUP_SIZE, NSA_QB_PER_ITER
    span = nq * qb
    cols = nq * r_n * qb
    n_cmp_pad = seq // CMP_STRIDE
    n_cmp = (seq - CMP_LEN) // CMP_STRIDE + 1
    half = CMP_STRIDE * dh
    n_blk = seq // SLC_BLOCK
    n_win = WINDOW + span
    it_per_slab = NSA_SLAB_KEYS // span
    nf = NSA_FEAT

    def hidden(a_ref, w1_ref, c1_ref):
        a = a_ref[0, 0]
        top = _dot(a, w1_ref[:half, :])
        bot = pltpu.roll(_dot(a, w1_ref[half:, :]), n_cmp_pad - 1, 0)
        return _gelu(top + bot + c1_ref[...]).astype(BF16)

    kcmp[...] = jnp.concatenate([_dot(hidden(kca_ref, kw1_ref, kc1_ref), kw2_ref[...]), cf_ref[...]],
                                axis=1).astype(BF16)
    vcmpt[...] = _dot_nt(vw2t_ref[...], hidden(vca_ref, vw1_ref, vc1_ref)).astype(BF16)
    for n in range(seq // span):
        vs3[n] = vst_ref[0, :, n * span:(n + 1) * span]
        vw3[n] = vwt_ref[0, :, n * span:(n + 1) * span]

    def iota(shape, axis):
        return lax.broadcasted_iota(jnp.int32, shape, axis)

    slope = sl_ref[0]
    colc, colr = iota((cols, 1), 0), iota((1, cols), 1)
    tc_col = (colc // (r_n * qb)) * qb + colc % qb
    tc_row = (colr // (r_n * qb)) * qb + colr % qb
    qi_row = colr // (r_n * qb)
    lane_f = iota((1, nf), 1)
    jr = iota((n_cmp_pad, 1), 0)
    nr = iota((nf, 1), 0)
    lane_s = iota((1, span), 1)
    qi_s = lane_s // qb
    key_d = iota((span, 1), 0)
    key_w = iota((n_win, 1), 0)
    eye_s = (iota((span, span), 0) == iota((span, span), 1)).astype(BF16)

    def prepare(p):
        t0 = pl.multiple_of(p * span, span)
        qf = jnp.concatenate([q_ref[pl.ds(t0 + qi * qb, qb), r * dh:(r + 1) * dh]
                              for qi in range(nq) for r in range(r_n)], axis=0).astype(F32)
        c1, c2, c3 = _split3(-(slope * (t0 + tc_col).astype(F32)))
        f_plain = jnp.where(lane_f == F_ONE, c1, jnp.where(lane_f == F_ONE + 1, c2,
                            jnp.where(lane_f == F_ONE + 2, c3, sf_ref[0])))
        qa_plain = jnp.concatenate([qf, f_plain], axis=1).astype(BF16)

        ok_c = (jr * CMP_STRIDE + (CMP_LEN - 1) <= t0 + tc_row) & (jr < n_cmp)
        s_c = jnp.where(ok_c, _dot_nt(kcmp[...], qa_plain), NEG_INF)
        e_c = jnp.where(ok_c, jnp.exp2(s_c - jnp.max(s_c, axis=0, keepdims=True)), 0.0)
        l_c = jnp.sum(e_c, axis=0, keepdims=True)
        p_c = e_c / jnp.where(l_c > 0.0, l_c, 1.0)
        oc_s[...] = _dot(vcmpt[...], p_c.astype(BF16))

        imp4 = sum(_dot(ovt_ref[...], part.astype(BF16)) for part in _split3(p_c))
        halves = []
        for qi in range(nq):
            a = imp4[:, (2 * qi) * LANES:(2 * qi + 1) * LANES] + imp4[:, (2 * qi + 1) * LANES:(2 * qi + 2) * LANES]
            halves.append(a + pltpu.roll(a, qb, 1))
        low_half = lax.broadcasted_iota(jnp.int32, (1, LANES), 1) < qb
        imp = jnp.concatenate([jnp.where(low_half, halves[2 * v], halves[2 * v + 1]) for v in range(nq // 2)],
                              axis=1)

        iq = nq * p + qi_s
        cand = (nr > 0) & (nr < iq)
        sc = jnp.where(cand, imp, NEG_INF)
        sel = (nr == 0) & (iq > 0)
        for _ in range(N_SELECT - 2):
            best = jnp.max(sc, axis=0, keepdims=True)
            first = jnp.min(jnp.where(sc == best, nr, nf), axis=0, keepdims=True)
            pick = (nr == first) & cand
            sel = sel | pick
            sc = jnp.where(pick, NEG_INF, sc)
        neg_slab = jnp.where(sel & (nr < nq * p), 0.0, NEG_INF)
        neg_cur = jnp.where(nr == iq, 0.0, jnp.where((nr >= nq * p) & (nr < iq) & sel, 0.0, NEG_INF))
        neg_t = jnp.concatenate([neg_slab, neg_cur], axis=0).astype(BF16)
        neg = _dot_nt(eye_s, neg_t)
        rows_of = lambda x: jnp.concatenate([x[qi * qb:(qi + 1) * qb] for qi in range(nq) for _ in range(r_n)], axis=0)
        f_slab = jnp.where(lane_f < n_blk, rows_of(neg[:, :nf]), f_plain)
        f_cur = jnp.where(lane_f < n_blk, rows_of(neg[:, nf:]), f_plain)
        qa_s[0] = qa_plain
        qa_s[1] = jnp.concatenate([qf, f_slab], axis=1).astype(BF16)
        qa_s[2] = jnp.concatenate([qf, f_cur], axis=1).astype(BF16)

    n_iter = seq // span

    def make_body(n_keys):
        def body(p, carry):
            t0 = pl.multiple_of(p * span, span)
            w0 = pl.multiple_of(jnp.maximum(t0 - WINDOW, 0), span)
            oc_t = oc_s[...]
            s_cu = _dot_nt(ksa_ref[pl.ds(t0, span), :], qa_s[2])
            s_w = _dot_nt(kwa_ref[pl.ds(w0, n_win), :], qa_s[0])
            s_sl = _dot_nt(ksa_ref[0:n_keys, :], qa_s[1])
            prepare(jnp.minimum(p + 1, n_iter - 1))

            own = (key_d // qb) == qi_row
            s_cu = jnp.where(own & (key_d % qb > tc_row % qb), NEG_INF, s_cu)
            m_s = jnp.maximum(jnp.max(s_sl, axis=0, keepdims=True), jnp.max(s_cu, axis=0, keepdims=True))
            e_sl = jnp.exp2(s_sl - m_s)
            e_cu = jnp.exp2(s_cu - m_s)
            l_s = jnp.sum(e_sl, axis=0, keepdims=True) + jnp.sum(e_cu, axis=0, keepdims=True)
            os_t = (_dot(vst_ref[0, :, 0:n_keys], e_sl.astype(BF16)) + _dot(vs3[p], e_cu.astype(BF16))) / l_s

            rel = (t0 - w0) + tc_row - key_w
            s_w = jnp.where((rel >= 0) & (rel < WINDOW), s_w, NEG_INF)
            e_w = jnp.exp2(s_w - jnp.max(s_w, axis=0, keepdims=True))
            e_wb = e_w.astype(BF16)
            b0 = w0 // span
            ow_t = _dot(vw3[b0], e_wb[0:span])
            for j in range(1, n_win // span):
                ow_t = ow_t + _dot(vw3[b0 + j], e_wb[j * span:(j + 1) * span])
            ow_t = ow_t / jnp.sum(e_w, axis=0, keepdims=True)

            o_all = jnp.concatenate([os_t, ow_t, oc_t, jnp.zeros_like(oc_t)], axis=0).T
            gate = _sigmoid(gl_ref[pl.ds(t0, span), 0:3 * r_n])
            for qi in range(nq):
                g_q = gate[qi * qb:(qi + 1) * qb]
                heads = []
                for r in range(r_n):
                    rs = slice((qi * r_n + r) * qb, (qi * r_n + r + 1) * qb)
                    heads.append(g_q[:, 3 * r:3 * r + 1] * o_all[rs, 2 * dh:3 * dh]
                                 + g_q[:, 3 * r + 1:3 * r + 2] * o_all[rs, :dh]
                                 + g_q[:, 3 * r + 2:3 * r + 3] * o_all[rs, dh:2 * dh])
                o_ref[pl.ds(t0 + qi * qb, qb), :] = jnp.concatenate(heads, axis=1).astype(o_ref.dtype)
            return carry
        return body

    prepare(jnp.int32(0))
    for sb in range(seq // NSA_SLAB_KEYS):
        lax.fori_loop(sb * it_per_slab, (sb + 1) * it_per_slab, make_body((sb + 1) * NSA_SLAB_KEYS), 0)


def _nsa(q, ksa, kwa, vst, vwt, kc, vc, gl, bsz, seq, pe_k, pe_v, k_w1, k_b1, k_w2, v_w1, v_b1, v_w2):
    g_n, r_n, dh, qb = NSA_KV_GROUPS, NSA_GROUP_SIZE, NSA_HEAD_DIM, NSA_Q_BLOCK
    t = bsz * seq
    n_cmp_pad = seq // CMP_STRIDE
    n_blk = seq // SLC_BLOCK
    span = NSA_QB_PER_ITER * qb
    assert n_blk <= F_POS and seq % NSA_SLAB_KEYS == 0 and seq >= WINDOW + span

    def cmp_rows(x):
        return (x.reshape(bsz, n_cmp_pad, CMP_STRIDE, g_n, dh).transpose(0, 3, 1, 2, 4)
                .reshape(bsz, g_n, n_cmp_pad, CMP_STRIDE * dh))

    slopes = np.asarray([2.0 ** (-8.0 * (h + 1) / NSA_HEADS) for h in range(NSA_HEADS)], dtype=np.float32)
    cols = NSA_QB_PER_ITER * r_n * qb
    slope_rows = jnp.asarray(np.tile(np.repeat(slopes.reshape(g_n, r_n), qb, axis=1), (1, NSA_QB_PER_ITER))
                             .reshape(g_n, cols, 1)) * LOG2E
    s3 = jnp.concatenate(_split3(slope_rows) * 2, axis=2)
    sfeat = jnp.pad(s3, ((0, 0), (0, 0), (F_POS, NSA_FEAT - F_POS - 6)))
    cf = np.zeros((n_cmp_pad, NSA_FEAT), np.float32)
    cf[:, F_POS:F_POS + 3] = (np.arange(n_cmp_pad) * CMP_STRIDE)[:, None]
    cf[:, F_POS + 3:F_POS + 6] = (CMP_LEN - 1) * 0.5
    cf[:, F_ONE:F_ONE + 3] = 1.0
    tok = np.arange(n_cmp_pad)[:, None] * CMP_STRIDE + np.arange(CMP_LEN)[None, :]
    overlap_t = ((tok // SLC_BLOCK)[:, :, None] == np.arange(NSA_FEAT)[None, None, :]).mean(axis=1).astype(np.float32).T
    def pos_const(pe, w1, b1):
        return _dot(pe.astype(BF16).reshape(1, -1), w1.astype(BF16)) + b1.astype(F32).reshape(1, -1)

    cmp_spec = pl.BlockSpec((1, 1, n_cmp_pad, CMP_STRIDE * dh), lambda b, g: (b, g, 0, 0))
    q_spec = pl.BlockSpec((seq, r_n * dh), lambda b, g: (b, g))
    k_spec = pl.BlockSpec((seq, LANES), lambda b, g: (b, g))
    v_spec = pl.BlockSpec((1, dh, seq), lambda b, g: (b, g, 0))
    w1s, b1s, w2s = (CMP_LEN * dh, CMP_HIDDEN), (1, CMP_HIDDEN), (CMP_HIDDEN, dh)
    return pl.pallas_call(
        functools.partial(_nsa_kernel, seq=seq),
        out_shape=jax.ShapeDtypeStruct((t, NSA_HEADS * dh), BF16),
        grid=(bsz, g_n),
        in_specs=[q_spec, k_spec, k_spec, v_spec, v_spec, cmp_spec, cmp_spec, k_spec,
                  pl.BlockSpec((1, cols, 1), lambda b, g: (g, 0, 0)),
                  pl.BlockSpec((1, cols, NSA_FEAT), lambda b, g: (g, 0, 0)),
                  _const_spec((n_cmp_pad, NSA_FEAT)),
                  _const_spec(w1s), _const_spec(b1s), _const_spec(w2s),
                  _const_spec(w1s), _const_spec(b1s), _const_spec((dh, CMP_HIDDEN)),
                  _const_spec((NSA_FEAT, n_cmp_pad))],
        out_specs=q_spec,
        scratch_shapes=[pltpu.VMEM((n_cmp_pad, dh + NSA_FEAT), BF16), pltpu.VMEM((dh, n_cmp_pad), BF16),
                        pltpu.VMEM((seq // span, dh, span), BF16), pltpu.VMEM((seq // span, dh, span), BF16),
                        pltpu.VMEM((3, cols, dh + NSA_FEAT), BF16), pltpu.VMEM((dh, cols), F32)],
        compiler_params=_cparams(2),
        name="nsa_attention",
    )(q, ksa, kwa, vst, vwt, cmp_rows(kc), cmp_rows(vc), gl, slope_rows, sfeat, jnp.asarray(cf),
      k_w1.astype(BF16), pos_const(pe_k, k_w1, k_b1), k_w2.astype(BF16),
      v_w1.astype(BF16), pos_const(pe_v, v_w1, v_b1), v_w2.astype(BF16).T, jnp.asarray(overlap_t, dtype=BF16))


def _pad_cols(w, n):
    return jnp.pad(w, ((0, 0), (0, n - w.shape[1])))


def kernel(x, ab_w_in, ab_w_out, s5_lambda_re, s5_lambda_im, s5_log_step, s5_b_re, s5_b_im, s5_c_re, s5_c_im, s5_d, s5_w_glu, s5_b_glu, gdn_conv_w, gdn_a_log, gdn_dt_bias, gdn_norm_w, nsa_w_in, nsa_w_out, nsa_pe_k, nsa_pe_v, nsa_k_w1, nsa_k_b1, nsa_k_w2, nsa_v_w1, nsa_v_b1, nsa_v_w2, ffn_w_in, ffn_conv_w, ffn_conv_b, ffn_w_out, norm_mix, norm_ffn, norm_final):
    bsz, seq, d = x.shape
    depth = ffn_w_in.shape[0]
    h = x.astype(F32).reshape(bsz * seq, d)
    for layer in range(depth):
        i = layer // 2
        if layer % 2 == 0:
            u, q, k, v, z, ba = _ab_proj(h, norm_mix[layer], ab_w_in[i], gdn_conv_w[i], seq)
            prep = _s5_prep(s5_lambda_re[i], s5_lambda_im[i], s5_log_step[i], s5_b_re[i], s5_b_im[i],
                            s5_c_re[i], s5_c_im[i], s5_d[i])
            y_a = _s5(u, seq, prep, s5_w_glu[i], s5_b_glu[i])
            y_b = _gdn(q, k, v, z, ba, seq, gdn_a_log[i], gdn_dt_bias[i], gdn_norm_w[i])
            wo = ab_w_out[i].astype(BF16)
            mix, wms = [y_a, y_b], [wo[:S5_WIDTH], wo[S5_WIDTH:]]
        else:
            q, ksa, kwa, vst, vwt, kc, vc, gl = _nsa_proj(h, norm_mix[layer], nsa_w_in[i], bsz, seq)
            o = _nsa(q, ksa, kwa, vst, vwt, kc, vc, gl, bsz, seq, nsa_pe_k[i], nsa_pe_v[i], nsa_k_w1[i],
                     nsa_k_b1[i], nsa_k_w2[i], nsa_v_w1[i], nsa_v_b1[i], nsa_v_w2[i])
            mix, wms = [o], [nsa_w_out[i].astype(BF16)]
        h = _mix_ffn(h, mix, wms, seq, norm_ffn[layer], ffn_w_in[layer], ffn_conv_w[layer], ffn_conv_b[layer],
                     ffn_w_out[layer], norm_final, final=(layer == depth - 1))
    return h.reshape(bsz, seq, d).astype(x.dtype)
```

```python
import functools
import math

import numpy as np
import jax
import jax.numpy as jnp
from jax import lax
from jax.experimental import pallas as pl
from jax.experimental.pallas import tpu as pltpu

F32 = jnp.float32
BF16 = jnp.bfloat16
HI = lax.Precision.HIGHEST

D_MODEL = 1024
S5_WIDTH = 256
S5_GROUP = 16
S5_GROUPS = 16
S5_STATE = 64
S5_CHUNK = 16
GDN_HEAD_DIM = 128
GDN_HEADS = 6
GDN_WIDTH = 768
GDN_CONV = 4
GDN_CHUNK = 64
GDN_HEADS_PER_STEP = 3
GDN_CHUNKS_PER_ITER = 8
NSA_HEADS = 16
NSA_HEAD_DIM = 64
NSA_KV_GROUPS = 4
NSA_GROUP_SIZE = 4
NSA_KV_WIDTH = 256
CMP_LEN = 32
CMP_STRIDE = 16
CMP_HIDDEN = 256
SLC_BLOCK = 64
N_SELECT = 4
WINDOW = 256
NSA_Q_BLOCK = 64
NSA_SLAB_KEYS = 512
NSA_QB_PER_ITER = 2
LOG2E = math.log2(math.e)
FFN_HIDDEN = 2816
FFN_CONV = 3
FFN_COL_CHUNK = 256
RMS_EPS = 1e-6
NEG_INF = -1e30
LANES = 128
BF16_SUBLANES = 16
VMEM_LIMIT = 56 * 1024 * 1024
ROW_TILE = 512


def _cparams(n_axes):
    return pltpu.CompilerParams(dimension_semantics=("arbitrary",) * n_axes,
                                vmem_limit_bytes=VMEM_LIMIT)


def _rms(x, g):
    return x * lax.rsqrt(jnp.mean(x * x, axis=-1, keepdims=True) + RMS_EPS) * g


def _gelu(x):
    return 0.5 * x * (1.0 + jnp.tanh(math.sqrt(2.0 / math.pi) * (x + 0.044715 * (x * x * x))))


def _sigmoid(x):
    return 1.0 / (1.0 + jnp.exp(-x))


def _silu(x):
    return x * _sigmoid(x)


def _dot(a, b):
    return jnp.dot(a, b, preferred_element_type=F32)


def _dot_nt(a, b):
    return lax.dot_general(a, b, (((1,), (1,)), ((), ())), preferred_element_type=F32)


def _dot_tn(a, b):
    return lax.dot_general(a, b, (((0,), (0,)), ((), ())), preferred_element_type=F32)


def _const_spec(shape):
    nd = len(shape)
    return pl.BlockSpec(shape, lambda *_: (0,) * nd)


def _ab_proj_kernel(x_ref, g_ref, wu_ref, wq_ref, wk_ref, wv_ref, wz_ref, wba_ref, cw_ref,
                    u_ref, q_ref, k_ref, v_ref, z_ref, ba_ref, ext_ref, *, tiles_per_seq):
    rows = x_ref.shape[0]
    pad = 8
    dh = GDN_HEAD_DIM
    xn = _rms(x_ref[...], g_ref[...]).astype(BF16)
    seq_start = pl.program_id(0) % tiles_per_seq == 0

    @pl.when(pl.program_id(0) == 0)
    def _():
        ext_ref[...] = jnp.zeros_like(ext_ref)

    def finish(idx, pre, o_ref):
        prev = jnp.where(seq_start, 0.0, ext_ref[idx, rows:rows + pad, :])
        ext_ref[idx, 0:pad, :] = prev
        ext_ref[idx, pad:, :] = pre
        cw = cw_ref[idx]
        y = pre * cw[GDN_CONV - 1:GDN_CONV]
        for sh in range(1, GDN_CONV):
            y = y + ext_ref[idx, pad - sh:pad - sh + rows, :] * cw[GDN_CONV - 1 - sh:GDN_CONV - sh]
        y = _silu(y)
        if idx < 2:
            parts = []
            for j in range(GDN_HEADS):
                p = y[:, j * dh:(j + 1) * dh]
                p = p * lax.rsqrt(jnp.sum(p * p, axis=-1, keepdims=True) + 1e-6)
                parts.append(p * (dh ** -0.5) if idx == 0 else p)
            y = jnp.concatenate(parts, axis=1)
        o_ref[...] = y

    pre_q = _dot(xn, wq_ref[...])
    pre_k = _dot(xn, wk_ref[...])
    finish(0, pre_q, q_ref)
    pre_v = _dot(xn, wv_ref[...])
    finish(1, pre_k, k_ref)
    z_ref[...] = _dot(xn, wz_ref[...])
    finish(2, pre_v, v_ref)
    u_ref[...] = _dot(xn, wu_ref[...])
    ba_ref[...] = _dot(xn, wba_ref[...])


def _ab_proj(h, gain, w_in, conv_w, seq):
    t, d = h.shape
    w = w_in.astype(BF16)
    cuts = np.cumsum([0, S5_WIDTH, GDN_WIDTH, GDN_WIDTH, GDN_WIDTH, GDN_WIDTH])
    ws = [w[:, cuts[j]:cuts[j + 1]] for j in range(5)] + [_pad_cols(w[:, cuts[5]:], LANES)]
    cw = conv_w.astype(F32).reshape(GDN_CONV, 3, GDN_WIDTH).transpose(1, 0, 2)
    row_spec = lambda n: pl.BlockSpec((ROW_TILE, n), lambda i: (i, 0))
    return pl.pallas_call(
        functools.partial(_ab_proj_kernel, tiles_per_seq=seq // ROW_TILE),
        out_shape=[jax.ShapeDtypeStruct((t, x.shape[1]), F32) for x in ws],
        grid=(t // ROW_TILE,),
        in_specs=[row_spec(d), _const_spec((1, d))] + [_const_spec(x.shape) for x in ws] + [_const_spec(cw.shape)],
        out_specs=[row_spec(x.shape[1]) for x in ws],
        scratch_shapes=[pltpu.VMEM((3, ROW_TILE + 8, GDN_WIDTH), F32)],
        compiler_params=_cparams(1),
        name="ab_proj",
    )(h, gain.reshape(1, d), *ws, cw)


def _ffn_kernel(h_ref, hp_ref, *refs, n_mix, tiles_per_seq, n_chunks, final):
    y_refs, yp_refs, wm_refs = refs[:n_mix], refs[n_mix:2 * n_mix], refs[2 * n_mix:3 * n_mix]
    (g_ref, win_ref, cw_ref, cb_ref, wout_ref, gf_ref, o_ref,
     xn_ref, up0_ref, up1_ref, act_ref, x1_ref) = refs[3 * n_mix:]
    halo = BF16_SUBLANES
    rows = h_ref.shape[0]
    sub = 128
    fcw = act_ref.shape[2]
    g = g_ref[...]
    n_piece = rows // sub
    x1, x1p = h_ref[...], hp_ref[...]
    for y_ref, yp_ref, wm_ref in zip(y_refs, yp_refs, wm_refs):
        x1 = x1 + _dot(y_ref[...], wm_ref[...])
        x1p = x1p + _dot(yp_ref[...], wm_ref[...])
    x1_ref[...] = x1
    seq_start = pl.program_id(0) % tiles_per_seq == 0
    xn_ref[0:halo, :] = jnp.where(seq_start, 0.0, _rms(x1p, g)).astype(BF16)
    xn_ref[halo:, :] = _rms(x1, g).astype(BF16)

    def gate_piece(j, buf, r):
        r0 = r * sub

        def conv(half, cw, cb):
            y = cb
            for k in range(FFN_CONV):
                start = halo + r0 - (FFN_CONV - 1) + k
                y = y + buf[half, start:start + sub, :] * cw[k:k + 1]
            return y
        a = conv(0, cw_ref[j], cb_ref[j])
        b = conv(1, cw_ref[n_chunks + j], cb_ref[n_chunks + j])
        act_ref[j, r0:r0 + sub, :] = (_silu(a) * b).astype(BF16)

    def up_and_gate(j_up, buf_up, j_gate, buf_gate):
        xe = xn_ref[...]
        for half in range(2):
            col = pl.multiple_of((half * n_chunks + j_up) * fcw, fcw)
            buf_up[half] = _dot(xe, win_ref[:, pl.ds(col, fcw)])
            if j_gate is not None:
                for r in range(half * n_piece // 2, (half + 1) * n_piece // 2):
                    gate_piece(j_gate, buf_gate, r)

    def body(i, carry):
        up_and_gate(2 * i + 1, up1_ref, 2 * i, up0_ref)
        up_and_gate(2 * i + 2, up0_ref, 2 * i + 1, up1_ref)
        return carry

    up_and_gate(0, up0_ref, None, None)
    lax.fori_loop(0, (n_chunks - 1) // 2, body, 0)
    for r in range(n_piece):
        gate_piece(n_chunks - 1, up0_ref, r)

    out = x1_ref[...]
    for j in range(n_chunks):
        out = out + _dot(act_ref[j], wout_ref[j])
    if final:
        out = _rms(out, gf_ref[...])
    o_ref[...] = out


def _mix_ffn(h, ys, wms, seq, gain, w_in, conv_w, conv_b, w_out, gain_final, final):
    t, d = h.shape
    n_mix = len(ys)
    fc = FFN_COL_CHUNK
    n_chunks = FFN_HIDDEN // fc
    assert n_chunks % 2 == 1
    halo = BF16_SUBLANES
    win = w_in.astype(BF16)
    cw = conv_w.astype(F32).reshape(FFN_CONV, 2 * n_chunks, fc).transpose(1, 0, 2)
    cb = conv_b.astype(F32).reshape(2 * n_chunks, 1, fc)
    wout = w_out.astype(BF16).reshape(n_chunks, fc, d)
    blocks_per_tile = ROW_TILE // halo
    tile_spec = lambda n: pl.BlockSpec((ROW_TILE, n), lambda i: (i, 0))
    prev_spec = lambda n: pl.BlockSpec((halo, n), lambda i: (jnp.maximum(i * blocks_per_tile - 1, 0), 0))
    return pl.pallas_call(
        functools.partial(_ffn_kernel, n_mix=n_mix, tiles_per_seq=seq // ROW_TILE, n_chunks=n_chunks, final=final),
        out_shape=jax.ShapeDtypeStruct((t, d), F32),
        grid=(t // ROW_TILE,),
        in_specs=[tile_spec(d), prev_spec(d)]
        + [tile_spec(y.shape[1]) for y in ys] + [prev_spec(y.shape[1]) for y in ys]
        + [_const_spec(w.shape) for w in wms]
        + [_const_spec((1, d)),
                  pl.BlockSpec(win.shape, lambda i: (0, 0), pipeline_mode=pl.Buffered(1)),
                  _const_spec(cw.shape), _const_spec(cb.shape),
                  pl.BlockSpec(wout.shape, lambda i: (0, 0, 0), pipeline_mode=pl.Buffered(1)),
                  _const_spec((1, d))],
        out_specs=pl.BlockSpec((ROW_TILE, d), lambda i: (i, 0)),
        scratch_shapes=[pltpu.VMEM((ROW_TILE + halo, d), BF16), pltpu.VMEM((2, ROW_TILE + halo, fc), F32),
                        pltpu.VMEM((2, ROW_TILE + halo, fc), F32), pltpu.VMEM((n_chunks, ROW_TILE, fc), BF16),
                        pltpu.VMEM((ROW_TILE, d), F32)],
        compiler_params=_cparams(1),
        name="mix_ffn",
    )(h, h, *ys, *ys, *wms, gain.reshape(1, d), win, cw, cb, wout, gain_final.reshape(1, d))


def _s5_prep(lam_re, lam_im, log_step, b_re, b_im, c_re, c_im, d_skip):
    g_n, p_n, h_n, L = S5_GROUPS, S5_STATE, S5_GROUP, S5_CHUNK
    step = jnp.exp(log_step.astype(F32))[:, None]
    lr, li = lam_re.astype(F32), lam_im.astype(F32)
    mag = jnp.exp(lr * step)
    a_re = mag * jnp.cos(li * step)
    a_im = mag * jnp.sin(li * step)
    den = lr * lr + li * li
    n_re, n_im = a_re - 1.0, a_im
    z_re = (n_re * lr + n_im * li) / den
    z_im = (n_im * lr - n_re * li) / den
    b_re, b_im = b_re.astype(F32), b_im.astype(F32)
    bb_re = z_re[..., None] * b_re - z_im[..., None] * b_im
    bb_im = z_re[..., None] * b_im + z_im[..., None] * b_re
    c_re, c_im = c_re.astype(F32), c_im.astype(F32)
    pw_re, pw_im = [jnp.ones_like(a_re)], [jnp.zeros_like(a_im)]
    for _ in range(L):
        pr, pi = pw_re[-1], pw_im[-1]
        pw_re.append(pr * a_re - pi * a_im)
        pw_im.append(pr * a_im + pi * a_re)
    eye_g = jnp.eye(g_n, dtype=F32)
    ks = []
    for j in range(L):
        ab_re = pw_re[j][..., None] * bb_re - pw_im[j][..., None] * bb_im
        ab_im = pw_re[j][..., None] * bb_im + pw_im[j][..., None] * bb_re
        kj = (jnp.einsum('gop,gpi->gio', c_re, ab_re, precision=HI)
              - jnp.einsum('gop,gpi->gio', c_im, ab_im, precision=HI))
        ks.append(jnp.einsum('gio,gk->giko', kj, eye_g).reshape(g_n * h_n, g_n * h_n))
    kstack = jnp.concatenate(ks[::-1], axis=0).astype(BF16)
    bb = jnp.stack([jnp.einsum('gph,gk->ghkp', bb_re, eye_g), jnp.einsum('gph,gk->ghkp', bb_im, eye_g)], axis=2)
    bb = bb.reshape(g_n * h_n, 2 * g_n * p_n).astype(BF16)
    cc = jnp.stack([jnp.einsum('ghp,gk->gpkh', c_re, eye_g), -jnp.einsum('ghp,gk->gpkh', c_im, eye_g)], axis=0)
    cc = cc.reshape(2 * g_n * p_n, g_n * h_n).astype(BF16)
    apow = jnp.stack([jnp.concatenate([r.reshape(-1), i.reshape(-1)]) for r, i in zip(pw_re, pw_im)], axis=0)
    dvec = jnp.tile(d_skip.astype(F32).reshape(1, g_n * h_n), (1, L))
    return kstack, bb, cc, apow, dvec


def _s5_kernel(u_ref, ks_ref, bb_ref, cc_ref, ap_ref, dv_ref, wg_ref, bg_ref, o_ref, x_scr, hp_scr,
               *, n_batch, n_chunk):
    L, w = S5_CHUNK, S5_WIDTH
    half = S5_GROUPS * S5_STATE
    rows = n_batch * n_chunk
    u2 = u_ref[...].reshape(rows, L * w)
    u2b = u2.astype(BF16)

    def cmul(j, x):
        ar, ai = ap_ref[j:j + 1, :half], ap_ref[j:j + 1, half:]
        xr, xi = x[:, :half], x[:, half:]
        return jnp.concatenate([ar * xr - ai * xi, ar * xi + ai * xr], axis=1)

    xin = _dot(u2b[:, (L - 1) * w:], bb_ref[...])
    for s in range(L - 1):
        xin = xin + cmul(L - 1 - s, _dot(u2b[:, s * w:(s + 1) * w], bb_ref[...]))
    x_scr[...] = xin

    a_l = ap_ref[L:L + 1, :]
    alr, ali = a_l[:, :half], a_l[:, half:]

    def scan_body(k, hs):
        new = []
        for b in range(n_batch):
            r = b * n_chunk + k
            h = hs[b]
            hp_scr[pl.ds(r, 1), :] = h
            hr, hi = h[:, :half], h[:, half:]
            new.append(jnp.concatenate([alr * hr - ali * hi, alr * hi + ali * hr], axis=1)
                       + x_scr[pl.ds(r, 1), :])
        return tuple(new)

    lax.fori_loop(0, n_chunk, scan_body, tuple(jnp.zeros((1, 2 * half), F32) for _ in range(n_batch)))
    hp = hp_scr[...]

    for t in range(L):
        y = _dot(u2b[:, :(t + 1) * w], ks_ref[(L - 1 - t) * w:, :])
        y = y + _dot(cmul(t + 1, hp).astype(BF16), cc_ref[...])
        y = y + dv_ref[:, t * w:(t + 1) * w] * u2[:, t * w:(t + 1) * w]
        z = _gelu(y)
        gate = _sigmoid(_dot(z.astype(BF16), wg_ref[...]) + bg_ref[...])
        o_ref[:, :, t * w:(t + 1) * w] = (z * gate).astype(o_ref.dtype).reshape(n_batch, n_chunk, w)


def _s5(u, seq, prep, w_glu, b_glu):
    t = u.shape[0]
    bsz = t // seq
    L, w = S5_CHUNK, S5_WIDTH
    n_chunk = seq // L
    n_batch = 2
    kstack, bb, cc, apow, dvec = prep
    u3 = u.reshape(bsz, n_chunk, L * w)
    out = pl.pallas_call(
        functools.partial(_s5_kernel, n_batch=n_batch, n_chunk=n_chunk),
        out_shape=jax.ShapeDtypeStruct((bsz, n_chunk, L * w), BF16),
        grid=(bsz // n_batch,),
        in_specs=[pl.BlockSpec((n_batch, n_chunk, L * w), lambda i: (i, 0, 0)),
                  _const_spec(kstack.shape), _const_spec(bb.shape), _const_spec(cc.shape),
                  _const_spec(apow.shape), _const_spec(dvec.shape),
                  _const_spec((w, w)), _const_spec((1, w))],
        out_specs=pl.BlockSpec((n_batch, n_chunk, L * w), lambda i: (i, 0, 0)),
        scratch_shapes=[pltpu.VMEM((n_batch * n_chunk, 2 * S5_GROUPS * S5_STATE), F32),
                        pltpu.VMEM((n_batch * n_chunk, 2 * S5_GROUPS * S5_STATE), F32)],
        compiler_params=_cparams(1),
        name="s5_mixer",
    )(u3, kstack, bb, cc, apow, dvec, w_glu.astype(BF16), b_glu.astype(F32).reshape(1, w))
    return out.reshape(t, w)


def _gdn_kernel(qs, ks, vs, z_ref, bc_ref, ac_ref, ar_ref, hp_ref, hr_ref, nw_ref,
                o_ref, xs, ns, qe, o0, gls, gcc_s, gcr_s, xs2, ns2, gls2, *, seq, n_head):
    c, dh = GDN_CHUNK, GDN_HEAD_DIM
    n_chunk = seq // c

    hp = hp_ref[0]
    hr = hr_ref[0]

    def softplus(x):
        return jnp.maximum(x, 0.0) + jnp.log(1.0 + jnp.exp(-jnp.abs(x)))

    ii = lax.broadcasted_iota(jnp.int32, (c, c), 0)
    jj = lax.broadcasted_iota(jnp.int32, (c, c), 1)
    eye = (ii == jj).astype(F32)
    g_col = -jnp.exp(hp[0:1, :]) * softplus(ac_ref[0, 0] + hp[1:2, :])
    g_row = -jnp.exp(hr[:, 0:1]) * softplus(ar_ref[0, 0] + hr[:, 1:2])
    gc_col = jnp.dot((ii >= jj).astype(F32), g_col, precision=HI, preferred_element_type=F32)
    gcr_s[...] = jnp.dot(g_row, (ii <= jj).astype(F32), precision=HI, preferred_element_type=F32)
    for ci in range(n_chunk):
        gcc_s[ci * c:(ci + 1) * c, :] = gc_col[:, ci * n_head:(ci + 1) * n_head]

    def local_body(it, carry):
        pr = []
        for cc in range(GDN_CHUNKS_PER_ITER):
            ci = it * GDN_CHUNKS_PER_ITER + cc
            rs = pl.ds(pl.multiple_of(ci * c, c), c)
            beta = _sigmoid(bc_ref[0, 0, rs, :])
            gccs = gcc_s[rs, :]
            for j in range(n_head):
                sl = slice(j * dh, (j + 1) * dh)
                gcc = gccs[:, j:j + 1]
                gcr = gcr_s[pl.ds(ci * n_head + j, 1), :]
                q, k, v = qs[rs, sl], ks[rs, sl], vs[rs, sl]
                bj = beta[:, j:j + 1]
                kb = k * bj
                e_g = jnp.exp(gcc)
                g_last = gcc[c - 1:c, :]
                pr.append(dict(ci=ci, j=j, q=q, kf=k.astype(BF16), kbb=kb.astype(BF16),
                               decay=jnp.exp(jnp.where(ii >= jj, gcc - gcr, NEG_INF)),
                               rhs=jnp.concatenate([v * bj, kb * e_g], axis=1).astype(BF16),
                               qd=q * e_g, kd=(k * jnp.exp(g_last - gcc)).astype(BF16),
                               gl=jnp.broadcast_to(jnp.exp(g_last), (8, dh))))
        kk = [_dot_nt(p["kbb"], p["kf"]) for p in pr]
        qk = [_dot_nt(p["q"].astype(BF16), p["kf"]) for p in pr]
        ps = [-jnp.where(ii > jj, a * p["decay"], 0.0) for a, p in zip(kk, pr)]
        ts = [eye + n for n in ps]
        span = 2
        while span < c:
            ps = [_dot(n.astype(BF16), n.astype(BF16)) for n in ps]
            ts = [t + _dot(t.astype(BF16), n.astype(BF16)) for t, n in zip(ts, ps)]
            span *= 2
        uws = [_dot(t.astype(BF16), p["rhs"]).astype(BF16) for t, p in zip(ts, pr)]
        intras = [(s * p["decay"]).astype(BF16) for s, p in zip(qk, pr)]
        i_uws = [_dot(a, uw) for a, uw in zip(intras, uws)]
        kd_uws = [_dot_tn(p["kd"], uw) for p, uw in zip(pr, uws)]
        for p, i_uw, kd_uw in zip(pr, i_uws, kd_uws):
            j, ci = p["j"], p["ci"]
            ns[j, ci] = kd_uw[:, :dh]
            xs[j, ci] = kd_uw[:, dh:].astype(BF16)
            o0[j, ci] = i_uw[:, :dh]
            qe[j, ci] = (p["qd"] - i_uw[:, dh:]).astype(BF16)
            gls[j, ci] = p["gl"]
        first = [idx for idx, p in enumerate(pr) if (idx // n_head) % 2 == 0]
        x1b = [kd_uws[idx + n_head][:, dh:].astype(BF16) for idx in first]
        x1x0 = [_dot(x1, kd_uws[idx][:, dh:].astype(BF16)) for x1, idx in zip(x1b, first)]
        x1n0 = [_dot(x1, kd_uws[idx][:, :dh].astype(BF16)) for x1, idx in zip(x1b, first)]
        for idx, xx, xn in zip(first, x1x0, x1n0):
            p0, p1 = pr[idx], pr[idx + n_head]
            g0, g1 = p0["gl"][0:1, :], p1["gl"][0:1, :]
            pi = it * (GDN_CHUNKS_PER_ITER // 2) + (idx // n_head) // 2
            xs2[p0["j"], pi] = (g1 * kd_uws[idx][:, dh:] + g0 * kd_uws[idx + n_head][:, dh:] - xx).astype(BF16)
            ns2[p0["j"], pi] = g1 * kd_uws[idx][:, :dh] - xn + kd_uws[idx + n_head][:, :dh]
            gls2[p0["j"], pi] = p0["gl"] * p1["gl"]
        return carry

    lax.fori_loop(0, n_chunk // GDN_CHUNKS_PER_ITER, local_body, 0)

    def rec_body(pi, states):
        c0, c1 = 2 * pi, 2 * pi + 1
        s_bs = [s.astype(BF16) for s in states]
        xd2 = [_dot(xs2[j, pi], s_bs[j]) for j in range(n_head)]
        xd0 = [_dot(xs[j, c0], s_bs[j]) for j in range(n_head)]
        od0 = [_dot(qe[j, c0], s_bs[j]) for j in range(n_head)]
        new = tuple(gls2[j, pi][0:1, :] * states[j] - xd2[j] + ns2[j, pi] for j in range(n_head))
        mid = [(gls[j, c0][0:1, :] * states[j] - xd0[j] + ns[j, c0]).astype(BF16) for j in range(n_head)]
        od1 = [_dot(qe[j, c1], mid[j]) for j in range(n_head)]
        for ci, od in ((c0, od0), (c1, od1)):
            rs = pl.ds(pl.multiple_of(ci * c, c), c)
            for j in range(n_head):
                sl = slice(j * dh, (j + 1) * dh)
                o = od[j] + o0[j, ci]
                on = o * lax.rsqrt(jnp.mean(o * o, axis=-1, keepdims=True) + RMS_EPS) * nw_ref[...]
                o_ref[rs, sl] = (on * _silu(z_ref[rs, sl])).astype(o_ref.dtype)
        return new

    lax.fori_loop(0, n_chunk // 2, rec_body, tuple(jnp.zeros((dh, dh), F32) for _ in range(n_head)), unroll=2)


def _gdn(q, k, v, z, ba, seq, a_log, dt_bias, norm_w):
    t = q.shape[0]
    bsz = t // seq
    nh = GDN_HEADS_PER_STEP
    n_grp = GDN_HEADS // nh
    wd = nh * GDN_HEAD_DIM
    bl = ba[:, :GDN_HEADS].reshape(bsz, seq, n_grp, nh)
    al = ba[:, GDN_HEADS:2 * GDN_HEADS].reshape(bsz, seq, n_grp, nh)
    b_col = bl.transpose(0, 2, 1, 3)
    n_chunk = seq // GDN_CHUNK
    n_prob = n_chunk * nh
    a5 = al.reshape(bsz, n_chunk, GDN_CHUNK, n_grp, nh)
    a_col = a5.transpose(0, 3, 2, 1, 4).reshape(bsz, n_grp, GDN_CHUNK, n_prob)
    a_row = a5.transpose(0, 3, 1, 4, 2).reshape(bsz, n_grp, n_prob, GDN_CHUNK)
    hp = jnp.stack([jnp.tile(a_log.astype(F32).reshape(n_grp, nh), (1, n_chunk)),
                    jnp.tile(dt_bias.astype(F32).reshape(n_grp, nh), (1, n_chunk))], axis=1)
    hr = hp.transpose(0, 2, 1)
    act_spec = pl.BlockSpec((seq, wd), lambda b, g: (b, g))
    col_spec = pl.BlockSpec((1, 1, seq, nh), lambda b, g: (b, g, 0, 0))
    return pl.pallas_call(
        functools.partial(_gdn_kernel, seq=seq, n_head=nh),
        out_shape=jax.ShapeDtypeStruct((t, GDN_WIDTH), BF16),
        grid=(bsz, n_grp),
        in_specs=[act_spec, act_spec, act_spec, act_spec, col_spec,
                  pl.BlockSpec((1, 1, GDN_CHUNK, n_prob), lambda b, g: (b, g, 0, 0)),
                  pl.BlockSpec((1, 1, n_prob, GDN_CHUNK), lambda b, g: (b, g, 0, 0)),
                  pl.BlockSpec((1, 2, n_prob), lambda b, g: (g, 0, 0)),
                  pl.BlockSpec((1, n_prob, 2), lambda b, g: (g, 0, 0)),
                  _const_spec((1, GDN_HEAD_DIM))],
        out_specs=act_spec,
        scratch_shapes=[pltpu.VMEM((nh, n_chunk, GDN_HEAD_DIM, GDN_HEAD_DIM), BF16),
                        pltpu.VMEM((nh, n_chunk, GDN_HEAD_DIM, GDN_HEAD_DIM), F32),
                        pltpu.VMEM((nh, n_chunk, GDN_CHUNK, GDN_HEAD_DIM), BF16),
                        pltpu.VMEM((nh, n_chunk, GDN_CHUNK, GDN_HEAD_DIM), F32),
                        pltpu.VMEM((nh, n_chunk, 8, GDN_HEAD_DIM), F32),
                        pltpu.VMEM((seq, nh), F32), pltpu.VMEM((n_prob, GDN_CHUNK), F32),
                        pltpu.VMEM((nh, n_chunk // 2, GDN_HEAD_DIM, GDN_HEAD_DIM), BF16),
                        pltpu.VMEM((nh, n_chunk // 2, GDN_HEAD_DIM, GDN_HEAD_DIM), F32),
                        pltpu.VMEM((nh, n_chunk // 2, 8, GDN_HEAD_DIM), F32)],
        compiler_params=_cparams(2),
        name="gated_deltanet",
    )(q, k, v, z, b_col, a_col, a_row, hp, hr, norm_w.astype(F32).reshape(1, GDN_HEAD_DIM))


NSA_FEAT = 64
F_BLK, F_POS, F_ONE = 0, 32, 38


def _split3(x):
    x1 = x.astype(BF16).astype(F32)
    x2 = (x - x1).astype(BF16).astype(F32)
    return x1, x2, x - x1 - x2


def _nsa_proj_kernel(x_ref, g_ref, wq_ref, wks_ref, wkw_ref, wvs_ref, wvw_ref, wkc_ref, wvc_ref, wg_ref, kf_ref,
                     perm_ref, kw1_ref, vw1_ref,
                     q_ref, ksa_ref, kwa_ref, vst_ref, vwt_ref, kct_ref, kcb_ref, vct_ref, vcb_ref, gl_ref):
    g_n, dh = NSA_KV_GROUPS, NSA_HEAD_DIM
    n_row = x_ref.shape[0] // CMP_STRIDE
    half = CMP_STRIDE * dh

    def compress_halves(c, w1_ref, top_ref, bot_ref):
        rows = _dot(perm_ref[...], c.astype(BF16)).astype(BF16)
        a = jnp.concatenate(
            [jnp.concatenate([rows[l * n_row:(l + 1) * n_row, g * dh:(g + 1) * dh] for l in range(CMP_STRIDE)], axis=1)
             for g in range(g_n)], axis=0)
        top, bot = _dot(a, w1_ref[:half, :]), _dot(a, w1_ref[half:, :])
        for g in range(g_n):
            top_ref[0, g] = top[g * n_row:(g + 1) * n_row]
            bot_ref[0, g] = bot[g * n_row:(g + 1) * n_row]

    xn = _rms(x_ref[...], g_ref[...]).astype(BF16)
    q_ref[...] = (_dot(xn, wq_ref[...]) * (NSA_HEAD_DIM ** -0.5 * LOG2E)).astype(BF16)
    kf = jnp.concatenate([kf_ref[...]] * NSA_KV_GROUPS, axis=1)
    ksa_ref[...] = (_dot(xn, wks_ref[...]) + kf).astype(BF16)
    kwa_ref[...] = (_dot(xn, wkw_ref[...]) + kf).astype(BF16)
    vst_ref[0] = _dot_nt(wvs_ref[...], xn).astype(BF16)
    vwt_ref[0] = _dot_nt(wvw_ref[...], xn).astype(BF16)
    compress_halves(_dot(xn, wkc_ref[...]), kw1_ref, kct_ref, kcb_ref)
    compress_halves(_dot(xn, wvc_ref[...]), vw1_ref, vct_ref, vcb_ref)
    gl_ref[...] = _dot(xn, wg_ref[...])


def _nsa_proj(h, gain, w_in, k_w1, v_w1, bsz, seq):
    t, d = h.shape
    g_n, dh = NSA_KV_GROUPS, NSA_HEAD_DIM
    nq, kvw = NSA_HEADS * dh, NSA_KV_WIDTH
    w = w_in.astype(BF16)
    wq = w[:, :nq]
    wkc, wvc, wks, wvs, wkw, wvw = (w[:, nq + j * kvw: nq + (j + 1) * kvw] for j in range(6))
    wgl = w[:, nq + 6 * kvw:]

    def lane_padded(x, width):
        return jnp.pad(x.reshape(d, g_n, width), ((0, 0), (0, 0), (0, LANES - width))).reshape(d, g_n * LANES)

    pos = np.arange(seq)
    kf = np.zeros((seq, LANES), np.float32)
    kf[pos, dh + F_BLK + pos // SLC_BLOCK] = 1.0
    kf[:, dh + F_POS:dh + F_POS + 3] = ((pos // SLC_BLOCK) * SLC_BLOCK)[:, None]
    kf[:, dh + F_POS + 3:dh + F_POS + 6] = (pos % SLC_BLOCK)[:, None]
    kf[:, dh + F_ONE:dh + F_ONE + 3] = 1.0
    tps = seq // ROW_TILE
    row_spec = lambda n: pl.BlockSpec((ROW_TILE, n), lambda i: (i, 0))
    tr_spec = pl.BlockSpec((1, kvw, ROW_TILE), lambda i: (i // tps, 0, i % tps))
    ws = [wq, lane_padded(wks, dh), lane_padded(wkw, dh), wvs.T, wvw.T, wkc, wvc, lane_padded(wgl, 3 * NSA_GROUP_SIZE)]
    n_row = ROW_TILE // CMP_STRIDE
    tok = np.arange(ROW_TILE)
    perm = np.zeros((ROW_TILE, ROW_TILE), np.float32)
    perm[(tok % CMP_STRIDE) * n_row + tok // CMP_STRIDE, tok] = 1.0
    cmp_consts = [jnp.asarray(perm, dtype=BF16), k_w1.astype(BF16), v_w1.astype(BF16)]
    half_spec = pl.BlockSpec((1, g_n, n_row, CMP_HIDDEN), lambda i: (i // tps, 0, i % tps, 0))
    half_shape = jax.ShapeDtypeStruct((bsz, g_n, seq // CMP_STRIDE, CMP_HIDDEN), F32)
    return pl.pallas_call(
        _nsa_proj_kernel,
        out_shape=[jax.ShapeDtypeStruct((t, nq), BF16),
                   jax.ShapeDtypeStruct((t, g_n * LANES), BF16), jax.ShapeDtypeStruct((t, g_n * LANES), BF16),
                   jax.ShapeDtypeStruct((bsz, kvw, seq), BF16), jax.ShapeDtypeStruct((bsz, kvw, seq), BF16),
                   half_shape, half_shape, half_shape, half_shape,
                   jax.ShapeDtypeStruct((t, g_n * LANES), F32)],
        grid=(t // ROW_TILE,),
        in_specs=[row_spec(d), _const_spec((1, d))] + [_const_spec(x.shape) for x in ws]
        + [pl.BlockSpec((ROW_TILE, LANES), lambda i: (i % tps, 0))] + [_const_spec(x.shape) for x in cmp_consts],
        out_specs=[row_spec(nq), row_spec(g_n * LANES), row_spec(g_n * LANES), tr_spec, tr_spec,
                   half_spec, half_spec, half_spec, half_spec, row_spec(g_n * LANES)],
        compiler_params=_cparams(1),
        name="nsa_proj",
    )(h, gain.reshape(1, d), *ws, jnp.asarray(kf), *cmp_consts)


def _nsa_kernel(q_ref, ksa_ref, kwa_ref, vst_ref, vwt_ref, kct_ref, kcb_ref, vct_ref, vcb_ref, gl_ref, sl_ref, sf_ref,
                cf_ref, kc1_ref, kw2_ref, vc1_ref, vw2t_ref, ovt_ref, o_ref,
                kcmp, vcmpt, vs3, vw3, qa_s, oc_s, *, seq):
    qb, dh, r_n, nq = NSA_Q_BLOCK, NSA_HEAD_DIM, NSA_GROUP_SIZE, NSA_QB_PER_ITER
    span = nq * qb
    cols = nq * r_n * qb
    n_cmp_pad = seq // CMP_STRIDE
    n_cmp = (seq - CMP_LEN) // CMP_STRIDE + 1
    n_blk = seq // SLC_BLOCK
    n_win = WINDOW + span
    it_per_slab = NSA_SLAB_KEYS // span
    nf = NSA_FEAT

    def hidden(top_ref, bot_ref, c1_ref):
        bot = pltpu.roll(bot_ref[0, 0], n_cmp_pad - 1, 0)
        return _gelu(top_ref[0, 0] + bot + c1_ref[...]).astype(BF16)

    kcmp[...] = jnp.concatenate([_dot(hidden(kct_ref, kcb_ref, kc1_ref), kw2_ref[...]), cf_ref[...]],
                                axis=1).astype(BF16)
    vcmpt[...] = _dot_nt(vw2t_ref[...], hidden(vct_ref, vcb_ref, vc1_ref)).astype(BF16)
    for n in range(seq // span):
        vs3[n] = vst_ref[0, :, n * span:(n + 1) * span]
        vw3[n] = vwt_ref[0, :, n * span:(n + 1) * span]

    def iota(shape, axis):
        return lax.broadcasted_iota(jnp.int32, shape, axis)

    slope = sl_ref[0]
    colc, colr = iota((cols, 1), 0), iota((1, cols), 1)
    tc_col = (colc // (r_n * qb)) * qb + colc % qb
    tc_row = (colr // (r_n * qb)) * qb + colr % qb
    qi_row = colr // (r_n * qb)
    lane_f = iota((1, nf), 1)
    jr = iota((n_cmp_pad, 1), 0)
    nr = iota((nf, 1), 0)
    lane_s = iota((1, span), 1)
    qi_s = lane_s // qb
    key_d = iota((span, 1), 0)
    key_w = iota((n_win, 1), 0)
    eye_s = (iota((span, span), 0) == iota((span, span), 1)).astype(BF16)

    def prepare(p):
        t0 = pl.multiple_of(p * span, span)
        qf = jnp.concatenate([q_ref[pl.ds(t0 + qi * qb, qb), r * dh:(r + 1) * dh]
                              for qi in range(nq) for r in range(r_n)], axis=0).astype(F32)
        c1, c2, c3 = _split3(-(slope * (t0 + tc_col).astype(F32)))
        f_plain = jnp.where(lane_f == F_ONE, c1, jnp.where(lane_f == F_ONE + 1, c2,
                            jnp.where(lane_f == F_ONE + 2, c3, sf_ref[0])))
        qa_plain = jnp.concatenate([qf, f_plain], axis=1).astype(BF16)

        ok_c = (jr * CMP_STRIDE + (CMP_LEN - 1) <= t0 + tc_row) & (jr < n_cmp)
        s_c = jnp.where(ok_c, _dot_nt(kcmp[...], qa_plain), NEG_INF)
        e_c = jnp.where(ok_c, jnp.exp2(s_c - jnp.max(s_c, axis=0, keepdims=True)), 0.0)
        l_c = jnp.sum(e_c, axis=0, keepdims=True)
        p_c = e_c / jnp.where(l_c > 0.0, l_c, 1.0)
        oc_s[...] = _dot(vcmpt[...], p_c.astype(BF16))

        imp4 = sum(_dot(ovt_ref[...], part.astype(BF16)) for part in _split3(p_c))
        halves = []
        for qi in range(nq):
            a = imp4[:, (2 * qi) * LANES:(2 * qi + 1) * LANES] + imp4[:, (2 * qi + 1) * LANES:(2 * qi + 2) * LANES]
            halves.append(a + pltpu.roll(a, qb, 1))
        low_half = lax.broadcasted_iota(jnp.int32, (1, LANES), 1) < qb
        imp = jnp.concatenate([jnp.where(low_half, halves[2 * v], halves[2 * v + 1]) for v in range(nq // 2)],
                              axis=1)

        iq = nq * p + qi_s
        cand = (nr > 0) & (nr < iq)
        sc = jnp.where(cand, imp, NEG_INF)
        sel = (nr == 0) & (iq > 0)
        for _ in range(N_SELECT - 2):
            best = jnp.max(sc, axis=0, keepdims=True)
            first = jnp.min(jnp.where(sc == best, nr, nf), axis=0, keepdims=True)
            pick = (nr == first) & cand
            sel = sel | pick
            sc = jnp.where(pick, NEG_INF, sc)
        neg_slab = jnp.where(sel & (nr < nq * p), 0.0, NEG_INF)
        neg_cur = jnp.where(nr == iq, 0.0, jnp.where((nr >= nq * p) & (nr < iq) & sel, 0.0, NEG_INF))
        neg_t = jnp.concatenate([neg_slab, neg_cur], axis=0).astype(BF16)
        neg = _dot_nt(eye_s, neg_t)
        rows_of = lambda x: jnp.concatenate([x[qi * qb:(qi + 1) * qb] for qi in range(nq) for _ in range(r_n)], axis=0)
        f_slab = jnp.where(lane_f < n_blk, rows_of(neg[:, :nf]), f_plain)
        f_cur = jnp.where(lane_f < n_blk, rows_of(neg[:, nf:]), f_plain)
        qa_s[0] = qa_plain
        qa_s[1] = jnp.concatenate([qf, f_slab], axis=1).astype(BF16)
        qa_s[2] = jnp.concatenate([qf, f_cur], axis=1).astype(BF16)

    n_iter = seq // span

    def make_body(n_keys):
        def body(p, carry):
            t0 = pl.multiple_of(p * span, span)
            w0 = pl.multiple_of(jnp.maximum(t0 - WINDOW, 0), span)
            oc_t = oc_s[...]
            s_cu = _dot_nt(ksa_ref[pl.ds(t0, span), :], qa_s[2])
            s_w = _dot_nt(kwa_ref[pl.ds(w0, n_win), :], qa_s[0])
            s_sl = _dot_nt(ksa_ref[0:n_keys, :], qa_s[1])
            prepare(jnp.minimum(p + 1, n_iter - 1))

            own = (key_d // qb) == qi_row
            s_cu = jnp.where(own & (key_d % qb > tc_row % qb), NEG_INF, s_cu)
            m_s = jnp.maximum(jnp.max(s_sl, axis=0, keepdims=True), jnp.max(s_cu, axis=0, keepdims=True))
            e_sl = jnp.exp2(s_sl - m_s)
            e_cu = jnp.exp2(s_cu - m_s)
            l_s = jnp.sum(e_sl, axis=0, keepdims=True) + jnp.sum(e_cu, axis=0, keepdims=True)
            os_t = (_dot(vst_ref[0, :, 0:n_keys], e_sl.astype(BF16)) + _dot(vs3[p], e_cu.astype(BF16))) / l_s

            rel = (t0 - w0) + tc_row - key_w
            s_w = jnp.where((rel >= 0) & (rel < WINDOW), s_w, NEG_INF)
            e_w = jnp.exp2(s_w - jnp.max(s_w, axis=0, keepdims=True))
            e_wb = e_w.astype(BF16)
            b0 = w0 // span
            ow_t = _dot(vw3[b0], e_wb[0:span])
            for j in range(1, n_win // span):
                ow_t = ow_t + _dot(vw3[b0 + j], e_wb[j * span:(j + 1) * span])
            ow_t = ow_t / jnp.sum(e_w, axis=0, keepdims=True)

            o_all = jnp.concatenate([os_t, ow_t, oc_t, jnp.zeros_like(oc_t)], axis=0).T
            gate = _sigmoid(gl_ref[pl.ds(t0, span), 0:3 * r_n])
            for qi in range(nq):
                g_q = gate[qi * qb:(qi + 1) * qb]
                heads = []
                for r in range(r_n):
                    rs = slice((qi * r_n + r) * qb, (qi * r_n + r + 1) * qb)
                    heads.append(g_q[:, 3 * r:3 * r + 1] * o_all[rs, 2 * dh:3 * dh]
                                 + g_q[:, 3 * r + 1:3 * r + 2] * o_all[rs, :dh]
                                 + g_q[:, 3 * r + 2:3 * r + 3] * o_all[rs, dh:2 * dh])
                o_ref[pl.ds(t0 + qi * qb, qb), :] = jnp.concatenate(heads, axis=1).astype(o_ref.dtype)
            return carry
        return body

    prepare(jnp.int32(0))
    for sb in range(seq // NSA_SLAB_KEYS):
        lax.fori_loop(sb * it_per_slab, (sb + 1) * it_per_slab, make_body((sb + 1) * NSA_SLAB_KEYS), 0)


def _nsa(q, ksa, kwa, vst, vwt, kct, kcb, vct, vcb, gl, bsz, seq, pe_k, pe_v, k_w1, k_b1, k_w2, v_w1, v_b1, v_w2):
    g_n, r_n, dh, qb = NSA_KV_GROUPS, NSA_GROUP_SIZE, NSA_HEAD_DIM, NSA_Q_BLOCK
    t = bsz * seq
    n_cmp_pad = seq // CMP_STRIDE
    n_blk = seq // SLC_BLOCK
    span = NSA_QB_PER_ITER * qb
    assert n_blk <= F_POS and seq % NSA_SLAB_KEYS == 0 and seq >= WINDOW + span

    slopes = np.asarray([2.0 ** (-8.0 * (h + 1) / NSA_HEADS) for h in range(NSA_HEADS)], dtype=np.float32)
    cols = NSA_QB_PER_ITER * r_n * qb
    slope_rows = jnp.asarray(np.tile(np.repeat(slopes.reshape(g_n, r_n), qb, axis=1), (1, NSA_QB_PER_ITER))
                             .reshape(g_n, cols, 1)) * LOG2E
    s3 = jnp.concatenate(_split3(slope_rows) * 2, axis=2)
    sfeat = jnp.pad(s3, ((0, 0), (0, 0), (F_POS, NSA_FEAT - F_POS - 6)))
    cf = np.zeros((n_cmp_pad, NSA_FEAT), np.float32)
    cf[:, F_POS:F_POS + 3] = (np.arange(n_cmp_pad) * CMP_STRIDE)[:, None]
    cf[:, F_POS + 3:F_POS + 6] = (CMP_LEN - 1) * 0.5
    cf[:, F_ONE:F_ONE + 3] = 1.0
    tok = np.arange(n_cmp_pad)[:, None] * CMP_STRIDE + np.arange(CMP_LEN)[None, :]
    overlap_t = ((tok // SLC_BLOCK)[:, :, None] == np.arange(NSA_FEAT)[None, None, :]).mean(axis=1).astype(np.float32).T
    def pos_const(pe, w1, b1):
        return _dot(pe.astype(BF16).reshape(1, -1), w1.astype(BF16)) + b1.astype(F32).reshape(1, -1)

    cmp_spec = pl.BlockSpec((1, 1, n_cmp_pad, CMP_HIDDEN), lambda b, g: (b, g, 0, 0))
    q_spec = pl.BlockSpec((seq, r_n * dh), lambda b, g: (b, g))
    k_spec = pl.BlockSpec((seq, LANES), lambda b, g: (b, g))
    v_spec = pl.BlockSpec((1, dh, seq), lambda b, g: (b, g, 0))
    w1s, b1s, w2s = (CMP_LEN * dh, CMP_HIDDEN), (1, CMP_HIDDEN), (CMP_HIDDEN, dh)
    return pl.pallas_call(
        functools.partial(_nsa_kernel, seq=seq),
        out_shape=jax.ShapeDtypeStruct((t, NSA_HEADS * dh), BF16),
        grid=(bsz, g_n),
        in_specs=[q_spec, k_spec, k_spec, v_spec, v_spec, cmp_spec, cmp_spec, cmp_spec, cmp_spec, k_spec,
                  pl.BlockSpec((1, cols, 1), lambda b, g: (g, 0, 0)),
                  pl.BlockSpec((1, cols, NSA_FEAT), lambda b, g: (g, 0, 0)),
                  _const_spec((n_cmp_pad, NSA_FEAT)),
                  _const_spec(b1s), _const_spec(w2s), _const_spec(b1s), _const_spec((dh, CMP_HIDDEN)),
                  _const_spec((NSA_FEAT, n_cmp_pad))],
        out_specs=q_spec,
        scratch_shapes=[pltpu.VMEM((n_cmp_pad, dh + NSA_FEAT), BF16), pltpu.VMEM((dh, n_cmp_pad), BF16),
                        pltpu.VMEM((seq // span, dh, span), BF16), pltpu.VMEM((seq // span, dh, span), BF16),
                        pltpu.VMEM((3, cols, dh + NSA_FEAT), BF16), pltpu.VMEM((dh, cols), F32)],
        compiler_params=_cparams(2),
        name="nsa_attention",
    )(q, ksa, kwa, vst, vwt, kct, kcb, vct, vcb, gl, slope_rows, sfeat, jnp.asarray(cf),
      pos_const(pe_k, k_w1, k_b1), k_w2.astype(BF16), pos_const(pe_v, v_w1, v_b1), v_w2.astype(BF16).T, jnp.asarray(overlap_t, dtype=BF16))


def _pad_cols(w, n):
    return jnp.pad(w, ((0, 0), (0, n - w.shape[1])))


def kernel(x, ab_w_in, ab_w_out, s5_lambda_re, s5_lambda_im, s5_log_step, s5_b_re, s5_b_im, s5_c_re, s5_c_im, s5_d, s5_w_glu, s5_b_glu, gdn_conv_w, gdn_a_log, gdn_dt_bias, gdn_norm_w, nsa_w_in, nsa_w_out, nsa_pe_k, nsa_pe_v, nsa_k_w1, nsa_k_b1, nsa_k_w2, nsa_v_w1, nsa_v_b1, nsa_v_w2, ffn_w_in, ffn_conv_w, ffn_conv_b, ffn_w_out, norm_mix, norm_ffn, norm_final):
    bsz, seq, d = x.shape
    depth = ffn_w_in.shape[0]
    h = x.astype(F32).reshape(bsz * seq, d)
    for layer in range(depth):
        i = layer // 2
        if layer % 2 == 0:
            u, q, k, v, z, ba = _ab_proj(h, norm_mix[layer], ab_w_in[i], gdn_conv_w[i], seq)
            prep = _s5_prep(s5_lambda_re[i], s5_lambda_im[i], s5_log_step[i], s5_b_re[i], s5_b_im[i],
                            s5_c_re[i], s5_c_im[i], s5_d[i])
            y_a = _s5(u, seq, prep, s5_w_glu[i], s5_b_glu[i])
            y_b = _gdn(q, k, v, z, ba, seq, gdn_a_log[i], gdn_dt_bias[i], gdn_norm_w[i])
            wo = ab_w_out[i].astype(BF16)
            mix, wms = [y_a, y_b], [wo[:S5_WIDTH], wo[S5_WIDTH:]]
        else:
            proj = _nsa_proj(h, norm_mix[layer], nsa_w_in[i], nsa_k_w1[i], nsa_v_w1[i], bsz, seq)
            o = _nsa(*proj, bsz, seq, nsa_pe_k[i], nsa_pe_v[i], nsa_k_w1[i],
                     nsa_k_b1[i], nsa_k_w2[i], nsa_v_w1[i], nsa_v_b1[i], nsa_v_w2[i])
            mix, wms = [o], [nsa_w_out[i].astype(BF16)]
        h = _mix_ffn(h, mix, wms, seq, norm_ffn[layer], ffn_w_in[layer], ffn_conv_w[layer], ffn_conv_b[layer],
                     ffn_w_out[layer], norm_final, final=(layer == depth - 1))
    return h.reshape(bsz, seq, d).astype(x.dtype)
```

```python
import functools
import math

import numpy as np
import jax
import jax.numpy as jnp
from jax import lax
from jax.experimental import pallas as pl
from jax.experimental.pallas import tpu as pltpu

F32 = jnp.float32
BF16 = jnp.bfloat16
HI = lax.Precision.HIGHEST

D_MODEL = 1024
S5_WIDTH = 256
S5_GROUP = 16
S5_GROUPS = 16
S5_STATE = 64
S5_CHUNK = 16
GDN_HEAD_DIM = 128
GDN_HEADS = 6
GDN_WIDTH = 768
GDN_CONV = 4
GDN_CHUNK = 64
GDN_HEADS_PER_STEP = 3
GDN_CHUNKS_PER_ITER = 8
NSA_HEADS = 16
NSA_HEAD_DIM = 64
NSA_KV_GROUPS = 4
NSA_GROUP_SIZE = 4
NSA_KV_WIDTH = 256
CMP_LEN = 32
CMP_STRIDE = 16
CMP_HIDDEN = 256
SLC_BLOCK = 64
N_SELECT = 4
WINDOW = 256
NSA_Q_BLOCK = 64
NSA_SLAB_KEYS = 512
NSA_QB_PER_ITER = 2
LOG2E = math.log2(math.e)
FFN_HIDDEN = 2816
FFN_CONV = 3
FFN_COL_CHUNK = 256
RMS_EPS = 1e-6
NEG_INF = -1e30
LANES = 128
BF16_SUBLANES = 16
VMEM_LIMIT = 56 * 1024 * 1024
ROW_TILE = 512


def _cparams(n_axes):
    return pltpu.CompilerParams(dimension_semantics=("arbitrary",) * n_axes,
                                vmem_limit_bytes=VMEM_LIMIT)


def _rms(x, g):
    return x * lax.rsqrt(jnp.mean(x * x, axis=-1, keepdims=True) + RMS_EPS) * g


def _gelu(x):
    return 0.5 * x * (1.0 + jnp.tanh(math.sqrt(2.0 / math.pi) * (x + 0.044715 * (x * x * x))))


def _sigmoid(x):
    return 1.0 / (1.0 + jnp.exp(-x))


def _silu(x):
    return x * _sigmoid(x)


def _dot(a, b):
    return jnp.dot(a, b, preferred_element_type=F32)


def _dot_nt(a, b):
    return lax.dot_general(a, b, (((1,), (1,)), ((), ())), preferred_element_type=F32)


def _dot_tn(a, b):
    return lax.dot_general(a, b, (((0,), (0,)), ((), ())), preferred_element_type=F32)


def _const_spec(shape):
    nd = len(shape)
    return pl.BlockSpec(shape, lambda *_: (0,) * nd)


def _ab_proj_kernel(x_ref, g_ref, wu_ref, wq_ref, wk_ref, wv_ref, wz_ref, wba_ref, cw_ref,
                    u_ref, q_ref, k_ref, v_ref, z_ref, ba_ref, ext_ref, *, tiles_per_seq):
    rows = x_ref.shape[0]
    pad = 8
    dh = GDN_HEAD_DIM
    xn = _rms(x_ref[...], g_ref[...]).astype(BF16)
    seq_start = pl.program_id(0) % tiles_per_seq == 0

    @pl.when(pl.program_id(0) == 0)
    def _():
        ext_ref[...] = jnp.zeros_like(ext_ref)

    def finish(idx, pre, o_ref):
        prev = jnp.where(seq_start, 0.0, ext_ref[idx, rows:rows + pad, :])
        ext_ref[idx, 0:pad, :] = prev
        ext_ref[idx, pad:, :] = pre
        cw = cw_ref[idx]
        y = pre * cw[GDN_CONV - 1:GDN_CONV]
        for sh in range(1, GDN_CONV):
            y = y + ext_ref[idx, pad - sh:pad - sh + rows, :] * cw[GDN_CONV - 1 - sh:GDN_CONV - sh]
        y = _silu(y)
        if idx < 2:
            parts = []
            for j in range(GDN_HEADS):
                p = y[:, j * dh:(j + 1) * dh]
                p = p * lax.rsqrt(jnp.sum(p * p, axis=-1, keepdims=True) + 1e-6)
                parts.append(p * (dh ** -0.5) if idx == 0 else p)
            y = jnp.concatenate(parts, axis=1)
        o_ref[...] = y

    pre_q = _dot(xn, wq_ref[...])
    pre_k = _dot(xn, wk_ref[...])
    finish(0, pre_q, q_ref)
    pre_v = _dot(xn, wv_ref[...])
    finish(1, pre_k, k_ref)
    z_ref[...] = _dot(xn, wz_ref[...])
    finish(2, pre_v, v_ref)
    u_ref[...] = _dot(xn, wu_ref[...])
    ba_ref[...] = _dot(xn, wba_ref[...])


def _ab_proj(h, gain, w_in, conv_w, seq):
    t, d = h.shape
    w = w_in.astype(BF16)
    cuts = np.cumsum([0, S5_WIDTH, GDN_WIDTH, GDN_WIDTH, GDN_WIDTH, GDN_WIDTH])
    ws = [w[:, cuts[j]:cuts[j + 1]] for j in range(5)] + [_pad_cols(w[:, cuts[5]:], LANES)]
    cw = conv_w.astype(F32).reshape(GDN_CONV, 3, GDN_WIDTH).transpose(1, 0, 2)
    row_spec = lambda n: pl.BlockSpec((ROW_TILE, n), lambda i: (i, 0))
    return pl.pallas_call(
        functools.partial(_ab_proj_kernel, tiles_per_seq=seq // ROW_TILE),
        out_shape=[jax.ShapeDtypeStruct((t, x.shape[1]), F32) for x in ws],
        grid=(t // ROW_TILE,),
        in_specs=[row_spec(d), _const_spec((1, d))] + [_const_spec(x.shape) for x in ws] + [_const_spec(cw.shape)],
        out_specs=[row_spec(x.shape[1]) for x in ws],
        scratch_shapes=[pltpu.VMEM((3, ROW_TILE + 8, GDN_WIDTH), F32)],
        compiler_params=_cparams(1),
        name="ab_proj",
    )(h, gain.reshape(1, d), *ws, cw)


def _ffn_kernel(h_ref, hp_ref, *refs, n_mix, tiles_per_seq, n_chunks, final):
    y_refs, yp_refs, wm_refs = refs[:n_mix], refs[n_mix:2 * n_mix], refs[2 * n_mix:3 * n_mix]
    (g_ref, win_ref, cw_ref, cb_ref, wout_ref, gf_ref, o_ref,
     xn_ref, up0_ref, up1_ref, act_ref, x1_ref) = refs[3 * n_mix:]
    halo = BF16_SUBLANES
    rows = h_ref.shape[0]
    sub = 128
    fcw = act_ref.shape[2]
    g = g_ref[...]
    n_piece = rows // sub
    x1, x1p = h_ref[...], hp_ref[...]
    for y_ref, yp_ref, wm_ref in zip(y_refs, yp_refs, wm_refs):
        x1 = x1 + _dot(y_ref[...], wm_ref[...])
        x1p = x1p + _dot(yp_ref[...], wm_ref[...])
    x1_ref[...] = x1
    seq_start = pl.program_id(0) % tiles_per_seq == 0
    xn_ref[0:halo, :] = jnp.where(seq_start, 0.0, _rms(x1p, g)).astype(BF16)
    xn_ref[halo:, :] = _rms(x1, g).astype(BF16)

    def gate_piece(j, buf, r):
        r0 = r * sub

        def conv(half, cw, cb):
            y = cb
            for k in range(FFN_CONV):
                start = halo + r0 - (FFN_CONV - 1) + k
                y = y + buf[half, start:start + sub, :] * cw[k:k + 1]
            return y
        a = conv(0, cw_ref[j], cb_ref[j])
        b = conv(1, cw_ref[n_chunks + j], cb_ref[n_chunks + j])
        act_ref[j, r0:r0 + sub, :] = (_silu(a) * b).astype(BF16)

    def up_and_gate(j_up, buf_up, j_gate, buf_gate):
        cut = halo + rows // 2
        for piece, (r_lo, r_hi) in enumerate(((0, cut), (cut, halo + rows))):
            xe = xn_ref[r_lo:r_hi, :]
            for half in range(2):
                col = pl.multiple_of((half * n_chunks + j_up) * fcw, fcw)
                buf_up[half, r_lo:r_hi, :] = _dot(xe, win_ref[:, pl.ds(col, fcw)])
            if j_gate is not None:
                for r in range(piece * n_piece // 2, (piece + 1) * n_piece // 2):
                    gate_piece(j_gate, buf_gate, r)

    def body(i, carry):
        up_and_gate(2 * i + 1, up1_ref, 2 * i, up0_ref)
        up_and_gate(2 * i + 2, up0_ref, 2 * i + 1, up1_ref)
        return carry

    up_and_gate(0, up0_ref, None, None)
    lax.fori_loop(0, (n_chunks - 1) // 2, body, 0)
    for r in range(n_piece):
        gate_piece(n_chunks - 1, up0_ref, r)

    out = x1_ref[...]
    for j in range(n_chunks):
        out = out + _dot(act_ref[j], wout_ref[j])
    if final:
        out = _rms(out, gf_ref[...])
    o_ref[...] = out


def _mix_ffn(h, ys, wms, seq, gain, w_in, conv_w, conv_b, w_out, gain_final, final):
    t, d = h.shape
    n_mix = len(ys)
    fc = FFN_COL_CHUNK
    n_chunks = FFN_HIDDEN // fc
    assert n_chunks % 2 == 1
    halo = BF16_SUBLANES
    win = w_in.astype(BF16)
    cw = conv_w.astype(F32).reshape(FFN_CONV, 2 * n_chunks, fc).transpose(1, 0, 2)
    cb = conv_b.astype(F32).reshape(2 * n_chunks, 1, fc)
    wout = w_out.astype(BF16).reshape(n_chunks, fc, d)
    blocks_per_tile = ROW_TILE // halo
    tile_spec = lambda n: pl.BlockSpec((ROW_TILE, n), lambda i: (i, 0))
    prev_spec = lambda n: pl.BlockSpec((halo, n), lambda i: (jnp.maximum(i * blocks_per_tile - 1, 0), 0))
    return pl.pallas_call(
        functools.partial(_ffn_kernel, n_mix=n_mix, tiles_per_seq=seq // ROW_TILE, n_chunks=n_chunks, final=final),
        out_shape=jax.ShapeDtypeStruct((t, d), F32),
        grid=(t // ROW_TILE,),
        in_specs=[tile_spec(d), prev_spec(d)]
        + [tile_spec(y.shape[1]) for y in ys] + [prev_spec(y.shape[1]) for y in ys]
        + [_const_spec(w.shape) for w in wms]
        + [_const_spec((1, d)),
                  pl.BlockSpec(win.shape, lambda i: (0, 0), pipeline_mode=pl.Buffered(1)),
                  _const_spec(cw.shape), _const_spec(cb.shape),
                  pl.BlockSpec(wout.shape, lambda i: (0, 0, 0), pipeline_mode=pl.Buffered(1)),
                  _const_spec((1, d))],
        out_specs=pl.BlockSpec((ROW_TILE, d), lambda i: (i, 0)),
        scratch_shapes=[pltpu.VMEM((ROW_TILE + halo, d), BF16), pltpu.VMEM((2, ROW_TILE + halo, fc), F32),
                        pltpu.VMEM((2, ROW_TILE + halo, fc), F32), pltpu.VMEM((n_chunks, ROW_TILE, fc), BF16),
                        pltpu.VMEM((ROW_TILE, d), F32)],
        compiler_params=_cparams(1),
        name="mix_ffn",
    )(h, h, *ys, *ys, *wms, gain.reshape(1, d), win, cw, cb, wout, gain_final.reshape(1, d))


def _s5_prep(lam_re, lam_im, log_step, b_re, b_im, c_re, c_im, d_skip):
    g_n, p_n, h_n, L = S5_GROUPS, S5_STATE, S5_GROUP, S5_CHUNK
    step = jnp.exp(log_step.astype(F32))[:, None]
    lr, li = lam_re.astype(F32), lam_im.astype(F32)
    mag = jnp.exp(lr * step)
    a_re = mag * jnp.cos(li * step)
    a_im = mag * jnp.sin(li * step)
    den = lr * lr + li * li
    n_re, n_im = a_re - 1.0, a_im
    z_re = (n_re * lr + n_im * li) / den
    z_im = (n_im * lr - n_re * li) / den
    b_re, b_im = b_re.astype(F32), b_im.astype(F32)
    bb_re = z_re[..., None] * b_re - z_im[..., None] * b_im
    bb_im = z_re[..., None] * b_im + z_im[..., None] * b_re
    c_re, c_im = c_re.astype(F32), c_im.astype(F32)
    pw_re, pw_im = [jnp.ones_like(a_re)], [jnp.zeros_like(a_im)]
    for _ in range(L):
        pr, pi = pw_re[-1], pw_im[-1]
        pw_re.append(pr * a_re - pi * a_im)
        pw_im.append(pr * a_im + pi * a_re)
    eye_g = jnp.eye(g_n, dtype=F32)
    ks = []
    for j in range(L):
        ab_re = pw_re[j][..., None] * bb_re - pw_im[j][..., None] * bb_im
        ab_im = pw_re[j][..., None] * bb_im + pw_im[j][..., None] * bb_re
        kj = (jnp.einsum('gop,gpi->gio', c_re, ab_re, precision=HI)
              - jnp.einsum('gop,gpi->gio', c_im, ab_im, precision=HI))
        ks.append(jnp.einsum('gio,gk->giko', kj, eye_g).reshape(g_n * h_n, g_n * h_n))
    kstack = jnp.concatenate(ks[::-1], axis=0).astype(BF16)
    bb = jnp.stack([jnp.einsum('gph,gk->ghkp', bb_re, eye_g), jnp.einsum('gph,gk->ghkp', bb_im, eye_g)], axis=2)
    bb = bb.reshape(g_n * h_n, 2 * g_n * p_n).astype(BF16)
    cc = jnp.stack([jnp.einsum('ghp,gk->gpkh', c_re, eye_g), -jnp.einsum('ghp,gk->gpkh', c_im, eye_g)], axis=0)
    cc = cc.reshape(2 * g_n * p_n, g_n * h_n).astype(BF16)
    apow = jnp.stack([jnp.concatenate([r.reshape(-1), i.reshape(-1)]) for r, i in zip(pw_re, pw_im)], axis=0)
    dvec = jnp.tile(d_skip.astype(F32).reshape(1, g_n * h_n), (1, L))
    return kstack, bb, cc, apow, dvec


def _s5_kernel(u_ref, ks_ref, bb_ref, cc_ref, ap_ref, dv_ref, wg_ref, bg_ref, o_ref, x_scr, hp_scr,
               *, n_batch, n_chunk):
    L, w = S5_CHUNK, S5_WIDTH
    half = S5_GROUPS * S5_STATE
    rows = n_batch * n_chunk
    u2 = u_ref[...].reshape(rows, L * w)
    u2b = u2.astype(BF16)

    def cmul(j, x):
        ar, ai = ap_ref[j:j + 1, :half], ap_ref[j:j + 1, half:]
        xr, xi = x[:, :half], x[:, half:]
        return jnp.concatenate([ar * xr - ai * xi, ar * xi + ai * xr], axis=1)

    xin = _dot(u2b[:, (L - 1) * w:], bb_ref[...])
    for s in range(L - 1):
        xin = xin + cmul(L - 1 - s, _dot(u2b[:, s * w:(s + 1) * w], bb_ref[...]))
    x_scr[...] = xin

    a_l = ap_ref[L:L + 1, :]
    alr, ali = a_l[:, :half], a_l[:, half:]

    def scan_body(k, hs):
        new = []
        for b in range(n_batch):
            r = b * n_chunk + k
            h = hs[b]
            hp_scr[pl.ds(r, 1), :] = h
            hr, hi = h[:, :half], h[:, half:]
            new.append(jnp.concatenate([alr * hr - ali * hi, alr * hi + ali * hr], axis=1)
                       + x_scr[pl.ds(r, 1), :])
        return tuple(new)

    lax.fori_loop(0, n_chunk, scan_body, tuple(jnp.zeros((1, 2 * half), F32) for _ in range(n_batch)))
    hp = hp_scr[...]

    for t in range(L):
        y = _dot(u2b[:, :(t + 1) * w], ks_ref[(L - 1 - t) * w:, :])
        y = y + _dot(cmul(t + 1, hp).astype(BF16), cc_ref[...])
        y = y + dv_ref[:, t * w:(t + 1) * w] * u2[:, t * w:(t + 1) * w]
        z = _gelu(y)
        gate = _sigmoid(_dot(z.astype(BF16), wg_ref[...]) + bg_ref[...])
        o_ref[:, :, t * w:(t + 1) * w] = (z * gate).astype(o_ref.dtype).reshape(n_batch, n_chunk, w)


def _s5(u, seq, prep, w_glu, b_glu):
    t = u.shape[0]
    bsz = t // seq
    L, w = S5_CHUNK, S5_WIDTH
    n_chunk = seq // L
    n_batch = 2
    kstack, bb, cc, apow, dvec = prep
    u3 = u.reshape(bsz, n_chunk, L * w)
    out = pl.pallas_call(
        functools.partial(_s5_kernel, n_batch=n_batch, n_chunk=n_chunk),
        out_shape=jax.ShapeDtypeStruct((bsz, n_chunk, L * w), BF16),
        grid=(bsz // n_batch,),
        in_specs=[pl.BlockSpec((n_batch, n_chunk, L * w), lambda i: (i, 0, 0)),
                  _const_spec(kstack.shape), _const_spec(bb.shape), _const_spec(cc.shape),
                  _const_spec(apow.shape), _const_spec(dvec.shape),
                  _const_spec((w, w)), _const_spec((1, w))],
        out_specs=pl.BlockSpec((n_batch, n_chunk, L * w), lambda i: (i, 0, 0)),
        scratch_shapes=[pltpu.VMEM((n_batch * n_chunk, 2 * S5_GROUPS * S5_STATE), F32),
                        pltpu.VMEM((n_batch * n_chunk, 2 * S5_GROUPS * S5_STATE), F32)],
        compiler_params=_cparams(1),
        name="s5_mixer",
    )(u3, kstack, bb, cc, apow, dvec, w_glu.astype(BF16), b_glu.astype(F32).reshape(1, w))
    return out.reshape(t, w)


def _gdn_kernel(qs, ks, vs, z_ref, bc_ref, ac_ref, ar_ref, hp_ref, hr_ref, nw_ref,
                o_ref, xs, ns, qe, o0, gls, gcc_s, gcr_s, xs2, ns2, gls2, *, seq, n_head):
    c, dh = GDN_CHUNK, GDN_HEAD_DIM
    n_chunk = seq // c

    hp = hp_ref[0]
    hr = hr_ref[0]

    def softplus(x):
        return jnp.maximum(x, 0.0) + jnp.log(1.0 + jnp.exp(-jnp.abs(x)))

    ii = lax.broadcasted_iota(jnp.int32, (c, c), 0)
    jj = lax.broadcasted_iota(jnp.int32, (c, c), 1)
    eye = (ii == jj).astype(F32)
    g_col = -jnp.exp(hp[0:1, :]) * softplus(ac_ref[0, 0] + hp[1:2, :])
    g_row = -jnp.exp(hr[:, 0:1]) * softplus(ar_ref[0, 0] + hr[:, 1:2])
    gc_col = jnp.dot((ii >= jj).astype(F32), g_col, precision=HI, preferred_element_type=F32)
    gcr_s[...] = jnp.dot(g_row, (ii <= jj).astype(F32), precision=HI, preferred_element_type=F32)
    for ci in range(n_chunk):
        gcc_s[ci * c:(ci + 1) * c, :] = gc_col[:, ci * n_head:(ci + 1) * n_head]

    def local_body(it, carry):
        pr = []
        for cc in range(GDN_CHUNKS_PER_ITER):
            ci = it * GDN_CHUNKS_PER_ITER + cc
            rs = pl.ds(pl.multiple_of(ci * c, c), c)
            beta = _sigmoid(bc_ref[0, 0, rs, :])
            gccs = gcc_s[rs, :]
            for j in range(n_head):
                sl = slice(j * dh, (j + 1) * dh)
                gcc = gccs[:, j:j + 1]
                gcr = gcr_s[pl.ds(ci * n_head + j, 1), :]
                q, k, v = qs[rs, sl], ks[rs, sl], vs[rs, sl]
                bj = beta[:, j:j + 1]
                kb = k * bj
                e_g = jnp.exp(gcc)
                g_last = gcc[c - 1:c, :]
                pr.append(dict(ci=ci, j=j, q=q, kf=k.astype(BF16), kbb=kb.astype(BF16),
                               decay=jnp.exp(jnp.where(ii >= jj, gcc - gcr, NEG_INF)),
                               rhs=jnp.concatenate([v * bj, kb * e_g], axis=1).astype(BF16),
                               qd=q * e_g, kd=(k * jnp.exp(g_last - gcc)).astype(BF16),
                               gl=jnp.broadcast_to(jnp.exp(g_last), (8, dh))))
        kk = [_dot_nt(p["kbb"], p["kf"]) for p in pr]
        qk = [_dot_nt(p["q"].astype(BF16), p["kf"]) for p in pr]
        ps = [-jnp.where(ii > jj, a * p["decay"], 0.0) for a, p in zip(kk, pr)]
        ts = [eye + n for n in ps]
        span = 2
        while span < c:
            ps = [_dot(n.astype(BF16), n.astype(BF16)) for n in ps]
            ts = [t + _dot(t.astype(BF16), n.astype(BF16)) for t, n in zip(ts, ps)]
            span *= 2
        uws = [_dot(t.astype(BF16), p["rhs"]).astype(BF16) for t, p in zip(ts, pr)]
        intras = [(s * p["decay"]).astype(BF16) for s, p in zip(qk, pr)]
        i_uws = [_dot(a, uw) for a, uw in zip(intras, uws)]
        kd_uws = [_dot_tn(p["kd"], uw) for p, uw in zip(pr, uws)]
        for p, i_uw, kd_uw in zip(pr, i_uws, kd_uws):
            j, ci = p["j"], p["ci"]
            ns[j, ci] = kd_uw[:, :dh]
            xs[j, ci] = kd_uw[:, dh:].astype(BF16)
            o0[j, ci] = i_uw[:, :dh]
            qe[j, ci] = (p["qd"] - i_uw[:, dh:]).astype(BF16)
            gls[j, ci] = p["gl"]
        first = [idx for idx, p in enumerate(pr) if (idx // n_head) % 2 == 0]
        x1b = [kd_uws[idx + n_head][:, dh:].astype(BF16) for idx in first]
        x1x0 = [_dot(x1, kd_uws[idx][:, dh:].astype(BF16)) for x1, idx in zip(x1b, first)]
        x1n0 = [_dot(x1, kd_uws[idx][:, :dh].astype(BF16)) for x1, idx in zip(x1b, first)]
        for idx, xx, xn in zip(first, x1x0, x1n0):
            p0, p1 = pr[idx], pr[idx + n_head]
            g0, g1 = p0["gl"][0:1, :], p1["gl"][0:1, :]
            pi = it * (GDN_CHUNKS_PER_ITER // 2) + (idx // n_head) // 2
            xs2[p0["j"], pi] = (g1 * kd_uws[idx][:, dh:] + g0 * kd_uws[idx + n_head][:, dh:] - xx).astype(BF16)
            ns2[p0["j"], pi] = g1 * kd_uws[idx][:, :dh] - xn + kd_uws[idx + n_head][:, :dh]
            gls2[p0["j"], pi] = p0["gl"] * p1["gl"]
        return carry

    lax.fori_loop(0, n_chunk // GDN_CHUNKS_PER_ITER, local_body, 0)

    def rec_body(pi, states):
        c0, c1 = 2 * pi, 2 * pi + 1
        s_bs = [s.astype(BF16) for s in states]
        xd2 = [_dot(xs2[j, pi], s_bs[j]) for j in range(n_head)]
        xd0 = [_dot(xs[j, c0], s_bs[j]) for j in range(n_head)]
        od0 = [_dot(qe[j, c0], s_bs[j]) for j in range(n_head)]
        new = tuple(gls2[j, pi][0:1, :] * states[j] - xd2[j] + ns2[j, pi] for j in range(n_head))
        mid = [(gls[j, c0][0:1, :] * states[j] - xd0[j] + ns[j, c0]).astype(BF16) for j in range(n_head)]
        od1 = [_dot(qe[j, c1], mid[j]) for j in range(n_head)]
        for ci, od in ((c0, od0), (c1, od1)):
            rs = pl.ds(pl.multiple_of(ci * c, c), c)
            for j in range(n_head):
                sl = slice(j * dh, (j + 1) * dh)
                o = od[j] + o0[j, ci]
                on = o * lax.rsqrt(jnp.mean(o * o, axis=-1, keepdims=True) + RMS_EPS) * nw_ref[...]
                o_ref[rs, sl] = (on * _silu(z_ref[rs, sl])).astype(o_ref.dtype)
        return new

    lax.fori_loop(0, n_chunk // 2, rec_body, tuple(jnp.zeros((dh, dh), F32) for _ in range(n_head)), unroll=2)


def _gdn(q, k, v, z, ba, seq, a_log, dt_bias, norm_w):
    t = q.shape[0]
    bsz = t // seq
    nh = GDN_HEADS_PER_STEP
    n_grp = GDN_HEADS // nh
    wd = nh * GDN_HEAD_DIM
    bl = ba[:, :GDN_HEADS].reshape(bsz, seq, n_grp, nh)
    al = ba[:, GDN_HEADS:2 * GDN_HEADS].reshape(bsz, seq, n_grp, nh)
    b_col = bl.transpose(0, 2, 1, 3)
    n_chunk = seq // GDN_CHUNK
    n_prob = n_chunk * nh
    a5 = al.reshape(bsz, n_chunk, GDN_CHUNK, n_grp, nh)
    a_col = a5.transpose(0, 3, 2, 1, 4).reshape(bsz, n_grp, GDN_CHUNK, n_prob)
    a_row = a5.transpose(0, 3, 1, 4, 2).reshape(bsz, n_grp, n_prob, GDN_CHUNK)
    hp = jnp.stack([jnp.tile(a_log.astype(F32).reshape(n_grp, nh), (1, n_chunk)),
                    jnp.tile(dt_bias.astype(F32).reshape(n_grp, nh), (1, n_chunk))], axis=1)
    hr = hp.transpose(0, 2, 1)
    act_spec = pl.BlockSpec((seq, wd), lambda b, g: (b, g))
    col_spec = pl.BlockSpec((1, 1, seq, nh), lambda b, g: (b, g, 0, 0))
    return pl.pallas_call(
        functools.partial(_gdn_kernel, seq=seq, n_head=nh),
        out_shape=jax.ShapeDtypeStruct((t, GDN_WIDTH), BF16),
        grid=(bsz, n_grp),
        in_specs=[act_spec, act_spec, act_spec, act_spec, col_spec,
                  pl.BlockSpec((1, 1, GDN_CHUNK, n_prob), lambda b, g: (b, g, 0, 0)),
                  pl.BlockSpec((1, 1, n_prob, GDN_CHUNK), lambda b, g: (b, g, 0, 0)),
                  pl.BlockSpec((1, 2, n_prob), lambda b, g: (g, 0, 0)),
                  pl.BlockSpec((1, n_prob, 2), lambda b, g: (g, 0, 0)),
                  _const_spec((1, GDN_HEAD_DIM))],
        out_specs=act_spec,
        scratch_shapes=[pltpu.VMEM((nh, n_chunk, GDN_HEAD_DIM, GDN_HEAD_DIM), BF16),
                        pltpu.VMEM((nh, n_chunk, GDN_HEAD_DIM, GDN_HEAD_DIM), F32),
                        pltpu.VMEM((nh, n_chunk, GDN_CHUNK, GDN_HEAD_DIM), BF16),
                        pltpu.VMEM((nh, n_chunk, GDN_CHUNK, GDN_HEAD_DIM), F32),
                        pltpu.VMEM((nh, n_chunk, 8, GDN_HEAD_DIM), F32),
                        pltpu.VMEM((seq, nh), F32), pltpu.VMEM((n_prob, GDN_CHUNK), F32),
                        pltpu.VMEM((nh, n_chunk // 2, GDN_HEAD_DIM, GDN_HEAD_DIM), BF16),
                        pltpu.VMEM((nh, n_chunk // 2, GDN_HEAD_DIM, GDN_HEAD_DIM), F32),
                        pltpu.VMEM((nh, n_chunk // 2, 8, GDN_HEAD_DIM), F32)],
        compiler_params=_cparams(2),
        name="gated_deltanet",
    )(q, k, v, z, b_col, a_col, a_row, hp, hr, norm_w.astype(F32).reshape(1, GDN_HEAD_DIM))


NSA_FEAT = 64
F_BLK, F_POS, F_ONE = 0, 32, 38


def _split3(x):
    x1 = x.astype(BF16).astype(F32)
    x2 = (x - x1).astype(BF16).astype(F32)
    return x1, x2, x - x1 - x2


def _nsa_proj_kernel(x_ref, g_ref, wq_ref, wks_ref, wkw_ref, wvs_ref, wvw_ref, wkc_ref, wvc_ref, wg_ref, kf_ref,
                     perm_ref, kw1_ref, vw1_ref,
                     q_ref, ksa_ref, kwa_ref, vst_ref, vwt_ref, kct_ref, kcb_ref, vct_ref, vcb_ref, gl_ref):
    g_n, dh = NSA_KV_GROUPS, NSA_HEAD_DIM
    n_row = x_ref.shape[0] // CMP_STRIDE
    half = CMP_STRIDE * dh

    def compress_halves(c, w1_ref, top_ref, bot_ref):
        rows = _dot(perm_ref[...], c.astype(BF16)).astype(BF16)
        a = jnp.concatenate(
            [jnp.concatenate([rows[l * n_row:(l + 1) * n_row, g * dh:(g + 1) * dh] for l in range(CMP_STRIDE)], axis=1)
             for g in range(g_n)], axis=0)
        top, bot = _dot(a, w1_ref[:half, :]), _dot(a, w1_ref[half:, :])
        for g in range(g_n):
            top_ref[0, g] = top[g * n_row:(g + 1) * n_row]
            bot_ref[0, g] = bot[g * n_row:(g + 1) * n_row]

    xn = _rms(x_ref[...], g_ref[...]).astype(BF16)
    q_ref[...] = (_dot(xn, wq_ref[...]) * (NSA_HEAD_DIM ** -0.5 * LOG2E)).astype(BF16)
    kf = jnp.concatenate([kf_ref[...]] * NSA_KV_GROUPS, axis=1)
    ksa_ref[...] = (_dot(xn, wks_ref[...]) + kf).astype(BF16)
    kwa_ref[...] = (_dot(xn, wkw_ref[...]) + kf).astype(BF16)
    vst_ref[0] = _dot_nt(wvs_ref[...], xn).astype(BF16)
    vwt_ref[0] = _dot_nt(wvw_ref[...], xn).astype(BF16)
    compress_halves(_dot(xn, wkc_ref[...]), kw1_ref, kct_ref, kcb_ref)
    compress_halves(_dot(xn, wvc_ref[...]), vw1_ref, vct_ref, vcb_ref)
    gl_ref[...] = _dot(xn, wg_ref[...])


def _nsa_proj(h, gain, w_in, k_w1, v_w1, bsz, seq):
    t, d = h.shape
    g_n, dh = NSA_KV_GROUPS, NSA_HEAD_DIM
    nq, kvw = NSA_HEADS * dh, NSA_KV_WIDTH
    w = w_in.astype(BF16)
    wq = w[:, :nq]
    wkc, wvc, wks, wvs, wkw, wvw = (w[:, nq + j * kvw: nq + (j + 1) * kvw] for j in range(6))
    wgl = w[:, nq + 6 * kvw:]

    def lane_padded(x, width):
        return jnp.pad(x.reshape(d, g_n, width), ((0, 0), (0, 0), (0, LANES - width))).reshape(d, g_n * LANES)

    pos = np.arange(seq)
    kf = np.zeros((seq, LANES), np.float32)
    kf[pos, dh + F_BLK + pos // SLC_BLOCK] = 1.0
    kf[:, dh + F_POS:dh + F_POS + 3] = ((pos // SLC_BLOCK) * SLC_BLOCK)[:, None]
    kf[:, dh + F_POS + 3:dh + F_POS + 6] = (pos % SLC_BLOCK)[:, None]
    kf[:, dh + F_ONE:dh + F_ONE + 3] = 1.0
    tps = seq // ROW_TILE
    row_spec = lambda n: pl.BlockSpec((ROW_TILE, n), lambda i: (i, 0))
    tr_spec = pl.BlockSpec((1, kvw, ROW_TILE), lambda i: (i // tps, 0, i % tps))
    ws = [wq, lane_padded(wks, dh), lane_padded(wkw, dh), wvs.T, wvw.T, wkc, wvc, lane_padded(wgl, 3 * NSA_GROUP_SIZE)]
    n_row = ROW_TILE // CMP_STRIDE
    tok = np.arange(ROW_TILE)
    perm = np.zeros((ROW_TILE, ROW_TILE), np.float32)
    perm[(tok % CMP_STRIDE) * n_row + tok // CMP_STRIDE, tok] = 1.0
    cmp_consts = [jnp.asarray(perm, dtype=BF16), k_w1.astype(BF16), v_w1.astype(BF16)]
    half_spec = pl.BlockSpec((1, g_n, n_row, CMP_HIDDEN), lambda i: (i // tps, 0, i % tps, 0))
    half_shape = jax.ShapeDtypeStruct((bsz, g_n, seq // CMP_STRIDE, CMP_HIDDEN), F32)
    return pl.pallas_call(
        _nsa_proj_kernel,
        out_shape=[jax.ShapeDtypeStruct((t, nq), BF16),
                   jax.ShapeDtypeStruct((t, g_n * LANES), BF16), jax.ShapeDtypeStruct((t, g_n * LANES), BF16),
                   jax.ShapeDtypeStruct((bsz, kvw, seq), BF16), jax.ShapeDtypeStruct((bsz, kvw, seq), BF16),
                   half_shape, half_shape, half_shape, half_shape,
                   jax.ShapeDtypeStruct((t, g_n * LANES), F32)],
        grid=(t // ROW_TILE,),
        in_specs=[row_spec(d), _const_spec((1, d))] + [_const_spec(x.shape) for x in ws]
        + [pl.BlockSpec((ROW_TILE, LANES), lambda i: (i % tps, 0))] + [_const_spec(x.shape) for x in cmp_consts],
        out_specs=[row_spec(nq), row_spec(g_n * LANES), row_spec(g_n * LANES), tr_spec, tr_spec,
                   half_spec, half_spec, half_spec, half_spec, row_spec(g_n * LANES)],
        compiler_params=_cparams(1),
        name="nsa_proj",
    )(h, gain.reshape(1, d), *ws, jnp.asarray(kf), *cmp_consts)


def _nsa_kernel(q_ref, ksa_ref, kwa_ref, vst_ref, vwt_ref, kct_ref, kcb_ref, vct_ref, vcb_ref, gl_ref, sl_ref, sf_ref,
                cf_ref, kc1_ref, kw2_ref, vc1_ref, vw2t_ref, ovt_ref, o_ref,
                kcmp, vcmpt, vs3, vw3, qa_s, oc_s, *, seq):
    qb, dh, r_n, nq = NSA_Q_BLOCK, NSA_HEAD_DIM, NSA_GROUP_SIZE, NSA_QB_PER_ITER
    span = nq * qb
    cols = nq * r_n * qb
    n_cmp_pad = seq // CMP_STRIDE
    n_cmp = (seq - CMP_LEN) // CMP_STRIDE + 1
    n_blk = seq // SLC_BLOCK
    n_win = WINDOW + span
    it_per_slab = NSA_SLAB_KEYS // span
    nf = NSA_FEAT

    def hidden(top_ref, bot_ref, c1_ref):
        bot = pltpu.roll(bot_ref[0, 0], n_cmp_pad - 1, 0)
        return _gelu(top_ref[0, 0] + bot + c1_ref[...]).astype(BF16)

    kcmp[...] = jnp.concatenate([_dot(hidden(kct_ref, kcb_ref, kc1_ref), kw2_ref[...]), cf_ref[...]],
                                axis=1).astype(BF16)
    vcmpt[...] = _dot_nt(vw2t_ref[...], hidden(vct_ref, vcb_ref, vc1_ref)).astype(BF16)
    for n in range(seq // span):
        vs3[n] = vst_ref[0, :, n * span:(n + 1) * span]
        vw3[n] = vwt_ref[0, :, n * span:(n + 1) * span]

    def iota(shape, axis):
        return lax.broadcasted_iota(jnp.int32, shape, axis)

    slope = sl_ref[0]
    colc, colr = iota((cols, 1), 0), iota((1, cols), 1)
    tc_col = (colc // (r_n * qb)) * qb + colc % qb
    tc_row = (colr // (r_n * qb)) * qb + colr % qb
    qi_row = colr // (r_n * qb)
    lane_f = iota((1, nf), 1)
    jr = iota((n_cmp_pad, 1), 0)
    nr = iota((nf, 1), 0)
    lane_s = iota((1, span), 1)
    qi_s = lane_s // qb
    key_d = iota((span, 1), 0)
    key_w = iota((n_win, 1), 0)
    eye_s = (iota((span, span), 0) == iota((span, span), 1)).astype(BF16)

    def prepare(p):
        t0 = pl.multiple_of(p * span, span)
        qf = jnp.concatenate([q_ref[pl.ds(t0 + qi * qb, qb), r * dh:(r + 1) * dh]
                              for qi in range(nq) for r in range(r_n)], axis=0).astype(F32)
        c1, c2, c3 = _split3(-(slope * (t0 + tc_col).astype(F32)))
        f_plain = jnp.where(lane_f == F_ONE, c1, jnp.where(lane_f == F_ONE + 1, c2,
                            jnp.where(lane_f == F_ONE + 2, c3, sf_ref[0])))
        qa_plain = jnp.concatenate([qf, f_plain], axis=1).astype(BF16)

        ok_c = (jr * CMP_STRIDE + (CMP_LEN - 1) <= t0 + tc_row) & (jr < n_cmp)
        s_c = jnp.where(ok_c, _dot_nt(kcmp[...], qa_plain), NEG_INF)
        e_c = jnp.where(ok_c, jnp.exp2(s_c - jnp.max(s_c, axis=0, keepdims=True)), 0.0)
        l_c = jnp.sum(e_c, axis=0, keepdims=True)
        p_c = e_c / jnp.where(l_c > 0.0, l_c, 1.0)
        oc_s[...] = _dot(vcmpt[...], p_c.astype(BF16))

        imp4 = sum(_dot(ovt_ref[...], part.astype(BF16)) for part in _split3(p_c))
        halves = []
        for qi in range(nq):
            a = imp4[:, (2 * qi) * LANES:(2 * qi + 1) * LANES] + imp4[:, (2 * qi + 1) * LANES:(2 * qi + 2) * LANES]
            halves.append(a + pltpu.roll(a, qb, 1))
        low_half = lax.broadcasted_iota(jnp.int32, (1, LANES), 1) < qb
        imp = jnp.concatenate([jnp.where(low_half, halves[2 * v], halves[2 * v + 1]) for v in range(nq // 2)],
                              axis=1)

        iq = nq * p + qi_s
        cand = (nr > 0) & (nr < iq)
        sc = jnp.where(cand, imp, NEG_INF)
        sel = (nr == 0) & (iq > 0)
        for _ in range(N_SELECT - 2):
            best = jnp.max(sc, axis=0, keepdims=True)
            first = jnp.min(jnp.where(sc == best, nr, nf), axis=0, keepdims=True)
            pick = (nr == first) & cand
            sel = sel | pick
            sc = jnp.where(pick, NEG_INF, sc)
        neg_slab = jnp.where(sel & (nr < nq * p), 0.0, NEG_INF)
        neg_cur = jnp.where(nr == iq, 0.0, jnp.where((nr >= nq * p) & (nr < iq) & sel, 0.0, NEG_INF))
        neg_t = jnp.concatenate([neg_slab, neg_cur], axis=0).astype(BF16)
        neg = _dot_nt(eye_s, neg_t)
        rows_of = lambda x: jnp.concatenate([x[qi * qb:(qi + 1) * qb] for qi in range(nq) for _ in range(r_n)], axis=0)
        f_slab = jnp.where(lane_f < n_blk, rows_of(neg[:, :nf]), f_plain)
        f_cur = jnp.where(lane_f < n_blk, rows_of(neg[:, nf:]), f_plain)
        qa_s[0] = qa_plain
        qa_s[1] = jnp.concatenate([qf, f_slab], axis=1).astype(BF16)
        qa_s[2] = jnp.concatenate([qf, f_cur], axis=1).astype(BF16)

    n_iter = seq // span

    def make_body(n_keys):
        def body(p, carry):
            t0 = pl.multiple_of(p * span, span)
            w0 = pl.multiple_of(jnp.maximum(t0 - WINDOW, 0), span)
            oc_t = oc_s[...]
            s_cu = _dot_nt(ksa_ref[pl.ds(t0, span), :], qa_s[2])
            s_w = _dot_nt(kwa_ref[pl.ds(w0, n_win), :], qa_s[0])
            s_sl = _dot_nt(ksa_ref[0:n_keys, :], qa_s[1])
            prepare(jnp.minimum(p + 1, n_iter - 1))

            own = (key_d // qb) == qi_row
            s_cu = jnp.where(own & (key_d % qb > tc_row % qb), NEG_INF, s_cu)
            m_s = jnp.maximum(jnp.max(s_sl, axis=0, keepdims=True), jnp.max(s_cu, axis=0, keepdims=True))
            e_sl = jnp.exp2(s_sl - m_s)
            e_cu = jnp.exp2(s_cu - m_s)
            l_s = jnp.sum(e_sl, axis=0, keepdims=True) + jnp.sum(e_cu, axis=0, keepdims=True)
            os_t = (_dot(vst_ref[0, :, 0:n_keys], e_sl.astype(BF16)) + _dot(vs3[p], e_cu.astype(BF16))) / l_s

            rel = (t0 - w0) + tc_row - key_w
            s_w = jnp.where((rel >= 0) & (rel < WINDOW), s_w, NEG_INF)
            e_w = jnp.exp2(s_w - jnp.max(s_w, axis=0, keepdims=True))
            e_wb = e_w.astype(BF16)
            b0 = w0 // span
            ow_t = _dot(vw3[b0], e_wb[0:span])
            for j in range(1, n_win // span):
                ow_t = ow_t + _dot(vw3[b0 + j], e_wb[j * span:(j + 1) * span])
            ow_t = ow_t / jnp.sum(e_w, axis=0, keepdims=True)

            o_all = jnp.concatenate([os_t, ow_t, oc_t, jnp.zeros_like(oc_t)], axis=0).T
            gate = _sigmoid(gl_ref[pl.ds(t0, span), 0:3 * r_n])
            for qi in range(nq):
                g_q = gate[qi * qb:(qi + 1) * qb]
                heads = []
                for r in range(r_n):
                    rs = slice((qi * r_n + r) * qb, (qi * r_n + r + 1) * qb)
                    heads.append(g_q[:, 3 * r:3 * r + 1] * o_all[rs, 2 * dh:3 * dh]
                                 + g_q[:, 3 * r + 1:3 * r + 2] * o_all[rs, :dh]
                                 + g_q[:, 3 * r + 2:3 * r + 3] * o_all[rs, dh:2 * dh])
                o_ref[pl.ds(t0 + qi * qb, qb), :] = jnp.concatenate(heads, axis=1).astype(o_ref.dtype)
            return carry
        return body

    prepare(jnp.int32(0))
    for sb in range(seq // NSA_SLAB_KEYS):
        lax.fori_loop(sb * it_per_slab, (sb + 1) * it_per_slab, make_body((sb + 1) * NSA_SLAB_KEYS), 0)


def _nsa(q, ksa, kwa, vst, vwt, kct, kcb, vct, vcb, gl, bsz, seq, pe_k, pe_v, k_w1, k_b1, k_w2, v_w1, v_b1, v_w2):
    g_n, r_n, dh, qb = NSA_KV_GROUPS, NSA_GROUP_SIZE, NSA_HEAD_DIM, NSA_Q_BLOCK
    t = bsz * seq
    n_cmp_pad = seq // CMP_STRIDE
    n_blk = seq // SLC_BLOCK
    span = NSA_QB_PER_ITER * qb
    assert n_blk <= F_POS and seq % NSA_SLAB_KEYS == 0 and seq >= WINDOW + span

    slopes = np.asarray([2.0 ** (-8.0 * (h + 1) / NSA_HEADS) for h in range(NSA_HEADS)], dtype=np.float32)
    cols = NSA_QB_PER_ITER * r_n * qb
    slope_rows = jnp.asarray(np.tile(np.repeat(slopes.reshape(g_n, r_n), qb, axis=1), (1, NSA_QB_PER_ITER))
                             .reshape(g_n, cols, 1)) * LOG2E
    s3 = jnp.concatenate(_split3(slope_rows) * 2, axis=2)
    sfeat = jnp.pad(s3, ((0, 0), (0, 0), (F_POS, NSA_FEAT - F_POS - 6)))
    cf = np.zeros((n_cmp_pad, NSA_FEAT), np.float32)
    cf[:, F_POS:F_POS + 3] = (np.arange(n_cmp_pad) * CMP_STRIDE)[:, None]
    cf[:, F_POS + 3:F_POS + 6] = (CMP_LEN - 1) * 0.5
    cf[:, F_ONE:F_ONE + 3] = 1.0
    tok = np.arange(n_cmp_pad)[:, None] * CMP_STRIDE + np.arange(CMP_LEN)[None, :]
    overlap_t = ((tok // SLC_BLOCK)[:, :, None] == np.arange(NSA_FEAT)[None, None, :]).mean(axis=1).astype(np.float32).T
    def pos_const(pe, w1, b1):
        return _dot(pe.astype(BF16).reshape(1, -1), w1.astype(BF16)) + b1.astype(F32).reshape(1, -1)

    cmp_spec = pl.BlockSpec((1, 1, n_cmp_pad, CMP_HIDDEN), lambda b, g: (b, g, 0, 0))
    q_spec = pl.BlockSpec((seq, r_n * dh), lambda b, g: (b, g))
    k_spec = pl.BlockSpec((seq, LANES), lambda b, g: (b, g))
    v_spec = pl.BlockSpec((1, dh, seq), lambda b, g: (b, g, 0))
    w1s, b1s, w2s = (CMP_LEN * dh, CMP_HIDDEN), (1, CMP_HIDDEN), (CMP_HIDDEN, dh)
    return pl.pallas_call(
        functools.partial(_nsa_kernel, seq=seq),
        out_shape=jax.ShapeDtypeStruct((t, NSA_HEADS * dh), BF16),
        grid=(bsz, g_n),
        in_specs=[q_spec, k_spec, k_spec, v_spec, v_spec, cmp_spec, cmp_spec, cmp_spec, cmp_spec, k_spec,
                  pl.BlockSpec((1, cols, 1), lambda b, g: (g, 0, 0)),
                  pl.BlockSpec((1, cols, NSA_FEAT), lambda b, g: (g, 0, 0)),
                  _const_spec((n_cmp_pad, NSA_FEAT)),
                  _const_spec(b1s), _const_spec(w2s), _const_spec(b1s), _const_spec((dh, CMP_HIDDEN)),
                  _const_spec((NSA_FEAT, n_cmp_pad))],
        out_specs=q_spec,
        scratch_shapes=[pltpu.VMEM((n_cmp_pad, dh + NSA_FEAT), BF16), pltpu.VMEM((dh, n_cmp_pad), BF16),
                        pltpu.VMEM((seq // span, dh, span), BF16), pltpu.VMEM((seq // span, dh, span), BF16),
                        pltpu.VMEM((3, cols, dh + NSA_FEAT), BF16), pltpu.VMEM((dh, cols), F32)],
        compiler_params=_cparams(2),
        name="nsa_attention",
    )(q, ksa, kwa, vst, vwt, kct, kcb, vct, vcb, gl, slope_rows, sfeat, jnp.asarray(cf),
      pos_const(pe_k, k_w1, k_b1), k_w2.astype(BF16), pos_const(pe_v, v_w1, v_b1), v_w2.astype(BF16).T, jnp.asarray(overlap_t, dtype=BF16))


def _pad_cols(w, n):
    return jnp.pad(w, ((0, 0), (0, n - w.shape[1])))


def kernel(x, ab_w_in, ab_w_out, s5_lambda_re, s5_lambda_im, s5_log_step, s5_b_re, s5_b_im, s5_c_re, s5_c_im, s5_d, s5_w_glu, s5_b_glu, gdn_conv_w, gdn_a_log, gdn_dt_bias, gdn_norm_w, nsa_w_in, nsa_w_out, nsa_pe_k, nsa_pe_v, nsa_k_w1, nsa_k_b1, nsa_k_w2, nsa_v_w1, nsa_v_b1, nsa_v_w2, ffn_w_in, ffn_conv_w, ffn_conv_b, ffn_w_out, norm_mix, norm_ffn, norm_final):
    bsz, seq, d = x.shape
    depth = ffn_w_in.shape[0]
    h = x.astype(F32).reshape(bsz * seq, d)
    for layer in range(depth):
        i = layer // 2
        if layer % 2 == 0:
            u, q, k, v, z, ba = _ab_proj(h, norm_mix[layer], ab_w_in[i], gdn_conv_w[i], seq)
            prep = _s5_prep(s5_lambda_re[i], s5_lambda_im[i], s5_log_step[i], s5_b_re[i], s5_b_im[i],
                            s5_c_re[i], s5_c_im[i], s5_d[i])
            y_a = _s5(u, seq, prep, s5_w_glu[i], s5_b_glu[i])
            y_b = _gdn(q, k, v, z, ba, seq, gdn_a_log[i], gdn_dt_bias[i], gdn_norm_w[i])
            wo = ab_w_out[i].astype(BF16)
            mix, wms = [y_a, y_b], [wo[:S5_WIDTH], wo[S5_WIDTH:]]
        else:
            proj = _nsa_proj(h, norm_mix[layer], nsa_w_in[i], nsa_k_w1[i], nsa_v_w1[i], bsz, seq)
            o = _nsa(*proj, bsz, seq, nsa_pe_k[i], nsa_pe_v[i], nsa_k_w1[i],
                     nsa_k_b1[i], nsa_k_w2[i], nsa_v_w1[i], nsa_v_b1[i], nsa_v_w2[i])
            mix, wms = [o], [nsa_w_out[i].astype(BF16)]
        h = _mix_ffn(h, mix, wms, seq, norm_ffn[layer], ffn_w_in[layer], ffn_conv_w[layer], ffn_conv_b[layer],
                     ffn_w_out[layer], norm_final, final=(layer == depth - 1))
    return h.reshape(bsz, seq, d).astype(x.dtype)
```

```python
import functools
import math

import numpy as np
import jax
import jax.numpy as jnp
from jax import lax
from jax.experimental import pallas as pl
from jax.experimental.pallas import tpu as pltpu

F32 = jnp.float32
BF16 = jnp.bfloat16
HI = lax.Precision.HIGHEST

D_MODEL = 1024
S5_WIDTH = 256
S5_GROUP = 16
S5_GROUPS = 16
S5_STATE = 64
S5_CHUNK = 16
GDN_HEAD_DIM = 128
GDN_HEADS = 6
GDN_WIDTH = 768
GDN_CONV = 4
GDN_CHUNK = 64
GDN_HEADS_PER_STEP = 3
GDN_INV_BLOCK = 8
GDN_CHUNKS_PER_ITER = 8
NSA_HEADS = 16
NSA_HEAD_DIM = 64
NSA_KV_GROUPS = 4
NSA_GROUP_SIZE = 4
NSA_KV_WIDTH = 256
CMP_LEN = 32
CMP_STRIDE = 16
CMP_HIDDEN = 256
SLC_BLOCK = 64
N_SELECT = 4
WINDOW = 256
NSA_Q_BLOCK = 64
NSA_SLAB_KEYS = 512
NSA_QB_PER_ITER = 2
LOG2E = math.log2(math.e)
FFN_HIDDEN = 2816
FFN_CONV = 3
FFN_COL_CHUNK = 256
RMS_EPS = 1e-6
NEG_INF = -1e30
LANES = 128
BF16_SUBLANES = 16
VMEM_LIMIT = 56 * 1024 * 1024
ROW_TILE = 512


def _cparams(n_axes):
    return pltpu.CompilerParams(dimension_semantics=("arbitrary",) * n_axes,
                                vmem_limit_bytes=VMEM_LIMIT)


def _rms(x, g):
    return x * lax.rsqrt(jnp.mean(x * x, axis=-1, keepdims=True) + RMS_EPS) * g


def _gelu(x):
    return 0.5 * x * (1.0 + jnp.tanh(math.sqrt(2.0 / math.pi) * (x + 0.044715 * (x * x * x))))


def _sigmoid(x):
    return 1.0 / (1.0 + jnp.exp(-x))


def _silu(x):
    return x * _sigmoid(x)


def _dot(a, b):
    return jnp.dot(a, b, preferred_element_type=F32)


def _dot_nt(a, b):
    return lax.dot_general(a, b, (((1,), (1,)), ((), ())), preferred_element_type=F32)


def _dot_tn(a, b):
    return lax.dot_general(a, b, (((0,), (0,)), ((), ())), preferred_element_type=F32)


def _const_spec(shape):
    nd = len(shape)
    return pl.BlockSpec(shape, lambda *_: (0,) * nd)


def _ab_proj_kernel(x_ref, g_ref, wu_ref, wq_ref, wk_ref, wv_ref, wz_ref, wba_ref, cw_ref,
                    u_ref, q_ref, k_ref, v_ref, z_ref, ba_ref, ext_ref, *, tiles_per_seq):
    rows = x_ref.shape[0]
    pad = 8
    dh = GDN_HEAD_DIM
    xn = _rms(x_ref[...], g_ref[...]).astype(BF16)
    seq_start = pl.program_id(0) % tiles_per_seq == 0

    @pl.when(pl.program_id(0) == 0)
    def _():
        ext_ref[...] = jnp.zeros_like(ext_ref)

    def finish(idx, pre, o_ref):
        prev = jnp.where(seq_start, 0.0, ext_ref[idx, rows:rows + pad, :])
        ext_ref[idx, 0:pad, :] = prev
        ext_ref[idx, pad:, :] = pre
        cw = cw_ref[idx]
        y = pre * cw[GDN_CONV - 1:GDN_CONV]
        for sh in range(1, GDN_CONV):
            y = y + ext_ref[idx, pad - sh:pad - sh + rows, :] * cw[GDN_CONV - 1 - sh:GDN_CONV - sh]
        y = _silu(y)
        if idx < 2:
            parts = []
            for j in range(GDN_HEADS):
                p = y[:, j * dh:(j + 1) * dh]
                p = p * lax.rsqrt(jnp.sum(p * p, axis=-1, keepdims=True) + 1e-6)
                parts.append(p * (dh ** -0.5) if idx == 0 else p)
            y = jnp.concatenate(parts, axis=1)
        o_ref[...] = y

    pre_q = _dot(xn, wq_ref[...])
    pre_k = _dot(xn, wk_ref[...])
    finish(0, pre_q, q_ref)
    pre_v = _dot(xn, wv_ref[...])
    finish(1, pre_k, k_ref)
    z_ref[...] = _dot(xn, wz_ref[...])
    finish(2, pre_v, v_ref)
    u_ref[...] = _dot(xn, wu_ref[...])
    ba_ref[...] = _dot(xn, wba_ref[...])


def _ab_proj(h, gain, w_in, conv_w, seq):
    t, d = h.shape
    w = w_in.astype(BF16)
    cuts = np.cumsum([0, S5_WIDTH, GDN_WIDTH, GDN_WIDTH, GDN_WIDTH, GDN_WIDTH])
    ws = [w[:, cuts[j]:cuts[j + 1]] for j in range(5)] + [_pad_cols(w[:, cuts[5]:], LANES)]
    cw = conv_w.astype(F32).reshape(GDN_CONV, 3, GDN_WIDTH).transpose(1, 0, 2)
    row_spec = lambda n: pl.BlockSpec((ROW_TILE, n), lambda i: (i, 0))
    return pl.pallas_call(
        functools.partial(_ab_proj_kernel, tiles_per_seq=seq // ROW_TILE),
        out_shape=[jax.ShapeDtypeStruct((t, x.shape[1]), F32) for x in ws],
        grid=(t // ROW_TILE,),
        in_specs=[row_spec(d), _const_spec((1, d))] + [_const_spec(x.shape) for x in ws] + [_const_spec(cw.shape)],
        out_specs=[row_spec(x.shape[1]) for x in ws],
        scratch_shapes=[pltpu.VMEM((3, ROW_TILE + 8, GDN_WIDTH), F32)],
        compiler_params=_cparams(1),
        name="ab_proj",
    )(h, gain.reshape(1, d), *ws, cw)


def _ffn_kernel(h_ref, hp_ref, *refs, n_mix, tiles_per_seq, n_chunks, final):
    y_refs, yp_refs, wm_refs = refs[:n_mix], refs[n_mix:2 * n_mix], refs[2 * n_mix:3 * n_mix]
    (g_ref, win_ref, cw_ref, cb_ref, wout_ref, gf_ref, o_ref,
     xn_ref, up0_ref, up1_ref, act_ref, x1_ref) = refs[3 * n_mix:]
    halo = BF16_SUBLANES
    rows = h_ref.shape[0]
    sub = 128
    fcw = act_ref.shape[2]
    g = g_ref[...]
    n_piece = rows // sub
    x1, x1p = h_ref[...], hp_ref[...]
    for y_ref, yp_ref, wm_ref in zip(y_refs, yp_refs, wm_refs):
        x1 = x1 + _dot(y_ref[...], wm_ref[...])
        x1p = x1p + _dot(yp_ref[...], wm_ref[...])
    x1_ref[...] = x1
    seq_start = pl.program_id(0) % tiles_per_seq == 0
    xn_ref[0:halo, :] = jnp.where(seq_start, 0.0, _rms(x1p, g)).astype(BF16)
    xn_ref[halo:, :] = _rms(x1, g).astype(BF16)

    def gate_piece(j, buf, r):
        r0 = r * sub

        def conv(half, cw, cb):
            y = cb
            for k in range(FFN_CONV):
                start = halo + r0 - (FFN_CONV - 1) + k
                y = y + buf[half, start:start + sub, :] * cw[k:k + 1]
            return y
        a = conv(0, cw_ref[j], cb_ref[j])
        b = conv(1, cw_ref[n_chunks + j], cb_ref[n_chunks + j])
        act_ref[j, r0:r0 + sub, :] = (_silu(a) * b).astype(BF16)

    def up_and_gate(j_up, buf_up, j_gate, buf_gate):
        xe = xn_ref[...]
        for half in range(2):
            col = pl.multiple_of((half * n_chunks + j_up) * fcw, fcw)
            buf_up[half] = _dot(xe, win_ref[:, pl.ds(col, fcw)])
            if j_gate is not None:
                for r in range(half * n_piece // 2, (half + 1) * n_piece // 2):
                    gate_piece(j_gate, buf_gate, r)

    def body(i, carry):
        up_and_gate(2 * i + 1, up1_ref, 2 * i, up0_ref)
        up_and_gate(2 * i + 2, up0_ref, 2 * i + 1, up1_ref)
        return carry

    up_and_gate(0, up0_ref, None, None)
    lax.fori_loop(0, (n_chunks - 1) // 2, body, 0)
    for r in range(n_piece):
        gate_piece(n_chunks - 1, up0_ref, r)

    out = x1_ref[...]
    for j in range(n_chunks):
        out = out + _dot(act_ref[j], wout_ref[j])
    if final:
        out = _rms(out, gf_ref[...])
    o_ref[...] = out


def _mix_ffn(h, ys, wms, seq, gain, w_in, conv_w, conv_b, w_out, gain_final, final):
    t, d = h.shape
    n_mix = len(ys)
    fc = FFN_COL_CHUNK
    n_chunks = FFN_HIDDEN // fc
    assert n_chunks % 2 == 1
    halo = BF16_SUBLANES
    win = w_in.astype(BF16)
    cw = conv_w.astype(F32).reshape(FFN_CONV, 2 * n_chunks, fc).transpose(1, 0, 2)
    cb = conv_b.astype(F32).reshape(2 * n_chunks, 1, fc)
    wout = w_out.astype(BF16).reshape(n_chunks, fc, d)
    blocks_per_tile = ROW_TILE // halo
    tile_spec = lambda n: pl.BlockSpec((ROW_TILE, n), lambda i: (i, 0))
    prev_spec = lambda n: pl.BlockSpec((halo, n), lambda i: (jnp.maximum(i * blocks_per_tile - 1, 0), 0))
    return pl.pallas_call(
        functools.partial(_ffn_kernel, n_mix=n_mix, tiles_per_seq=seq // ROW_TILE, n_chunks=n_chunks, final=final),
        out_shape=jax.ShapeDtypeStruct((t, d), F32),
        grid=(t // ROW_TILE,),
        in_specs=[tile_spec(d), prev_spec(d)]
        + [tile_spec(y.shape[1]) for y in ys] + [prev_spec(y.shape[1]) for y in ys]
        + [_const_spec(w.shape) for w in wms]
        + [_const_spec((1, d)),
                  pl.BlockSpec(win.shape, lambda i: (0, 0), pipeline_mode=pl.Buffered(1)),
                  _const_spec(cw.shape), _const_spec(cb.shape),
                  pl.BlockSpec(wout.shape, lambda i: (0, 0, 0), pipeline_mode=pl.Buffered(1)),
                  _const_spec((1, d))],
        out_specs=pl.BlockSpec((ROW_TILE, d), lambda i: (i, 0)),
        scratch_shapes=[pltpu.VMEM((ROW_TILE + halo, d), BF16), pltpu.VMEM((2, ROW_TILE + halo, fc), F32),
                        pltpu.VMEM((2, ROW_TILE + halo, fc), F32), pltpu.VMEM((n_chunks, ROW_TILE, fc), BF16),
                        pltpu.VMEM((ROW_TILE, d), F32)],
        compiler_params=_cparams(1),
        name="mix_ffn",
    )(h, h, *ys, *ys, *wms, gain.reshape(1, d), win, cw, cb, wout, gain_final.reshape(1, d))


def _s5_prep(lam_re, lam_im, log_step, b_re, b_im, c_re, c_im, d_skip):
    g_n, p_n, h_n, L = S5_GROUPS, S5_STATE, S5_GROUP, S5_CHUNK
    step = jnp.exp(log_step.astype(F32))[:, None]
    lr, li = lam_re.astype(F32), lam_im.astype(F32)
    mag = jnp.exp(lr * step)
    a_re = mag * jnp.cos(li * step)
    a_im = mag * jnp.sin(li * step)
    den = lr * lr + li * li
    n_re, n_im = a_re - 1.0, a_im
    z_re = (n_re * lr + n_im * li) / den
    z_im = (n_im * lr - n_re * li) / den
    b_re, b_im = b_re.astype(F32), b_im.astype(F32)
    bb_re = z_re[..., None] * b_re - z_im[..., None] * b_im
    bb_im = z_re[..., None] * b_im + z_im[..., None] * b_re
    c_re, c_im = c_re.astype(F32), c_im.astype(F32)
    pw_re, pw_im = [jnp.ones_like(a_re)], [jnp.zeros_like(a_im)]
    for _ in range(L):
        pr, pi = pw_re[-1], pw_im[-1]
        pw_re.append(pr * a_re - pi * a_im)
        pw_im.append(pr * a_im + pi * a_re)
    eye_g = jnp.eye(g_n, dtype=F32)
    ks = []
    for j in range(L):
        ab_re = pw_re[j][..., None] * bb_re - pw_im[j][..., None] * bb_im
        ab_im = pw_re[j][..., None] * bb_im + pw_im[j][..., None] * bb_re
        kj = (jnp.einsum('gop,gpi->gio', c_re, ab_re, precision=HI)
              - jnp.einsum('gop,gpi->gio', c_im, ab_im, precision=HI))
        ks.append(jnp.einsum('gio,gk->giko', kj, eye_g).reshape(g_n * h_n, g_n * h_n))
    kstack = jnp.concatenate(ks[::-1], axis=0).astype(BF16)
    bb = jnp.stack([jnp.einsum('gph,gk->ghkp', bb_re, eye_g), jnp.einsum('gph,gk->ghkp', bb_im, eye_g)], axis=2)
    bb = bb.reshape(g_n * h_n, 2 * g_n * p_n).astype(BF16)
    cc = jnp.stack([jnp.einsum('ghp,gk->gpkh', c_re, eye_g), -jnp.einsum('ghp,gk->gpkh', c_im, eye_g)], axis=0)
    cc = cc.reshape(2 * g_n * p_n, g_n * h_n).astype(BF16)
    apow = jnp.stack([jnp.concatenate([r.reshape(-1), i.reshape(-1)]) for r, i in zip(pw_re, pw_im)], axis=0)
    dvec = jnp.tile(d_skip.astype(F32).reshape(1, g_n * h_n), (1, L))
    return kstack, bb, cc, apow, dvec


def _s5_kernel(u_ref, ks_ref, bb_ref, cc_ref, ap_ref, dv_ref, wg_ref, bg_ref, o_ref, x_scr, hp_scr,
               *, n_batch, n_chunk):
    L, w = S5_CHUNK, S5_WIDTH
    half = S5_GROUPS * S5_STATE
    rows = n_batch * n_chunk
    u2 = u_ref[...].reshape(rows, L * w)
    u2b = u2.astype(BF16)

    def cmul(j, x):
        ar, ai = ap_ref[j:j + 1, :half], ap_ref[j:j + 1, half:]
        xr, xi = x[:, :half], x[:, half:]
        return jnp.concatenate([ar * xr - ai * xi, ar * xi + ai * xr], axis=1)

    xin = _dot(u2b[:, (L - 1) * w:], bb_ref[...])
    for s in range(L - 1):
        xin = xin + cmul(L - 1 - s, _dot(u2b[:, s * w:(s + 1) * w], bb_ref[...]))
    x_scr[...] = xin

    a_l = ap_ref[L:L + 1, :]
    alr, ali = a_l[:, :half], a_l[:, half:]

    def scan_body(k, hs):
        new = []
        for b in range(n_batch):
            r = b * n_chunk + k
            h = hs[b]
            hp_scr[pl.ds(r, 1), :] = h
            hr, hi = h[:, :half], h[:, half:]
            new.append(jnp.concatenate([alr * hr - ali * hi, alr * hi + ali * hr], axis=1)
                       + x_scr[pl.ds(r, 1), :])
        return tuple(new)

    lax.fori_loop(0, n_chunk, scan_body, tuple(jnp.zeros((1, 2 * half), F32) for _ in range(n_batch)))
    hp = hp_scr[...]

    for t in range(L):
        y = _dot(u2b[:, :(t + 1) * w], ks_ref[(L - 1 - t) * w:, :])
        y = y + _dot(cmul(t + 1, hp).astype(BF16), cc_ref[...])
        y = y + dv_ref[:, t * w:(t + 1) * w] * u2[:, t * w:(t + 1) * w]
        z = _gelu(y)
        gate = _sigmoid(_dot(z.astype(BF16), wg_ref[...]) + bg_ref[...])
        o_ref[:, :, t * w:(t + 1) * w] = (z * gate).astype(o_ref.dtype).reshape(n_batch, n_chunk, w)


def _s5(u, seq, prep, w_glu, b_glu):
    t = u.shape[0]
    bsz = t // seq
    L, w = S5_CHUNK, S5_WIDTH
    n_chunk = seq // L
    n_batch = 2
    kstack, bb, cc, apow, dvec = prep
    u3 = u.reshape(bsz, n_chunk, L * w)
    out = pl.pallas_call(
        functools.partial(_s5_kernel, n_batch=n_batch, n_chunk=n_chunk),
        out_shape=jax.ShapeDtypeStruct((bsz, n_chunk, L * w), BF16),
        grid=(bsz // n_batch,),
        in_specs=[pl.BlockSpec((n_batch, n_chunk, L * w), lambda i: (i, 0, 0)),
                  _const_spec(kstack.shape), _const_spec(bb.shape), _const_spec(cc.shape),
                  _const_spec(apow.shape), _const_spec(dvec.shape),
                  _const_spec((w, w)), _const_spec((1, w))],
        out_specs=pl.BlockSpec((n_batch, n_chunk, L * w), lambda i: (i, 0, 0)),
        scratch_shapes=[pltpu.VMEM((n_batch * n_chunk, 2 * S5_GROUPS * S5_STATE), F32),
                        pltpu.VMEM((n_batch * n_chunk, 2 * S5_GROUPS * S5_STATE), F32)],
        compiler_params=_cparams(1),
        name="s5_mixer",
    )(u3, kstack, bb, cc, apow, dvec, w_glu.astype(BF16), b_glu.astype(F32).reshape(1, w))
    return out.reshape(t, w)


def _gdn_kernel(qs, ks, vs, z_ref, bc_ref, ac_ref, ar_ref, hp_ref, hr_ref, nw_ref,
                o_ref, xs, ns, qe, o0, gls, gcc_s, gcr_s, xs2, ns2, gls2, *, seq, n_head):
    c, dh = GDN_CHUNK, GDN_HEAD_DIM
    n_chunk = seq // c

    hp = hp_ref[0]
    hr = hr_ref[0]

    def softplus(x):
        return jnp.maximum(x, 0.0) + jnp.log(1.0 + jnp.exp(-jnp.abs(x)))

    ii = lax.broadcasted_iota(jnp.int32, (c, c), 0)
    jj = lax.broadcasted_iota(jnp.int32, (c, c), 1)
    eye = (ii == jj).astype(F32)
    same_block = (ii // GDN_INV_BLOCK) == (jj // GDN_INV_BLOCK)
    g_col = -jnp.exp(hp[0:1, :]) * softplus(ac_ref[0, 0] + hp[1:2, :])
    g_row = -jnp.exp(hr[:, 0:1]) * softplus(ar_ref[0, 0] + hr[:, 1:2])
    gc_col = jnp.dot((ii >= jj).astype(F32), g_col, precision=HI, preferred_element_type=F32)
    gcr_s[...] = jnp.dot(g_row, (ii <= jj).astype(F32), precision=HI, preferred_element_type=F32)
    for ci in range(n_chunk):
        gcc_s[ci * c:(ci + 1) * c, :] = gc_col[:, ci * n_head:(ci + 1) * n_head]

    def local_body(it, carry):
        pr = []
        for cc in range(GDN_CHUNKS_PER_ITER):
            ci = it * GDN_CHUNKS_PER_ITER + cc
            rs = pl.ds(pl.multiple_of(ci * c, c), c)
            beta = _sigmoid(bc_ref[0, 0, rs, :])
            gccs = gcc_s[rs, :]
            for j in range(n_head):
                sl = slice(j * dh, (j + 1) * dh)
                gcc = gccs[:, j:j + 1]
                gcr = gcr_s[pl.ds(ci * n_head + j, 1), :]
                q, k, v = qs[rs, sl], ks[rs, sl], vs[rs, sl]
                bj = beta[:, j:j + 1]
                kb = k * bj
                e_g = jnp.exp(gcc)
                g_last = gcc[c - 1:c, :]
                pr.append(dict(ci=ci, j=j, q=q, kf=k.astype(BF16), kbb=kb.astype(BF16),
                               decay=jnp.exp(jnp.where(ii >= jj, gcc - gcr, NEG_INF)),
                               rhs=jnp.concatenate([v * bj, kb * e_g], axis=1).astype(BF16),
                               qd=q * e_g, kd=(k * jnp.exp(g_last - gcc)).astype(BF16),
                               gl=jnp.broadcast_to(jnp.exp(g_last), (8, dh))))
        kk = [_dot_nt(p["kbb"], p["kf"]) for p in pr]
        qk = [_dot_nt(p["q"].astype(BF16), p["kf"]) for p in pr]
        ps = [-jnp.where(ii > jj, a * p["decay"], 0.0) for a, p in zip(kk, pr)]
        ds = [jnp.where(same_block, n, 0.0) for n in ps]
        offs = [(n - d).astype(BF16) for n, d in zip(ps, ds)]
        ts = [eye + d for d in ds]
        reach = 2
        while reach < GDN_INV_BLOCK:
            ds = [_dot(d.astype(BF16), d.astype(BF16)) for d in ds]
            ts = [t + _dot(t.astype(BF16), d.astype(BF16)) for t, d in zip(ts, ds)]
            reach *= 2
        ms = [_dot(t.astype(BF16), o) for t, o in zip(ts, offs)]
        ts = [t + _dot(m.astype(BF16), t.astype(BF16)) for t, m in zip(ts, ms)]
        reach = 2
        while reach < c // GDN_INV_BLOCK:
            ms = [_dot(m.astype(BF16), m.astype(BF16)) for m in ms]
            ts = [t + _dot(m.astype(BF16), t.astype(BF16)) for t, m in zip(ts, ms)]
            reach *= 2
        uws = [_dot(t.astype(BF16), p["rhs"]).astype(BF16) for t, p in zip(ts, pr)]
        intras = [(s * p["decay"]).astype(BF16) for s, p in zip(qk, pr)]
        i_uws = [_dot(a, uw) for a, uw in zip(intras, uws)]
        kd_uws = [_dot_tn(p["kd"], uw) for p, uw in zip(pr, uws)]
        for p, i_uw, kd_uw in zip(pr, i_uws, kd_uws):
            j, ci = p["j"], p["ci"]
            ns[j, ci] = kd_uw[:, :dh]
            xs[j, ci] = kd_uw[:, dh:].astype(BF16)
            o0[j, ci] = i_uw[:, :dh]
            qe[j, ci] = (p["qd"] - i_uw[:, dh:]).astype(BF16)
            gls[j, ci] = p["gl"]
        first = [idx for idx, p in enumerate(pr) if (idx // n_head) % 2 == 0]
        x1b = [kd_uws[idx + n_head][:, dh:].astype(BF16) for idx in first]
        x1x0 = [_dot(x1, kd_uws[idx][:, dh:].astype(BF16)) for x1, idx in zip(x1b, first)]
        x1n0 = [_dot(x1, kd_uws[idx][:, :dh].astype(BF16)) for x1, idx in zip(x1b, first)]
        for idx, xx, xn in zip(first, x1x0, x1n0):
            p0, p1 = pr[idx], pr[idx + n_head]
            g0, g1 = p0["gl"][0:1, :], p1["gl"][0:1, :]
            pi = it * (GDN_CHUNKS_PER_ITER // 2) + (idx // n_head) // 2
            xs2[p0["j"], pi] = (g1 * kd_uws[idx][:, dh:] + g0 * kd_uws[idx + n_head][:, dh:] - xx).astype(BF16)
            ns2[p0["j"], pi] = g1 * kd_uws[idx][:, :dh] - xn + kd_uws[idx + n_head][:, :dh]
            gls2[p0["j"], pi] = p0["gl"] * p1["gl"]
        return carry

    lax.fori_loop(0, n_chunk // GDN_CHUNKS_PER_ITER, local_body, 0)

    def rec_body(pi, states):
        c0, c1 = 2 * pi, 2 * pi + 1
        s_bs = [s.astype(BF16) for s in states]
        xd2 = [_dot(xs2[j, pi], s_bs[j]) for j in range(n_head)]
        xd0 = [_dot(xs[j, c0], s_bs[j]) for j in range(n_head)]
        od0 = [_dot(qe[j, c0], s_bs[j]) for j in range(n_head)]
        new = tuple(gls2[j, pi][0:1, :] * states[j] - xd2[j] + ns2[j, pi] for j in range(n_head))
        mid = [(gls[j, c0][0:1, :] * states[j] - xd0[j] + ns[j, c0]).astype(BF16) for j in range(n_head)]
        od1 = [_dot(qe[j, c1], mid[j]) for j in range(n_head)]
        for ci, od in ((c0, od0), (c1, od1)):
            rs = pl.ds(pl.multiple_of(ci * c, c), c)
            for j in range(n_head):
                sl = slice(j * dh, (j + 1) * dh)
                o = od[j] + o0[j, ci]
                on = o * lax.rsqrt(jnp.mean(o * o, axis=-1, keepdims=True) + RMS_EPS) * nw_ref[...]
                o_ref[rs, sl] = (on * _silu(z_ref[rs, sl])).astype(o_ref.dtype)
        return new

    lax.fori_loop(0, n_chunk // 2, rec_body, tuple(jnp.zeros((dh, dh), F32) for _ in range(n_head)), unroll=2)


def _gdn(q, k, v, z, ba, seq, a_log, dt_bias, norm_w):
    t = q.shape[0]
    bsz = t // seq
    nh = GDN_HEADS_PER_STEP
    n_grp = GDN_HEADS // nh
    wd = nh * GDN_HEAD_DIM
    bl = ba[:, :GDN_HEADS].reshape(bsz, seq, n_grp, nh)
    al = ba[:, GDN_HEADS:2 * GDN_HEADS].reshape(bsz, seq, n_grp, nh)
    b_col = bl.transpose(0, 2, 1, 3)
    n_chunk = seq // GDN_CHUNK
    n_prob = n_chunk * nh
    a5 = al.reshape(bsz, n_chunk, GDN_CHUNK, n_grp, nh)
    a_col = a5.transpose(0, 3, 2, 1, 4).reshape(bsz, n_grp, GDN_CHUNK, n_prob)
    a_row = a5.transpose(0, 3, 1, 4, 2).reshape(bsz, n_grp, n_prob, GDN_CHUNK)
    hp = jnp.stack([jnp.tile(a_log.astype(F32).reshape(n_grp, nh), (1, n_chunk)),
                    jnp.tile(dt_bias.astype(F32).reshape(n_grp, nh), (1, n_chunk))], axis=1)
    hr = hp.transpose(0, 2, 1)
    act_spec = pl.BlockSpec((seq, wd), lambda b, g: (b, g))
    col_spec = pl.BlockSpec((1, 1, seq, nh), lambda b, g: (b, g, 0, 0))
    return pl.pallas_call(
        functools.partial(_gdn_kernel, seq=seq, n_head=nh),
        out_shape=jax.ShapeDtypeStruct((t, GDN_WIDTH), BF16),
        grid=(bsz, n_grp),
        in_specs=[act_spec, act_spec, act_spec, act_spec, col_spec,
                  pl.BlockSpec((1, 1, GDN_CHUNK, n_prob), lambda b, g: (b, g, 0, 0)),
                  pl.BlockSpec((1, 1, n_prob, GDN_CHUNK), lambda b, g: (b, g, 0, 0)),
                  pl.BlockSpec((1, 2, n_prob), lambda b, g: (g, 0, 0)),
                  pl.BlockSpec((1, n_prob, 2), lambda b, g: (g, 0, 0)),
                  _const_spec((1, GDN_HEAD_DIM))],
        out_specs=act_spec,
        scratch_shapes=[pltpu.VMEM((nh, n_chunk, GDN_HEAD_DIM, GDN_HEAD_DIM), BF16),
                        pltpu.VMEM((nh, n_chunk, GDN_HEAD_DIM, GDN_HEAD_DIM), F32),
                        pltpu.VMEM((nh, n_chunk, GDN_CHUNK, GDN_HEAD_DIM), BF16),
                        pltpu.VMEM((nh, n_chunk, GDN_CHUNK, GDN_HEAD_DIM), F32),
                        pltpu.VMEM((nh, n_chunk, 8, GDN_HEAD_DIM), F32),
                        pltpu.VMEM((seq, nh), F32), pltpu.VMEM((n_prob, GDN_CHUNK), F32),
                        pltpu.VMEM((nh, n_chunk // 2, GDN_HEAD_DIM, GDN_HEAD_DIM), BF16),
                        pltpu.VMEM((nh, n_chunk // 2, GDN_HEAD_DIM, GDN_HEAD_DIM), F32),
                        pltpu.VMEM((nh, n_chunk // 2, 8, GDN_HEAD_DIM), F32)],
        compiler_params=_cparams(2),
        name="gated_deltanet",
    )(q, k, v, z, b_col, a_col, a_row, hp, hr, norm_w.astype(F32).reshape(1, GDN_HEAD_DIM))


NSA_FEAT = 64
F_BLK, F_POS, F_ONE = 0, 32, 38


def _split3(x):
    x1 = x.astype(BF16).astype(F32)
    x2 = (x - x1).astype(BF16).astype(F32)
    return x1, x2, x - x1 - x2


def _nsa_proj_kernel(x_ref, g_ref, wq_ref, wks_ref, wkw_ref, wvs_ref, wvw_ref, wkc_ref, wvc_ref, wg_ref, kf_ref,
                     perm_ref, kw1_ref, vw1_ref,
                     q_ref, ksa_ref, kwa_ref, vst_ref, vwt_ref, kct_ref, kcb_ref, vct_ref, vcb_ref, gl_ref):
    g_n, dh = NSA_KV_GROUPS, NSA_HEAD_DIM
    n_row = x_ref.shape[0] // CMP_STRIDE
    half = CMP_STRIDE * dh

    def compress_halves(c, w1_ref, top_ref, bot_ref):
        rows = _dot(perm_ref[...], c.astype(BF16)).astype(BF16)
        a = jnp.concatenate(
            [jnp.concatenate([rows[l * n_row:(l + 1) * n_row, g * dh:(g + 1) * dh] for l in range(CMP_STRIDE)], axis=1)
             for g in range(g_n)], axis=0)
        top, bot = _dot(a, w1_ref[:half, :]), _dot(a, w1_ref[half:, :])
        for g in range(g_n):
            top_ref[0, g] = top[g * n_row:(g + 1) * n_row]
            bot_ref[0, g] = bot[g * n_row:(g + 1) * n_row]

    xn = _rms(x_ref[...], g_ref[...]).astype(BF16)
    q_ref[...] = (_dot(xn, wq_ref[...]) * (NSA_HEAD_DIM ** -0.5 * LOG2E)).astype(BF16)
    kf = jnp.concatenate([kf_ref[...]] * NSA_KV_GROUPS, axis=1)
    ksa_ref[...] = (_dot(xn, wks_ref[...]) + kf).astype(BF16)
    kwa_ref[...] = (_dot(xn, wkw_ref[...]) + kf).astype(BF16)
    vst_ref[0] = _dot_nt(wvs_ref[...], xn).astype(BF16)
    vwt_ref[0] = _dot_nt(wvw_ref[...], xn).astype(BF16)
    compress_halves(_dot(xn, wkc_ref[...]), kw1_ref, kct_ref, kcb_ref)
    compress_halves(_dot(xn, wvc_ref[...]), vw1_ref, vct_ref, vcb_ref)
    gl_ref[...] = _dot(xn, wg_ref[...])


def _nsa_proj(h, gain, w_in, k_w1, v_w1, bsz, seq):
    t, d = h.shape
    g_n, dh = NSA_KV_GROUPS, NSA_HEAD_DIM
    nq, kvw = NSA_HEADS * dh, NSA_KV_WIDTH
    w = w_in.astype(BF16)
    wq = w[:, :nq]
    wkc, wvc, wks, wvs, wkw, wvw = (w[:, nq + j * kvw: nq + (j + 1) * kvw] for j in range(6))
    wgl = w[:, nq + 6 * kvw:]

    def lane_padded(x, width):
        return jnp.pad(x.reshape(d, g_n, width), ((0, 0), (0, 0), (0, LANES - width))).reshape(d, g_n * LANES)

    pos = np.arange(seq)
    kf = np.zeros((seq, LANES), np.float32)
    kf[pos, dh + F_BLK + pos // SLC_BLOCK] = 1.0
    kf[:, dh + F_POS:dh + F_POS + 3] = ((pos // SLC_BLOCK) * SLC_BLOCK)[:, None]
    kf[:, dh + F_POS + 3:dh + F_POS + 6] = (pos % SLC_BLOCK)[:, None]
    kf[:, dh + F_ONE:dh + F_ONE + 3] = 1.0
    tps = seq // ROW_TILE
    row_spec = lambda n: pl.BlockSpec((ROW_TILE, n), lambda i: (i, 0))
    tr_spec = pl.BlockSpec((1, kvw, ROW_TILE), lambda i: (i // tps, 0, i % tps))
    ws = [wq, lane_padded(wks, dh), lane_padded(wkw, dh), wvs.T, wvw.T, wkc, wvc, lane_padded(wgl, 3 * NSA_GROUP_SIZE)]
    n_row = ROW_TILE // CMP_STRIDE
    tok = np.arange(ROW_TILE)
    perm = np.zeros((ROW_TILE, ROW_TILE), np.float32)
    perm[(tok % CMP_STRIDE) * n_row + tok // CMP_STRIDE, tok] = 1.0
    cmp_consts = [jnp.asarray(perm, dtype=BF16), k_w1.astype(BF16), v_w1.astype(BF16)]
    half_spec = pl.BlockSpec((1, g_n, n_row, CMP_HIDDEN), lambda i: (i // tps, 0, i % tps, 0))
    half_shape = jax.ShapeDtypeStruct((bsz, g_n, seq // CMP_STRIDE, CMP_HIDDEN), F32)
    return pl.pallas_call(
        _nsa_proj_kernel,
        out_shape=[jax.ShapeDtypeStruct((t, nq), BF16),
                   jax.ShapeDtypeStruct((t, g_n * LANES), BF16), jax.ShapeDtypeStruct((t, g_n * LANES), BF16),
                   jax.ShapeDtypeStruct((bsz, kvw, seq), BF16), jax.ShapeDtypeStruct((bsz, kvw, seq), BF16),
                   half_shape, half_shape, half_shape, half_shape,
                   jax.ShapeDtypeStruct((t, g_n * LANES), F32)],
        grid=(t // ROW_TILE,),
        in_specs=[row_spec(d), _const_spec((1, d))] + [_const_spec(x.shape) for x in ws]
        + [pl.BlockSpec((ROW_TILE, LANES), lambda i: (i % tps, 0))] + [_const_spec(x.shape) for x in cmp_consts],
        out_specs=[row_spec(nq), row_spec(g_n * LANES), row_spec(g_n * LANES), tr_spec, tr_spec,
                   half_spec, half_spec, half_spec, half_spec, row_spec(g_n * LANES)],
        compiler_params=_cparams(1),
        name="nsa_proj",
    )(h, gain.reshape(1, d), *ws, jnp.asarray(kf), *cmp_consts)


def _nsa_kernel(q_ref, ksa_ref, kwa_ref, vst_ref, vwt_ref, kct_ref, kcb_ref, vct_ref, vcb_ref, gl_ref, sl_ref, sf_ref,
                cf_ref, kc1_ref, kw2_ref, vc1_ref, vw2t_ref, ovt_ref, o_ref,
                kcmp, vcmpt, vs3, vw3, qa_s, oc_s, *, seq):
    qb, dh, r_n, nq = NSA_Q_BLOCK, NSA_HEAD_DIM, NSA_GROUP_SIZE, NSA_QB_PER_ITER
    span = nq * qb
    cols = nq * r_n * qb
    n_cmp_pad = seq // CMP_STRIDE
    n_cmp = (seq - CMP_LEN) // CMP_STRIDE + 1
    n_blk = seq // SLC_BLOCK
    n_win = WINDOW + span
    it_per_slab = NSA_SLAB_KEYS // span
    nf = NSA_FEAT

    def hidden(top_ref, bot_ref, c1_ref):
        bot = pltpu.roll(bot_ref[0, 0], n_cmp_pad - 1, 0)
        return _gelu(top_ref[0, 0] + bot + c1_ref[...]).astype(BF16)

    kcmp[...] = jnp.concatenate([_dot(hidden(kct_ref, kcb_ref, kc1_ref), kw2_ref[...]), cf_ref[...]],
                                axis=1).astype(BF16)
    vcmpt[...] = _dot_nt(vw2t_ref[...], hidden(vct_ref, vcb_ref, vc1_ref)).astype(BF16)
    for n in range(seq // span):
        vs3[n] = vst_ref[0, :, n * span:(n + 1) * span]
        vw3[n] = vwt_ref[0, :, n * span:(n + 1) * span]

    def iota(shape, axis):
        return lax.broadcasted_iota(jnp.int32, shape, axis)

    slope = sl_ref[0]
    colc, colr = iota((cols, 1), 0), iota((1, cols), 1)
    tc_col = (colc // (r_n * qb)) * qb + colc % qb
    tc_row = (colr // (r_n * qb)) * qb + colr % qb
    qi_row = colr // (r_n * qb)
    lane_f = iota((1, nf), 1)
    jr = iota((n_cmp_pad, 1), 0)
    nr = iota((nf, 1), 0)
    lane_s = iota((1, span), 1)
    qi_s = lane_s // qb
    key_d = iota((span, 1), 0)
    key_w = iota((n_win, 1), 0)
    eye_s = (iota((span, span), 0) == iota((span, span), 1)).astype(BF16)

    def prepare(p):
        t0 = pl.multiple_of(p * span, span)
        qf = jnp.concatenate([q_ref[pl.ds(t0 + qi * qb, qb), r * dh:(r + 1) * dh]
                              for qi in range(nq) for r in range(r_n)], axis=0).astype(F32)
        c1, c2, c3 = _split3(-(slope * (t0 + tc_col).astype(F32)))
        f_plain = jnp.where(lane_f == F_ONE, c1, jnp.where(lane_f == F_ONE + 1, c2,
                            jnp.where(lane_f == F_ONE + 2, c3, sf_ref[0])))
        qa_plain = jnp.concatenate([qf, f_plain], axis=1).astype(BF16)

        ok_c = (jr * CMP_STRIDE + (CMP_LEN - 1) <= t0 + tc_row) & (jr < n_cmp)
        s_c = jnp.where(ok_c, _dot_nt(kcmp[...], qa_plain), NEG_INF)
        e_c = jnp.where(ok_c, jnp.exp2(s_c - jnp.max(s_c, axis=0, keepdims=True)), 0.0)
        l_c = jnp.sum(e_c, axis=0, keepdims=True)
        p_c = e_c / jnp.where(l_c > 0.0, l_c, 1.0)
        oc_s[...] = _dot(vcmpt[...], p_c.astype(BF16))

        imp4 = sum(_dot(ovt_ref[...], part.astype(BF16)) for part in _split3(p_c))
        halves = []
        for qi in range(nq):
            a = imp4[:, (2 * qi) * LANES:(2 * qi + 1) * LANES] + imp4[:, (2 * qi + 1) * LANES:(2 * qi + 2) * LANES]
            halves.append(a + pltpu.roll(a, qb, 1))
        low_half = lax.broadcasted_iota(jnp.int32, (1, LANES), 1) < qb
        imp = jnp.concatenate([jnp.where(low_half, halves[2 * v], halves[2 * v + 1]) for v in range(nq // 2)],
                              axis=1)

        iq = nq * p + qi_s
        cand = (nr > 0) & (nr < iq)
        sc = jnp.where(cand, imp, NEG_INF)
        sel = (nr == 0) & (iq > 0)
        for _ in range(N_SELECT - 2):
            best = jnp.max(sc, axis=0, keepdims=True)
            first = jnp.min(jnp.where(sc == best, nr, nf), axis=0, keepdims=True)
            pick = (nr == first) & cand
            sel = sel | pick
            sc = jnp.where(pick, NEG_INF, sc)
        neg_slab = jnp.where(sel & (nr < nq * p), 0.0, NEG_INF)
        neg_cur = jnp.where(nr == iq, 0.0, jnp.where((nr >= nq * p) & (nr < iq) & sel, 0.0, NEG_INF))
        neg_t = jnp.concatenate([neg_slab, neg_cur], axis=0).astype(BF16)
        neg = _dot_nt(eye_s, neg_t)
        rows_of = lambda x: jnp.concatenate([x[qi * qb:(qi + 1) * qb] for qi in range(nq) for _ in range(r_n)], axis=0)
        f_slab = jnp.where(lane_f < n_blk, rows_of(neg[:, :nf]), f_plain)
        f_cur = jnp.where(lane_f < n_blk, rows_of(neg[:, nf:]), f_plain)
        qa_s[0] = qa_plain
        qa_s[1] = jnp.concatenate([qf, f_slab], axis=1).astype(BF16)
        qa_s[2] = jnp.concatenate([qf, f_cur], axis=1).astype(BF16)

    n_iter = seq // span

    def make_body(n_keys):
        def body(p, carry):
            t0 = pl.multiple_of(p * span, span)
            w0 = pl.multiple_of(jnp.maximum(t0 - WINDOW, 0), span)
            oc_t = oc_s[...]
            s_cu = _dot_nt(ksa_ref[pl.ds(t0, span), :], qa_s[2])
            s_w = _dot_nt(kwa_ref[pl.ds(w0, n_win), :], qa_s[0])
            s_sl = _dot_nt(ksa_ref[0:n_keys, :], qa_s[1])
            prepare(jnp.minimum(p + 1, n_iter - 1))

            own = (key_d // qb) == qi_row
            s_cu = jnp.where(own & (key_d % qb > tc_row % qb), NEG_INF, s_cu)
            m_s = jnp.maximum(jnp.max(s_sl, axis=0, keepdims=True), jnp.max(s_cu, axis=0, keepdims=True))
            e_sl = jnp.exp2(s_sl - m_s)
            e_cu = jnp.exp2(s_cu - m_s)
            l_s = jnp.sum(e_sl, axis=0, keepdims=True) + jnp.sum(e_cu, axis=0, keepdims=True)
            os_t = (_dot(vst_ref[0, :, 0:n_keys], e_sl.astype(BF16)) + _dot(vs3[p], e_cu.astype(BF16))) / l_s

            rel = (t0 - w0) + tc_row - key_w
            s_w = jnp.where((rel >= 0) & (rel < WINDOW), s_w, NEG_INF)
            e_w = jnp.exp2(s_w - jnp.max(s_w, axis=0, keepdims=True))
            e_wb = e_w.astype(BF16)
            b0 = w0 // span
            ow_t = _dot(vw3[b0], e_wb[0:span])
            for j in range(1, n_win // span):
                ow_t = ow_t + _dot(vw3[b0 + j], e_wb[j * span:(j + 1) * span])
            ow_t = ow_t / jnp.sum(e_w, axis=0, keepdims=True)

            o_all = jnp.concatenate([os_t, ow_t, oc_t, jnp.zeros_like(oc_t)], axis=0).T
            gate = _sigmoid(gl_ref[pl.ds(t0, span), 0:3 * r_n])
            for qi in range(nq):
                g_q = gate[qi * qb:(qi + 1) * qb]
                heads = []
                for r in range(r_n):
                    rs = slice((qi * r_n + r) * qb, (qi * r_n + r + 1) * qb)
                    heads.append(g_q[:, 3 * r:3 * r + 1] * o_all[rs, 2 * dh:3 * dh]
                                 + g_q[:, 3 * r + 1:3 * r + 2] * o_all[rs, :dh]
                                 + g_q[:, 3 * r + 2:3 * r + 3] * o_all[rs, dh:2 * dh])
                o_ref[pl.ds(t0 + qi * qb, qb), :] = jnp.concatenate(heads, axis=1).astype(o_ref.dtype)
            return carry
        return body

    prepare(jnp.int32(0))
    for sb in range(seq // NSA_SLAB_KEYS):
        lax.fori_loop(sb * it_per_slab, (sb + 1) * it_per_slab, make_body((sb + 1) * NSA_SLAB_KEYS), 0)


def _nsa(q, ksa, kwa, vst, vwt, kct, kcb, vct, vcb, gl, bsz, seq, pe_k, pe_v, k_w1, k_b1, k_w2, v_w1, v_b1, v_w2):
    g_n, r_n, dh, qb = NSA_KV_GROUPS, NSA_GROUP_SIZE, NSA_HEAD_DIM, NSA_Q_BLOCK
    t = bsz * seq
    n_cmp_pad = seq // CMP_STRIDE
    n_blk = seq // SLC_BLOCK
    span = NSA_QB_PER_ITER * qb
    assert n_blk <= F_POS and seq % NSA_SLAB_KEYS == 0 and seq >= WINDOW + span

    slopes = np.asarray([2.0 ** (-8.0 * (h + 1) / NSA_HEADS) for h in range(NSA_HEADS)], dtype=np.float32)
    cols = NSA_QB_PER_ITER * r_n * qb
    slope_rows = jnp.asarray(np.tile(np.repeat(slopes.reshape(g_n, r_n), qb, axis=1), (1, NSA_QB_PER_ITER))
                             .reshape(g_n, cols, 1)) * LOG2E
    s3 = jnp.concatenate(_split3(slope_rows) * 2, axis=2)
    sfeat = jnp.pad(s3, ((0, 0), (0, 0), (F_POS, NSA_FEAT - F_POS - 6)))
    cf = np.zeros((n_cmp_pad, NSA_FEAT), np.float32)
    cf[:, F_POS:F_POS + 3] = (np.arange(n_cmp_pad) * CMP_STRIDE)[:, None]
    cf[:, F_POS + 3:F_POS + 6] = (CMP_LEN - 1) * 0.5
    cf[:, F_ONE:F_ONE + 3] = 1.0
    tok = np.arange(n_cmp_pad)[:, None] * CMP_STRIDE + np.arange(CMP_LEN)[None, :]
    overlap_t = ((tok // SLC_BLOCK)[:, :, None] == np.arange(NSA_FEAT)[None, None, :]).mean(axis=1).astype(np.float32).T
    def pos_const(pe, w1, b1):
        return _dot(pe.astype(BF16).reshape(1, -1), w1.astype(BF16)) + b1.astype(F32).reshape(1, -1)

    cmp_spec = pl.BlockSpec((1, 1, n_cmp_pad, CMP_HIDDEN), lambda b, g: (b, g, 0, 0))
    q_spec = pl.BlockSpec((seq, r_n * dh), lambda b, g: (b, g))
    k_spec = pl.BlockSpec((seq, LANES), lambda b, g: (b, g))
    v_spec = pl.BlockSpec((1, dh, seq), lambda b, g: (b, g, 0))
    w1s, b1s, w2s = (CMP_LEN * dh, CMP_HIDDEN), (1, CMP_HIDDEN), (CMP_HIDDEN, dh)
    return pl.pallas_call(
        functools.partial(_nsa_kernel, seq=seq),
        out_shape=jax.ShapeDtypeStruct((t, NSA_HEADS * dh), BF16),
        grid=(bsz, g_n),
        in_specs=[q_spec, k_spec, k_spec, v_spec, v_spec, cmp_spec, cmp_spec, cmp_spec, cmp_spec, k_spec,
                  pl.BlockSpec((1, cols, 1), lambda b, g: (g, 0, 0)),
                  pl.BlockSpec((1, cols, NSA_FEAT), lambda b, g: (g, 0, 0)),
                  _const_spec((n_cmp_pad, NSA_FEAT)),
                  _const_spec(b1s), _const_spec(w2s), _const_spec(b1s), _const_spec((dh, CMP_HIDDEN)),
                  _const_spec((NSA_FEAT, n_cmp_pad))],
        out_specs=q_spec,
        scratch_shapes=[pltpu.VMEM((n_cmp_pad, dh + NSA_FEAT), BF16), pltpu.VMEM((dh, n_cmp_pad), BF16),
                        pltpu.VMEM((seq // span, dh, span), BF16), pltpu.VMEM((seq // span, dh, span), BF16),
                        pltpu.VMEM((3, cols, dh + NSA_FEAT), BF16), pltpu.VMEM((dh, cols), F32)],
        compiler_params=_cparams(2),
        name="nsa_attention",
    )(q, ksa, kwa, vst, vwt, kct, kcb, vct, vcb, gl, slope_rows, sfeat, jnp.asarray(cf),
      pos_const(pe_k, k_w1, k_b1), k_w2.astype(BF16), pos_const(pe_v, v_w1, v_b1), v_w2.astype(BF16).T, jnp.asarray(overlap_t, dtype=BF16))


def _pad_cols(w, n):
    return jnp.pad(w, ((0, 0), (0, n - w.shape[1])))


def kernel(x, ab_w_in, ab_w_out, s5_lambda_re, s5_lambda_im, s5_log_step, s5_b_re, s5_b_im, s5_c_re, s5_c_im, s5_d, s5_w_glu, s5_b_glu, gdn_conv_w, gdn_a_log, gdn_dt_bias, gdn_norm_w, nsa_w_in, nsa_w_out, nsa_pe_k, nsa_pe_v, nsa_k_w1, nsa_k_b1, nsa_k_w2, nsa_v_w1, nsa_v_b1, nsa_v_w2, ffn_w_in, ffn_conv_w, ffn_conv_b, ffn_w_out, norm_mix, norm_ffn, norm_final):
    bsz, seq, d = x.shape
    depth = ffn_w_in.shape[0]
    h = x.astype(F32).reshape(bsz * seq, d)
    for layer in range(depth):
        i = layer // 2
        if layer % 2 == 0:
            u, q, k, v, z, ba = _ab_proj(h, norm_mix[layer], ab_w_in[i], gdn_conv_w[i], seq)
            prep = _s5_prep(s5_lambda_re[i], s5_lambda_im[i], s5_log_step[i], s5_b_re[i], s5_b_im[i],
                            s5_c_re[i], s5_c_im[i], s5_d[i])
            y_a = _s5(u, seq, prep, s5_w_glu[i], s5_b_glu[i])
            y_b = _gdn(q, k, v, z, ba, seq, gdn_a_log[i], gdn_dt_bias[i], gdn_norm_w[i])
            wo = ab_w_out[i].astype(BF16)
            mix, wms = [y_a, y_b], [wo[:S5_WIDTH], wo[S5_WIDTH:]]
        else:
            proj = _nsa_proj(h, norm_mix[layer], nsa_w_in[i], nsa_k_w1[i], nsa_v_w1[i], bsz, seq)
            o = _nsa(*proj, bsz, seq, nsa_pe_k[i], nsa_pe_v[i], nsa_k_w1[i],
                     nsa_k_b1[i], nsa_k_w2[i], nsa_v_w1[i], nsa_v_b1[i], nsa_v_w2[i])
            mix, wms = [o], [nsa_w_out[i].astype(BF16)]
        h = _mix_ffn(h, mix, wms, seq, norm_ffn[layer], ffn_w_in[layer], ffn_conv_w[layer], ffn_conv_b[layer],
                     ffn_w_out[layer], norm_final, final=(layer == depth - 1))
    return h.reshape(bsz, seq, d).astype(x.dtype)
```

```python
import functools
import math

import numpy as np
import jax
import jax.numpy as jnp
from jax import lax
from jax.experimental import pallas as pl
from jax.experimental.pallas import tpu as pltpu

F32 = jnp.float32
BF16 = jnp.bfloat16
HI = lax.Precision.HIGHEST

D_MODEL = 1024
S5_WIDTH = 256
S5_GROUP = 16
S5_GROUPS = 16
S5_STATE = 64
S5_CHUNK = 16
GDN_HEAD_DIM = 128
GDN_HEADS = 6
GDN_WIDTH = 768
GDN_CONV = 4
GDN_CHUNK = 64
GDN_HEADS_PER_STEP = 3
GDN_INV_BLOCK = 8
GDN_CHUNKS_PER_ITER = 8
NSA_HEADS = 16
NSA_HEAD_DIM = 64
NSA_KV_GROUPS = 4
NSA_GROUP_SIZE = 4
NSA_KV_WIDTH = 256
CMP_LEN = 32
CMP_STRIDE = 16
CMP_HIDDEN = 256
SLC_BLOCK = 64
N_SELECT = 4
WINDOW = 256
NSA_Q_BLOCK = 64
NSA_SLAB_KEYS = 512
NSA_QB_PER_ITER = 2
LOG2E = math.log2(math.e)
FFN_HIDDEN = 2816
FFN_CONV = 3
FFN_COL_CHUNK = 256
RMS_EPS = 1e-6
NEG_INF = -1e30
LANES = 128
BF16_SUBLANES = 16
VMEM_LIMIT = 56 * 1024 * 1024
ROW_TILE = 512


def _cparams(n_axes):
    return pltpu.CompilerParams(dimension_semantics=("arbitrary",) * n_axes,
                                vmem_limit_bytes=VMEM_LIMIT)


def _rms(x, g):
    return x * lax.rsqrt(jnp.mean(x * x, axis=-1, keepdims=True) + RMS_EPS) * g


def _gelu(x):
    return 0.5 * x * (1.0 + jnp.tanh(math.sqrt(2.0 / math.pi) * (x + 0.044715 * (x * x * x))))


def _sigmoid(x):
    return 1.0 / (1.0 + jnp.exp(-x))


def _silu(x):
    return x * _sigmoid(x)


def _dot(a, b):
    return jnp.dot(a, b, preferred_element_type=F32)


def _dot_nt(a, b):
    return lax.dot_general(a, b, (((1,), (1,)), ((), ())), preferred_element_type=F32)


def _dot_tn(a, b):
    return lax.dot_general(a, b, (((0,), (0,)), ((), ())), preferred_element_type=F32)


def _const_spec(shape):
    nd = len(shape)
    return pl.BlockSpec(shape, lambda *_: (0,) * nd)


def _ab_proj_kernel(x_ref, g_ref, wu_ref, wq_ref, wk_ref, wv_ref, wz_ref, wba_ref, cw_ref,
                    u_ref, q_ref, k_ref, v_ref, z_ref, ba_ref, ext_ref, *, tiles_per_seq):
    rows = x_ref.shape[0]
    pad = 8
    dh = GDN_HEAD_DIM
    xn = _rms(x_ref[...], g_ref[...]).astype(BF16)
    seq_start = pl.program_id(0) % tiles_per_seq == 0

    @pl.when(pl.program_id(0) == 0)
    def _():
        ext_ref[...] = jnp.zeros_like(ext_ref)

    def finish(idx, pre, o_ref):
        prev = jnp.where(seq_start, 0.0, ext_ref[idx, rows:rows + pad, :])
        ext_ref[idx, 0:pad, :] = prev
        ext_ref[idx, pad:, :] = pre
        cw = cw_ref[idx]
        y = pre * cw[GDN_CONV - 1:GDN_CONV]
        for sh in range(1, GDN_CONV):
            y = y + ext_ref[idx, pad - sh:pad - sh + rows, :] * cw[GDN_CONV - 1 - sh:GDN_CONV - sh]
        y = _silu(y)
        if idx < 2:
            parts = []
            for j in range(GDN_HEADS):
                p = y[:, j * dh:(j + 1) * dh]
                p = p * lax.rsqrt(jnp.sum(p * p, axis=-1, keepdims=True) + 1e-6)
                parts.append(p * (dh ** -0.5) if idx == 0 else p)
            y = jnp.concatenate(parts, axis=1)
        o_ref[...] = y

    pre_q = _dot(xn, wq_ref[...])
    pre_k = _dot(xn, wk_ref[...])
    finish(0, pre_q, q_ref)
    pre_v = _dot(xn, wv_ref[...])
    finish(1, pre_k, k_ref)
    z_ref[...] = _dot(xn, wz_ref[...])
    finish(2, pre_v, v_ref)
    u_ref[...] = _dot(xn, wu_ref[...])
    ba_ref[...] = _dot(xn, wba_ref[...])


def _ab_proj(h, gain, w_in, conv_w, seq):
    t, d = h.shape
    w = w_in.astype(BF16)
    cuts = np.cumsum([0, S5_WIDTH, GDN_WIDTH, GDN_WIDTH, GDN_WIDTH, GDN_WIDTH])
    ws = [w[:, cuts[j]:cuts[j + 1]] for j in range(5)] + [_pad_cols(w[:, cuts[5]:], LANES)]
    cw = conv_w.astype(F32).reshape(GDN_CONV, 3, GDN_WIDTH).transpose(1, 0, 2)
    row_spec = lambda n: pl.BlockSpec((ROW_TILE, n), lambda i: (i, 0))
    return pl.pallas_call(
        functools.partial(_ab_proj_kernel, tiles_per_seq=seq // ROW_TILE),
        out_shape=[jax.ShapeDtypeStruct((t, x.shape[1]), F32) for x in ws],
        grid=(t // ROW_TILE,),
        in_specs=[row_spec(d), _const_spec((1, d))] + [_const_spec(x.shape) for x in ws] + [_const_spec(cw.shape)],
        out_specs=[row_spec(x.shape[1]) for x in ws],
        scratch_shapes=[pltpu.VMEM((3, ROW_TILE + 8, GDN_WIDTH), F32)],
        compiler_params=_cparams(1),
        name="ab_proj",
    )(h, gain.reshape(1, d), *ws, cw)


def _ffn_kernel(h_ref, hp_ref, *refs, n_mix, tiles_per_seq, n_chunks, final):
    y_refs, yp_refs, wm_refs = refs[:n_mix], refs[n_mix:2 * n_mix], refs[2 * n_mix:3 * n_mix]
    (g_ref, win_ref, cw_ref, cb_ref, wout_ref, gf_ref, o_ref,
     xn_ref, up0_ref, up1_ref, act_ref, x1_ref) = refs[3 * n_mix:]
    halo = BF16_SUBLANES
    rows = h_ref.shape[0]
    sub = 128
    fcw = act_ref.shape[2]
    g = g_ref[...]
    n_piece = rows // sub
    x1, x1p = h_ref[...], hp_ref[...]
    for y_ref, yp_ref, wm_ref in zip(y_refs, yp_refs, wm_refs):
        x1 = x1 + _dot(y_ref[...], wm_ref[...])
        x1p = x1p + _dot(yp_ref[...], wm_ref[...])
    x1_ref[...] = x1
    seq_start = pl.program_id(0) % tiles_per_seq == 0
    xn_ref[0:halo, :] = jnp.where(seq_start, 0.0, _rms(x1p, g)).astype(BF16)
    xn_ref[halo:, :] = _rms(x1, g).astype(BF16)

    def gate_piece(j, buf, r):
        r0 = r * sub

        def conv(half, cw, cb):
            y = cb
            for k in range(FFN_CONV):
                start = halo + r0 - (FFN_CONV - 1) + k
                y = y + buf[half, start:start + sub, :] * cw[k:k + 1]
            return y
        a = conv(0, cw_ref[j], cb_ref[j])
        b = conv(1, cw_ref[n_chunks + j], cb_ref[n_chunks + j])
        act_ref[j, r0:r0 + sub, :] = _silu(a.astype(BF16)) * b.astype(BF16)

    def up_and_gate(j_up, buf_up, j_gate, buf_gate):
        xe = xn_ref[...]
        for half in range(2):
            col = pl.multiple_of((half * n_chunks + j_up) * fcw, fcw)
            buf_up[half] = _dot(xe, win_ref[:, pl.ds(col, fcw)])
            if j_gate is not None:
                for r in range(half * n_piece // 2, (half + 1) * n_piece // 2):
                    gate_piece(j_gate, buf_gate, r)

    def body(i, carry):
        up_and_gate(2 * i + 1, up1_ref, 2 * i, up0_ref)
        up_and_gate(2 * i + 2, up0_ref, 2 * i + 1, up1_ref)
        return carry

    up_and_gate(0, up0_ref, None, None)
    lax.fori_loop(0, (n_chunks - 1) // 2, body, 0)
    for r in range(n_piece):
        gate_piece(n_chunks - 1, up0_ref, r)

    out = x1_ref[...]
    for j in range(n_chunks):
        out = out + _dot(act_ref[j], wout_ref[j])
    if final:
        out = _rms(out, gf_ref[...])
    o_ref[...] = out


def _mix_ffn(h, ys, wms, seq, gain, w_in, conv_w, conv_b, w_out, gain_final, final):
    t, d = h.shape
    n_mix = len(ys)
    fc = FFN_COL_CHUNK
    n_chunks = FFN_HIDDEN // fc
    assert n_chunks % 2 == 1
    halo = BF16_SUBLANES
    win = w_in.astype(BF16)
    cw = conv_w.astype(F32).reshape(FFN_CONV, 2 * n_chunks, fc).transpose(1, 0, 2)
    cb = conv_b.astype(F32).reshape(2 * n_chunks, 1, fc)
    wout = w_out.astype(BF16).reshape(n_chunks, fc, d)
    blocks_per_tile = ROW_TILE // halo
    tile_spec = lambda n: pl.BlockSpec((ROW_TILE, n), lambda i: (i, 0))
    prev_spec = lambda n: pl.BlockSpec((halo, n), lambda i: (jnp.maximum(i * blocks_per_tile - 1, 0), 0))
    return pl.pallas_call(
        functools.partial(_ffn_kernel, n_mix=n_mix, tiles_per_seq=seq // ROW_TILE, n_chunks=n_chunks, final=final),
        out_shape=jax.ShapeDtypeStruct((t, d), F32),
        grid=(t // ROW_TILE,),
        in_specs=[tile_spec(d), prev_spec(d)]
        + [tile_spec(y.shape[1]) for y in ys] + [prev_spec(y.shape[1]) for y in ys]
        + [_const_spec(w.shape) for w in wms]
        + [_const_spec((1, d)),
                  pl.BlockSpec(win.shape, lambda i: (0, 0), pipeline_mode=pl.Buffered(1)),
                  _const_spec(cw.shape), _const_spec(cb.shape),
                  pl.BlockSpec(wout.shape, lambda i: (0, 0, 0), pipeline_mode=pl.Buffered(1)),
                  _const_spec((1, d))],
        out_specs=pl.BlockSpec((ROW_TILE, d), lambda i: (i, 0)),
        scratch_shapes=[pltpu.VMEM((ROW_TILE + halo, d), BF16), pltpu.VMEM((2, ROW_TILE + halo, fc), F32),
                        pltpu.VMEM((2, ROW_TILE + halo, fc), F32), pltpu.VMEM((n_chunks, ROW_TILE, fc), BF16),
                        pltpu.VMEM((ROW_TILE, d), F32)],
        compiler_params=_cparams(1),
        name="mix_ffn",
    )(h, h, *ys, *ys, *wms, gain.reshape(1, d), win, cw, cb, wout, gain_final.reshape(1, d))


def _s5_prep(lam_re, lam_im, log_step, b_re, b_im, c_re, c_im, d_skip):
    g_n, p_n, h_n, L = S5_GROUPS, S5_STATE, S5_GROUP, S5_CHUNK
    step = jnp.exp(log_step.astype(F32))[:, None]
    lr, li = lam_re.astype(F32), lam_im.astype(F32)
    mag = jnp.exp(lr * step)
    a_re = mag * jnp.cos(li * step)
    a_im = mag * jnp.sin(li * step)
    den = lr * lr + li * li
    n_re, n_im = a_re - 1.0, a_im
    z_re = (n_re * lr + n_im * li) / den
    z_im = (n_im * lr - n_re * li) / den
    b_re, b_im = b_re.astype(F32), b_im.astype(F32)
    bb_re = z_re[..., None] * b_re - z_im[..., None] * b_im
    bb_im = z_re[..., None] * b_im + z_im[..., None] * b_re
    c_re, c_im = c_re.astype(F32), c_im.astype(F32)
    pw_re, pw_im = [jnp.ones_like(a_re)], [jnp.zeros_like(a_im)]
    for _ in range(L):
        pr, pi = pw_re[-1], pw_im[-1]
        pw_re.append(pr * a_re - pi * a_im)
        pw_im.append(pr * a_im + pi * a_re)
    eye_g = jnp.eye(g_n, dtype=F32)
    ks = []
    for j in range(L):
        ab_re = pw_re[j][..., None] * bb_re - pw_im[j][..., None] * bb_im
        ab_im = pw_re[j][..., None] * bb_im + pw_im[j][..., None] * bb_re
        kj = (jnp.einsum('gop,gpi->gio', c_re, ab_re, precision=HI)
              - jnp.einsum('gop,gpi->gio', c_im, ab_im, precision=HI))
        ks.append(jnp.einsum('gio,gk->giko', kj, eye_g).reshape(g_n * h_n, g_n * h_n))
    kstack = jnp.concatenate(ks[::-1], axis=0).astype(BF16)
    bb = jnp.stack([jnp.einsum('gph,gk->ghkp', bb_re, eye_g), jnp.einsum('gph,gk->ghkp', bb_im, eye_g)], axis=2)
    bb = bb.reshape(g_n * h_n, 2 * g_n * p_n).astype(BF16)
    cc = jnp.stack([jnp.einsum('ghp,gk->gpkh', c_re, eye_g), -jnp.einsum('ghp,gk->gpkh', c_im, eye_g)], axis=0)
    cc = cc.reshape(2 * g_n * p_n, g_n * h_n).astype(BF16)
    apow = jnp.stack([jnp.concatenate([r.reshape(-1), i.reshape(-1)]) for r, i in zip(pw_re, pw_im)], axis=0)
    dvec = jnp.tile(d_skip.astype(F32).reshape(1, g_n * h_n), (1, L))
    return kstack, bb, cc, apow, dvec


def _s5_kernel(u_ref, ks_ref, bb_ref, cc_ref, ap_ref, dv_ref, wg_ref, bg_ref, o_ref, x_scr, hp_scr,
               *, n_batch, n_chunk):
    L, w = S5_CHUNK, S5_WIDTH
    half = S5_GROUPS * S5_STATE
    rows = n_batch * n_chunk
    u2 = u_ref[...].reshape(rows, L * w)
    u2b = u2.astype(BF16)

    def cmul(j, x):
        ar, ai = ap_ref[j:j + 1, :half], ap_ref[j:j + 1, half:]
        xr, xi = x[:, :half], x[:, half:]
        return jnp.concatenate([ar * xr - ai * xi, ar * xi + ai * xr], axis=1)

    xin = _dot(u2b[:, (L - 1) * w:], bb_ref[...])
    for s in range(L - 1):
        xin = xin + cmul(L - 1 - s, _dot(u2b[:, s * w:(s + 1) * w], bb_ref[...]))
    x_scr[...] = xin

    a_l = ap_ref[L:L + 1, :]
    alr, ali = a_l[:, :half], a_l[:, half:]

    def scan_body(k, hs):
        new = []
        for b in range(n_batch):
            r = b * n_chunk + k
            h = hs[b]
            hp_scr[pl.ds(r, 1), :] = h
            hr, hi = h[:, :half], h[:, half:]
            new.append(jnp.concatenate([alr * hr - ali * hi, alr * hi + ali * hr], axis=1)
                       + x_scr[pl.ds(r, 1), :])
        return tuple(new)

    lax.fori_loop(0, n_chunk, scan_body, tuple(jnp.zeros((1, 2 * half), F32) for _ in range(n_batch)))
    hp = hp_scr[...]

    for t in range(L):
        y = _dot(u2b[:, :(t + 1) * w], ks_ref[(L - 1 - t) * w:, :])
        y = y + _dot(cmul(t + 1, hp).astype(BF16), cc_ref[...])
        y = y + dv_ref[:, t * w:(t + 1) * w] * u2[:, t * w:(t + 1) * w]
        z = _gelu(y)
        gate = _sigmoid(_dot(z.astype(BF16), wg_ref[...]) + bg_ref[...])
        o_ref[:, :, t * w:(t + 1) * w] = (z * gate).astype(o_ref.dtype).reshape(n_batch, n_chunk, w)


def _s5(u, seq, prep, w_glu, b_glu):
    t = u.shape[0]
    bsz = t // seq
    L, w = S5_CHUNK, S5_WIDTH
    n_chunk = seq // L
    n_batch = 2
    kstack, bb, cc, apow, dvec = prep
    u3 = u.reshape(bsz, n_chunk, L * w)
    out = pl.pallas_call(
        functools.partial(_s5_kernel, n_batch=n_batch, n_chunk=n_chunk),
        out_shape=jax.ShapeDtypeStruct((bsz, n_chunk, L * w), BF16),
        grid=(bsz // n_batch,),
        in_specs=[pl.BlockSpec((n_batch, n_chunk, L * w), lambda i: (i, 0, 0)),
                  _const_spec(kstack.shape), _const_spec(bb.shape), _const_spec(cc.shape),
                  _const_spec(apow.shape), _const_spec(dvec.shape),
                  _const_spec((w, w)), _const_spec((1, w))],
        out_specs=pl.BlockSpec((n_batch, n_chunk, L * w), lambda i: (i, 0, 0)),
        scratch_shapes=[pltpu.VMEM((n_batch * n_chunk, 2 * S5_GROUPS * S5_STATE), F32),
                        pltpu.VMEM((n_batch * n_chunk, 2 * S5_GROUPS * S5_STATE), F32)],
        compiler_params=_cparams(1),
        name="s5_mixer",
    )(u3, kstack, bb, cc, apow, dvec, w_glu.astype(BF16), b_glu.astype(F32).reshape(1, w))
    return out.reshape(t, w)


def _gdn_kernel(qs, ks, vs, z_ref, bc_ref, ac_ref, ar_ref, hp_ref, hr_ref, nw_ref,
                o_ref, xs, ns, qe, o0, gls, gcc_s, gcr_s, xs2, ns2, gls2, *, seq, n_head):
    c, dh = GDN_CHUNK, GDN_HEAD_DIM
    n_chunk = seq // c

    hp = hp_ref[0]
    hr = hr_ref[0]

    def softplus(x):
        return jnp.maximum(x, 0.0) + jnp.log(1.0 + jnp.exp(-jnp.abs(x)))

    ii = lax.broadcasted_iota(jnp.int32, (c, c), 0)
    jj = lax.broadcasted_iota(jnp.int32, (c, c), 1)
    eye = (ii == jj).astype(F32)
    same_block = (ii // GDN_INV_BLOCK) == (jj // GDN_INV_BLOCK)
    g_col = -jnp.exp(hp[0:1, :]) * softplus(ac_ref[0, 0] + hp[1:2, :])
    g_row = -jnp.exp(hr[:, 0:1]) * softplus(ar_ref[0, 0] + hr[:, 1:2])
    gc_col = jnp.dot((ii >= jj).astype(F32), g_col, precision=HI, preferred_element_type=F32)
    gcr_s[...] = jnp.dot(g_row, (ii <= jj).astype(F32), precision=HI, preferred_element_type=F32)
    for ci in range(n_chunk):
        gcc_s[ci * c:(ci + 1) * c, :] = gc_col[:, ci * n_head:(ci + 1) * n_head]

    def local_body(it, carry):
        pr = []
        for cc in range(GDN_CHUNKS_PER_ITER):
            ci = it * GDN_CHUNKS_PER_ITER + cc
            rs = pl.ds(pl.multiple_of(ci * c, c), c)
            beta = _sigmoid(bc_ref[0, 0, rs, :])
            gccs = gcc_s[rs, :]
            for j in range(n_head):
                sl = slice(j * dh, (j + 1) * dh)
                gcc = gccs[:, j:j + 1]
                gcr = gcr_s[pl.ds(ci * n_head + j, 1), :]
                q, k, v = qs[rs, sl], ks[rs, sl], vs[rs, sl]
                bj = beta[:, j:j + 1]
                kb = k * bj
                e_g = jnp.exp(gcc)
                g_last = gcc[c - 1:c, :]
                pr.append(dict(ci=ci, j=j, q=q, kf=k.astype(BF16), kbb=kb.astype(BF16),
                               decay=jnp.exp(jnp.where(ii >= jj, gcc - gcr, NEG_INF)),
                               rhs=jnp.concatenate([v * bj, kb * e_g], axis=1).astype(BF16),
                               qd=q * e_g, kd=(k * jnp.exp(g_last - gcc)).astype(BF16),
                               gl=jnp.broadcast_to(jnp.exp(g_last), (8, dh))))
        kk = [_dot_nt(p["kbb"], p["kf"]) for p in pr]
        qk = [_dot_nt(p["q"].astype(BF16), p["kf"]) for p in pr]
        ps = [-jnp.where(ii > jj, a * p["decay"], 0.0) for a, p in zip(kk, pr)]
        ds = [jnp.where(same_block, n, 0.0) for n in ps]
        offs = [(n - d).astype(BF16) for n, d in zip(ps, ds)]
        ts = [eye + d for d in ds]
        reach = 2
        while reach < GDN_INV_BLOCK:
            ds = [_dot(d.astype(BF16), d.astype(BF16)) for d in ds]
            ts = [t + _dot(t.astype(BF16), d.astype(BF16)) for t, d in zip(ts, ds)]
            reach *= 2
        ms = [_dot(t.astype(BF16), o) for t, o in zip(ts, offs)]
        ts = [t + _dot(m.astype(BF16), t.astype(BF16)) for t, m in zip(ts, ms)]
        reach = 2
        while reach < c // GDN_INV_BLOCK:
            ms = [_dot(m.astype(BF16), m.astype(BF16)) for m in ms]
            ts = [t + _dot(m.astype(BF16), t.astype(BF16)) for t, m in zip(ts, ms)]
            reach *= 2
        uws = [_dot(t.astype(BF16), p["rhs"]).astype(BF16) for t, p in zip(ts, pr)]
        intras = [(s * p["decay"]).astype(BF16) for s, p in zip(qk, pr)]
        i_uws = [_dot(a, uw) for a, uw in zip(intras, uws)]
        kd_uws = [_dot_tn(p["kd"], uw) for p, uw in zip(pr, uws)]
        for p, i_uw, kd_uw in zip(pr, i_uws, kd_uws):
            j, ci = p["j"], p["ci"]
            ns[j, ci] = kd_uw[:, :dh]
            xs[j, ci] = kd_uw[:, dh:].astype(BF16)
            o0[j, ci] = i_uw[:, :dh]
            qe[j, ci] = (p["qd"] - i_uw[:, dh:]).astype(BF16)
            gls[j, ci] = p["gl"]
        first = [idx for idx, p in enumerate(pr) if (idx // n_head) % 2 == 0]
        x1b = [kd_uws[idx + n_head][:, dh:].astype(BF16) for idx in first]
        x1x0 = [_dot(x1, kd_uws[idx][:, dh:].astype(BF16)) for x1, idx in zip(x1b, first)]
        x1n0 = [_dot(x1, kd_uws[idx][:, :dh].astype(BF16)) for x1, idx in zip(x1b, first)]
        for idx, xx, xn in zip(first, x1x0, x1n0):
            p0, p1 = pr[idx], pr[idx + n_head]
            g0, g1 = p0["gl"][0:1, :], p1["gl"][0:1, :]
            pi = it * (GDN_CHUNKS_PER_ITER // 2) + (idx // n_head) // 2
            xs2[p0["j"], pi] = (g1 * kd_uws[idx][:, dh:] + g0 * kd_uws[idx + n_head][:, dh:] - xx).astype(BF16)
            ns2[p0["j"], pi] = g1 * kd_uws[idx][:, :dh] - xn + kd_uws[idx + n_head][:, :dh]
            gls2[p0["j"], pi] = p0["gl"] * p1["gl"]
        return carry

    lax.fori_loop(0, n_chunk // GDN_CHUNKS_PER_ITER, local_body, 0)

    def rec_body(pi, states):
        c0, c1 = 2 * pi, 2 * pi + 1
        s_bs = [s.astype(BF16) for s in states]
        xd2 = [_dot(xs2[j, pi], s_bs[j]) for j in range(n_head)]
        xd0 = [_dot(xs[j, c0], s_bs[j]) for j in range(n_head)]
        od0 = [_dot(qe[j, c0], s_bs[j]) for j in range(n_head)]
        new = tuple(gls2[j, pi][0:1, :] * states[j] - xd2[j] + ns2[j, pi] for j in range(n_head))
        mid = [(gls[j, c0][0:1, :] * states[j] - xd0[j] + ns[j, c0]).astype(BF16) for j in range(n_head)]
        od1 = [_dot(qe[j, c1], mid[j]) for j in range(n_head)]
        for ci, od in ((c0, od0), (c1, od1)):
            rs = pl.ds(pl.multiple_of(ci * c, c), c)
            for j in range(n_head):
                sl = slice(j * dh, (j + 1) * dh)
                o = od[j] + o0[j, ci]
                on = o * lax.rsqrt(jnp.mean(o * o, axis=-1, keepdims=True) + RMS_EPS) * nw_ref[...]
                o_ref[rs, sl] = (on * _silu(z_ref[rs, sl])).astype(o_ref.dtype)
        return new

    lax.fori_loop(0, n_chunk // 2, rec_body, tuple(jnp.zeros((dh, dh), F32) for _ in range(n_head)), unroll=2)


def _gdn(q, k, v, z, ba, seq, a_log, dt_bias, norm_w):
    t = q.shape[0]
    bsz = t // seq
    nh = GDN_HEADS_PER_STEP
    n_grp = GDN_HEADS // nh
    wd = nh * GDN_HEAD_DIM
    bl = ba[:, :GDN_HEADS].reshape(bsz, seq, n_grp, nh)
    al = ba[:, GDN_HEADS:2 * GDN_HEADS].reshape(bsz, seq, n_grp, nh)
    b_col = bl.transpose(0, 2, 1, 3)
    n_chunk = seq // GDN_CHUNK
    n_prob = n_chunk * nh
    a5 = al.reshape(bsz, n_chunk, GDN_CHUNK, n_grp, nh)
    a_col = a5.transpose(0, 3, 2, 1, 4).reshape(bsz, n_grp, GDN_CHUNK, n_prob)
    a_row = a5.transpose(0, 3, 1, 4, 2).reshape(bsz, n_grp, n_prob, GDN_CHUNK)
    hp = jnp.stack([jnp.tile(a_log.astype(F32).reshape(n_grp, nh), (1, n_chunk)),
                    jnp.tile(dt_bias.astype(F32).reshape(n_grp, nh), (1, n_chunk))], axis=1)
    hr = hp.transpose(0, 2, 1)
    act_spec = pl.BlockSpec((seq, wd), lambda b, g: (b, g))
    col_spec = pl.BlockSpec((1, 1, seq, nh), lambda b, g: (b, g, 0, 0))
    return pl.pallas_call(
        functools.partial(_gdn_kernel, seq=seq, n_head=nh),
        out_shape=jax.ShapeDtypeStruct((t, GDN_WIDTH), BF16),
        grid=(bsz, n_grp),
        in_specs=[act_spec, act_spec, act_spec, act_spec, col_spec,
                  pl.BlockSpec((1, 1, GDN_CHUNK, n_prob), lambda b, g: (b, g, 0, 0)),
                  pl.BlockSpec((1, 1, n_prob, GDN_CHUNK), lambda b, g: (b, g, 0, 0)),
                  pl.BlockSpec((1, 2, n_prob), lambda b, g: (g, 0, 0)),
                  pl.BlockSpec((1, n_prob, 2), lambda b, g: (g, 0, 0)),
                  _const_spec((1, GDN_HEAD_DIM))],
        out_specs=act_spec,
        scratch_shapes=[pltpu.VMEM((nh, n_chunk, GDN_HEAD_DIM, GDN_HEAD_DIM), BF16),
                        pltpu.VMEM((nh, n_chunk, GDN_HEAD_DIM, GDN_HEAD_DIM), F32),
                        pltpu.VMEM((nh, n_chunk, GDN_CHUNK, GDN_HEAD_DIM), BF16),
                        pltpu.VMEM((nh, n_chunk, GDN_CHUNK, GDN_HEAD_DIM), F32),
                        pltpu.VMEM((nh, n_chunk, 8, GDN_HEAD_DIM), F32),
                        pltpu.VMEM((seq, nh), F32), pltpu.VMEM((n_prob, GDN_CHUNK), F32),
                        pltpu.VMEM((nh, n_chunk // 2, GDN_HEAD_DIM, GDN_HEAD_DIM), BF16),
                        pltpu.VMEM((nh, n_chunk // 2, GDN_HEAD_DIM, GDN_HEAD_DIM), F32),
                        pltpu.VMEM((nh, n_chunk // 2, 8, GDN_HEAD_DIM), F32)],
        compiler_params=_cparams(2),
        name="gated_deltanet",
    )(q, k, v, z, b_col, a_col, a_row, hp, hr, norm_w.astype(F32).reshape(1, GDN_HEAD_DIM))


NSA_FEAT = 64
F_BLK, F_POS, F_ONE = 0, 32, 38


def _split3(x):
    x1 = x.astype(BF16).astype(F32)
    x2 = (x - x1).astype(BF16).astype(F32)
    return x1, x2, x - x1 - x2


def _nsa_proj_kernel(x_ref, g_ref, wq_ref, wks_ref, wkw_ref, wvs_ref, wvw_ref, wkc_ref, wvc_ref, wg_ref, kf_ref,
                     perm_ref, kw1_ref, vw1_ref,
                     q_ref, ksa_ref, kwa_ref, vst_ref, vwt_ref, kct_ref, kcb_ref, vct_ref, vcb_ref, gl_ref):
    g_n, dh = NSA_KV_GROUPS, NSA_HEAD_DIM
    n_row = x_ref.shape[0] // CMP_STRIDE
    half = CMP_STRIDE * dh

    def compress_halves(c, w1_ref, top_ref, bot_ref):
        rows = _dot(perm_ref[...], c.astype(BF16)).astype(BF16)
        a = jnp.concatenate(
            [jnp.concatenate([rows[l * n_row:(l + 1) * n_row, g * dh:(g + 1) * dh] for l in range(CMP_STRIDE)], axis=1)
             for g in range(g_n)], axis=0)
        top, bot = _dot(a, w1_ref[:half, :]), _dot(a, w1_ref[half:, :])
        for g in range(g_n):
            top_ref[0, g] = top[g * n_row:(g + 1) * n_row]
            bot_ref[0, g] = bot[g * n_row:(g + 1) * n_row]

    xn = _rms(x_ref[...], g_ref[...]).astype(BF16)
    q_ref[...] = (_dot(xn, wq_ref[...]) * (NSA_HEAD_DIM ** -0.5 * LOG2E)).astype(BF16)
    kf = jnp.concatenate([kf_ref[...]] * NSA_KV_GROUPS, axis=1)
    ksa_ref[...] = (_dot(xn, wks_ref[...]) + kf).astype(BF16)
    kwa_ref[...] = (_dot(xn, wkw_ref[...]) + kf).astype(BF16)
    vst_ref[0] = _dot_nt(wvs_ref[...], xn).astype(BF16)
    vwt_ref[0] = _dot_nt(wvw_ref[...], xn).astype(BF16)
    compress_halves(_dot(xn, wkc_ref[...]), kw1_ref, kct_ref, kcb_ref)
    compress_halves(_dot(xn, wvc_ref[...]), vw1_ref, vct_ref, vcb_ref)
    gl_ref[...] = _dot(xn, wg_ref[...])


def _nsa_proj(h, gain, w_in, k_w1, v_w1, bsz, seq):
    t, d = h.shape
    g_n, dh = NSA_KV_GROUPS, NSA_HEAD_DIM
    nq, kvw = NSA_HEADS * dh, NSA_KV_WIDTH
    w = w_in.astype(BF16)
    wq = w[:, :nq]
    wkc, wvc, wks, wvs, wkw, wvw = (w[:, nq + j * kvw: nq + (j + 1) * kvw] for j in range(6))
    wgl = w[:, nq + 6 * kvw:]

    def lane_padded(x, width):
        return jnp.pad(x.reshape(d, g_n, width), ((0, 0), (0, 0), (0, LANES - width))).reshape(d, g_n * LANES)

    pos = np.arange(seq)
    kf = np.zeros((seq, LANES), np.float32)
    kf[pos, dh + F_BLK + pos // SLC_BLOCK] = 1.0
    kf[:, dh + F_POS:dh + F_POS + 3] = ((pos // SLC_BLOCK) * SLC_BLOCK)[:, None]
    kf[:, dh + F_POS + 3:dh + F_POS + 6] = (pos % SLC_BLOCK)[:, None]
    kf[:, dh + F_ONE:dh + F_ONE + 3] = 1.0
    tps = seq // ROW_TILE
    row_spec = lambda n: pl.BlockSpec((ROW_TILE, n), lambda i: (i, 0))
    tr_spec = pl.BlockSpec((1, kvw, ROW_TILE), lambda i: (i // tps, 0, i % tps))
    ws = [wq, lane_padded(wks, dh), lane_padded(wkw, dh), wvs.T, wvw.T, wkc, wvc, lane_padded(wgl, 3 * NSA_GROUP_SIZE)]
    n_row = ROW_TILE // CMP_STRIDE
    tok = np.arange(ROW_TILE)
    perm = np.zeros((ROW_TILE, ROW_TILE), np.float32)
    perm[(tok % CMP_STRIDE) * n_row + tok // CMP_STRIDE, tok] = 1.0
    cmp_consts = [jnp.asarray(perm, dtype=BF16), k_w1.astype(BF16), v_w1.astype(BF16)]
    half_spec = pl.BlockSpec((1, g_n, n_row, CMP_HIDDEN), lambda i: (i // tps, 0, i % tps, 0))
    half_shape = jax.ShapeDtypeStruct((bsz, g_n, seq // CMP_STRIDE, CMP_HIDDEN), F32)
    return pl.pallas_call(
        _nsa_proj_kernel,
        out_shape=[jax.ShapeDtypeStruct((t, nq), BF16),
                   jax.ShapeDtypeStruct((t, g_n * LANES), BF16), jax.ShapeDtypeStruct((t, g_n * LANES), BF16),
                   jax.ShapeDtypeStruct((bsz, kvw, seq), BF16), jax.ShapeDtypeStruct((bsz, kvw, seq), BF16),
                   half_shape, half_shape, half_shape, half_shape,
                   jax.ShapeDtypeStruct((t, g_n * LANES), F32)],
        grid=(t // ROW_TILE,),
        in_specs=[row_spec(d), _const_spec((1, d))] + [_const_spec(x.shape) for x in ws]
        + [pl.BlockSpec((ROW_TILE, LANES), lambda i: (i % tps, 0))] + [_const_spec(x.shape) for x in cmp_consts],
        out_specs=[row_spec(nq), row_spec(g_n * LANES), row_spec(g_n * LANES), tr_spec, tr_spec,
                   half_spec, half_spec, half_spec, half_spec, row_spec(g_n * LANES)],
        compiler_params=_cparams(1),
        name="nsa_proj",
    )(h, gain.reshape(1, d), *ws, jnp.asarray(kf), *cmp_consts)


def _nsa_kernel(q_ref, ksa_ref, kwa_ref, vst_ref, vwt_ref, kct_ref, kcb_ref, vct_ref, vcb_ref, gl_ref, sl_ref, sf_ref,
                cf_ref, kc1_ref, kw2_ref, vc1_ref, vw2t_ref, ovt_ref, o_ref,
                kcmp, vcmpt, vs3, vw3, qa_s, oc_s, *, seq):
    qb, dh, r_n, nq = NSA_Q_BLOCK, NSA_HEAD_DIM, NSA_GROUP_SIZE, NSA_QB_PER_ITER
    span = nq * qb
    cols = nq * r_n * qb
    n_cmp_pad = seq // CMP_STRIDE
    n_cmp = (seq - CMP_LEN) // CMP_STRIDE + 1
    n_blk = seq // SLC_BLOCK
    n_win = WINDOW + span
    it_per_slab = NSA_SLAB_KEYS // span
    nf = NSA_FEAT

    def hidden(top_ref, bot_ref, c1_ref):
        bot = pltpu.roll(bot_ref[0, 0], n_cmp_pad - 1, 0)
        return _gelu(top_ref[0, 0] + bot + c1_ref[...]).astype(BF16)

    kcmp[...] = jnp.concatenate([_dot(hidden(kct_ref, kcb_ref, kc1_ref), kw2_ref[...]), cf_ref[...]],
                                axis=1).astype(BF16)
    vcmpt[...] = _dot_nt(vw2t_ref[...], hidden(vct_ref, vcb_ref, vc1_ref)).astype(BF16)
    for n in range(seq // span):
        vs3[n] = vst_ref[0, :, n * span:(n + 1) * span]
        vw3[n] = vwt_ref[0, :, n * span:(n + 1) * span]

    def iota(shape, axis):
        return lax.broadcasted_iota(jnp.int32, shape, axis)

    slope = sl_ref[0]
    colc, colr = iota((cols, 1), 0), iota((1, cols), 1)
    tc_col = (colc // (r_n * qb)) * qb + colc % qb
    tc_row = (colr // (r_n * qb)) * qb + colr % qb
    qi_row = colr // (r_n * qb)
    lane_f = iota((1, nf), 1)
    jr = iota((n_cmp_pad, 1), 0)
    nr = iota((nf, 1), 0)
    lane_s = iota((1, span), 1)
    qi_s = lane_s // qb
    key_d = iota((span, 1), 0)
    key_w = iota((n_win, 1), 0)
    eye_s = (iota((span, span), 0) == iota((span, span), 1)).astype(BF16)

    def prepare(p):
        t0 = pl.multiple_of(p * span, span)
        qf = jnp.concatenate([q_ref[pl.ds(t0 + qi * qb, qb), r * dh:(r + 1) * dh]
                              for qi in range(nq) for r in range(r_n)], axis=0).astype(F32)
        c1, c2, c3 = _split3(-(slope * (t0 + tc_col).astype(F32)))
        f_plain = jnp.where(lane_f == F_ONE, c1, jnp.where(lane_f == F_ONE + 1, c2,
                            jnp.where(lane_f == F_ONE + 2, c3, sf_ref[0])))
        qa_plain = jnp.concatenate([qf, f_plain], axis=1).astype(BF16)

        ok_c = (jr * CMP_STRIDE + (CMP_LEN - 1) <= t0 + tc_row) & (jr < n_cmp)
        s_c = jnp.where(ok_c, _dot_nt(kcmp[...], qa_plain), NEG_INF)
        e_c = jnp.where(ok_c, jnp.exp2(s_c - jnp.max(s_c, axis=0, keepdims=True)), 0.0)
        l_c = jnp.sum(e_c, axis=0, keepdims=True)
        p_c = e_c / jnp.where(l_c > 0.0, l_c, 1.0)
        oc_s[...] = _dot(vcmpt[...], p_c.astype(BF16))

        imp4 = sum(_dot(ovt_ref[...], part.astype(BF16)) for part in _split3(p_c))
        halves = []
        for qi in range(nq):
            a = imp4[:, (2 * qi) * LANES:(2 * qi + 1) * LANES] + imp4[:, (2 * qi + 1) * LANES:(2 * qi + 2) * LANES]
            halves.append(a + pltpu.roll(a, qb, 1))
        low_half = lax.broadcasted_iota(jnp.int32, (1, LANES), 1) < qb
        imp = jnp.concatenate([jnp.where(low_half, halves[2 * v], halves[2 * v + 1]) for v in range(nq // 2)],
                              axis=1)

        iq = nq * p + qi_s
        cand = (nr > 0) & (nr < iq)
        sc = jnp.where(cand, imp, NEG_INF)
        sel = (nr == 0) & (iq > 0)
        for _ in range(N_SELECT - 2):
            best = jnp.max(sc, axis=0, keepdims=True)
            first = jnp.min(jnp.where(sc == best, nr, nf), axis=0, keepdims=True)
            pick = (nr == first) & cand
            sel = sel | pick
            sc = jnp.where(pick, NEG_INF, sc)
        neg_slab = jnp.where(sel & (nr < nq * p), 0.0, NEG_INF)
        neg_cur = jnp.where(nr == iq, 0.0, jnp.where((nr >= nq * p) & (nr < iq) & sel, 0.0, NEG_INF))
        neg_t = jnp.concatenate([neg_slab, neg_cur], axis=0).astype(BF16)
        neg = _dot_nt(eye_s, neg_t)
        rows_of = lambda x: jnp.concatenate([x[qi * qb:(qi + 1) * qb] for qi in range(nq) for _ in range(r_n)], axis=0)
        f_slab = jnp.where(lane_f < n_blk, rows_of(neg[:, :nf]), f_plain)
        f_cur = jnp.where(lane_f < n_blk, rows_of(neg[:, nf:]), f_plain)
        qa_s[0] = qa_plain
        qa_s[1] = jnp.concatenate([qf, f_slab], axis=1).astype(BF16)
        qa_s[2] = jnp.concatenate([qf, f_cur], axis=1).astype(BF16)

    n_iter = seq // span

    def make_body(n_keys):
        def body(p, carry):
            t0 = pl.multiple_of(p * span, span)
            w0 = pl.multiple_of(jnp.maximum(t0 - WINDOW, 0), span)
            oc_t = oc_s[...]
            s_cu = _dot_nt(ksa_ref[pl.ds(t0, span), :], qa_s[2])
            s_w = _dot_nt(kwa_ref[pl.ds(w0, n_win), :], qa_s[0])
            s_sl = _dot_nt(ksa_ref[0:n_keys, :], qa_s[1])
            prepare(jnp.minimum(p + 1, n_iter - 1))

            own = (key_d // qb) == qi_row
            s_cu = jnp.where(own & (key_d % qb > tc_row % qb), NEG_INF, s_cu)
            m_s = jnp.maximum(jnp.max(s_sl, axis=0, keepdims=True), jnp.max(s_cu, axis=0, keepdims=True))
            e_sl = jnp.exp2(s_sl - m_s)
            e_cu = jnp.exp2(s_cu - m_s)
            l_s = jnp.sum(e_sl, axis=0, keepdims=True) + jnp.sum(e_cu, axis=0, keepdims=True)
            os_t = (_dot(vst_ref[0, :, 0:n_keys], e_sl.astype(BF16)) + _dot(vs3[p], e_cu.astype(BF16))) / l_s

            rel = (t0 - w0) + tc_row - key_w
            s_w = jnp.where((rel >= 0) & (rel < WINDOW), s_w, NEG_INF)
            e_w = jnp.exp2(s_w - jnp.max(s_w, axis=0, keepdims=True))
            e_wb = e_w.astype(BF16)
            b0 = w0 // span
            ow_t = _dot(vw3[b0], e_wb[0:span])
            for j in range(1, n_win // span):
                ow_t = ow_t + _dot(vw3[b0 + j], e_wb[j * span:(j + 1) * span])
            ow_t = ow_t / jnp.sum(e_w, axis=0, keepdims=True)

            o_all = jnp.concatenate([os_t, ow_t, oc_t, jnp.zeros_like(oc_t)], axis=0).T
            gate = _sigmoid(gl_ref[pl.ds(t0, span), 0:3 * r_n])
            for qi in range(nq):
                g_q = gate[qi * qb:(qi + 1) * qb]
                heads = []
                for r in range(r_n):
                    rs = slice((qi * r_n + r) * qb, (qi * r_n + r + 1) * qb)
                    heads.append(g_q[:, 3 * r:3 * r + 1] * o_all[rs, 2 * dh:3 * dh]
                                 + g_q[:, 3 * r + 1:3 * r + 2] * o_all[rs, :dh]
                                 + g_q[:, 3 * r + 2:3 * r + 3] * o_all[rs, dh:2 * dh])
                o_ref[pl.ds(t0 + qi * qb, qb), :] = jnp.concatenate(heads, axis=1).astype(o_ref.dtype)
            return carry
        return body

    prepare(jnp.int32(0))
    for sb in range(seq // NSA_SLAB_KEYS):
        lax.fori_loop(sb * it_per_slab, (sb + 1) * it_per_slab, make_body((sb + 1) * NSA_SLAB_KEYS), 0)


def _nsa(q, ksa, kwa, vst, vwt, kct, kcb, vct, vcb, gl, bsz, seq, pe_k, pe_v, k_w1, k_b1, k_w2, v_w1, v_b1, v_w2):
    g_n, r_n, dh, qb = NSA_KV_GROUPS, NSA_GROUP_SIZE, NSA_HEAD_DIM, NSA_Q_BLOCK
    t = bsz * seq
    n_cmp_pad = seq // CMP_STRIDE
    n_blk = seq // SLC_BLOCK
    span = NSA_QB_PER_ITER * qb
    assert n_blk <= F_POS and seq % NSA_SLAB_KEYS == 0 and seq >= WINDOW + span

    slopes = np.asarray([2.0 ** (-8.0 * (h + 1) / NSA_HEADS) for h in range(NSA_HEADS)], dtype=np.float32)
    cols = NSA_QB_PER_ITER * r_n * qb
    slope_rows = jnp.asarray(np.tile(np.repeat(slopes.reshape(g_n, r_n), qb, axis=1), (1, NSA_QB_PER_ITER))
                             .reshape(g_n, cols, 1)) * LOG2E
    s3 = jnp.concatenate(_split3(slope_rows) * 2, axis=2)
    sfeat = jnp.pad(s3, ((0, 0), (0, 0), (F_POS, NSA_FEAT - F_POS - 6)))
    cf = np.zeros((n_cmp_pad, NSA_FEAT), np.float32)
    cf[:, F_POS:F_POS + 3] = (np.arange(n_cmp_pad) * CMP_STRIDE)[:, None]
    cf[:, F_POS + 3:F_POS + 6] = (CMP_LEN - 1) * 0.5
    cf[:, F_ONE:F_ONE + 3] = 1.0
    tok = np.arange(n_cmp_pad)[:, None] * CMP_STRIDE + np.arange(CMP_LEN)[None, :]
    overlap_t = ((tok // SLC_BLOCK)[:, :, None] == np.arange(NSA_FEAT)[None, None, :]).mean(axis=1).astype(np.float32).T
    def pos_const(pe, w1, b1):
        return _dot(pe.astype(BF16).reshape(1, -1), w1.astype(BF16)) + b1.astype(F32).reshape(1, -1)

    cmp_spec = pl.BlockSpec((1, 1, n_cmp_pad, CMP_HIDDEN), lambda b, g: (b, g, 0, 0))
    q_spec = pl.BlockSpec((seq, r_n * dh), lambda b, g: (b, g))
    k_spec = pl.BlockSpec((seq, LANES), lambda b, g: (b, g))
    v_spec = pl.BlockSpec((1, dh, seq), lambda b, g: (b, g, 0))
    w1s, b1s, w2s = (CMP_LEN * dh, CMP_HIDDEN), (1, CMP_HIDDEN), (CMP_HIDDEN, dh)
    return pl.pallas_call(
        functools.partial(_nsa_kernel, seq=seq),
        out_shape=jax.ShapeDtypeStruct((t, NSA_HEADS * dh), BF16),
        grid=(bsz, g_n),
        in_specs=[q_spec, k_spec, k_spec, v_spec, v_spec, cmp_spec, cmp_spec, cmp_spec, cmp_spec, k_spec,
                  pl.BlockSpec((1, cols, 1), lambda b, g: (g, 0, 0)),
                  pl.BlockSpec((1, cols, NSA_FEAT), lambda b, g: (g, 0, 0)),
                  _const_spec((n_cmp_pad, NSA_FEAT)),
                  _const_spec(b1s), _const_spec(w2s), _const_spec(b1s), _const_spec((dh, CMP_HIDDEN)),
                  _const_spec((NSA_FEAT, n_cmp_pad))],
        out_specs=q_spec,
        scratch_shapes=[pltpu.VMEM((n_cmp_pad, dh + NSA_FEAT), BF16), pltpu.VMEM((dh, n_cmp_pad), BF16),
                        pltpu.VMEM((seq // span, dh, span), BF16), pltpu.VMEM((seq // span, dh, span), BF16),
                        pltpu.VMEM((3, cols, dh + NSA_FEAT), BF16), pltpu.VMEM((dh, cols), F32)],
        compiler_params=_cparams(2),
        name="nsa_attention",
    )(q, ksa, kwa, vst, vwt, kct, kcb, vct, vcb, gl, slope_rows, sfeat, jnp.asarray(cf),
      pos_const(pe_k, k_w1, k_b1), k_w2.astype(BF16), pos_const(pe_v, v_w1, v_b1), v_w2.astype(BF16).T, jnp.asarray(overlap_t, dtype=BF16))


def _pad_cols(w, n):
    return jnp.pad(w, ((0, 0), (0, n - w.shape[1])))


def kernel(x, ab_w_in, ab_w_out, s5_lambda_re, s5_lambda_im, s5_log_step, s5_b_re, s5_b_im, s5_c_re, s5_c_im, s5_d, s5_w_glu, s5_b_glu, gdn_conv_w, gdn_a_log, gdn_dt_bias, gdn_norm_w, nsa_w_in, nsa_w_out, nsa_pe_k, nsa_pe_v, nsa_k_w1, nsa_k_b1, nsa_k_w2, nsa_v_w1, nsa_v_b1, nsa_v_w2, ffn_w_in, ffn_conv_w, ffn_conv_b, ffn_w_out, norm_mix, norm_ffn, norm_final):
    bsz, seq, d = x.shape
    depth = ffn_w_in.shape[0]
    h = x.astype(F32).reshape(bsz * seq, d)
    for layer in range(depth):
        i = layer // 2
        if layer % 2 == 0:
            u, q, k, v, z, ba = _ab_proj(h, norm_mix[layer], ab_w_in[i], gdn_conv_w[i], seq)
            prep = _s5_prep(s5_lambda_re[i], s5_lambda_im[i], s5_log_step[i], s5_b_re[i], s5_b_im[i],
                            s5_c_re[i], s5_c_im[i], s5_d[i])
            y_a = _s5(u, seq, prep, s5_w_glu[i], s5_b_glu[i])
            y_b = _gdn(q, k, v, z, ba, seq, gdn_a_log[i], gdn_dt_bias[i], gdn_norm_w[i])
            wo = ab_w_out[i].astype(BF16)
            mix, wms = [y_a, y_b], [wo[:S5_WIDTH], wo[S5_WIDTH:]]
        else:
            proj = _nsa_proj(h, norm_mix[layer], nsa_w_in[i], nsa_k_w1[i], nsa_v_w1[i], bsz, seq)
            o = _nsa(*proj, bsz, seq, nsa_pe_k[i], nsa_pe_v[i], nsa_k_w1[i],
                     nsa_k_b1[i], nsa_k_w2[i], nsa_v_w1[i], nsa_v_b1[i], nsa_v_w2[i])
            mix, wms = [o], [nsa_w_out[i].astype(BF16)]
        h = _mix_ffn(h, mix, wms, seq, norm_ffn[layer], ffn_w_in[layer], ffn_conv_w[layer], ffn_conv_b[layer],
                     ffn_w_out[layer], norm_final, final=(layer == depth - 1))
    return h.reshape(bsz, seq, d).astype(x.dtype)
```

```python
import functools
import math

import numpy as np
import jax
import jax.numpy as jnp
from jax import lax
from jax.experimental import pallas as pl
from jax.experimental.pallas import tpu as pltpu

F32 = jnp.float32
BF16 = jnp.bfloat16
HI = lax.Precision.HIGHEST

D_MODEL = 1024
S5_WIDTH = 256
S5_GROUP = 16
S5_GROUPS = 16
S5_STATE = 64
S5_CHUNK = 16
GDN_HEAD_DIM = 128
GDN_HEADS = 6
GDN_WIDTH = 768
GDN_CONV = 4
GDN_CHUNK = 64
GDN_HEADS_PER_STEP = 3
GDN_INV_BLOCK = 8
GDN_CHUNKS_PER_ITER = 8
NSA_HEADS = 16
NSA_HEAD_DIM = 64
NSA_KV_GROUPS = 4
NSA_GROUP_SIZE = 4
NSA_KV_WIDTH = 256
CMP_LEN = 32
CMP_STRIDE = 16
CMP_HIDDEN = 256
SLC_BLOCK = 64
N_SELECT = 4
WINDOW = 256
NSA_Q_BLOCK = 64
NSA_SLAB_KEYS = 512
NSA_QB_PER_ITER = 2
LOG2E = math.log2(math.e)
FFN_HIDDEN = 2816
FFN_CONV = 3
FFN_COL_CHUNK = 256
RMS_EPS = 1e-6
NEG_INF = -1e30
LANES = 128
BF16_SUBLANES = 16
VMEM_LIMIT = 56 * 1024 * 1024
ROW_TILE = 512
FFN_ROW_TILE = 1024
FFN_VMEM_LIMIT = 62 * 1024 * 1024


def _cparams(n_axes, vmem_limit=VMEM_LIMIT):
    return pltpu.CompilerParams(dimension_semantics=("arbitrary",) * n_axes,
                                vmem_limit_bytes=vmem_limit)


def _rms(x, g):
    return x * lax.rsqrt(jnp.mean(x * x, axis=-1, keepdims=True) + RMS_EPS) * g


def _gelu(x):
    return 0.5 * x * (1.0 + jnp.tanh(math.sqrt(2.0 / math.pi) * (x + 0.044715 * (x * x * x))))


def _sigmoid(x):
    return 1.0 / (1.0 + jnp.exp(-x))


def _silu(x):
    return x * _sigmoid(x)


def _dot(a, b):
    return jnp.dot(a, b, preferred_element_type=F32)


def _dot_nt(a, b):
    return lax.dot_general(a, b, (((1,), (1,)), ((), ())), preferred_element_type=F32)


def _dot_tn(a, b):
    return lax.dot_general(a, b, (((0,), (0,)), ((), ())), preferred_element_type=F32)


def _const_spec(shape):
    nd = len(shape)
    return pl.BlockSpec(shape, lambda *_: (0,) * nd)


def _ab_proj_kernel(x_ref, g_ref, wu_ref, wq_ref, wk_ref, wv_ref, wz_ref, wba_ref, cw_ref,
                    u_ref, q_ref, k_ref, v_ref, z_ref, ba_ref, ext_ref, *, tiles_per_seq):
    rows = x_ref.shape[0]
    pad = 8
    dh = GDN_HEAD_DIM
    xn = _rms(x_ref[...], g_ref[...]).astype(BF16)
    seq_start = pl.program_id(0) % tiles_per_seq == 0

    @pl.when(pl.program_id(0) == 0)
    def _():
        ext_ref[...] = jnp.zeros_like(ext_ref)

    def finish(idx, pre, o_ref):
        prev = jnp.where(seq_start, 0.0, ext_ref[idx, rows:rows + pad, :])
        ext_ref[idx, 0:pad, :] = prev
        ext_ref[idx, pad:, :] = pre
        cw = cw_ref[idx]
        y = pre * cw[GDN_CONV - 1:GDN_CONV]
        for sh in range(1, GDN_CONV):
            y = y + ext_ref[idx, pad - sh:pad - sh + rows, :] * cw[GDN_CONV - 1 - sh:GDN_CONV - sh]
        y = _silu(y)
        if idx < 2:
            parts = []
            for j in range(GDN_HEADS):
                p = y[:, j * dh:(j + 1) * dh]
                p = p * lax.rsqrt(jnp.sum(p * p, axis=-1, keepdims=True) + 1e-6)
                parts.append(p * (dh ** -0.5) if idx == 0 else p)
            y = jnp.concatenate(parts, axis=1)
        o_ref[...] = y

    pre_q = _dot(xn, wq_ref[...])
    pre_k = _dot(xn, wk_ref[...])
    finish(0, pre_q, q_ref)
    pre_v = _dot(xn, wv_ref[...])
    finish(1, pre_k, k_ref)
    z_ref[...] = _dot(xn, wz_ref[...])
    finish(2, pre_v, v_ref)
    u_ref[...] = _dot(xn, wu_ref[...])
    ba_ref[...] = _dot(xn, wba_ref[...])


def _ab_proj(h, gain, w_in, conv_w, seq):
    t, d = h.shape
    w = w_in.astype(BF16)
    cuts = np.cumsum([0, S5_WIDTH, GDN_WIDTH, GDN_WIDTH, GDN_WIDTH, GDN_WIDTH])
    ws = [w[:, cuts[j]:cuts[j + 1]] for j in range(5)] + [_pad_cols(w[:, cuts[5]:], LANES)]
    cw = conv_w.astype(F32).reshape(GDN_CONV, 3, GDN_WIDTH).transpose(1, 0, 2)
    row_spec = lambda n: pl.BlockSpec((ROW_TILE, n), lambda i: (i, 0))
    return pl.pallas_call(
        functools.partial(_ab_proj_kernel, tiles_per_seq=seq // ROW_TILE),
        out_shape=[jax.ShapeDtypeStruct((t, x.shape[1]), F32) for x in ws],
        grid=(t // ROW_TILE,),
        in_specs=[row_spec(d), _const_spec((1, d))] + [_const_spec(x.shape) for x in ws] + [_const_spec(cw.shape)],
        out_specs=[row_spec(x.shape[1]) for x in ws],
        scratch_shapes=[pltpu.VMEM((3, ROW_TILE + 8, GDN_WIDTH), F32)],
        compiler_params=_cparams(1),
        name="ab_proj",
    )(h, gain.reshape(1, d), *ws, cw)


def _ffn_kernel(h_ref, hp_ref, *refs, n_mix, tiles_per_seq, n_chunks, final):
    y_refs, yp_refs, wm_refs = refs[:n_mix], refs[n_mix:2 * n_mix], refs[2 * n_mix:3 * n_mix]
    (g_ref, win_ref, cw_ref, cb_ref, wout_ref, gf_ref, o_ref,
     xn_ref, up0_ref, up1_ref, act_ref) = refs[3 * n_mix:]
    halo = BF16_SUBLANES
    rows = h_ref.shape[0]
    sub = 128
    fcw = act_ref.shape[2]
    g = g_ref[...]
    n_piece = rows // sub
    x1, x1p = h_ref[...], hp_ref[...]
    for y_ref, yp_ref, wm_ref in zip(y_refs, yp_refs, wm_refs):
        x1 = x1 + _dot(y_ref[...], wm_ref[...])
        x1p = x1p + _dot(yp_ref[...], wm_ref[...])
    o_ref[...] = x1
    seq_start = pl.program_id(0) % tiles_per_seq == 0
    xn_ref[0:halo, :] = jnp.where(seq_start, 0.0, _rms(x1p, g)).astype(BF16)
    xn_ref[halo:, :] = _rms(x1, g).astype(BF16)

    def gate_piece(j, buf, r):
        r0 = r * sub

        def conv(half, cw, cb):
            y = cb
            for k in range(FFN_CONV):
                start = halo + r0 - (FFN_CONV - 1) + k
                y = y + buf[half, start:start + sub, :] * cw[k:k + 1]
            return y
        a = conv(0, cw_ref[j], cb_ref[j])
        b = conv(1, cw_ref[n_chunks + j], cb_ref[n_chunks + j])
        act_ref[j, r0:r0 + sub, :] = (_silu(a) * b).astype(BF16)

    def up_and_gate(j_up, buf_up, j_gate, buf_gate):
        xe = xn_ref[...]
        for half in range(2):
            col = pl.multiple_of((half * n_chunks + j_up) * fcw, fcw)
            buf_up[half] = _dot(xe, win_ref[:, pl.ds(col, fcw)])
            if j_gate is not None:
                for r in range(half * n_piece // 2, (half + 1) * n_piece // 2):
                    gate_piece(j_gate, buf_gate, r)

    def body(i, carry):
        up_and_gate(2 * i + 1, up1_ref, 2 * i, up0_ref)
        up_and_gate(2 * i + 2, up0_ref, 2 * i + 1, up1_ref)
        return carry

    up_and_gate(0, up0_ref, None, None)
    lax.fori_loop(0, (n_chunks - 1) // 2, body, 0)
    for r in range(n_piece):
        gate_piece(n_chunks - 1, up0_ref, r)

    out = o_ref[...]
    for j in range(n_chunks):
        out = out + _dot(act_ref[j], wout_ref[j])
    if final:
        out = _rms(out, gf_ref[...])
    o_ref[...] = out


def _mix_ffn(h, ys, wms, seq, gain, w_in, conv_w, conv_b, w_out, gain_final, final):
    t, d = h.shape
    n_mix = len(ys)
    fc = FFN_COL_CHUNK
    n_chunks = FFN_HIDDEN // fc
    assert n_chunks % 2 == 1
    halo = BF16_SUBLANES
    win = w_in.astype(BF16)
    cw = conv_w.astype(F32).reshape(FFN_CONV, 2 * n_chunks, fc).transpose(1, 0, 2)
    cb = conv_b.astype(F32).reshape(2 * n_chunks, 1, fc)
    wout = w_out.astype(BF16).reshape(n_chunks, fc, d)
    rt = FFN_ROW_TILE
    blocks_per_tile = rt // halo
    tile_spec = lambda n: pl.BlockSpec((rt, n), lambda i: (i, 0))
    prev_spec = lambda n: pl.BlockSpec((halo, n), lambda i: (jnp.maximum(i * blocks_per_tile - 1, 0), 0))
    return pl.pallas_call(
        functools.partial(_ffn_kernel, n_mix=n_mix, tiles_per_seq=seq // rt, n_chunks=n_chunks, final=final),
        out_shape=jax.ShapeDtypeStruct((t, d), F32),
        grid=(t // rt,),
        in_specs=[tile_spec(d), prev_spec(d)]
        + [tile_spec(y.shape[1]) for y in ys] + [prev_spec(y.shape[1]) for y in ys]
        + [_const_spec(w.shape) for w in wms]
        + [_const_spec((1, d)),
                  pl.BlockSpec(win.shape, lambda i: (0, 0), pipeline_mode=pl.Buffered(1)),
                  _const_spec(cw.shape), _const_spec(cb.shape),
                  pl.BlockSpec(wout.shape, lambda i: (0, 0, 0), pipeline_mode=pl.Buffered(1)),
                  _const_spec((1, d))],
        out_specs=pl.BlockSpec((rt, d), lambda i: (i, 0)),
        scratch_shapes=[pltpu.VMEM((rt + halo, d), BF16), pltpu.VMEM((2, rt + halo, fc), F32),
                        pltpu.VMEM((2, rt + halo, fc), F32), pltpu.VMEM((n_chunks, rt, fc), BF16)],
        compiler_params=_cparams(1, FFN_VMEM_LIMIT),
        name="mix_ffn",
    )(h, h, *ys, *ys, *wms, gain.reshape(1, d), win, cw, cb, wout, gain_final.reshape(1, d))


def _s5_prep(lam_re, lam_im, log_step, b_re, b_im, c_re, c_im, d_skip):
    g_n, p_n, h_n, L = S5_GROUPS, S5_STATE, S5_GROUP, S5_CHUNK
    step = jnp.exp(log_step.astype(F32))[:, None]
    lr, li = lam_re.astype(F32), lam_im.astype(F32)
    mag = jnp.exp(lr * step)
    a_re = mag * jnp.cos(li * step)
    a_im = mag * jnp.sin(li * step)
    den = lr * lr + li * li
    n_re, n_im = a_re - 1.0, a_im
    z_re = (n_re * lr + n_im * li) / den
    z_im = (n_im * lr - n_re * li) / den
    b_re, b_im = b_re.astype(F32), b_im.astype(F32)
    bb_re = z_re[..., None] * b_re - z_im[..., None] * b_im
    bb_im = z_re[..., None] * b_im + z_im[..., None] * b_re
    c_re, c_im = c_re.astype(F32), c_im.astype(F32)
    pw_re, pw_im = [jnp.ones_like(a_re)], [jnp.zeros_like(a_im)]
    for _ in range(L):
        pr, pi = pw_re[-1], pw_im[-1]
        pw_re.append(pr * a_re - pi * a_im)
        pw_im.append(pr * a_im + pi * a_re)
    eye_g = jnp.eye(g_n, dtype=F32)
    ks = []
    for j in range(L):
        ab_re = pw_re[j][..., None] * bb_re - pw_im[j][..., None] * bb_im
        ab_im = pw_re[j][..., None] * bb_im + pw_im[j][..., None] * bb_re
        kj = (jnp.einsum('gop,gpi->gio', c_re, ab_re, precision=HI)
              - jnp.einsum('gop,gpi->gio', c_im, ab_im, precision=HI))
        ks.append(jnp.einsum('gio,gk->giko', kj, eye_g).reshape(g_n * h_n, g_n * h_n))
    kstack = jnp.concatenate(ks[::-1], axis=0).astype(BF16)
    bb = jnp.stack([jnp.einsum('gph,gk->ghkp', bb_re, eye_g), jnp.einsum('gph,gk->ghkp', bb_im, eye_g)], axis=2)
    bb = bb.reshape(g_n * h_n, 2 * g_n * p_n).astype(BF16)
    cc = jnp.stack([jnp.einsum('ghp,gk->gpkh', c_re, eye_g), -jnp.einsum('ghp,gk->gpkh', c_im, eye_g)], axis=0)
    cc = cc.reshape(2 * g_n * p_n, g_n * h_n).astype(BF16)
    apow = jnp.stack([jnp.concatenate([r.reshape(-1), i.reshape(-1)]) for r, i in zip(pw_re, pw_im)], axis=0)
    dvec = jnp.tile(d_skip.astype(F32).reshape(1, g_n * h_n), (1, L))
    return kstack, bb, cc, apow, dvec


def _s5_kernel(u_ref, ks_ref, bb_ref, cc_ref, ap_ref, dv_ref, wg_ref, bg_ref, o_ref, x_scr, hp_scr,
               *, n_batch, n_chunk):
    L, w = S5_CHUNK, S5_WIDTH
    half = S5_GROUPS * S5_STATE
    rows = n_batch * n_chunk
    u2 = u_ref[...].reshape(rows, L * w)
    u2b = u2.astype(BF16)

    def cmul(j, x):
        ar, ai = ap_ref[j:j + 1, :half], ap_ref[j:j + 1, half:]
        xr, xi = x[:, :half], x[:, half:]
        return jnp.concatenate([ar * xr - ai * xi, ar * xi + ai * xr], axis=1)

    xin = _dot(u2b[:, (L - 1) * w:], bb_ref[...])
    for s in range(L - 1):
        xin = xin + cmul(L - 1 - s, _dot(u2b[:, s * w:(s + 1) * w], bb_ref[...]))
    x_scr[...] = xin

    a_l = ap_ref[L:L + 1, :]
    alr, ali = a_l[:, :half], a_l[:, half:]

    def scan_body(k, hs):
        new = []
        for b in range(n_batch):
            r = b * n_chunk + k
            h = hs[b]
            hp_scr[pl.ds(r, 1), :] = h
            hr, hi = h[:, :half], h[:, half:]
            new.append(jnp.concatenate([alr * hr - ali * hi, alr * hi + ali * hr], axis=1)
                       + x_scr[pl.ds(r, 1), :])
        return tuple(new)

    lax.fori_loop(0, n_chunk, scan_body, tuple(jnp.zeros((1, 2 * half), F32) for _ in range(n_batch)))
    hp = hp_scr[...]

    for t in range(L):
        y = _dot(u2b[:, :(t + 1) * w], ks_ref[(L - 1 - t) * w:, :])
        y = y + _dot(cmul(t + 1, hp).astype(BF16), cc_ref[...])
        y = y + dv_ref[:, t * w:(t + 1) * w] * u2[:, t * w:(t + 1) * w]
        z = _gelu(y)
        gate = _sigmoid(_dot(z.astype(BF16), wg_ref[...]) + bg_ref[...])
        o_ref[:, :, t * w:(t + 1) * w] = (z * gate).astype(o_ref.dtype).reshape(n_batch, n_chunk, w)


def _s5(u, seq, prep, w_glu, b_glu):
    t = u.shape[0]
    bsz = t // seq
    L, w = S5_CHUNK, S5_WIDTH
    n_chunk = seq // L
    n_batch = 2
    kstack, bb, cc, apow, dvec = prep
    u3 = u.reshape(bsz, n_chunk, L * w)
    out = pl.pallas_call(
        functools.partial(_s5_kernel, n_batch=n_batch, n_chunk=n_chunk),
        out_shape=jax.ShapeDtypeStruct((bsz, n_chunk, L * w), BF16),
        grid=(bsz // n_batch,),
        in_specs=[pl.BlockSpec((n_batch, n_chunk, L * w), lambda i: (i, 0, 0)),
                  _const_spec(kstack.shape), _const_spec(bb.shape), _const_spec(cc.shape),
                  _const_spec(apow.shape), _const_spec(dvec.shape),
                  _const_spec((w, w)), _const_spec((1, w))],
        out_specs=pl.BlockSpec((n_batch, n_chunk, L * w), lambda i: (i, 0, 0)),
        scratch_shapes=[pltpu.VMEM((n_batch * n_chunk, 2 * S5_GROUPS * S5_STATE), F32),
                        pltpu.VMEM((n_batch * n_chunk, 2 * S5_GROUPS * S5_STATE), F32)],
        compiler_params=_cparams(1),
        name="s5_mixer",
    )(u3, kstack, bb, cc, apow, dvec, w_glu.astype(BF16), b_glu.astype(F32).reshape(1, w))
    return out.reshape(t, w)


def _gdn_kernel(qs, ks, vs, z_ref, bc_ref, ac_ref, ar_ref, hp_ref, hr_ref, nw_ref,
                o_ref, xs, ns, qe, o0, gls, gcc_s, gcr_s, xs2, ns2, gls2, *, seq, n_head):
    c, dh = GDN_CHUNK, GDN_HEAD_DIM
    n_chunk = seq // c

    hp = hp_ref[0]
    hr = hr_ref[0]

    def softplus(x):
        return jnp.maximum(x, 0.0) + jnp.log(1.0 + jnp.exp(-jnp.abs(x)))

    ii = lax.broadcasted_iota(jnp.int32, (c, c), 0)
    jj = lax.broadcasted_iota(jnp.int32, (c, c), 1)
    eye = (ii == jj).astype(F32)
    same_block = (ii // GDN_INV_BLOCK) == (jj // GDN_INV_BLOCK)
    g_col = -jnp.exp(hp[0:1, :]) * softplus(ac_ref[0, 0] + hp[1:2, :])
    g_row = -jnp.exp(hr[:, 0:1]) * softplus(ar_ref[0, 0] + hr[:, 1:2])
    gc_col = jnp.dot((ii >= jj).astype(F32), g_col, precision=HI, preferred_element_type=F32)
    gcr_s[...] = jnp.dot(g_row, (ii <= jj).astype(F32), precision=HI, preferred_element_type=F32)
    for ci in range(n_chunk):
        gcc_s[ci * c:(ci + 1) * c, :] = gc_col[:, ci * n_head:(ci + 1) * n_head]

    def local_body(it, carry):
        pr = []
        for cc in range(GDN_CHUNKS_PER_ITER):
            ci = it * GDN_CHUNKS_PER_ITER + cc
            rs = pl.ds(pl.multiple_of(ci * c, c), c)
            beta = _sigmoid(bc_ref[0, 0, rs, :])
            gccs = gcc_s[rs, :]
            for j in range(n_head):
                sl = slice(j * dh, (j + 1) * dh)
                gcc = gccs[:, j:j + 1]
                gcr = gcr_s[pl.ds(ci * n_head + j, 1), :]
                q, k, v = qs[rs, sl], ks[rs, sl], vs[rs, sl]
                bj = beta[:, j:j + 1]
                kb = k * bj
                e_g = jnp.exp(gcc)
                g_last = gcc[c - 1:c, :]
                pr.append(dict(ci=ci, j=j, q=q, kf=k.astype(BF16), kbb=kb.astype(BF16),
                               decay=jnp.exp(jnp.where(ii >= jj, gcc - gcr, NEG_INF)),
                               rhs=jnp.concatenate([v * bj, kb * e_g], axis=1).astype(BF16),
                               qd=q * e_g, kd=(k * jnp.exp(g_last - gcc)).astype(BF16),
                               gl=jnp.broadcast_to(jnp.exp(g_last), (8, dh))))
        kk = [_dot_nt(p["kbb"], p["kf"]) for p in pr]
        qk = [_dot_nt(p["q"].astype(BF16), p["kf"]) for p in pr]
        ps = [-jnp.where(ii > jj, a * p["decay"], 0.0) for a, p in zip(kk, pr)]
        ds = [jnp.where(same_block, n, 0.0) for n in ps]
        offs = [(n - d).astype(BF16) for n, d in zip(ps, ds)]
        ts = [eye + d for d in ds]
        reach = 2
        while reach < GDN_INV_BLOCK:
            ds = [_dot(d.astype(BF16), d.astype(BF16)) for d in ds]
            ts = [t + _dot(t.astype(BF16), d.astype(BF16)) for t, d in zip(ts, ds)]
            reach *= 2
        ms = [_dot(t.astype(BF16), o) for t, o in zip(ts, offs)]
        ts = [t + _dot(m.astype(BF16), t.astype(BF16)) for t, m in zip(ts, ms)]
        reach = 2
        while reach < c // GDN_INV_BLOCK:
            ms = [_dot(m.astype(BF16), m.astype(BF16)) for m in ms]
            ts = [t + _dot(m.astype(BF16), t.astype(BF16)) for t, m in zip(ts, ms)]
            reach *= 2
        uws = [_dot(t.astype(BF16), p["rhs"]).astype(BF16) for t, p in zip(ts, pr)]
        intras = [(s * p["decay"]).astype(BF16) for s, p in zip(qk, pr)]
        i_uws = [_dot(a, uw) for a, uw in zip(intras, uws)]
        kd_uws = [_dot_tn(p["kd"], uw) for p, uw in zip(pr, uws)]
        for p, i_uw, kd_uw in zip(pr, i_uws, kd_uws):
            j, ci = p["j"], p["ci"]
            ns[j, ci] = kd_uw[:, :dh]
            xs[j, ci] = kd_uw[:, dh:].astype(BF16)
            o0[j, ci] = i_uw[:, :dh]
            qe[j, ci] = (p["qd"] - i_uw[:, dh:]).astype(BF16)
            gls[j, ci] = p["gl"]
        first = [idx for idx, p in enumerate(pr) if (idx // n_head) % 2 == 0]
        x1b = [kd_uws[idx + n_head][:, dh:].astype(BF16) for idx in first]
        x1x0 = [_dot(x1, kd_uws[idx][:, dh:].astype(BF16)) for x1, idx in zip(x1b, first)]
        x1n0 = [_dot(x1, kd_uws[idx][:, :dh].astype(BF16)) for x1, idx in zip(x1b, first)]
        for idx, xx, xn in zip(first, x1x0, x1n0):
            p0, p1 = pr[idx], pr[idx + n_head]
            g0, g1 = p0["gl"][0:1, :], p1["gl"][0:1, :]
            pi = it * (GDN_CHUNKS_PER_ITER // 2) + (idx // n_head) // 2
            xs2[p0["j"], pi] = (g1 * kd_uws[idx][:, dh:] + g0 * kd_uws[idx + n_head][:, dh:] - xx).astype(BF16)
            ns2[p0["j"], pi] = g1 * kd_uws[idx][:, :dh] - xn + kd_uws[idx + n_head][:, :dh]
            gls2[p0["j"], pi] = p0["gl"] * p1["gl"]
        return carry

    lax.fori_loop(0, n_chunk // GDN_CHUNKS_PER_ITER, local_body, 0)

    def rec_body(pi, states):
        c0, c1 = 2 * pi, 2 * pi + 1
        s_bs = [s.astype(BF16) for s in states]
        xd2 = [_dot(xs2[j, pi], s_bs[j]) for j in range(n_head)]
        xd0 = [_dot(xs[j, c0], s_bs[j]) for j in range(n_head)]
        od0 = [_dot(qe[j, c0], s_bs[j]) for j in range(n_head)]
        new = tuple(gls2[j, pi][0:1, :] * states[j] - xd2[j] + ns2[j, pi] for j in range(n_head))
        mid = [(gls[j, c0][0:1, :] * states[j] - xd0[j] + ns[j, c0]).astype(BF16) for j in range(n_head)]
        od1 = [_dot(qe[j, c1], mid[j]) for j in range(n_head)]
        for ci, od in ((c0, od0), (c1, od1)):
            rs = pl.ds(pl.multiple_of(ci * c, c), c)
            for j in range(n_head):
                sl = slice(j * dh, (j + 1) * dh)
                o = od[j] + o0[j, ci]
                on = o * lax.rsqrt(jnp.mean(o * o, axis=-1, keepdims=True) + RMS_EPS) * nw_ref[...]
                o_ref[rs, sl] = (on * _silu(z_ref[rs, sl])).astype(o_ref.dtype)
        return new

    lax.fori_loop(0, n_chunk // 2, rec_body, tuple(jnp.zeros((dh, dh), F32) for _ in range(n_head)), unroll=2)


def _gdn(q, k, v, z, ba, seq, a_log, dt_bias, norm_w):
    t = q.shape[0]
    bsz = t // seq
    nh = GDN_HEADS_PER_STEP
    n_grp = GDN_HEADS // nh
    wd = nh * GDN_HEAD_DIM
    bl = ba[:, :GDN_HEADS].reshape(bsz, seq, n_grp, nh)
    al = ba[:, GDN_HEADS:2 * GDN_HEADS].reshape(bsz, seq, n_grp, nh)
    b_col = bl.transpose(0, 2, 1, 3)
    n_chunk = seq // GDN_CHUNK
    n_prob = n_chunk * nh
    a5 = al.reshape(bsz, n_chunk, GDN_CHUNK, n_grp, nh)
    a_col = a5.transpose(0, 3, 2, 1, 4).reshape(bsz, n_grp, GDN_CHUNK, n_prob)
    a_row = a5.transpose(0, 3, 1, 4, 2).reshape(bsz, n_grp, n_prob, GDN_CHUNK)
    hp = jnp.stack([jnp.tile(a_log.astype(F32).reshape(n_grp, nh), (1, n_chunk)),
                    jnp.tile(dt_bias.astype(F32).reshape(n_grp, nh), (1, n_chunk))], axis=1)
    hr = hp.transpose(0, 2, 1)
    act_spec = pl.BlockSpec((seq, wd), lambda b, g: (b, g))
    col_spec = pl.BlockSpec((1, 1, seq, nh), lambda b, g: (b, g, 0, 0))
    return pl.pallas_call(
        functools.partial(_gdn_kernel, seq=seq, n_head=nh),
        out_shape=jax.ShapeDtypeStruct((t, GDN_WIDTH), BF16),
        grid=(bsz, n_grp),
        in_specs=[act_spec, act_spec, act_spec, act_spec, col_spec,
                  pl.BlockSpec((1, 1, GDN_CHUNK, n_prob), lambda b, g: (b, g, 0, 0)),
                  pl.BlockSpec((1, 1, n_prob, GDN_CHUNK), lambda b, g: (b, g, 0, 0)),
                  pl.BlockSpec((1, 2, n_prob), lambda b, g: (g, 0, 0)),
                  pl.BlockSpec((1, n_prob, 2), lambda b, g: (g, 0, 0)),
                  _const_spec((1, GDN_HEAD_DIM))],
        out_specs=act_spec,
        scratch_shapes=[pltpu.VMEM((nh, n_chunk, GDN_HEAD_DIM, GDN_HEAD_DIM), BF16),
                        pltpu.VMEM((nh, n_chunk, GDN_HEAD_DIM, GDN_HEAD_DIM), F32),
                        pltpu.VMEM((nh, n_chunk, GDN_CHUNK, GDN_HEAD_DIM), BF16),
                        pltpu.VMEM((nh, n_chunk, GDN_CHUNK, GDN_HEAD_DIM), F32),
                        pltpu.VMEM((nh, n_chunk, 8, GDN_HEAD_DIM), F32),
                        pltpu.VMEM((seq, nh), F32), pltpu.VMEM((n_prob, GDN_CHUNK), F32),
                        pltpu.VMEM((nh, n_chunk // 2, GDN_HEAD_DIM, GDN_HEAD_DIM), BF16),
                        pltpu.VMEM((nh, n_chunk // 2, GDN_HEAD_DIM, GDN_HEAD_DIM), F32),
                        pltpu.VMEM((nh, n_chunk // 2, 8, GDN_HEAD_DIM), F32)],
        compiler_params=_cparams(2),
        name="gated_deltanet",
    )(q, k, v, z, b_col, a_col, a_row, hp, hr, norm_w.astype(F32).reshape(1, GDN_HEAD_DIM))


NSA_FEAT = 64
F_BLK, F_POS, F_ONE = 0, 32, 38


def _split3(x):
    x1 = x.astype(BF16).astype(F32)
    x2 = (x - x1).astype(BF16).astype(F32)
    return x1, x2, x - x1 - x2


def _nsa_proj_kernel(x_ref, g_ref, wq_ref, wks_ref, wkw_ref, wvs_ref, wvw_ref, wkc_ref, wvc_ref, wg_ref, kf_ref,
                     perm_ref, kw1_ref, vw1_ref,
                     q_ref, ksa_ref, kwa_ref, vst_ref, vwt_ref, kct_ref, kcb_ref, vct_ref, vcb_ref, gl_ref):
    g_n, dh = NSA_KV_GROUPS, NSA_HEAD_DIM
    n_row = x_ref.shape[0] // CMP_STRIDE
    half = CMP_STRIDE * dh

    def compress_halves(c, w1_ref, top_ref, bot_ref):
        rows = _dot(perm_ref[...], c.astype(BF16)).astype(BF16)
        a = jnp.concatenate(
            [jnp.concatenate([rows[l * n_row:(l + 1) * n_row, g * dh:(g + 1) * dh] for l in range(CMP_STRIDE)], axis=1)
             for g in range(g_n)], axis=0)
        top, bot = _dot(a, w1_ref[:half, :]), _dot(a, w1_ref[half:, :])
        for g in range(g_n):
            top_ref[0, g] = top[g * n_row:(g + 1) * n_row]
            bot_ref[0, g] = bot[g * n_row:(g + 1) * n_row]

    xn = _rms(x_ref[...], g_ref[...]).astype(BF16)
    q_ref[...] = (_dot(xn, wq_ref[...]) * (NSA_HEAD_DIM ** -0.5 * LOG2E)).astype(BF16)
    kf = jnp.concatenate([kf_ref[...]] * NSA_KV_GROUPS, axis=1)
    ksa_ref[...] = (_dot(xn, wks_ref[...]) + kf).astype(BF16)
    kwa_ref[...] = (_dot(xn, wkw_ref[...]) + kf).astype(BF16)
    vst_ref[0] = _dot_nt(wvs_ref[...], xn).astype(BF16)
    vwt_ref[0] = _dot_nt(wvw_ref[...], xn).astype(BF16)
    compress_halves(_dot(xn, wkc_ref[...]), kw1_ref, kct_ref, kcb_ref)
    compress_halves(_dot(xn, wvc_ref[...]), vw1_ref, vct_ref, vcb_ref)
    gl_ref[...] = _dot(xn, wg_ref[...])


def _nsa_proj(h, gain, w_in, k_w1, v_w1, bsz, seq):
    t, d = h.shape
    g_n, dh = NSA_KV_GROUPS, NSA_HEAD_DIM
    nq, kvw = NSA_HEADS * dh, NSA_KV_WIDTH
    w = w_in.astype(BF16)
    wq = w[:, :nq]
    wkc, wvc, wks, wvs, wkw, wvw = (w[:, nq + j * kvw: nq + (j + 1) * kvw] for j in range(6))
    wgl = w[:, nq + 6 * kvw:]

    def lane_padded(x, width):
        return jnp.pad(x.reshape(d, g_n, width), ((0, 0), (0, 0), (0, LANES - width))).reshape(d, g_n * LANES)

    pos = np.arange(seq)
    kf = np.zeros((seq, LANES), np.float32)
    kf[pos, dh + F_BLK + pos // SLC_BLOCK] = 1.0
    kf[:, dh + F_POS:dh + F_POS + 3] = ((pos // SLC_BLOCK) * SLC_BLOCK)[:, None]
    kf[:, dh + F_POS + 3:dh + F_POS + 6] = (pos % SLC_BLOCK)[:, None]
    kf[:, dh + F_ONE:dh + F_ONE + 3] = 1.0
    tps = seq // ROW_TILE
    row_spec = lambda n: pl.BlockSpec((ROW_TILE, n), lambda i: (i, 0))
    tr_spec = pl.BlockSpec((1, kvw, ROW_TILE), lambda i: (i // tps, 0, i % tps))
    ws = [wq, lane_padded(wks, dh), lane_padded(wkw, dh), wvs.T, wvw.T, wkc, wvc, lane_padded(wgl, 3 * NSA_GROUP_SIZE)]
    n_row = ROW_TILE // CMP_STRIDE
    tok = np.arange(ROW_TILE)
    perm = np.zeros((ROW_TILE, ROW_TILE), np.float32)
    perm[(tok % CMP_STRIDE) * n_row + tok // CMP_STRIDE, tok] = 1.0
    cmp_consts = [jnp.asarray(perm, dtype=BF16), k_w1.astype(BF16), v_w1.astype(BF16)]
    half_spec = pl.BlockSpec((1, g_n, n_row, CMP_HIDDEN), lambda i: (i // tps, 0, i % tps, 0))
    half_shape = jax.ShapeDtypeStruct((bsz, g_n, seq // CMP_STRIDE, CMP_HIDDEN), F32)
    return pl.pallas_call(
        _nsa_proj_kernel,
        out_shape=[jax.ShapeDtypeStruct((t, nq), BF16),
                   jax.ShapeDtypeStruct((t, g_n * LANES), BF16), jax.ShapeDtypeStruct((t, g_n * LANES), BF16),
                   jax.ShapeDtypeStruct((bsz, kvw, seq), BF16), jax.ShapeDtypeStruct((bsz, kvw, seq), BF16),
                   half_shape, half_shape, half_shape, half_shape,
                   jax.ShapeDtypeStruct((t, g_n * LANES), F32)],
        grid=(t // ROW_TILE,),
        in_specs=[row_spec(d), _const_spec((1, d))] + [_const_spec(x.shape) for x in ws]
        + [pl.BlockSpec((ROW_TILE, LANES), lambda i: (i % tps, 0))] + [_const_spec(x.shape) for x in cmp_consts],
        out_specs=[row_spec(nq), row_spec(g_n * LANES), row_spec(g_n * LANES), tr_spec, tr_spec,
                   half_spec, half_spec, half_spec, half_spec, row_spec(g_n * LANES)],
        compiler_params=_cparams(1),
        name="nsa_proj",
    )(h, gain.reshape(1, d), *ws, jnp.asarray(kf), *cmp_consts)


def _nsa_kernel(q_ref, ksa_ref, kwa_ref, vst_ref, vwt_ref, kct_ref, kcb_ref, vct_ref, vcb_ref, gl_ref, sl_ref, sf_ref,
                cf_ref, kc1_ref, kw2_ref, vc1_ref, vw2t_ref, ovt_ref, o_ref,
                kcmp, vcmpt, vs3, vw3, qa_s, oc_s, *, seq):
    qb, dh, r_n, nq = NSA_Q_BLOCK, NSA_HEAD_DIM, NSA_GROUP_SIZE, NSA_QB_PER_ITER
    span = nq * qb
    cols = nq * r_n * qb
    n_cmp_pad = seq // CMP_STRIDE
    n_cmp = (seq - CMP_LEN) // CMP_STRIDE + 1
    n_blk = seq // SLC_BLOCK
    n_win = WINDOW + span
    it_per_slab = NSA_SLAB_KEYS // span
    nf = NSA_FEAT

    def hidden(top_ref, bot_ref, c1_ref):
        bot = pltpu.roll(bot_ref[0, 0], n_cmp_pad - 1, 0)
        return _gelu(top_ref[0, 0] + bot + c1_ref[...]).astype(BF16)

    kcmp[...] = jnp.concatenate([_dot(hidden(kct_ref, kcb_ref, kc1_ref), kw2_ref[...]), cf_ref[...]],
                                axis=1).astype(BF16)
    vcmpt[...] = _dot_nt(vw2t_ref[...], hidden(vct_ref, vcb_ref, vc1_ref)).astype(BF16)
    for n in range(seq // span):
        vs3[n] = vst_ref[0, :, n * span:(n + 1) * span]
        vw3[n] = vwt_ref[0, :, n * span:(n + 1) * span]

    def iota(shape, axis):
        return lax.broadcasted_iota(jnp.int32, shape, axis)

    slope = sl_ref[0]
    colc, colr = iota((cols, 1), 0), iota((1, cols), 1)
    tc_col = (colc // (r_n * qb)) * qb + colc % qb
    tc_row = (colr // (r_n * qb)) * qb + colr % qb
    qi_row = colr // (r_n * qb)
    lane_f = iota((1, nf), 1)
    jr = iota((n_cmp_pad, 1), 0)
    nr = iota((nf, 1), 0)
    lane_s = iota((1, span), 1)
    qi_s = lane_s // qb
    key_d = iota((span, 1), 0)
    key_w = iota((n_win, 1), 0)
    eye_s = (iota((span, span), 0) == iota((span, span), 1)).astype(BF16)

    def prepare(p):
        t0 = pl.multiple_of(p * span, span)
        qf = jnp.concatenate([q_ref[pl.ds(t0 + qi * qb, qb), r * dh:(r + 1) * dh]
                              for qi in range(nq) for r in range(r_n)], axis=0).astype(F32)
        c1, c2, c3 = _split3(-(slope * (t0 + tc_col).astype(F32)))
        f_plain = jnp.where(lane_f == F_ONE, c1, jnp.where(lane_f == F_ONE + 1, c2,
                            jnp.where(lane_f == F_ONE + 2, c3, sf_ref[0])))
        qa_plain = jnp.concatenate([qf, f_plain], axis=1).astype(BF16)

        ok_c = (jr * CMP_STRIDE + (CMP_LEN - 1) <= t0 + tc_row) & (jr < n_cmp)
        s_c = jnp.where(ok_c, _dot_nt(kcmp[...], qa_plain), NEG_INF)
        e_c = jnp.where(ok_c, jnp.exp2(s_c - jnp.max(s_c, axis=0, keepdims=True)), 0.0)
        l_c = jnp.sum(e_c, axis=0, keepdims=True)
        p_c = e_c / jnp.where(l_c > 0.0, l_c, 1.0)
        oc_s[...] = _dot(vcmpt[...], p_c.astype(BF16))

        imp4 = sum(_dot(ovt_ref[...], part.astype(BF16)) for part in _split3(p_c))
        halves = []
        for qi in range(nq):
            a = imp4[:, (2 * qi) * LANES:(2 * qi + 1) * LANES] + imp4[:, (2 * qi + 1) * LANES:(2 * qi + 2) * LANES]
            halves.append(a + pltpu.roll(a, qb, 1))
        low_half = lax.broadcasted_iota(jnp.int32, (1, LANES), 1) < qb
        imp = jnp.concatenate([jnp.where(low_half, halves[2 * v], halves[2 * v + 1]) for v in range(nq // 2)],
                              axis=1)

        iq = nq * p + qi_s
        cand = (nr > 0) & (nr < iq)
        sc = jnp.where(cand, imp, NEG_INF)
        sel = (nr == 0) & (iq > 0)
        for _ in range(N_SELECT - 2):
            best = jnp.max(sc, axis=0, keepdims=True)
            first = jnp.min(jnp.where(sc == best, nr, nf), axis=0, keepdims=True)
            pick = (nr == first) & cand
            sel = sel | pick
            sc = jnp.where(pick, NEG_INF, sc)
        neg_slab = jnp.where(sel & (nr < nq * p), 0.0, NEG_INF)
        neg_cur = jnp.where(nr == iq, 0.0, jnp.where((nr >= nq * p) & (nr < iq) & sel, 0.0, NEG_INF))
        neg_t = jnp.concatenate([neg_slab, neg_cur], axis=0).astype(BF16)
        neg = _dot_nt(eye_s, neg_t)
        rows_of = lambda x: jnp.concatenate([x[qi * qb:(qi + 1) * qb] for qi in range(nq) for _ in range(r_n)], axis=0)
        f_slab = jnp.where(lane_f < n_blk, rows_of(neg[:, :nf]), f_plain)
        f_cur = jnp.where(lane_f < n_blk, rows_of(neg[:, nf:]), f_plain)
        qa_s[0] = qa_plain
        qa_s[1] = jnp.concatenate([qf, f_slab], axis=1).astype(BF16)
        qa_s[2] = jnp.concatenate([qf, f_cur], axis=1).astype(BF16)

    n_iter = seq // span

    def make_body(n_keys):
        def body(p, carry):
            t0 = pl.multiple_of(p * span, span)
            w0 = pl.multiple_of(jnp.maximum(t0 - WINDOW, 0), span)
            oc_t = oc_s[...]
            s_cu = _dot_nt(ksa_ref[pl.ds(t0, span), :], qa_s[2])
            s_w = _dot_nt(kwa_ref[pl.ds(w0, n_win), :], qa_s[0])
            s_sl = _dot_nt(ksa_ref[0:n_keys, :], qa_s[1])
            prepare(jnp.minimum(p + 1, n_iter - 1))

            own = (key_d // qb) == qi_row
            s_cu = jnp.where(own & (key_d % qb > tc_row % qb), NEG_INF, s_cu)
            m_s = jnp.maximum(jnp.max(s_sl, axis=0, keepdims=True), jnp.max(s_cu, axis=0, keepdims=True))
            e_sl = jnp.exp2(s_sl - m_s)
            e_cu = jnp.exp2(s_cu - m_s)
            l_s = jnp.sum(e_sl, axis=0, keepdims=True) + jnp.sum(e_cu, axis=0, keepdims=True)
            os_t = (_dot(vst_ref[0, :, 0:n_keys], e_sl.astype(BF16)) + _dot(vs3[p], e_cu.astype(BF16))) / l_s

            rel = (t0 - w0) + tc_row - key_w
            s_w = jnp.where((rel >= 0) & (rel < WINDOW), s_w, NEG_INF)
            e_w = jnp.exp2(s_w - jnp.max(s_w, axis=0, keepdims=True))
            e_wb = e_w.astype(BF16)
            b0 = w0 // span
            ow_t = _dot(vw3[b0], e_wb[0:span])
            for j in range(1, n_win // span):
                ow_t = ow_t + _dot(vw3[b0 + j], e_wb[j * span:(j + 1) * span])
            ow_t = ow_t / jnp.sum(e_w, axis=0, keepdims=True)

            o_all = jnp.concatenate([os_t, ow_t, oc_t, jnp.zeros_like(oc_t)], axis=0).T
            gate = _sigmoid(gl_ref[pl.ds(t0, span), 0:3 * r_n])
            for qi in range(nq):
                g_q = gate[qi * qb:(qi + 1) * qb]
                heads = []
                for r in range(r_n):
                    rs = slice((qi * r_n + r) * qb, (qi * r_n + r + 1) * qb)
                    heads.append(g_q[:, 3 * r:3 * r + 1] * o_all[rs, 2 * dh:3 * dh]
                                 + g_q[:, 3 * r + 1:3 * r + 2] * o_all[rs, :dh]
                                 + g_q[:, 3 * r + 2:3 * r + 3] * o_all[rs, dh:2 * dh])
                o_ref[pl.ds(t0 + qi * qb, qb), :] = jnp.concatenate(heads, axis=1).astype(o_ref.dtype)
            return carry
        return body

    prepare(jnp.int32(0))
    for sb in range(seq // NSA_SLAB_KEYS):
        lax.fori_loop(sb * it_per_slab, (sb + 1) * it_per_slab, make_body((sb + 1) * NSA_SLAB_KEYS), 0)


def _nsa(q, ksa, kwa, vst, vwt, kct, kcb, vct, vcb, gl, bsz, seq, pe_k, pe_v, k_w1, k_b1, k_w2, v_w1, v_b1, v_w2):
    g_n, r_n, dh, qb = NSA_KV_GROUPS, NSA_GROUP_SIZE, NSA_HEAD_DIM, NSA_Q_BLOCK
    t = bsz * seq
    n_cmp_pad = seq // CMP_STRIDE
    n_blk = seq // SLC_BLOCK
    span = NSA_QB_PER_ITER * qb
    assert n_blk <= F_POS and seq % NSA_SLAB_KEYS == 0 and seq >= WINDOW + span

    slopes = np.asarray([2.0 ** (-8.0 * (h + 1) / NSA_HEADS) for h in range(NSA_HEADS)], dtype=np.float32)
    cols = NSA_QB_PER_ITER * r_n * qb
    slope_rows = jnp.asarray(np.tile(np.repeat(slopes.reshape(g_n, r_n), qb, axis=1), (1, NSA_QB_PER_ITER))
                             .reshape(g_n, cols, 1)) * LOG2E
    s3 = jnp.concatenate(_split3(slope_rows) * 2, axis=2)
    sfeat = jnp.pad(s3, ((0, 0), (0, 0), (F_POS, NSA_FEAT - F_POS - 6)))
    cf = np.zeros((n_cmp_pad, NSA_FEAT), np.float32)
    cf[:, F_POS:F_POS + 3] = (np.arange(n_cmp_pad) * CMP_STRIDE)[:, None]
    cf[:, F_POS + 3:F_POS + 6] = (CMP_LEN - 1) * 0.5
    cf[:, F_ONE:F_ONE + 3] = 1.0
    tok = np.arange(n_cmp_pad)[:, None] * CMP_STRIDE + np.arange(CMP_LEN)[None, :]
    overlap_t = ((tok // SLC_BLOCK)[:, :, None] == np.arange(NSA_FEAT)[None, None, :]).mean(axis=1).astype(np.float32).T
    def pos_const(pe, w1, b1):
        return _dot(pe.astype(BF16).reshape(1, -1), w1.astype(BF16)) + b1.astype(F32).reshape(1, -1)

    cmp_spec = pl.BlockSpec((1, 1, n_cmp_pad, CMP_HIDDEN), lambda b, g: (b, g, 0, 0))
    q_spec = pl.BlockSpec((seq, r_n * dh), lambda b, g: (b, g))
    k_spec = pl.BlockSpec((seq, LANES), lambda b, g: (b, g))
    v_spec = pl.BlockSpec((1, dh, seq), lambda b, g: (b, g, 0))
    w1s, b1s, w2s = (CMP_LEN * dh, CMP_HIDDEN), (1, CMP_HIDDEN), (CMP_HIDDEN, dh)
    return pl.pallas_call(
        functools.partial(_nsa_kernel, seq=seq),
        out_shape=jax.ShapeDtypeStruct((t, NSA_HEADS * dh), BF16),
        grid=(bsz, g_n),
        in_specs=[q_spec, k_spec, k_spec, v_spec, v_spec, cmp_spec, cmp_spec, cmp_spec, cmp_spec, k_spec,
                  pl.BlockSpec((1, cols, 1), lambda b, g: (g, 0, 0)),
                  pl.BlockSpec((1, cols, NSA_FEAT), lambda b, g: (g, 0, 0)),
                  _const_spec((n_cmp_pad, NSA_FEAT)),
                  _const_spec(b1s), _const_spec(w2s), _const_spec(b1s), _const_spec((dh, CMP_HIDDEN)),
                  _const_spec((NSA_FEAT, n_cmp_pad))],
        out_specs=q_spec,
        scratch_shapes=[pltpu.VMEM((n_cmp_pad, dh + NSA_FEAT), BF16), pltpu.VMEM((dh, n_cmp_pad), BF16),
                        pltpu.VMEM((seq // span, dh, span), BF16), pltpu.VMEM((seq // span, dh, span), BF16),
                        pltpu.VMEM((3, cols, dh + NSA_FEAT), BF16), pltpu.VMEM((dh, cols), F32)],
        compiler_params=_cparams(2),
        name="nsa_attention",
    )(q, ksa, kwa, vst, vwt, kct, kcb, vct, vcb, gl, slope_rows, sfeat, jnp.asarray(cf),
      pos_const(pe_k, k_w1, k_b1), k_w2.astype(BF16), pos_const(pe_v, v_w1, v_b1), v_w2.astype(BF16).T, jnp.asarray(overlap_t, dtype=BF16))


def _pad_cols(w, n):
    return jnp.pad(w, ((0, 0), (0, n - w.shape[1])))


def kernel(x, ab_w_in, ab_w_out, s5_lambda_re, s5_lambda_im, s5_log_step, s5_b_re, s5_b_im, s5_c_re, s5_c_im, s5_d, s5_w_glu, s5_b_glu, gdn_conv_w, gdn_a_log, gdn_dt_bias, gdn_norm_w, nsa_w_in, nsa_w_out, nsa_pe_k, nsa_pe_v, nsa_k_w1, nsa_k_b1, nsa_k_w2, nsa_v_w1, nsa_v_b1, nsa_v_w2, ffn_w_in, ffn_conv_w, ffn_conv_b, ffn_w_out, norm_mix, norm_ffn, norm_final):
    bsz, seq, d = x.shape
    depth = ffn_w_in.shape[0]
    h = x.astype(F32).reshape(bsz * seq, d)
    for layer in range(depth):
        i = layer // 2
        if layer % 2 == 0:
            u, q, k, v, z, ba = _ab_proj(h, norm_mix[layer], ab_w_in[i], gdn_conv_w[i], seq)
            prep = _s5_prep(s5_lambda_re[i], s5_lambda_im[i], s5_log_step[i], s5_b_re[i], s5_b_im[i],
                            s5_c_re[i], s5_c_im[i], s5_d[i])
            y_a = _s5(u, seq, prep, s5_w_glu[i], s5_b_glu[i])
            y_b = _gdn(q, k, v, z, ba, seq, gdn_a_log[i], gdn_dt_bias[i], gdn_norm_w[i])
            wo = ab_w_out[i].astype(BF16)
            mix, wms = [y_a, y_b], [wo[:S5_WIDTH], wo[S5_WIDTH:]]
        else:
            proj = _nsa_proj(h, norm_mix[layer], nsa_w_in[i], nsa_k_w1[i], nsa_v_w1[i], bsz, seq)
            o = _nsa(*proj, bsz, seq, nsa_pe_k[i], nsa_pe_v[i], nsa_k_w1[i],
                     nsa_k_b1[i], nsa_k_w2[i], nsa_v_w1[i], nsa_v_b1[i], nsa_v_w2[i])
            mix, wms = [o], [nsa_w_out[i].astype(BF16)]
        h = _mix_ffn(h, mix, wms, seq, norm_ffn[layer], ffn_w_in[layer], ffn_conv_w[layer], ffn_conv_b[layer],
                     ffn_w_out[layer], norm_final, final=(layer == depth - 1))
    return h.reshape(bsz, seq, d).astype(x.dtype)
```
